```python
import math
import jax, jax.numpy as jnp
from jax import lax
import numpy as np

D_MODEL = 2048
BATCH = 1
SEQ = 8192
DEPTH = 2
DEC_BATCH = 32
DEC_SEQ = 1
PAST_LEN = 8192
PAGE_SIZE = 128

N_A_LAYERS = DEPTH // 2
N_B_LAYERS = DEPTH - N_A_LAYERS
N_DENSE = (DEPTH + 1) // 2
N_MOE = DEPTH // 2

ML_HEADS = 8
ML_DQK = D_MODEL // ML_HEADS // 2
ML_DV = D_MODEL // ML_HEADS
ML_CHUNK = 128
ML_IN = 2 * ML_HEADS * ML_DQK + ML_HEADS * ML_DV + D_MODEL + 2 * ML_HEADS

ATT_HEAD_DIM = 128
ATT_GROUP_HEADS = 8
ATT_N_GROUPS = 3
ATT_WINDOWS = (128, 512, 2048)
ATT_DILATIONS = (1, 4, 16)
ATT_GROUP_W = ATT_GROUP_HEADS * ATT_HEAD_DIM
ROPE_THETA = 10000.0

FFN_DENSE = 5632
N_EXPERTS = 8
TOP_K = 2
FFN_EXPERT = 7168
RMS_EPS = 1e-6

kernel_name = 'yoco_mlstm_dilated_swa_decoder_step'


def rmsnorm(x, g):
    xf = x.astype(jnp.float32)
    y = xf * lax.rsqrt(jnp.mean(xf * xf, axis=-1, keepdims=True) + RMS_EPS)
    return (y * g.astype(jnp.float32)).astype(x.dtype)


def rope(x, pos):
    half = x.shape[-1] // 2
    inv = ROPE_THETA ** (-jnp.arange(half, dtype=jnp.float32) / half)
    ang = pos.astype(jnp.float32)[:, None] * inv[None, :]
    cos = jnp.cos(ang)[:, None, :]
    sin = jnp.sin(ang)[:, None, :]
    xf = x.astype(jnp.float32)
    x1, x2 = xf[..., :half], xf[..., half:]
    return jnp.concatenate([x1 * cos - x2 * sin, x2 * cos + x1 * sin], axis=-1).astype(x.dtype)


def swiglu(x, wg, wu, wd):
    return (jax.nn.silu(x @ wg) * (x @ wu)) @ wd


def moe_ffn(x, w_router, b_router, wg, wu, wd):
    logits = (x @ w_router).astype(jnp.float32) + b_router.astype(jnp.float32)
    probs = jax.nn.softmax(logits, axis=-1)
    top_p, top_i = lax.top_k(probs, TOP_K)
    top_p = top_p / jnp.sum(top_p, axis=-1, keepdims=True)
    gates = jnp.sum(jax.nn.one_hot(top_i, N_EXPERTS, dtype=jnp.float32) * top_p[..., None], axis=-2)
    y = jnp.zeros_like(x)
    for e in range(N_EXPERTS):
        y = y + gates[..., e:e + 1].astype(x.dtype) * swiglu(x, wg[e], wu[e], wd[e])
    return y


def mlstm_scan(q, k, v, ig, lf, C0, n0, m0):
    B, T, H, _ = q.shape
    L = math.gcd(T, ML_CHUNK)
    nc = T // L

    def chunks(a):
        a = a.astype(jnp.float32).reshape((B, nc, L) + a.shape[2:])
        return jnp.swapaxes(jnp.moveaxis(a, 1, 0), 2, 3)

    causal = jnp.tril(jnp.ones((L, L), dtype=bool))

    def step(carry, xs):
        C, n, m = carry
        qc, kc, vc, ic, fc = xs
        b = jnp.cumsum(fc, axis=-1)
        logd = jnp.where(causal, b[..., :, None] - b[..., None, :] + ic[..., None, :], -jnp.inf)
        inter = b + m[..., None]
        mt = jnp.maximum(inter, jnp.max(logd, axis=-1))
        a = jnp.einsum('bhtd,bhsd->bhts', qc, kc) * jnp.exp(logd - mt[..., None])
        w_inter = jnp.exp(inter - mt)
        num = jnp.einsum('bhts,bhsv->bhtv', a, vc) + w_inter[..., None] * jnp.einsum('bhvd,bhtd->bhtv', C, qc)
        den = jnp.sum(a, axis=-1) + w_inter * jnp.einsum('bhd,bhtd->bht', n, qc)
        h = num / jnp.maximum(jnp.abs(den), jnp.exp(-mt))[..., None]
        m_new = mt[..., -1]
        decay = jnp.exp(b[..., -1] + m - m_new)
        wj = jnp.exp(b[..., -1:] - b + ic - m_new[..., None])
        C_new = decay[..., None, None] * C + jnp.einsum('bhs,bhsv,bhsd->bhvd', wj, vc, kc)
        n_new = decay[..., None] * n + jnp.einsum('bhs,bhsd->bhd', wj, kc)
        return (C_new, n_new, m_new), h

    init = (C0.astype(jnp.float32), n0.astype(jnp.float32), m0.astype(jnp.float32))
    (C, n, m), hs = lax.scan(step, init, (chunks(q), chunks(k), chunks(v), chunks(ig), chunks(lf)))
    h = jnp.transpose(hs, (1, 0, 3, 2, 4)).reshape(B, T, H, -1)
    return h, C.astype(C0.dtype), n.astype(n0.dtype), m.astype(m0.dtype)


def mlstm_mixer(xn, w_in, b_gates, b_o, g_hn, w_out, C0, n0, m0):
    B, T, _ = xn.shape
    nqk = ML_HEADS * ML_DQK
    nv = ML_HEADS * ML_DV
    z = xn @ w_in
    q = z[..., :nqk].reshape(B, T, ML_HEADS, ML_DQK)
    k = z[..., nqk:2 * nqk].reshape(B, T, ML_HEADS, ML_DQK) * (ML_DQK ** -0.5)
    v = z[..., 2 * nqk:2 * nqk + nv].reshape(B, T, ML_HEADS, ML_DV)
    o = jax.nn.sigmoid(z[..., 2 * nqk + nv:2 * nqk + nv + D_MODEL] + b_o)
    gates = (z[..., 2 * nqk + nv + D_MODEL:] + b_gates).astype(jnp.float32)
    ig = gates[..., :ML_HEADS]
    lf = jax.nn.log_sigmoid(gates[..., ML_HEADS:])
    h, C, n, m = mlstm_scan(q, k, v, ig, lf, C0, n0, m0)
    h = rmsnorm(h, g_hn.reshape(ML_HEADS, ML_DV)).reshape(B, T, nv)
    y = (h.astype(xn.dtype) * o) @ w_out
    return y, C, n, m


def dilated_attn_prompt(q, k, v, dil, steps):
    B, S, H, Dh = q.shape
    L = S // dil

    def split(a):
        a = a.astype(jnp.float32).reshape(B, L, dil, H, Dh)
        return jnp.swapaxes(a, 1, 2).reshape(B * dil, L, H, Dh)

    qs, ks, vs = split(q), split(k), split(v)
    nb = -(-L // steps)
    Lp = nb * steps
    N = B * dil
    qb = jnp.pad(qs, ((0, 0), (0, Lp - L), (0, 0), (0, 0))).reshape(N, nb, steps, H, Dh)
    kp = jnp.pad(ks, ((0, 0), (steps, Lp - L), (0, 0), (0, 0))).reshape(N, nb + 1, steps, H, Dh)
    vp = jnp.pad(vs, ((0, 0), (steps, Lp - L), (0, 0), (0, 0))).reshape(N, nb + 1, steps, H, Dh)
    kb = jnp.concatenate([kp[:, :-1], kp[:, 1:]], axis=2)
    vb = jnp.concatenate([vp[:, :-1], vp[:, 1:]], axis=2)
    qi = jnp.arange(steps)[:, None]
    kj = jnp.arange(2 * steps)[None, :]
    dist = qi - kj + steps
    kidx = jnp.arange(nb)[:, None, None] * steps + kj[None] - steps
    valid = ((dist >= 0) & (dist <= steps))[None] & (kidx >= 0)
    s = jnp.einsum('nbqhd,nbkhd->nbhqk', qb, kb) * (Dh ** -0.5)
    s = jnp.where(valid[None, :, None], s, -jnp.inf)
    mx = jnp.max(s, axis=-1, keepdims=True)
    p = jnp.exp(s - mx)
    den = jnp.sum(p, axis=-1)
    out = jnp.einsum('nbhqk,nbkhd->nbqhd', p, vb) / jnp.swapaxes(den, 2, 3)[..., None]
    lse = jnp.swapaxes(mx[..., 0] + jnp.log(den), 2, 3)
    out = out.reshape(N, Lp, H, Dh)[:, :L].reshape(B, dil, L, H, Dh)
    lse = lse.reshape(N, Lp, H)[:, :L].reshape(B, dil, L, H)
    out = jnp.swapaxes(out, 1, 2).reshape(B, S, H, Dh)
    lse = jnp.swapaxes(lse, 1, 2).reshape(B, S, H)
    return out, lse


def dilated_attn_sample(q, k_new, v_new, kv_buf, dil, steps):
    N, T, H, Dh = q.shape
    Lb = kv_buf.shape[1]
    kc = jnp.concatenate([kv_buf[:, :, 0], k_new.astype(kv_buf.dtype)], axis=1).astype(jnp.float32)
    vc = jnp.concatenate([kv_buf[:, :, 1], v_new.astype(kv_buf.dtype)], axis=1).astype(jnp.float32)
    idx = Lb + jnp.arange(T)[:, None] - dil * jnp.arange(steps + 1)[None, :]
    valid = idx >= 0
    idx_c = jnp.maximum(idx, 0)
    kg = kc[:, idx_c]
    vg = vc[:, idx_c]
    s = jnp.einsum('nthd,ntkhd->nthk', q.astype(jnp.float32), kg) * (Dh ** -0.5)
    s = jnp.where(valid[None, :, None, :], s, -jnp.inf)
    mx = jnp.max(s, axis=-1, keepdims=True)
    p = jnp.exp(s - mx)
    den = jnp.sum(p, axis=-1)
    out = jnp.einsum('nthk,ntkhd->nthd', p, vg) / den[..., None]
    lse = mx[..., 0] + jnp.log(den)
    return out, lse


def shared_kv(h, g_kv, w_kv, pos):
    B, T, _ = h.shape
    z = (rmsnorm(h, g_kv) @ w_kv).reshape(B, T, ATT_N_GROUPS, 2, ATT_GROUP_HEADS, ATT_HEAD_DIM)
    return [(rope(z[:, :, g, 0], pos), z[:, :, g, 1]) for g in range(ATT_N_GROUPS)]


def new_buffers(kv_list, kv_bufs, prompt):
    out = []
    for g in range(ATT_N_GROUPS):
        k, v = kv_list[g]
        rows = jnp.stack([k, v], axis=2)
        if prompt:
            keep = min(ATT_WINDOWS[g], rows.shape[1])
            out.append(rows[:, rows.shape[1] - keep:])
        else:
            keep = kv_bufs[g].shape[1]
            out.append(jnp.concatenate([kv_bufs[g], rows.astype(kv_bufs[g].dtype)], axis=1)[:, -keep:])
    return out


def dilated_mixer(xn, w_q, w_o, kv_list, kv_bufs, pos, prompt):
    B, T, _ = xn.shape
    q = (xn @ w_q).reshape(B, T, ATT_N_GROUPS, ATT_GROUP_HEADS, ATT_HEAD_DIM)
    outs, lses = [], []
    for g in range(ATT_N_GROUPS):
        qg = rope(q[:, :, g], pos)
        k, v = kv_list[g]
        dil = ATT_DILATIONS[g]
        steps = ATT_WINDOWS[g] // dil
        if prompt:
            o, l = dilated_attn_prompt(qg, k, v, dil, steps)
        else:
            o, l = dilated_attn_sample(qg, k, v, kv_bufs[g], dil, steps)
        outs.append(o)
        lses.append(l)
    wgt = jax.nn.softmax(jnp.stack(lses, axis=0), axis=0)
    merged = jnp.sum(wgt[..., None] * jnp.stack(outs, axis=0), axis=0)
    return merged.reshape(B, T, ATT_GROUP_W).astype(xn.dtype) @ w_o


def trunk(x, pos, C0, n0, m0, kv_bufs, p, prompt):
    h = x
    Cs, ns, ms = [], [], []
    kv_list, bufs = None, None
    for layer in range(DEPTH):
        xn = rmsnorm(h, p['g_mix'][layer])
        if layer < N_A_LAYERS:
            a = layer
            y, C, n, m = mlstm_mixer(xn, p['w_ml_in'][a], p['b_ml_gates'][a], p['b_ml_o'][a],
                                     p['g_ml_hnorm'][a], p['w_ml_out'][a], C0[a], n0[a], m0[a])
            Cs.append(C)
            ns.append(n)
            ms.append(m)
        else:
            if kv_list is None:
                kv_list = shared_kv(h, p['g_kv'], p['w_kv'], pos)
                bufs = new_buffers(kv_list, kv_bufs, prompt)
            bi = layer - N_A_LAYERS
            y = dilated_mixer(xn, p['w_q'][bi], p['w_o'][bi], kv_list, kv_bufs, pos, prompt)
        h = h + y
        xn = rmsnorm(h, p['g_ffn'][layer])
        if layer % 2 == 0:
            li = layer // 2
            f = swiglu(xn, p['w_ffn_gate'][li], p['w_ffn_up'][li], p['w_ffn_down'][li])
        else:
            li = layer // 2
            f = moe_ffn(xn, p['w_router'][li], p['b_router'][li], p['w_exp_gate'][li],
                        p['w_exp_up'][li], p['w_exp_down'][li])
        h = h + f
    return rmsnorm(h, p['g_final']), jnp.stack(Cs, 0), jnp.stack(ns, 0), jnp.stack(ms, 0), bufs


def setup_inputs(seed: int = 0) -> dict:
    key = jax.random.key(seed)
    ks = jax.random.split(key, 32)
    D = D_MODEL
    f32 = jnp.float32

    def nrm(k, shape, scale):
        return jax.random.normal(k, shape, f32) * scale

    buf_lens = [min(w, PAST_LEN) for w in ATT_WINDOWS]
    gate_bias = jnp.concatenate([nrm(ks[13], (N_A_LAYERS, ML_HEADS), 0.1),
                                 3.0 + nrm(ks[14], (N_A_LAYERS, ML_HEADS), 0.5)], axis=-1)
    return {
        'x_prompt': nrm(ks[0], (BATCH, SEQ, D), 1.0),
        'x_sample': nrm(ks[1], (DEC_BATCH, DEC_SEQ, D), 1.0),
        'state_mlstm_C': nrm(ks[2], (N_A_LAYERS, DEC_BATCH, ML_HEADS, ML_DV, ML_DQK), 1.0),
        'state_mlstm_n': nrm(ks[3], (N_A_LAYERS, DEC_BATCH, ML_HEADS, ML_DQK), 1.0),
        'state_mlstm_m': nrm(ks[4], (N_A_LAYERS, DEC_BATCH, ML_HEADS), 1.0),
        'cache_kv_w128': nrm(ks[5], (DEC_BATCH, buf_lens[0], 2, ATT_GROUP_HEADS, ATT_HEAD_DIM), 1.0),
        'cache_kv_w512': nrm(ks[6], (DEC_BATCH, buf_lens[1], 2, ATT_GROUP_HEADS, ATT_HEAD_DIM), 1.0),
        'cache_kv_w2048': nrm(ks[7], (DEC_BATCH, buf_lens[2], 2, ATT_GROUP_HEADS, ATT_HEAD_DIM), 1.0),
        'g_mix': 1.0 + nrm(ks[8], (DEPTH, D), 0.02),
        'g_ffn': 1.0 + nrm(ks[9], (DEPTH, D), 0.02),
        'w_ml_in': nrm(ks[10], (N_A_LAYERS, D, ML_IN), D ** -0.5),
        'b_ml_gates': gate_bias,
        'b_ml_o': nrm(ks[11], (N_A_LAYERS, D), 0.02),
        'g_ml_hnorm': 1.0 + nrm(ks[12], (N_A_LAYERS, ML_HEADS * ML_DV), 0.02),
        'w_ml_out': nrm(ks[15], (N_A_LAYERS, ML_HEADS * ML_DV, D), (ML_HEADS * ML_DV) ** -0.5),
        'g_kv': 1.0 + nrm(ks[16], (D,), 0.02),
        'w_kv': nrm(ks[17], (D, ATT_N_GROUPS * 2 * ATT_GROUP_W), D ** -0.5),
        'w_q': nrm(ks[18], (N_B_LAYERS, D, ATT_N_GROUPS * ATT_GROUP_W), D ** -0.5),
        'w_o': nrm(ks[19], (N_B_LAYERS, ATT_GROUP_W, D), ATT_GROUP_W ** -0.5),
        'w_ffn_gate': nrm(ks[20], (N_DENSE, D, FFN_DENSE), D ** -0.5),
        'w_ffn_up': nrm(ks[21], (N_DENSE, D, FFN_DENSE), D ** -0.5),
        'w_ffn_down': nrm(ks[22], (N_DENSE, FFN_DENSE, D), FFN_DENSE ** -0.5),
        'w_router': nrm(ks[23], (N_MOE, D, N_EXPERTS), D ** -0.5),
        'b_router': nrm(ks[24], (N_MOE, N_EXPERTS), 0.01),
        'w_exp_gate': nrm(ks[25], (N_MOE, N_EXPERTS, D, FFN_EXPERT), D ** -0.5),
        'w_exp_up': nrm(ks[26], (N_MOE, N_EXPERTS, D, FFN_EXPERT), D ** -0.5),
        'w_exp_down': nrm(ks[27], (N_MOE, N_EXPERTS, FFN_EXPERT, D), FFN_EXPERT ** -0.5),
        'g_final': 1.0 + nrm(ks[28], (D,), 0.02),
    }


def reference(x_prompt, x_sample, state_mlstm_C, state_mlstm_n, state_mlstm_m,
              cache_kv_w128, cache_kv_w512, cache_kv_w2048,
              g_mix, g_ffn, w_ml_in, b_ml_gates, b_ml_o, g_ml_hnorm, w_ml_out,
              g_kv, w_kv, w_q, w_o, w_ffn_gate, w_ffn_up, w_ffn_down,
              w_router, b_router, w_exp_gate, w_exp_up, w_exp_down, g_final):
    p = {'g_mix': g_mix, 'g_ffn': g_ffn, 'w_ml_in': w_ml_in, 'b_ml_gates': b_ml_gates,
         'b_ml_o': b_ml_o, 'g_ml_hnorm': g_ml_hnorm, 'w_ml_out': w_ml_out, 'g_kv': g_kv,
         'w_kv': w_kv, 'w_q': w_q, 'w_o': w_o, 'w_ffn_gate': w_ffn_gate, 'w_ffn_up': w_ffn_up,
         'w_ffn_down': w_ffn_down, 'w_router': w_router, 'b_router': b_router,
         'w_exp_gate': w_exp_gate, 'w_exp_up': w_exp_up, 'w_exp_down': w_exp_down,
         'g_final': g_final}
    Bp, Sp, _ = x_prompt.shape
    dt = x_prompt.dtype
    C0 = jnp.zeros((N_A_LAYERS, Bp, ML_HEADS, ML_DV, ML_DQK), dt)
    n0 = jnp.zeros((N_A_LAYERS, Bp, ML_HEADS, ML_DQK), dt)
    m0 = jnp.zeros((N_A_LAYERS, Bp, ML_HEADS), dt)
    pos_p = jnp.arange(Sp)
    y_prompt, pC, pn, pm, pbufs = trunk(x_prompt, pos_p, C0, n0, m0, None, p, True)
    pos_s = PAST_LEN + jnp.arange(x_sample.shape[1])
    kv_bufs = [cache_kv_w128, cache_kv_w512, cache_kv_w2048]
    y_sample, sC, sn, sm, sbufs = trunk(x_sample, pos_s, state_mlstm_C, state_mlstm_n, state_mlstm_m,
                                        kv_bufs, p, False)
    return (y_prompt, y_sample, pC, pn, pm, sC, sn, sm, pbufs[0], pbufs[1], pbufs[2], sbufs[0], sbufs[1], sbufs[2])
```

```python
import functools

import jax
import jax.numpy as jnp
from jax import lax
from jax.experimental import pallas as pl
from jax.experimental.pallas import tpu as pltpu

F32 = jnp.float32
BF16 = jnp.bfloat16
I32 = jnp.int32
HIGHEST = lax.Precision.HIGHEST

D_MODEL = 2048
ML_HEADS = 8
ML_DQK = 128
ML_DV = 256
ML_NQK = ML_HEADS * ML_DQK
ML_NV = ML_HEADS * ML_DV
ML_MAIN = 2 * ML_NQK + ML_NV + D_MODEL
ML_CHUNK = 128
ATT_HD = 128
ATT_H = 8
ATT_G = 3
ATT_WINDOWS = (128, 512, 2048)
ATT_DILS = (1, 4, 16)
ATT_STEPS = 128
ATT_GW = ATT_H * ATT_HD
ROPE_THETA = 10000.0
PAST_LEN = 8192
N_EXPERTS = 8
RMS_EPS = 1e-6
NEG_BIG = -1e30

VMEM_LIMIT_BYTES = 58 * 1024 * 1024
ROW_TILE = 256
MOE_TILE = 256


def _cparams(sem):
    return pltpu.CompilerParams(dimension_semantics=sem, vmem_limit_bytes=VMEM_LIMIT_BYTES)


def _rms(x):
    return x * lax.rsqrt(jnp.mean(x * x, axis=-1, keepdims=True) + RMS_EPS)


def _norm_kernel(x_ref, xs_ref, g_ref, *refs, n_out, has_proj):
    ins = 1 if has_proj else 0
    n_each = n_out + ins
    main = refs[ins:ins + n_each]
    side = refs[ins + n_each:ins + 2 * n_each]

    def emit(x, outs):
        y = _rms(x)
        for i in range(n_out):
            outs[i][...] = (y * g_ref[i:i + 1, :]).astype(outs[i].dtype)
        if has_proj:
            outs[n_out][...] = jnp.dot(y * g_ref[0:1, :], refs[0][...], precision=HIGHEST,
                                       preferred_element_type=F32)

    emit(x_ref[...], main)

    @pl.when(pl.program_id(0) == 0)
    def _():
        emit(xs_ref[...], side)


def _rmsnorm(x, xs, gains, proj=None, tm=ROW_TILE):
    m, d = x.shape
    s = xs.shape[0]
    n_out = gains.shape[0]
    const = lambda i: (0, 0)
    in_specs = [pl.BlockSpec((tm, d), lambda i: (i, 0)), pl.BlockSpec((s, d), const),
                pl.BlockSpec((n_out, d), const)]
    args = [x, xs, gains]
    main_shape = [jax.ShapeDtypeStruct((m, d), BF16)] * n_out
    main_specs = [pl.BlockSpec((tm, d), lambda i: (i, 0))] * n_out
    side_shape = [jax.ShapeDtypeStruct((s, d), F32)] * n_out
    side_specs = [pl.BlockSpec((s, d), const)] * n_out
    if proj is not None:
        p = proj.shape[1]
        in_specs.append(pl.BlockSpec((d, p), const))
        args.append(proj)
        main_shape.append(jax.ShapeDtypeStruct((m, p), F32))
        main_specs.append(pl.BlockSpec((tm, p), lambda i: (i, 0)))
        side_shape.append(jax.ShapeDtypeStruct((s, p), F32))
        side_specs.append(pl.BlockSpec((s, p), const))
    outs = pl.pallas_call(
        functools.partial(_norm_kernel, n_out=n_out, has_proj=proj is not None),
        grid=(m // tm,), in_specs=in_specs, out_specs=main_specs + side_specs,
        out_shape=main_shape + side_shape,
        compiler_params=_cparams(("arbitrary",)), name="rmsnorm",
    )(*args)
    k = len(main_shape)
    return outs[:k], outs[k:]


def _rope_heads(acc, cos, sin):
    outs = []
    for h in range(acc.shape[1] // ATT_HD):
        a = acc[:, h * ATT_HD:(h + 1) * ATT_HD]
        outs.append(a * cos + pltpu.roll(a, ATT_HD // 2, 1) * sin)
    return jnp.concatenate(outs, axis=1)


def _mm_kernel(te_ref, nv_ref, x_ref, *refs, n_w, has_scale, has_res, rope, dil, has_nat):
    n = pl.program_id(0)
    m = pl.program_id(1)
    it = iter(refs)
    w_refs = [next(it) for _ in range(n_w)]
    scale_ref = next(it) if has_scale else None
    res_ref = next(it) if has_res else None
    cos_ref, sin_ref = (next(it), next(it)) if rope else (None, None)
    xs_ref = next(it)
    res_s_ref = next(it) if has_res else None
    cos_s_ref, sin_s_ref = (next(it), next(it)) if rope else (None, None)
    o_ref = next(it)
    nat_ref = next(it) if has_nat else None
    os_ref = next(it)
    wb_refs = [next(it) for _ in range(n_w)]
    deint = next(it) if dil > 1 else None

    def finish(acc, up, res, cos, sin, store):
        if n_w == 2:
            acc = (acc * jax.nn.sigmoid(acc)) * up
        if has_scale:
            acc = acc * scale_ref[...]
        if has_res:
            acc = acc + res
        if rope == "all":
            store(_rope_heads(acc, cos, sin))
        elif rope == "even":
            @pl.when(n % 2 == 0)
            def _():
                store(_rope_heads(acc, cos, sin))

            @pl.when(n % 2 == 1)
            def _():
                store(acc)
        else:
            store(acc)

    def store_side(val):
        os_ref[0] = val

    def store_main(val):
        if has_nat:
            nat_ref[...] = val
        if dil == 1:
            o_ref[0] = val.astype(o_ref.dtype)
        else:
            rows = deint.shape[1] // dil
            for c in range(deint.shape[0]):
                lanes = slice(c * 128, (c + 1) * 128)
                deint[c] = val[:, lanes]
                for r in range(dil):
                    o_ref[r, :, lanes] = deint[c, pl.ds(r, rows, stride=dil), :].astype(o_ref.dtype)

    prev = jnp.maximum(m - 1, 0)
    new_weights = jnp.logical_or(m == 0, te_ref[m] != te_ref[prev])

    @pl.when(new_weights)
    def _():
        for w_ref, wb_ref in zip(w_refs, wb_refs):
            wb_ref[...] = w_ref[0].astype(BF16)
        xs = xs_ref[0]
        acc = jnp.dot(xs, w_refs[0][0], precision=HIGHEST, preferred_element_type=F32)
        up = jnp.dot(xs, w_refs[1][0], precision=HIGHEST, preferred_element_type=F32) if n_w == 2 else None
        finish(acc, up, res_s_ref[...] if has_res else None,
               cos_s_ref[...] if rope else None, sin_s_ref[...] if rope else None, store_side)

    @pl.when(m >= nv_ref[0])
    def _():
        o_ref[...] = jnp.zeros_like(o_ref)

    @pl.when(m < nv_ref[0])
    def _():
        xb = x_ref[...].astype(BF16)
        acc = jnp.dot(xb, wb_refs[0][...], preferred_element_type=F32)
        up = jnp.dot(xb, wb_refs[1][...], preferred_element_type=F32) if n_w == 2 else None
        finish(acc, up, res_ref[...] if has_res else None,
               cos_ref[...] if rope else None, sin_ref[...] if rope else None, store_main)


def _matmul(x, ws, n_cols, out_dtype, xs, *, tm, tn, col_off=0, te=None, nvalid=None, xs_per_expert=False,
            col_scale=None, res=None, res_s=None, rope=None, cos=None, sin=None, cos_s=None, sin_s=None,
            dil=1, natural=False, name="matmul"):
    m, k = x.shape
    s = xs.shape[1]
    n_m = m // tm
    n_n = n_cols // tn
    n_e = ws[0].shape[0]
    if te is None:
        te = jnp.zeros((n_m,), I32)
        nvalid = jnp.full((1,), n_m, I32)

    def row(mi, nv):
        return jnp.minimum(mi, nv[0] - 1)

    def exp(mi, te, nv):
        return te[row(mi, nv)]

    in_specs = [pl.BlockSpec((tm, k), lambda n, mi, te, nv: (row(mi, nv), 0))]
    args = [x]
    for w in ws:
        in_specs.append(pl.BlockSpec((1, k, tn), lambda n, mi, te, nv: (exp(mi, te, nv), 0, n + col_off)))
        args.append(w)
    if col_scale is not None:
        in_specs.append(pl.BlockSpec((1, tn), lambda n, mi, te, nv: (0, n)))
        args.append(col_scale)
    if res is not None:
        in_specs.append(pl.BlockSpec((tm, tn), lambda n, mi, te, nv: (row(mi, nv), n)))
        args.append(res)
    if rope is not None:
        for t in (cos, sin):
            in_specs.append(pl.BlockSpec((tm, ATT_HD), lambda n, mi, te, nv: (row(mi, nv), 0)))
            args.append(t)
    if xs_per_expert:
        in_specs.append(pl.BlockSpec((1, s, k), lambda n, mi, te, nv: (exp(mi, te, nv), 0, 0)))
    else:
        in_specs.append(pl.BlockSpec((1, s, k), lambda n, mi, te, nv: (0, 0, 0)))
    args.append(xs)
    if res is not None:
        in_specs.append(pl.BlockSpec((s, tn), lambda n, mi, te, nv: (0, n)))
        args.append(res_s)
    if rope is not None:
        for t in (cos_s, sin_s):
            in_specs.append(pl.BlockSpec((s, ATT_HD), lambda n, mi, te, nv: (0, 0)))
            args.append(t)

    out_shape = [jax.ShapeDtypeStruct((dil, m // dil, n_cols), out_dtype)]
    out_specs = [pl.BlockSpec((dil, tm // dil, tn), lambda n, mi, te, nv: (0, mi, n))]
    if natural:
        out_shape.append(jax.ShapeDtypeStruct((m, n_cols), F32))
        out_specs.append(pl.BlockSpec((tm, tn), lambda n, mi, te, nv: (mi, n)))
    out_shape.append(jax.ShapeDtypeStruct((n_e, s, n_cols), F32))
    out_specs.append(pl.BlockSpec((1, s, tn), lambda n, mi, te, nv: (exp(mi, te, nv), 0, n)))
    scratch = [pltpu.VMEM((k, tn), BF16) for _ in ws]
    if dil > 1:
        scratch.append(pltpu.VMEM((tn // 128, tm, 128), F32))
    kern = functools.partial(_mm_kernel, n_w=len(ws), has_scale=col_scale is not None, has_res=res is not None,
                             rope=rope, dil=dil, has_nat=natural)
    return pl.pallas_call(
        kern,
        grid_spec=pltpu.PrefetchScalarGridSpec(
            num_scalar_prefetch=2, grid=(n_n, n_m), in_specs=in_specs, out_specs=out_specs,
            scratch_shapes=scratch),
        out_shape=out_shape,
        compiler_params=_cparams(("arbitrary", "arbitrary")), name=name,
    )(te, nvalid, *args)


def _log_sigmoid(x):
    return jnp.minimum(x, 0.0) - jnp.log1p(jnp.exp(-jnp.abs(x)))


def _mlstm_prompt_kernel(q_ref, k_ref, v_ref, o_ref, gi_ref, gf_ref, bi_ref, bf_ref, bo_ref, ghn_ref,
                         h_ref, c_out_ref, n_out_ref, m_out_ref, ct_s, n_s, m_s):
    c = pl.program_id(0)
    L = ML_CHUNK

    @pl.when(c == 0)
    def _():
        ct_s[...] = jnp.zeros_like(ct_s)
        n_s[...] = jnp.zeros_like(n_s)
        m_s[...] = jnp.zeros_like(m_s)

    ig = gi_ref[...] + bi_ref[...]
    lf = _log_sigmoid(gf_ref[...] + bf_ref[...])
    r = lax.broadcasted_iota(I32, (L, L), 0)
    s = lax.broadcasted_iota(I32, (L, L), 1)
    causal = r >= s
    tril = causal.astype(F32)
    b = jnp.dot(tril, lf, precision=HIGHEST, preferred_element_type=F32)
    b_t = b.T
    ig_t = ig.T
    m_all = m_s[...]
    m_new_all = m_all
    lane = lax.broadcasted_iota(I32, (1, 128), 1)

    for h in range(ML_HEADS):
        qh = q_ref[:, h * ML_DQK:(h + 1) * ML_DQK]
        kh = k_ref[:, h * ML_DQK:(h + 1) * ML_DQK]
        vh = v_ref[:, h * ML_DV:(h + 1) * ML_DV]
        bc = b[:, h:h + 1]
        ic = ig[:, h:h + 1]
        br = b_t[h:h + 1, :]
        ir = ig_t[h:h + 1, :]
        m_h = m_all[:, h:h + 1]
        logd = jnp.where(causal, bc - br + ir, -jnp.inf)
        inter = bc + m_h
        mt = jnp.maximum(inter, jnp.max(logd, axis=1, keepdims=True))
        sc = lax.dot_general(qh, kh, (((1,), (1,)), ((), ())), preferred_element_type=F32)
        a = sc * jnp.exp(logd - mt)
        w_inter = jnp.exp(inter - mt)
        ct_h = ct_s[h]
        qc = jnp.dot(qh, ct_h.astype(BF16), preferred_element_type=F32)
        num = jnp.dot(a.astype(BF16), vh, preferred_element_type=F32) + w_inter * qc
        n_h = n_s[h:h + 1, :]
        qn = jnp.sum(qh.astype(F32) * n_h, axis=1, keepdims=True)
        den = jnp.sum(a, axis=1, keepdims=True) + w_inter * qn
        hh = num / jnp.maximum(jnp.abs(den), jnp.exp(-mt))
        hn = _rms(hh) * ghn_ref[:, h * ML_DV:(h + 1) * ML_DV]
        og = jax.nn.sigmoid(o_ref[:, h * ML_DV:(h + 1) * ML_DV].astype(F32) + bo_ref[:, h * ML_DV:(h + 1) * ML_DV])
        h_ref[:, h * ML_DV:(h + 1) * ML_DV] = (hn * og).astype(h_ref.dtype)
        m_new = mt[L - 1:L, :]
        b_last = bc[L - 1:L, :]
        decay = jnp.exp(b_last + m_h - m_new)
        wj = jnp.exp(b_last - bc + ic - m_new)
        kw = (kh.astype(F32) * wj).astype(BF16)
        upd = lax.dot_general(kw, vh, (((0,), (0,)), ((), ())), preferred_element_type=F32)
        ct_s[h] = decay * ct_h + upd
        n_s[h:h + 1, :] = decay * n_h + jnp.sum(kh.astype(F32) * wj, axis=0, keepdims=True)
        m_new_all = jnp.where(lane == h, m_new, m_new_all)

    m_s[...] = m_new_all

    @pl.when(c == pl.num_programs(0) - 1)
    def _():
        for h in range(ML_HEADS):
            c_out_ref[h] = ct_s[h].T
        n_out_ref[...] = n_s[...]
        m_out_ref[...] = m_s[...]


def _mlstm_prompt(z, gates, bi, bf, bo, ghn):
    seq = z.shape[0]
    nc = seq // ML_CHUNK
    L = ML_CHUNK
    const2 = lambda c: (0, 0)
    in_specs = [
        pl.BlockSpec((L, ML_NQK), lambda c: (c, 0)),
        pl.BlockSpec((L, ML_NQK), lambda c: (c, 1)),
        pl.BlockSpec((L, ML_NV), lambda c: (c, 1)),
        pl.BlockSpec((L, D_MODEL), lambda c: (c, 2)),
        pl.BlockSpec((L, 128), lambda c: (c, 0)),
        pl.BlockSpec((L, 128), lambda c: (c, 1)),
        pl.BlockSpec((1, 128), const2),
        pl.BlockSpec((1, 128), const2),
        pl.BlockSpec((1, D_MODEL), const2),
        pl.BlockSpec((1, ML_NV), const2),
    ]
    out_shape = [
        jax.ShapeDtypeStruct((seq, ML_NV), BF16),
        jax.ShapeDtypeStruct((ML_HEADS, ML_DV, ML_DQK), F32),
        jax.ShapeDtypeStruct((ML_HEADS, ML_DQK), F32),
        jax.ShapeDtypeStruct((1, 128), F32),
    ]
    out_specs = [
        pl.BlockSpec((L, ML_NV), lambda c: (c, 0)),
        pl.BlockSpec((ML_HEADS, ML_DV, ML_DQK), lambda c: (0, 0, 0)),
        pl.BlockSpec((ML_HEADS, ML_DQK), const2),
        pl.BlockSpec((1, 128), const2),
    ]
    return pl.pallas_call(
        _mlstm_prompt_kernel, grid=(nc,), in_specs=in_specs, out_specs=out_specs, out_shape=out_shape,
        scratch_shapes=[pltpu.VMEM((ML_HEADS, ML_DQK, ML_DV), F32), pltpu.VMEM((ML_HEADS, ML_DQK), F32),
                        pltpu.VMEM((1, 128), F32)],
        compiler_params=_cparams(("arbitrary",)), name="mlstm_prompt",
    )(z, z, z, z, gates, gates, bi, bf, bo, ghn)


def _mlstm_sample_kernel(z_ref, g_ref, c_ref, n_ref, m_ref, bi_ref, bf_ref, bo_ref, ghn_ref,
                         h_ref, c_out_ref, n_out_ref, m_out_ref):
    i = pl.program_id(0)
    z = z_ref[0]
    g = g_ref[0]
    ig_all = g[:, 0:128] + bi_ref[...]
    lf_all = _log_sigmoid(g[:, 128:256] + bf_ref[...])
    m_all = m_ref[0]
    mt_all = jnp.maximum(lf_all + m_all, ig_all)
    m_out_ref[0] = mt_all
    outs = []
    for h in range(ML_HEADS):
        q = z[:, h * ML_DQK:(h + 1) * ML_DQK]
        k = z[:, ML_NQK + h * ML_DQK:ML_NQK + (h + 1) * ML_DQK]
        v = z[:, 2 * ML_NQK + h * ML_DV:2 * ML_NQK + (h + 1) * ML_DV]
        ig = ig_all[:, h:h + 1]
        lf = lf_all[:, h:h + 1]
        m0 = m_all[:, h:h + 1]
        mt = mt_all[:, h:h + 1]
        w_inter = jnp.exp(lf + m0 - mt)
        wj = jnp.exp(ig - mt)
        a = jnp.sum(q * k, axis=1, keepdims=True) * wj
        c_h = c_ref[0, h]
        n_h = n_ref[0, h:h + 1, :]
        q8 = jnp.broadcast_to(q, (8, ML_DQK))
        cq = lax.dot_general(q8, c_h, (((1,), (1,)), ((), ())), precision=HIGHEST,
                             preferred_element_type=F32)[0:1, :]
        num = a * v + w_inter * cq
        den = a + w_inter * jnp.sum(n_h * q, axis=1, keepdims=True)
        hh = num / jnp.maximum(jnp.abs(den), jnp.exp(-mt))
        hn = _rms(hh) * ghn_ref[:, h * ML_DV:(h + 1) * ML_DV]
        og = jax.nn.sigmoid(z[:, 2 * ML_NQK + ML_NV + h * ML_DV:2 * ML_NQK + ML_NV + (h + 1) * ML_DV]
                            + bo_ref[:, h * ML_DV:(h + 1) * ML_DV])
        outs.append(hn * og)
        v_col = jnp.broadcast_to(v, (8, ML_DV)).T[:, 0:1]
        c_out_ref[0, h] = w_inter * c_h + wj * (v_col * k)
        n_out_ref[0, h:h + 1, :] = w_inter * n_h + wj * k
    h_ref[pl.ds(i, 1), :] = jnp.concatenate(outs, axis=1)


def _mlstm_sample(z_s, gates_s, c0, n0, m0, bi, bf, bo, ghn):
    ns = z_s.shape[0]
    const2 = lambda i: (0, 0)
    in_specs = [
        pl.BlockSpec((1, 1, ML_MAIN), lambda i: (i, 0, 0)),
        pl.BlockSpec((1, 1, 256), lambda i: (i, 0, 0)),
        pl.BlockSpec((1, ML_HEADS, ML_DV, ML_DQK), lambda i: (i, 0, 0, 0)),
        pl.BlockSpec((1, ML_HEADS, ML_DQK), lambda i: (i, 0, 0)),
        pl.BlockSpec((1, 1, 128), lambda i: (i, 0, 0)),
        pl.BlockSpec((1, 128), const2),
        pl.BlockSpec((1, 128), const2),
        pl.BlockSpec((1, D_MODEL), const2),
        pl.BlockSpec((1, ML_NV), const2),
    ]
    out_shape = [
        jax.ShapeDtypeStruct((ns, ML_NV), F32),
        jax.ShapeDtypeStruct((ns, ML_HEADS, ML_DV, ML_DQK), F32),
        jax.ShapeDtypeStruct((ns, ML_HEADS, ML_DQK), F32),
        jax.ShapeDtypeStruct((ns, 1, 128), F32),
    ]
    out_specs = [
        pl.BlockSpec((ns, ML_NV), const2),
        pl.BlockSpec((1, ML_HEADS, ML_DV, ML_DQK), lambda i: (i, 0, 0, 0)),
        pl.BlockSpec((1, ML_HEADS, ML_DQK), lambda i: (i, 0, 0)),
        pl.BlockSpec((1, 1, 128), lambda i: (i, 0, 0)),
    ]
    return pl.pallas_call(
        _mlstm_sample_kernel, grid=(ns,), in_specs=in_specs, out_specs=out_specs, out_shape=out_shape,
        compiler_params=_cparams(("arbitrary",)), name="mlstm_sample",
    )(z_s, gates_s, c0, n0, m0, bi, bf, bo, ghn)


def _attn_prompt_kernel(q_ref, kp_ref, kc_ref, vp_ref, vc_ref, o_ref, lse_ref):
    blk = pl.program_id(1)
    T = ATT_STEPS
    qi = lax.broadcasted_iota(I32, (T, T), 0)
    kj = lax.broadcasted_iota(I32, (T, T), 1)
    valid_prev = jnp.logical_and(kj >= qi, blk > 0)
    valid_cur = kj <= qi
    scale = ATT_HD ** -0.5
    nt = (((1,), (1,)), ((), ()))
    for h in range(ATT_H):
        sl = slice(h * ATT_HD, (h + 1) * ATT_HD)
        qh = q_ref[:, sl]
        s1 = jnp.where(valid_prev, lax.dot_general(qh, kp_ref[:, sl], nt, preferred_element_type=F32) * scale,
                       -jnp.inf)
        s2 = jnp.where(valid_cur, lax.dot_general(qh, kc_ref[:, sl], nt, preferred_element_type=F32) * scale,
                       -jnp.inf)
        mx = jnp.maximum(jnp.max(s1, axis=1, keepdims=True), jnp.max(s2, axis=1, keepdims=True))
        p1 = jnp.exp(s1 - mx)
        p2 = jnp.exp(s2 - mx)
        den = jnp.sum(p1, axis=1, keepdims=True) + jnp.sum(p2, axis=1, keepdims=True)
        acc = jnp.dot(p1.astype(BF16), vp_ref[:, sl], preferred_element_type=F32)
        acc = acc + jnp.dot(p2.astype(BF16), vc_ref[:, sl], preferred_element_type=F32)
        o_ref[:, sl] = acc / den
        lse_ref[:, sl] = jnp.broadcast_to(mx + jnp.log(den), (T, ATT_HD))


def _attn_prompt(q, kv, g):
    dil, L, _ = q.shape
    nb = L // ATT_STEPS
    T = ATT_STEPS
    blk = (None, T, ATT_GW)
    in_specs = [
        pl.BlockSpec(blk, lambda r, b: (r, b, 0)),
        pl.BlockSpec(blk, lambda r, b: (r, jnp.maximum(b - 1, 0), 0)),
        pl.BlockSpec(blk, lambda r, b: (r, b, 0)),
        pl.BlockSpec(blk, lambda r, b: (r, jnp.maximum(b - 1, 0), 1)),
        pl.BlockSpec(blk, lambda r, b: (r, b, 1)),
    ]
    out_spec = pl.BlockSpec(blk, lambda r, b: (r, b, 0))
    return pl.pallas_call(
        _attn_prompt_kernel, grid=(dil, nb), in_specs=in_specs, out_specs=[out_spec, out_spec],
        out_shape=[jax.ShapeDtypeStruct((dil, L, ATT_GW), F32)] * 2,
        compiler_params=_cparams(("arbitrary", "arbitrary")), name=f"attn_prompt_g{g}",
    )(q, kv, kv, kv, kv)


def _merge_kernel(*refs):
    in_refs, out_ref, scratch = refs[:2 * ATT_G], refs[2 * ATT_G], refs[2 * ATT_G + 1:]
    tm = out_ref.shape[0]
    for c in range(ATT_H):
        lanes = slice(c * ATT_HD, (c + 1) * ATT_HD)
        vals = []
        si = 0
        for g in range(ATT_G):
            dil = ATT_DILS[g]
            pair = []
            for ref in in_refs[2 * g:2 * g + 2]:
                if dil == 1:
                    pair.append(ref[0, :, lanes])
                else:
                    buf = scratch[si]
                    si += 1
                    for r in range(dil):
                        buf[pl.ds(r, tm // dil, stride=dil), :] = ref[r, :, lanes]
                    pair.append(buf[...])
            vals.append(pair)
        lses = [p[1] for p in vals]
        mx = jnp.maximum(jnp.maximum(lses[0], lses[1]), lses[2])
        es = [jnp.exp(l - mx) for l in lses]
        tot = es[0] + es[1] + es[2]
        out_ref[:, lanes] = ((es[0] / tot) * vals[0][0] + (es[1] / tot) * vals[1][0]
                             + (es[2] / tot) * vals[2][0]).astype(out_ref.dtype)


def _merge_groups(parts, seq, tm=512):
    in_specs, args, scratch = [], [], []
    for g, pair in enumerate(parts):
        dil = ATT_DILS[g]
        for a in pair:
            in_specs.append(pl.BlockSpec((dil, tm // dil, ATT_GW), lambda i: (0, i, 0)))
            args.append(a)
            if dil > 1:
                scratch.append(pltpu.VMEM((tm, ATT_HD), F32))
    return pl.pallas_call(
        _merge_kernel, grid=(seq // tm,), in_specs=in_specs,
        out_specs=pl.BlockSpec((tm, ATT_GW), lambda i: (i, 0)),
        out_shape=jax.ShapeDtypeStruct((seq, ATT_GW), BF16), scratch_shapes=scratch,
        compiler_params=_cparams(("arbitrary",)), name="attn_merge",
    )(*args)


def _attn_sample_kernel(q_ref, kvn_ref, b0_ref, b1_ref, b2_ref, out_ref):
    scale = ATT_HD ** -0.5
    outs, lses = [], []
    for g, b_ref in enumerate((b0_ref, b1_ref, b2_ref)):
        qg = q_ref[0, g]
        kn = kvn_ref[0, g, 0]
        vn = kvn_ref[0, g, 1]
        kb = b_ref[:, 0]
        vb = b_ref[:, 1]
        s = jnp.sum(kb * qg[None], axis=2, keepdims=True) * scale
        s_new = jnp.sum(kn * qg, axis=1, keepdims=True) * scale
        mx = jnp.maximum(jnp.max(s, axis=0), s_new)
        p = jnp.exp(s - mx[None])
        p_new = jnp.exp(s_new - mx)
        den = jnp.sum(p, axis=0) + p_new
        o = jnp.sum(p * vb, axis=0) + p_new * vn
        outs.append(o / den)
        lses.append(mx + jnp.log(den))
    mxl = jnp.maximum(jnp.maximum(lses[0], lses[1]), lses[2])
    es = [jnp.exp(l - mxl) for l in lses]
    tot = es[0] + es[1] + es[2]
    out_ref[0] = (es[0] / tot) * outs[0] + (es[1] / tot) * outs[1] + (es[2] / tot) * outs[2]


def _attn_sample(q_s, kv_s, caches):
    ns = q_s.shape[0]
    views, specs = [], []
    for g, cbuf in enumerate(caches):
        lb = cbuf.shape[1]
        dil = ATT_DILS[g]
        views.append(cbuf.reshape(ns, lb // dil, dil, 2, ATT_H, ATT_HD))
        specs.append(pl.BlockSpec((None, ATT_STEPS, None, 2, ATT_H, ATT_HD), lambda i: (i, 0, 0, 0, 0, 0)))
    in_specs = [
        pl.BlockSpec((1, ATT_G, ATT_H, ATT_HD), lambda i: (i, 0, 0, 0)),
        pl.BlockSpec((1, ATT_G, 2, ATT_H, ATT_HD), lambda i: (i, 0, 0, 0, 0)),
    ] + specs
    return pl.pallas_call(
        _attn_sample_kernel, grid=(ns,), in_specs=in_specs,
        out_specs=pl.BlockSpec((1, ATT_H, ATT_HD), lambda i: (i, 0, 0)),
        out_shape=jax.ShapeDtypeStruct((ns, ATT_H, ATT_HD), F32),
        compiler_params=_cparams(("arbitrary",)), name="attn_sample",
    )(q_s, kv_s, *views)


KV_SHIFT_CHUNKS = 4


def _kv_shift_kernel(*refs):
    caches, news, outs, sems = refs[:ATT_G], refs[ATT_G:2 * ATT_G], refs[2 * ATT_G:3 * ATT_G], refs[3 * ATT_G]
    copies = []

    def add(src, dst):
        copies.append(pltpu.make_async_copy(src, dst, sems.at[len(copies)]))

    for cbuf, new, out in zip(caches, news, outs):
        ns, lb = cbuf.shape[0], cbuf.shape[1]
        step = ns // KV_SHIFT_CHUNKS
        for c in range(KV_SHIFT_CHUNKS):
            rows = pl.ds(c * step, step)
            add(cbuf.at[rows, pl.ds(1, lb - 1)], out.at[rows, pl.ds(0, lb - 1)])
        add(new, out.at[:, pl.ds(lb - 1, 1)])
    for cp in copies:
        cp.start()
    for cp in copies:
        cp.wait()


def _kv_shift(caches, news):
    any_spec = pl.BlockSpec(memory_space=pl.ANY)
    return pl.pallas_call(
        _kv_shift_kernel, in_specs=[any_spec] * (2 * ATT_G), out_specs=[any_spec] * ATT_G,
        out_shape=[jax.ShapeDtypeStruct(c.shape, c.dtype) for c in caches],
        scratch_shapes=[pltpu.SemaphoreType.DMA((ATT_G * (KV_SHIFT_CHUNKS + 1),))], name="kv_shift",
    )(*caches, *news)


def _top2(y, wr_ref, br_ref):
    rows = y.shape[0]
    lane = lax.broadcasted_iota(I32, (rows, 128), 1)
    logits = jnp.dot(y, wr_ref[...], precision=HIGHEST, preferred_element_type=F32) + br_ref[...]
    logits = jnp.where(lane < N_EXPERTS, logits, NEG_BIG)
    e = jnp.exp(logits - jnp.max(logits, axis=1, keepdims=True))
    probs = e / jnp.sum(e, axis=1, keepdims=True)
    p1 = jnp.max(probs, axis=1, keepdims=True)
    i1 = jnp.min(jnp.where(probs == p1, lane, 128), axis=1, keepdims=True)
    probs2 = jnp.where(lane == i1, -1.0, probs)
    p2 = jnp.max(probs2, axis=1, keepdims=True)
    i2 = jnp.min(jnp.where(probs2 == p2, lane, 128), axis=1, keepdims=True)
    tot = p1 + p2
    return lane, i1, i2, p1 / tot, p2 / tot


def _router_kernel(x_ref, xs_ref, g_ref, wr_ref, br_ref, xn_ref, eid_ref, gate_ref, rank_ref, cnt_ref,
                   xns_ref, gs_ref, carry):
    i = pl.program_id(0)
    tm = x_ref.shape[0]

    @pl.when(i == 0)
    def _():
        carry[...] = jnp.zeros_like(carry)
        ys = _rms(xs_ref[...]) * g_ref[...]
        xns_ref[...] = ys
        lane, i1, i2, g1, g2 = _top2(ys, wr_ref, br_ref)
        gs_ref[...] = jnp.where(lane == i1, g1, jnp.where(lane == i2, g2, 0.0))

    y = _rms(x_ref[...]) * g_ref[...]
    xn_ref[...] = y
    lane, i1, i2, g1, g2 = _top2(y, wr_ref, br_ref)
    sel1 = lane == i1
    sel2 = lane == i2
    onehot = jnp.where(jnp.logical_or(sel1, sel2), 1.0, 0.0)
    rr = lax.broadcasted_iota(I32, (tm, tm), 0)
    cc = lax.broadcasted_iota(I32, (tm, tm), 1)
    before = (cc < rr).astype(BF16)
    prefix = jnp.dot(before, onehot.astype(BF16), preferred_element_type=F32) + carry[...]
    r1 = jnp.sum(jnp.where(sel1, prefix, 0.0), axis=1, keepdims=True)
    r2 = jnp.sum(jnp.where(sel2, prefix, 0.0), axis=1, keepdims=True)
    carry[...] = carry[...] + jnp.sum(onehot, axis=0, keepdims=True)
    eid_ref[...] = jnp.where(lane == 0, i1, jnp.where(lane == 1, i2, 0))
    gate_ref[...] = jnp.where(lane == 0, g1, jnp.where(lane == 1, g2, 0.0))
    rank_ref[...] = jnp.where(lane == 0, r1, jnp.where(lane == 1, r2, 0.0)).astype(I32)
    cnt_ref[...] = jnp.broadcast_to(carry[...], cnt_ref.shape)


def _router(h, hs, gain, w_router_pad, b_router_pad, tm=ROW_TILE):
    m, d = h.shape
    s = hs.shape[0]
    const = lambda i: (0, 0)
    row_spec = pl.BlockSpec((tm, 128), lambda i: (i, 0))
    return pl.pallas_call(
        _router_kernel, grid=(m // tm,),
        in_specs=[pl.BlockSpec((tm, d), lambda i: (i, 0)), pl.BlockSpec((s, d), const), pl.BlockSpec((1, d), const),
                  pl.BlockSpec((d, 128), const), pl.BlockSpec((1, 128), const)],
        out_specs=[pl.BlockSpec((tm, d), lambda i: (i, 0)), row_spec, row_spec, row_spec,
                   pl.BlockSpec((8, 128), const), pl.BlockSpec((s, d), const), pl.BlockSpec((s, 128), const)],
        out_shape=[jax.ShapeDtypeStruct((m, d), F32), jax.ShapeDtypeStruct((m, 128), I32),
                   jax.ShapeDtypeStruct((m, 128), F32), jax.ShapeDtypeStruct((m, 128), I32),
                   jax.ShapeDtypeStruct((8, 128), F32), jax.ShapeDtypeStruct((s, d), F32),
                   jax.ShapeDtypeStruct((s, 128), F32)],
        scratch_shapes=[pltpu.VMEM((1, 128), F32)],
        compiler_params=_cparams(("arbitrary",)), name="router",
    )(h, hs, gain, w_router_pad, b_router_pad)


def _dispatch_kernel(pos_ref, nv_ref, x_hbm, out_ref, inv, buf, sem, *, n_tok):
    i = pl.program_id(0)
    tg = out_ref.shape[0]

    @pl.when(i == 0)
    def _():
        def clear(s, c):
            inv[s] = 0
            return c

        lax.fori_loop(0, inv.shape[0], clear, 0)

        def fill(t, c):
            inv[pos_ref[2 * t]] = t
            inv[pos_ref[2 * t + 1]] = t
            return c

        lax.fori_loop(0, n_tok, fill, 0)

    @pl.when(i < nv_ref[0])
    def _():
        def issue(r, c):
            pltpu.make_async_copy(x_hbm.at[pl.ds(inv[i * tg + r], 1)], buf.at[pl.ds(r, 1)], sem).start()
            return c

        lax.fori_loop(0, tg, issue, 0)
        pltpu.make_async_copy(x_hbm.at[pl.ds(0, tg)], buf, sem).wait()
        out_ref[...] = buf[...].astype(out_ref.dtype)

    @pl.when(i >= nv_ref[0])
    def _():
        out_ref[...] = jnp.zeros_like(out_ref)


def _dispatch(pos_flat, nvalid, xn, n_tiles, tg):
    n_tok, d = xn.shape
    return pl.pallas_call(
        functools.partial(_dispatch_kernel, n_tok=n_tok),
        grid_spec=pltpu.PrefetchScalarGridSpec(
            num_scalar_prefetch=2, grid=(n_tiles,),
            in_specs=[pl.BlockSpec(memory_space=pl.ANY)],
            out_specs=pl.BlockSpec((tg, d), lambda i, pos, nv: (i, 0)),
            scratch_shapes=[pltpu.SMEM((n_tiles * tg,), I32), pltpu.VMEM((tg, d), xn.dtype),
                            pltpu.SemaphoreType.DMA]),
        out_shape=jax.ShapeDtypeStruct((n_tiles * tg, d), BF16),
        compiler_params=_cparams(("arbitrary",)), name="moe_dispatch",
    )(pos_flat, nvalid, xn)


def _combine_kernel(pos_ref, h_ref, gate_ref, g_ref, hs_ref, gs_ref, ys_s_ref, ys_hbm, out_ref, outs_ref, ybuf, sem):
    i = pl.program_id(0)
    tm = h_ref.shape[0]
    base = i * tm

    def issue(r, carry):
        for j in range(2):
            pltpu.make_async_copy(ys_hbm.at[pl.ds(pos_ref[2 * (base + r) + j], 1)],
                                  ybuf.at[j, pl.ds(r, 1)], sem).start()
        return carry

    lax.fori_loop(0, tm, issue, 0)

    @pl.when(i == 0)
    def _():
        gs = gs_ref[...]
        y = jnp.zeros(hs_ref.shape, F32)
        for e in range(N_EXPERTS):
            y = y + gs[:, e:e + 1] * ys_s_ref[e]
        outs_ref[...] = _rms(hs_ref[...] + y) * g_ref[...]

    for j in range(2):
        pltpu.make_async_copy(ys_hbm.at[pl.ds(0, tm)], ybuf.at[j], sem).wait()
    gate = gate_ref[...]
    y = h_ref[...] + (gate[:, 0:1] * ybuf[0] + gate[:, 1:2] * ybuf[1])
    out_ref[...] = _rms(y) * g_ref[...]


def _combine(pos_flat, h, gate, g_final, hs, gs, ys_s, ys, tm=ROW_TILE):
    m, d = h.shape
    s = hs.shape[0]
    c2 = lambda i, pos: (0, 0)
    return pl.pallas_call(
        _combine_kernel,
        grid_spec=pltpu.PrefetchScalarGridSpec(
            num_scalar_prefetch=1, grid=(m // tm,),
            in_specs=[pl.BlockSpec((tm, d), lambda i, pos: (i, 0)),
                      pl.BlockSpec((tm, 128), lambda i, pos: (i, 0)),
                      pl.BlockSpec((1, d), c2),
                      pl.BlockSpec((s, d), c2),
                      pl.BlockSpec((s, 128), c2),
                      pl.BlockSpec((N_EXPERTS, s, d), lambda i, pos: (0, 0, 0)),
                      pl.BlockSpec(memory_space=pl.ANY)],
            out_specs=[pl.BlockSpec((tm, d), lambda i, pos: (i, 0)), pl.BlockSpec((s, d), c2)],
            scratch_shapes=[pltpu.VMEM((2, tm, d), F32), pltpu.SemaphoreType.DMA]),
        out_shape=[jax.ShapeDtypeStruct((m, d), F32), jax.ShapeDtypeStruct((s, d), F32)],
        compiler_params=_cparams(("arbitrary",)), name="moe_combine",
    )(pos_flat, h, gate, g_final, hs, gs, ys_s, ys)


def _rope_tables(pos):
    half = ATT_HD // 2
    inv = ROPE_THETA ** (-jnp.arange(half, dtype=F32) / half)
    ang = pos.astype(F32)[:, None] * inv[None, :]
    cos, sin = jnp.cos(ang), jnp.sin(ang)
    return jnp.concatenate([cos, cos], axis=1), jnp.concatenate([-sin, sin], axis=1)


def kernel(x_prompt, x_sample, state_mlstm_C, state_mlstm_n, state_mlstm_m, cache_kv_w128, cache_kv_w512, cache_kv_w2048, g_mix, g_ffn, w_ml_in, b_ml_gates, b_ml_o, g_ml_hnorm, w_ml_out, g_kv, w_kv, w_q, w_o, w_ffn_gate, w_ffn_up, w_ffn_down, w_router, b_router, w_exp_gate, w_exp_up, w_exp_down, g_final):
    bp, seq, d = x_prompt.shape
    ns = x_sample.shape[0]
    caches = (cache_kv_w128, cache_kv_w512, cache_kv_w2048)
    assert bp == 1 and x_sample.shape[1] == 1 and d == D_MODEL and ns % 8 == 0
    assert seq % (ATT_STEPS * max(ATT_DILS)) == 0 and seq % 1024 == 0
    assert all(c.shape[1] == w for c, w in zip(caches, ATT_WINDOWS))
    tm = 512

    h0 = x_prompt[0]
    h0_s = x_sample[:, 0]

    w_gates = w_ml_in[0][:, ML_MAIN:]
    w_gates_pad = jnp.zeros((d, 256), F32).at[:, 0:ML_HEADS].set(w_gates[:, :ML_HEADS])
    w_gates_pad = w_gates_pad.at[:, 128:128 + ML_HEADS].set(w_gates[:, ML_HEADS:])
    bi = jnp.zeros((1, 128), F32).at[0, :ML_HEADS].set(b_ml_gates[0, :ML_HEADS])
    bf = jnp.zeros((1, 128), F32).at[0, :ML_HEADS].set(b_ml_gates[0, ML_HEADS:])
    (xn0, gates), (xn0_s, gates_s) = _rmsnorm(h0, h0_s, g_mix[0:1], proj=w_gates_pad)
    k_scale = jnp.concatenate([jnp.ones((1, ML_NQK), F32), jnp.full((1, ML_NQK), ML_DQK ** -0.5, F32),
                               jnp.ones((1, ML_NV + D_MODEL), F32)], axis=1)
    z, z_s = _matmul(xn0, [w_ml_in], ML_MAIN, BF16, xn0_s[None], tm=tm, tn=1024, col_scale=k_scale, name="ml_in")
    bo = b_ml_o[0:1]
    ghn = g_ml_hnorm[0:1]
    hg, p_c, p_n, p_m = _mlstm_prompt(z[0], gates, bi, bf, bo, ghn)
    m0 = jnp.zeros((ns, 1, 128), F32).at[:, 0, :ML_HEADS].set(state_mlstm_m[0])
    hg_s, s_c, s_n, s_m = _mlstm_sample(z_s[0].reshape(ns, 1, ML_MAIN), gates_s.reshape(ns, 1, 256),
                                        state_mlstm_C[0], state_mlstm_n[0], m0, bi, bf, bo, ghn)
    h1, h1_s = _matmul(hg, [w_ml_out], d, F32, hg_s[None], tm=tm, tn=1024, res=h0, res_s=h0_s, name="ml_out")
    h1, h1_s = h1[0], h1_s[0]

    (xf0,), (xf0_s,) = _rmsnorm(h1, h1_s, g_ffn[0:1])
    ffn_dense = w_ffn_gate.shape[2]
    hid, hid_s = _matmul(xf0, [w_ffn_gate, w_ffn_up], ffn_dense, BF16, xf0_s[None], tm=tm, tn=512, name="ffn_up")
    h2, h2_s = _matmul(hid[0], [w_ffn_down], d, F32, hid_s, tm=ROW_TILE, tn=512, res=h1, res_s=h1_s,
                       name="ffn_down")
    h2, h2_s = h2[0], h2_s[0]

    (xq, xkv), (xq_s, xkv_s) = _rmsnorm(h2, h2_s, jnp.stack([g_mix[1], g_kv]))
    cos, sin = _rope_tables(jnp.arange(seq))
    cos_s, sin_s = _rope_tables(jnp.full((ns,), PAST_LEN, I32))
    rope_args = dict(cos=cos, sin=sin, cos_s=cos_s, sin_s=sin_s)
    parts, kv_nat, kv_new, q_new = [], [], [], []
    for g in range(ATT_G):
        dil = ATT_DILS[g]
        kvd, kvn, kv_s = _matmul(xkv, [w_kv[None]], 2 * ATT_GW, BF16, xkv_s[None], tm=tm, tn=ATT_GW,
                                 col_off=2 * g, rope="even", dil=dil, natural=True, name=f"kv_proj_g{g}",
                                 **rope_args)
        qd, q_s = _matmul(xq, [w_q], ATT_GW, BF16, xq_s[None], tm=tm, tn=ATT_GW, col_off=g, rope="all",
                          dil=dil, name=f"q_proj_g{g}", **rope_args)
        parts.append(_attn_prompt(qd, kvd, g))
        kv_nat.append(kvn)
        kv_new.append(kv_s[0])
        q_new.append(q_s[0])
    att = _merge_groups(parts, seq)
    att_s = _attn_sample(jnp.stack(q_new, axis=1).reshape(ns, ATT_G, ATT_H, ATT_HD),
                         jnp.stack(kv_new, axis=1).reshape(ns, ATT_G, 2, ATT_H, ATT_HD), caches)
    h3, h3_s = _matmul(att, [w_o], d, F32, att_s.reshape(1, ns, ATT_GW), tm=tm, tn=1024, res=h2, res_s=h2_s,
                       name="attn_out")
    h3, h3_s = h3[0], h3_s[0]

    wr_pad = jnp.zeros((d, 128), F32).at[:, :N_EXPERTS].set(w_router[0])
    br_pad = jnp.zeros((1, 128), F32).at[0, :N_EXPERTS].set(b_router[0])
    xn2, eid, gate, rank, cnt, xn2_s, gates_moe_s = _router(h3, h3_s, g_ffn[1:2], wr_pad, br_pad)
    tg = MOE_TILE
    n_tiles = -(-(2 * seq + N_EXPERTS * (tg - 1)) // tg)
    counts = cnt[0, :N_EXPERTS].astype(I32)
    padded = jnp.maximum((counts + tg - 1) // tg, 1) * tg
    gend = jnp.cumsum(padded)
    gstart = gend - padded
    pos = (gstart[eid[:, :2]] + rank[:, :2]).astype(I32).reshape(-1)
    nvalid = (gend[-1] // tg).astype(I32)
    tile_start = jnp.arange(n_tiles, dtype=I32) * tg
    te = jnp.minimum(jnp.sum((tile_start[:, None] >= gend[None, :]).astype(I32), axis=1), N_EXPERTS - 1)
    te = te.astype(I32)
    nvalid = nvalid.reshape(1)
    xs = _dispatch(pos, nvalid, xn2, n_tiles, tg)
    ffn_e = w_exp_gate.shape[3]
    hs, hs_s = _matmul(xs, [w_exp_gate[0], w_exp_up[0]], ffn_e, BF16, xn2_s[None], tm=tg, tn=1024, te=te,
                       nvalid=nvalid, name="moe_up")
    ys, ys_s = _matmul(hs[0], [w_exp_down[0]], d, F32, hs_s, tm=tg, tn=512, te=te, nvalid=nvalid,
                       xs_per_expert=True, name="moe_down")
    y_p, y_s = _combine(pos, h3, gate, g_final.reshape(1, d), h3_s, gates_moe_s, ys_s, ys[0])

    p_bufs = []
    news = []
    for g in range(ATT_G):
        keep = min(ATT_WINDOWS[g], seq)
        p_bufs.append(kv_nat[g][seq - keep:].reshape(1, keep, 2, ATT_H, ATT_HD))
        news.append(kv_new[g].reshape(ns, 1, 2, ATT_H, ATT_HD))
    s_bufs = _kv_shift(caches, news)
    return (y_p.reshape(1, seq, d), y_s.reshape(ns, 1, d),
            p_c[None, None], p_n[None, None], p_m[:, :ML_HEADS][None],
            s_c[None], s_n[None], s_m[:, 0, :ML_HEADS][None],
            p_bufs[0], p_bufs[1], p_bufs[2], s_bufs[0], s_bufs[1], s_bufs[2])
```

```python
import functools

import jax
import jax.numpy as jnp
from jax import lax
from jax.experimental import pallas as pl
from jax.experimental.pallas import tpu as pltpu

F32 = jnp.float32
BF16 = jnp.bfloat16
I32 = jnp.int32
HIGHEST = lax.Precision.HIGHEST

D_MODEL = 2048
ML_HEADS = 8
ML_DQK = 128
ML_DV = 256
ML_NQK = ML_HEADS * ML_DQK
ML_NV = ML_HEADS * ML_DV
ML_MAIN = 2 * ML_NQK + ML_NV + D_MODEL
ML_CHUNK = 128
ATT_HD = 128
ATT_H = 8
ATT_G = 3
ATT_WINDOWS = (128, 512, 2048)
ATT_DILS = (1, 4, 16)
ATT_STEPS = 128
ATT_GW = ATT_H * ATT_HD
ROPE_THETA = 10000.0
PAST_LEN = 8192
N_EXPERTS = 8
RMS_EPS = 1e-6
NEG_BIG = -1e30

VMEM_LIMIT_BYTES = 58 * 1024 * 1024
ROW_TILE = 256
MOE_TILE = 256


def _cparams(sem):
    return pltpu.CompilerParams(dimension_semantics=sem, vmem_limit_bytes=VMEM_LIMIT_BYTES)


def _rms(x):
    return x * lax.rsqrt(jnp.mean(x * x, axis=-1, keepdims=True) + RMS_EPS)


def _norm_kernel(x_ref, xs_ref, g_ref, *refs, n_out, has_proj):
    ins = 1 if has_proj else 0
    n_each = n_out + ins
    main = refs[ins:ins + n_each]
    side = refs[ins + n_each:ins + 2 * n_each]

    def emit(x, outs):
        y = _rms(x)
        for i in range(n_out):
            outs[i][...] = (y * g_ref[i:i + 1, :]).astype(outs[i].dtype)
        if has_proj:
            outs[n_out][...] = jnp.dot(y * g_ref[0:1, :], refs[0][...], precision=HIGHEST,
                                       preferred_element_type=F32)

    emit(x_ref[...], main)

    @pl.when(pl.program_id(0) == 0)
    def _():
        emit(xs_ref[...], side)


def _rmsnorm(x, xs, gains, proj=None, tm=ROW_TILE):
    m, d = x.shape
    s = xs.shape[0]
    n_out = gains.shape[0]
    const = lambda i: (0, 0)
    in_specs = [pl.BlockSpec((tm, d), lambda i: (i, 0)), pl.BlockSpec((s, d), const),
                pl.BlockSpec((n_out, d), const)]
    args = [x, xs, gains]
    main_shape = [jax.ShapeDtypeStruct((m, d), BF16)] * n_out
    main_specs = [pl.BlockSpec((tm, d), lambda i: (i, 0))] * n_out
    side_shape = [jax.ShapeDtypeStruct((s, d), F32)] * n_out
    side_specs = [pl.BlockSpec((s, d), const)] * n_out
    if proj is not None:
        p = proj.shape[1]
        in_specs.append(pl.BlockSpec((d, p), const))
        args.append(proj)
        main_shape.append(jax.ShapeDtypeStruct((m, p), F32))
        main_specs.append(pl.BlockSpec((tm, p), lambda i: (i, 0)))
        side_shape.append(jax.ShapeDtypeStruct((s, p), F32))
        side_specs.append(pl.BlockSpec((s, p), const))
    outs = pl.pallas_call(
        functools.partial(_norm_kernel, n_out=n_out, has_proj=proj is not None),
        grid=(m // tm,), in_specs=in_specs, out_specs=main_specs + side_specs,
        out_shape=main_shape + side_shape,
        compiler_params=_cparams(("arbitrary",)), name="rmsnorm",
    )(*args)
    k = len(main_shape)
    return outs[:k], outs[k:]


def _rope_heads(acc, cos, sin):
    outs = []
    for h in range(acc.shape[1] // ATT_HD):
        a = acc[:, h * ATT_HD:(h + 1) * ATT_HD]
        outs.append(a * cos + pltpu.roll(a, ATT_HD // 2, 1) * sin)
    return jnp.concatenate(outs, axis=1)


def _mm_kernel(te_ref, nv_ref, x_ref, *refs, n_w, has_scale, has_res, rope, dil, has_nat, side_precise):
    n = pl.program_id(0)
    m = pl.program_id(1)
    it = iter(refs)
    w_refs = [next(it) for _ in range(n_w)]
    scale_ref = next(it) if has_scale else None
    res_ref = next(it) if has_res else None
    cos_ref, sin_ref = (next(it), next(it)) if rope else (None, None)
    xs_ref = next(it)
    res_s_ref = next(it) if has_res else None
    cos_s_ref, sin_s_ref = (next(it), next(it)) if rope else (None, None)
    o_ref = next(it)
    nat_ref = next(it) if has_nat else None
    os_ref = next(it)
    wb_refs = [next(it) for _ in range(n_w)]
    deint = next(it) if dil > 1 else None

    def finish(acc, up, res, cos, sin, store):
        if n_w == 2:
            acc = (acc * jax.nn.sigmoid(acc)) * up
        if has_scale:
            acc = acc * scale_ref[...]
        if has_res:
            acc = acc + res
        if rope == "all":
            store(_rope_heads(acc, cos, sin))
        elif rope == "even":
            @pl.when(n % 2 == 0)
            def _():
                store(_rope_heads(acc, cos, sin))

            @pl.when(n % 2 == 1)
            def _():
                store(acc)
        else:
            store(acc)

    def store_side(val):
        os_ref[0] = val

    def store_main(val):
        if has_nat:
            nat_ref[...] = val
        if dil == 1:
            o_ref[0] = val.astype(o_ref.dtype)
        else:
            rows = deint.shape[1] // dil
            for c in range(deint.shape[0]):
                lanes = slice(c * 128, (c + 1) * 128)
                deint[c] = val[:, lanes]
                for r in range(dil):
                    o_ref[r, :, lanes] = deint[c, pl.ds(r, rows, stride=dil), :].astype(o_ref.dtype)

    prev = jnp.maximum(m - 1, 0)
    new_weights = jnp.logical_or(m == 0, te_ref[m] != te_ref[prev])

    @pl.when(new_weights)
    def _():
        for w_ref, wb_ref in zip(w_refs, wb_refs):
            wb_ref[...] = w_ref[0].astype(BF16)
        if side_precise:
            xs = xs_ref[0]
            acc = jnp.dot(xs, w_refs[0][0], precision=HIGHEST, preferred_element_type=F32)
            up = jnp.dot(xs, w_refs[1][0], precision=HIGHEST, preferred_element_type=F32) if n_w == 2 else None
        else:
            xs = xs_ref[0].astype(BF16)
            acc = jnp.dot(xs, wb_refs[0][...], preferred_element_type=F32)
            up = jnp.dot(xs, wb_refs[1][...], preferred_element_type=F32) if n_w == 2 else None
        finish(acc, up, res_s_ref[...] if has_res else None,
               cos_s_ref[...] if rope else None, sin_s_ref[...] if rope else None, store_side)

    @pl.when(m >= nv_ref[0])
    def _():
        o_ref[...] = jnp.zeros_like(o_ref)

    @pl.when(m < nv_ref[0])
    def _():
        xb = x_ref[...].astype(BF16)
        acc = jnp.dot(xb, wb_refs[0][...], preferred_element_type=F32)
        up = jnp.dot(xb, wb_refs[1][...], preferred_element_type=F32) if n_w == 2 else None
        finish(acc, up, res_ref[...] if has_res else None,
               cos_ref[...] if rope else None, sin_ref[...] if rope else None, store_main)


def _matmul(x, ws, n_cols, out_dtype, xs, *, tm, tn, col_off=0, te=None, nvalid=None, xs_per_expert=False,
            col_scale=None, res=None, res_s=None, rope=None, cos=None, sin=None, cos_s=None, sin_s=None,
            dil=1, natural=False, side_precise=True, name="matmul"):
    m, k = x.shape
    s = xs.shape[1]
    n_m = m // tm
    n_n = n_cols // tn
    n_e = ws[0].shape[0]
    if te is None:
        te = jnp.zeros((n_m,), I32)
        nvalid = jnp.full((1,), n_m, I32)

    def row(mi, nv):
        return jnp.minimum(mi, nv[0] - 1)

    def exp(mi, te, nv):
        return te[row(mi, nv)]

    in_specs = [pl.BlockSpec((tm, k), lambda n, mi, te, nv: (row(mi, nv), 0))]
    args = [x]
    for w in ws:
        in_specs.append(pl.BlockSpec((1, k, tn), lambda n, mi, te, nv: (exp(mi, te, nv), 0, n + col_off)))
        args.append(w)
    if col_scale is not None:
        in_specs.append(pl.BlockSpec((1, tn), lambda n, mi, te, nv: (0, n)))
        args.append(col_scale)
    if res is not None:
        in_specs.append(pl.BlockSpec((tm, tn), lambda n, mi, te, nv: (row(mi, nv), n)))
        args.append(res)
    if rope is not None:
        for t in (cos, sin):
            in_specs.append(pl.BlockSpec((tm, ATT_HD), lambda n, mi, te, nv: (row(mi, nv), 0)))
            args.append(t)
    if xs_per_expert:
        in_specs.append(pl.BlockSpec((1, s, k), lambda n, mi, te, nv: (exp(mi, te, nv), 0, 0)))
    else:
        in_specs.append(pl.BlockSpec((1, s, k), lambda n, mi, te, nv: (0, 0, 0)))
    args.append(xs)
    if res is not None:
        in_specs.append(pl.BlockSpec((s, tn), lambda n, mi, te, nv: (0, n)))
        args.append(res_s)
    if rope is not None:
        for t in (cos_s, sin_s):
            in_specs.append(pl.BlockSpec((s, ATT_HD), lambda n, mi, te, nv: (0, 0)))
            args.append(t)

    out_shape = [jax.ShapeDtypeStruct((dil, m // dil, n_cols), out_dtype)]
    out_specs = [pl.BlockSpec((dil, tm // dil, tn), lambda n, mi, te, nv: (0, mi, n))]
    if natural:
        out_shape.append(jax.ShapeDtypeStruct((m, n_cols), F32))
        out_specs.append(pl.BlockSpec((tm, tn), lambda n, mi, te, nv: (mi, n)))
    out_shape.append(jax.ShapeDtypeStruct((n_e, s, n_cols), F32))
    out_specs.append(pl.BlockSpec((1, s, tn), lambda n, mi, te, nv: (exp(mi, te, nv), 0, n)))
    scratch = [pltpu.VMEM((k, tn), BF16) for _ in ws]
    if dil > 1:
        scratch.append(pltpu.VMEM((tn // 128, tm, 128), F32))
    kern = functools.partial(_mm_kernel, n_w=len(ws), has_scale=col_scale is not None, has_res=res is not None,
                             rope=rope, dil=dil, has_nat=natural, side_precise=side_precise)
    return pl.pallas_call(
        kern,
        grid_spec=pltpu.PrefetchScalarGridSpec(
            num_scalar_prefetch=2, grid=(n_n, n_m), in_specs=in_specs, out_specs=out_specs,
            scratch_shapes=scratch),
        out_shape=out_shape,
        compiler_params=_cparams(("arbitrary", "arbitrary")), name=name,
    )(te, nvalid, *args)


def _log_sigmoid(x):
    return jnp.minimum(x, 0.0) - jnp.log1p(jnp.exp(-jnp.abs(x)))


def _mlstm_prompt_kernel(q_ref, k_ref, v_ref, o_ref, gi_ref, gf_ref, bi_ref, bf_ref, bo_ref, ghn_ref,
                         h_ref, c_out_ref, n_out_ref, m_out_ref, ct_s, n_s, m_s):
    c = pl.program_id(0)
    L = ML_CHUNK

    @pl.when(c == 0)
    def _():
        ct_s[...] = jnp.zeros_like(ct_s)
        n_s[...] = jnp.zeros_like(n_s)
        m_s[...] = jnp.zeros_like(m_s)

    ig = gi_ref[...] + bi_ref[...]
    lf = _log_sigmoid(gf_ref[...] + bf_ref[...])
    r = lax.broadcasted_iota(I32, (L, L), 0)
    s = lax.broadcasted_iota(I32, (L, L), 1)
    causal = r >= s
    tril = causal.astype(F32)
    b = jnp.dot(tril, lf, precision=HIGHEST, preferred_element_type=F32)
    b_t = b.T
    ig_t = ig.T
    m_all = m_s[...]
    m_new_all = m_all
    lane = lax.broadcasted_iota(I32, (1, 128), 1)

    for h in range(ML_HEADS):
        qh = q_ref[:, h * ML_DQK:(h + 1) * ML_DQK]
        kh = k_ref[:, h * ML_DQK:(h + 1) * ML_DQK]
        vh = v_ref[:, h * ML_DV:(h + 1) * ML_DV]
        bc = b[:, h:h + 1]
        ic = ig[:, h:h + 1]
        br = b_t[h:h + 1, :]
        ir = ig_t[h:h + 1, :]
        m_h = m_all[:, h:h + 1]
        logd = jnp.where(causal, bc - br + ir, -jnp.inf)
        inter = bc + m_h
        mt = jnp.maximum(inter, jnp.max(logd, axis=1, keepdims=True))
        sc = lax.dot_general(qh, kh, (((1,), (1,)), ((), ())), preferred_element_type=F32)
        a = sc * jnp.exp(logd - mt)
        w_inter = jnp.exp(inter - mt)
        ct_h = ct_s[h]
        qc = jnp.dot(qh, ct_h.astype(BF16), preferred_element_type=F32)
        num = jnp.dot(a.astype(BF16), vh, preferred_element_type=F32) + w_inter * qc
        n_h = n_s[h:h + 1, :]
        qn = jnp.sum(qh.astype(F32) * n_h, axis=1, keepdims=True)
        den = jnp.sum(a, axis=1, keepdims=True) + w_inter * qn
        hh = num / jnp.maximum(jnp.abs(den), jnp.exp(-mt))
        hn = _rms(hh) * ghn_ref[:, h * ML_DV:(h + 1) * ML_DV]
        og = jax.nn.sigmoid(o_ref[:, h * ML_DV:(h + 1) * ML_DV].astype(F32) + bo_ref[:, h * ML_DV:(h + 1) * ML_DV])
        h_ref[:, h * ML_DV:(h + 1) * ML_DV] = (hn * og).astype(h_ref.dtype)
        m_new = mt[L - 1:L, :]
        b_last = bc[L - 1:L, :]
        decay = jnp.exp(b_last + m_h - m_new)
        wj = jnp.exp(b_last - bc + ic - m_new)
        kw = (kh.astype(F32) * wj).astype(BF16)
        upd = lax.dot_general(kw, vh, (((0,), (0,)), ((), ())), preferred_element_type=F32)
        ct_s[h] = decay * ct_h + upd
        n_s[h:h + 1, :] = decay * n_h + jnp.sum(kh.astype(F32) * wj, axis=0, keepdims=True)
        m_new_all = jnp.where(lane == h, m_new, m_new_all)

    m_s[...] = m_new_all

    @pl.when(c == pl.num_programs(0) - 1)
    def _():
        for h in range(ML_HEADS):
            c_out_ref[h] = ct_s[h].T
        n_out_ref[...] = n_s[...]
        m_out_ref[...] = m_s[...]


def _mlstm_prompt(z, gates, bi, bf, bo, ghn):
    seq = z.shape[0]
    nc = seq // ML_CHUNK
    L = ML_CHUNK
    const2 = lambda c: (0, 0)
    in_specs = [
        pl.BlockSpec((L, ML_NQK), lambda c: (c, 0)),
        pl.BlockSpec((L, ML_NQK), lambda c: (c, 1)),
        pl.BlockSpec((L, ML_NV), lambda c: (c, 1)),
        pl.BlockSpec((L, D_MODEL), lambda c: (c, 2)),
        pl.BlockSpec((L, 128), lambda c: (c, 0)),
        pl.BlockSpec((L, 128), lambda c: (c, 1)),
        pl.BlockSpec((1, 128), const2),
        pl.BlockSpec((1, 128), const2),
        pl.BlockSpec((1, D_MODEL), const2),
        pl.BlockSpec((1, ML_NV), const2),
    ]
    out_shape = [
        jax.ShapeDtypeStruct((seq, ML_NV), BF16),
        jax.ShapeDtypeStruct((ML_HEADS, ML_DV, ML_DQK), F32),
        jax.ShapeDtypeStruct((ML_HEADS, ML_DQK), F32),
        jax.ShapeDtypeStruct((1, 128), F32),
    ]
    out_specs = [
        pl.BlockSpec((L, ML_NV), lambda c: (c, 0)),
        pl.BlockSpec((ML_HEADS, ML_DV, ML_DQK), lambda c: (0, 0, 0)),
        pl.BlockSpec((ML_HEADS, ML_DQK), const2),
        pl.BlockSpec((1, 128), const2),
    ]
    return pl.pallas_call(
        _mlstm_prompt_kernel, grid=(nc,), in_specs=in_specs, out_specs=out_specs, out_shape=out_shape,
        scratch_shapes=[pltpu.VMEM((ML_HEADS, ML_DQK, ML_DV), F32), pltpu.VMEM((ML_HEADS, ML_DQK), F32),
                        pltpu.VMEM((1, 128), F32)],
        compiler_params=_cparams(("arbitrary",)), name="mlstm_prompt",
    )(z, z, z, z, gates, gates, bi, bf, bo, ghn)


def _mlstm_sample_kernel(z_ref, g_ref, c_ref, n_ref, m_ref, bi_ref, bf_ref, bo_ref, ghn_ref,
                         h_ref, c_out_ref, n_out_ref, m_out_ref):
    i = pl.program_id(0)
    z = z_ref[0]
    g = g_ref[0]
    ig_all = g[:, 0:128] + bi_ref[...]
    lf_all = _log_sigmoid(g[:, 128:256] + bf_ref[...])
    m_all = m_ref[0]
    mt_all = jnp.maximum(lf_all + m_all, ig_all)
    m_out_ref[0] = mt_all
    outs = []
    for h in range(ML_HEADS):
        q = z[:, h * ML_DQK:(h + 1) * ML_DQK]
        k = z[:, ML_NQK + h * ML_DQK:ML_NQK + (h + 1) * ML_DQK]
        v = z[:, 2 * ML_NQK + h * ML_DV:2 * ML_NQK + (h + 1) * ML_DV]
        ig = ig_all[:, h:h + 1]
        lf = lf_all[:, h:h + 1]
        m0 = m_all[:, h:h + 1]
        mt = mt_all[:, h:h + 1]
        w_inter = jnp.exp(lf + m0 - mt)
        wj = jnp.exp(ig - mt)
        a = jnp.sum(q * k, axis=1, keepdims=True) * wj
        c_h = c_ref[0, h]
        n_h = n_ref[0, h:h + 1, :]
        q8 = jnp.broadcast_to(q, (8, ML_DQK))
        cq = lax.dot_general(q8, c_h, (((1,), (1,)), ((), ())), precision=HIGHEST,
                             preferred_element_type=F32)[0:1, :]
        num = a * v + w_inter * cq
        den = a + w_inter * jnp.sum(n_h * q, axis=1, keepdims=True)
        hh = num / jnp.maximum(jnp.abs(den), jnp.exp(-mt))
        hn = _rms(hh) * ghn_ref[:, h * ML_DV:(h + 1) * ML_DV]
        og = jax.nn.sigmoid(z[:, 2 * ML_NQK + ML_NV + h * ML_DV:2 * ML_NQK + ML_NV + (h + 1) * ML_DV]
                            + bo_ref[:, h * ML_DV:(h + 1) * ML_DV])
        outs.append(hn * og)
        v_col = jnp.broadcast_to(v, (8, ML_DV)).T[:, 0:1]
        c_out_ref[0, h] = w_inter * c_h + wj * (v_col * k)
        n_out_ref[0, h:h + 1, :] = w_inter * n_h + wj * k
    h_ref[pl.ds(i, 1), :] = jnp.concatenate(outs, axis=1)


def _mlstm_sample(z_s, gates_s, c0, n0, m0, bi, bf, bo, ghn):
    ns = z_s.shape[0]
    const2 = lambda i: (0, 0)
    in_specs = [
        pl.BlockSpec((1, 1, ML_MAIN), lambda i: (i, 0, 0)),
        pl.BlockSpec((1, 1, 256), lambda i: (i, 0, 0)),
        pl.BlockSpec((1, ML_HEADS, ML_DV, ML_DQK), lambda i: (i, 0, 0, 0)),
        pl.BlockSpec((1, ML_HEADS, ML_DQK), lambda i: (i, 0, 0)),
        pl.BlockSpec((1, 1, 128), lambda i: (i, 0, 0)),
        pl.BlockSpec((1, 128), const2),
        pl.BlockSpec((1, 128), const2),
        pl.BlockSpec((1, D_MODEL), const2),
        pl.BlockSpec((1, ML_NV), const2),
    ]
    out_shape = [
        jax.ShapeDtypeStruct((ns, ML_NV), F32),
        jax.ShapeDtypeStruct((ns, ML_HEADS, ML_DV, ML_DQK), F32),
        jax.ShapeDtypeStruct((ns, ML_HEADS, ML_DQK), F32),
        jax.ShapeDtypeStruct((ns, 1, 128), F32),
    ]
    out_specs = [
        pl.BlockSpec((ns, ML_NV), const2),
        pl.BlockSpec((1, ML_HEADS, ML_DV, ML_DQK), lambda i: (i, 0, 0, 0)),
        pl.BlockSpec((1, ML_HEADS, ML_DQK), lambda i: (i, 0, 0)),
        pl.BlockSpec((1, 1, 128), lambda i: (i, 0, 0)),
    ]
    return pl.pallas_call(
        _mlstm_sample_kernel, grid=(ns,), in_specs=in_specs, out_specs=out_specs, out_shape=out_shape,
        compiler_params=_cparams(("arbitrary",)), name="mlstm_sample",
    )(z_s, gates_s, c0, n0, m0, bi, bf, bo, ghn)


def _attn_prompt_kernel(q_ref, kp_ref, kc_ref, vp_ref, vc_ref, o_ref, lse_ref):
    blk = pl.program_id(1)
    T = ATT_STEPS
    qi = lax.broadcasted_iota(I32, (T, T), 0)
    kj = lax.broadcasted_iota(I32, (T, T), 1)
    valid_prev = jnp.logical_and(kj >= qi, blk > 0)
    valid_cur = kj <= qi
    scale = ATT_HD ** -0.5
    nt = (((1,), (1,)), ((), ()))
    for h in range(ATT_H):
        sl = slice(h * ATT_HD, (h + 1) * ATT_HD)
        qh = q_ref[:, sl]
        s1 = jnp.where(valid_prev, lax.dot_general(qh, kp_ref[:, sl], nt, preferred_element_type=F32) * scale,
                       -jnp.inf)
        s2 = jnp.where(valid_cur, lax.dot_general(qh, kc_ref[:, sl], nt, preferred_element_type=F32) * scale,
                       -jnp.inf)
        mx = jnp.maximum(jnp.max(s1, axis=1, keepdims=True), jnp.max(s2, axis=1, keepdims=True))
        p1 = jnp.exp(s1 - mx)
        p2 = jnp.exp(s2 - mx)
        den = jnp.sum(p1, axis=1, keepdims=True) + jnp.sum(p2, axis=1, keepdims=True)
        acc = jnp.dot(p1.astype(BF16), vp_ref[:, sl], preferred_element_type=F32)
        acc = acc + jnp.dot(p2.astype(BF16), vc_ref[:, sl], preferred_element_type=F32)
        o_ref[:, sl] = acc / den
        lse_ref[:, sl] = jnp.broadcast_to(mx + jnp.log(den), (T, ATT_HD))


def _attn_prompt(q, kv, g):
    dil, L, _ = q.shape
    nb = L // ATT_STEPS
    T = ATT_STEPS
    blk = (None, T, ATT_GW)
    in_specs = [
        pl.BlockSpec(blk, lambda r, b: (r, b, 0)),
        pl.BlockSpec(blk, lambda r, b: (r, jnp.maximum(b - 1, 0), 0)),
        pl.BlockSpec(blk, lambda r, b: (r, b, 0)),
        pl.BlockSpec(blk, lambda r, b: (r, jnp.maximum(b - 1, 0), 1)),
        pl.BlockSpec(blk, lambda r, b: (r, b, 1)),
    ]
    out_spec = pl.BlockSpec(blk, lambda r, b: (r, b, 0))
    return pl.pallas_call(
        _attn_prompt_kernel, grid=(dil, nb), in_specs=in_specs, out_specs=[out_spec, out_spec],
        out_shape=[jax.ShapeDtypeStruct((dil, L, ATT_GW), F32)] * 2,
        compiler_params=_cparams(("arbitrary", "arbitrary")), name=f"attn_prompt_g{g}",
    )(q, kv, kv, kv, kv)


def _merge_kernel(*refs):
    in_refs, out_ref, scratch = refs[:2 * ATT_G], refs[2 * ATT_G], refs[2 * ATT_G + 1:]
    tm = out_ref.shape[0]
    for c in range(ATT_H):
        lanes = slice(c * ATT_HD, (c + 1) * ATT_HD)
        vals = []
        si = 0
        for g in range(ATT_G):
            dil = ATT_DILS[g]
            pair = []
            for ref in in_refs[2 * g:2 * g + 2]:
                if dil == 1:
                    pair.append(ref[0, :, lanes])
                else:
                    buf = scratch[si]
                    si += 1
                    for r in range(dil):
                        buf[pl.ds(r, tm // dil, stride=dil), :] = ref[r, :, lanes]
                    pair.append(buf[...])
            vals.append(pair)
        lses = [p[1] for p in vals]
        mx = jnp.maximum(jnp.maximum(lses[0], lses[1]), lses[2])
        es = [jnp.exp(l - mx) for l in lses]
        tot = es[0] + es[1] + es[2]
        out_ref[:, lanes] = ((es[0] / tot) * vals[0][0] + (es[1] / tot) * vals[1][0]
                             + (es[2] / tot) * vals[2][0]).astype(out_ref.dtype)


def _merge_groups(parts, seq, tm=512):
    in_specs, args, scratch = [], [], []
    for g, pair in enumerate(parts):
        dil = ATT_DILS[g]
        for a in pair:
            in_specs.append(pl.BlockSpec((dil, tm // dil, ATT_GW), lambda i: (0, i, 0)))
            args.append(a)
            if dil > 1:
                scratch.append(pltpu.VMEM((tm, ATT_HD), F32))
    return pl.pallas_call(
        _merge_kernel, grid=(seq // tm,), in_specs=in_specs,
        out_specs=pl.BlockSpec((tm, ATT_GW), lambda i: (i, 0)),
        out_shape=jax.ShapeDtypeStruct((seq, ATT_GW), BF16), scratch_shapes=scratch,
        compiler_params=_cparams(("arbitrary",)), name="attn_merge",
    )(*args)


def _attn_sample_kernel(q_ref, kvn_ref, b0_ref, b1_ref, b2_ref, out_ref):
    scale = ATT_HD ** -0.5
    outs, lses = [], []
    for g, b_ref in enumerate((b0_ref, b1_ref, b2_ref)):
        qg = q_ref[0, g]
        kn = kvn_ref[0, g, 0]
        vn = kvn_ref[0, g, 1]
        kb = b_ref[:, 0]
        vb = b_ref[:, 1]
        s = jnp.sum(kb * qg[None], axis=2, keepdims=True) * scale
        s_new = jnp.sum(kn * qg, axis=1, keepdims=True) * scale
        mx = jnp.maximum(jnp.max(s, axis=0), s_new)
        p = jnp.exp(s - mx[None])
        p_new = jnp.exp(s_new - mx)
        den = jnp.sum(p, axis=0) + p_new
        o = jnp.sum(p * vb, axis=0) + p_new * vn
        outs.append(o / den)
        lses.append(mx + jnp.log(den))
    mxl = jnp.maximum(jnp.maximum(lses[0], lses[1]), lses[2])
    es = [jnp.exp(l - mxl) for l in lses]
    tot = es[0] + es[1] + es[2]
    out_ref[0] = (es[0] / tot) * outs[0] + (es[1] / tot) * outs[1] + (es[2] / tot) * outs[2]


def _attn_sample(q_s, kv_s, caches):
    ns = q_s.shape[0]
    views, specs = [], []
    for g, cbuf in enumerate(caches):
        lb = cbuf.shape[1]
        dil = ATT_DILS[g]
        views.append(cbuf.reshape(ns, lb // dil, dil, 2, ATT_H, ATT_HD))
        specs.append(pl.BlockSpec((None, ATT_STEPS, None, 2, ATT_H, ATT_HD), lambda i: (i, 0, 0, 0, 0, 0)))
    in_specs = [
        pl.BlockSpec((1, ATT_G, ATT_H, ATT_HD), lambda i: (i, 0, 0, 0)),
        pl.BlockSpec((1, ATT_G, 2, ATT_H, ATT_HD), lambda i: (i, 0, 0, 0, 0)),
    ] + specs
    return pl.pallas_call(
        _attn_sample_kernel, grid=(ns,), in_specs=in_specs,
        out_specs=pl.BlockSpec((1, ATT_H, ATT_HD), lambda i: (i, 0, 0)),
        out_shape=jax.ShapeDtypeStruct((ns, ATT_H, ATT_HD), F32),
        compiler_params=_cparams(("arbitrary",)), name="attn_sample",
    )(q_s, kv_s, *views)


KV_ROW = 2 * ATT_H
KV_SHIFT_BLOCK = 8192


def _kv_shift_kernel(cur_ref, nxt_ref, new_ref, out_ref):
    blk = out_ref.shape[1]
    out_ref[0, :blk - KV_ROW] = cur_ref[0, KV_ROW:]
    last = pl.program_id(1) == pl.num_programs(1) - 1
    out_ref[0, blk - KV_ROW:] = jnp.where(last, new_ref[0], nxt_ref[0])


def _kv_shift(cache, new):
    ns, lb = cache.shape[0], cache.shape[1]
    rows = lb * KV_ROW
    blk = min(KV_SHIFT_BLOCK, rows)
    nb = rows // blk
    per = blk // KV_ROW
    flat = cache.reshape(ns, rows, ATT_HD)
    out = pl.pallas_call(
        _kv_shift_kernel, grid=(ns, nb),
        in_specs=[pl.BlockSpec((1, blk, ATT_HD), lambda i, j: (i, j, 0)),
                  pl.BlockSpec((1, KV_ROW, ATT_HD), lambda i, j: (i, jnp.minimum((j + 1) * per, lb - 1), 0)),
                  pl.BlockSpec((1, KV_ROW, ATT_HD), lambda i, j: (i, 0, 0))],
        out_specs=pl.BlockSpec((1, blk, ATT_HD), lambda i, j: (i, j, 0)),
        out_shape=jax.ShapeDtypeStruct(flat.shape, flat.dtype),
        compiler_params=_cparams(("arbitrary", "arbitrary")), name="kv_shift",
    )(flat, flat, new.reshape(ns, KV_ROW, ATT_HD))
    return out.reshape(cache.shape)


def _top2(y, wr_ref, br_ref):
    rows = y.shape[0]
    lane = lax.broadcasted_iota(I32, (rows, 128), 1)
    logits = jnp.dot(y, wr_ref[...], precision=HIGHEST, preferred_element_type=F32) + br_ref[...]
    logits = jnp.where(lane < N_EXPERTS, logits, NEG_BIG)
    e = jnp.exp(logits - jnp.max(logits, axis=1, keepdims=True))
    probs = e / jnp.sum(e, axis=1, keepdims=True)
    p1 = jnp.max(probs, axis=1, keepdims=True)
    i1 = jnp.min(jnp.where(probs == p1, lane, 128), axis=1, keepdims=True)
    probs2 = jnp.where(lane == i1, -1.0, probs)
    p2 = jnp.max(probs2, axis=1, keepdims=True)
    i2 = jnp.min(jnp.where(probs2 == p2, lane, 128), axis=1, keepdims=True)
    tot = p1 + p2
    return lane, i1, i2, p1 / tot, p2 / tot


def _router_kernel(x_ref, xs_ref, g_ref, wr_ref, br_ref, xn_ref, eid_ref, gate_ref, rank_ref, cnt_ref,
                   xns_ref, gs_ref, carry):
    i = pl.program_id(0)
    tm = x_ref.shape[0]

    @pl.when(i == 0)
    def _():
        carry[...] = jnp.zeros_like(carry)
        ys = _rms(xs_ref[...]) * g_ref[...]
        xns_ref[...] = ys
        lane, i1, i2, g1, g2 = _top2(ys, wr_ref, br_ref)
        gs_ref[...] = jnp.where(lane == i1, g1, jnp.where(lane == i2, g2, 0.0))

    y = _rms(x_ref[...]) * g_ref[...]
    xn_ref[...] = y
    lane, i1, i2, g1, g2 = _top2(y, wr_ref, br_ref)
    sel1 = lane == i1
    sel2 = lane == i2
    onehot = jnp.where(jnp.logical_or(sel1, sel2), 1.0, 0.0)
    rr = lax.broadcasted_iota(I32, (tm, tm), 0)
    cc = lax.broadcasted_iota(I32, (tm, tm), 1)
    before = (cc < rr).astype(BF16)
    prefix = jnp.dot(before, onehot.astype(BF16), preferred_element_type=F32) + carry[...]
    r1 = jnp.sum(jnp.where(sel1, prefix, 0.0), axis=1, keepdims=True)
    r2 = jnp.sum(jnp.where(sel2, prefix, 0.0), axis=1, keepdims=True)
    carry[...] = carry[...] + jnp.sum(onehot, axis=0, keepdims=True)
    eid_ref[...] = jnp.where(lane == 0, i1, jnp.where(lane == 1, i2, 0))
    gate_ref[...] = jnp.where(lane == 0, g1, jnp.where(lane == 1, g2, 0.0))
    rank_ref[...] = jnp.where(lane == 0, r1, jnp.where(lane == 1, r2, 0.0)).astype(I32)
    cnt_ref[...] = jnp.broadcast_to(carry[...], cnt_ref.shape)


def _router(h, hs, gain, w_router_pad, b_router_pad, tm=ROW_TILE):
    m, d = h.shape
    s = hs.shape[0]
    const = lambda i: (0, 0)
    row_spec = pl.BlockSpec((tm, 128), lambda i: (i, 0))
    return pl.pallas_call(
        _router_kernel, grid=(m // tm,),
        in_specs=[pl.BlockSpec((tm, d), lambda i: (i, 0)), pl.BlockSpec((s, d), const), pl.BlockSpec((1, d), const),
                  pl.BlockSpec((d, 128), const), pl.BlockSpec((1, 128), const)],
        out_specs=[pl.BlockSpec((tm, d), lambda i: (i, 0)), row_spec, row_spec, row_spec,
                   pl.BlockSpec((8, 128), const), pl.BlockSpec((s, d), const), pl.BlockSpec((s, 128), const)],
        out_shape=[jax.ShapeDtypeStruct((m, d), F32), jax.ShapeDtypeStruct((m, 128), I32),
                   jax.ShapeDtypeStruct((m, 128), F32), jax.ShapeDtypeStruct((m, 128), I32),
                   jax.ShapeDtypeStruct((8, 128), F32), jax.ShapeDtypeStruct((s, d), F32),
                   jax.ShapeDtypeStruct((s, 128), F32)],
        scratch_shapes=[pltpu.VMEM((1, 128), F32)],
        compiler_params=_cparams(("arbitrary",)), name="router",
    )(h, hs, gain, w_router_pad, b_router_pad)


def _dispatch_kernel(pos_ref, nv_ref, x_hbm, out_ref, inv, buf, sem, *, n_tok):
    i = pl.program_id(0)
    tg = out_ref.shape[0]

    @pl.when(i == 0)
    def _():
        def clear(s, c):
            inv[s] = 0
            return c

        lax.fori_loop(0, inv.shape[0], clear, 0, unroll=8)

        def fill(t, c):
            inv[pos_ref[2 * t]] = t
            inv[pos_ref[2 * t + 1]] = t
            return c

        lax.fori_loop(0, n_tok, fill, 0, unroll=8)

    def start_gather(tile):
        slot = tile % 2

        def issue(r, c):
            pltpu.make_async_copy(x_hbm.at[pl.ds(inv[tile * tg + r], 1)], buf.at[slot, pl.ds(r, 1)],
                                  sem.at[slot]).start()
            return c

        lax.fori_loop(0, tg, issue, 0, unroll=8)

    @pl.when(i == 0)
    def _():
        start_gather(i)

    @pl.when(i + 1 < nv_ref[0])
    def _():
        start_gather(i + 1)

    @pl.when(i < nv_ref[0])
    def _():
        slot = i % 2
        pltpu.make_async_copy(x_hbm.at[pl.ds(0, tg)], buf.at[slot], sem.at[slot]).wait()
        out_ref[...] = buf[slot].astype(out_ref.dtype)

    @pl.when(i >= nv_ref[0])
    def _():
        out_ref[...] = jnp.zeros_like(out_ref)


def _dispatch(pos_flat, nvalid, xn, n_tiles, tg):
    n_tok, d = xn.shape
    return pl.pallas_call(
        functools.partial(_dispatch_kernel, n_tok=n_tok),
        grid_spec=pltpu.PrefetchScalarGridSpec(
            num_scalar_prefetch=2, grid=(n_tiles,),
            in_specs=[pl.BlockSpec(memory_space=pl.ANY)],
            out_specs=pl.BlockSpec((tg, d), lambda i, pos, nv: (i, 0)),
            scratch_shapes=[pltpu.SMEM((n_tiles * tg,), I32), pltpu.VMEM((2, tg, d), xn.dtype),
                            pltpu.SemaphoreType.DMA((2,))]),
        out_shape=jax.ShapeDtypeStruct((n_tiles * tg, d), BF16),
        compiler_params=_cparams(("arbitrary",)), name="moe_dispatch",
    )(pos_flat, nvalid, xn)


def _combine_kernel(pos_ref, h_ref, gate_ref, g_ref, hs_ref, gs_ref, ys_s_ref, ys_hbm, out_ref, outs_ref, ybuf, sem):
    i = pl.program_id(0)
    tm = h_ref.shape[0]

    def start_gather(tile):
        slot = tile % 2

        def issue(r, carry):
            for j in range(2):
                pltpu.make_async_copy(ys_hbm.at[pl.ds(pos_ref[2 * (tile * tm + r) + j], 1)],
                                      ybuf.at[slot, j, pl.ds(r, 1)], sem.at[slot]).start()
            return carry

        lax.fori_loop(0, tm, issue, 0, unroll=4)

    @pl.when(i == 0)
    def _():
        start_gather(i)

    @pl.when(i + 1 < pl.num_programs(0))
    def _():
        start_gather(i + 1)

    @pl.when(i == 0)
    def _():
        gs = gs_ref[...]
        y = jnp.zeros(hs_ref.shape, F32)
        for e in range(N_EXPERTS):
            y = y + gs[:, e:e + 1] * ys_s_ref[e]
        outs_ref[...] = _rms(hs_ref[...] + y) * g_ref[...]

    slot = i % 2
    for j in range(2):
        pltpu.make_async_copy(ys_hbm.at[pl.ds(0, tm)], ybuf.at[slot, j], sem.at[slot]).wait()
    gate = gate_ref[...]
    y = h_ref[...] + (gate[:, 0:1] * ybuf[slot, 0] + gate[:, 1:2] * ybuf[slot, 1])
    out_ref[...] = _rms(y) * g_ref[...]


def _combine(pos_flat, h, gate, g_final, hs, gs, ys_s, ys, tm=ROW_TILE):
    m, d = h.shape
    s = hs.shape[0]
    c2 = lambda i, pos: (0, 0)
    return pl.pallas_call(
        _combine_kernel,
        grid_spec=pltpu.PrefetchScalarGridSpec(
            num_scalar_prefetch=1, grid=(m // tm,),
            in_specs=[pl.BlockSpec((tm, d), lambda i, pos: (i, 0)),
                      pl.BlockSpec((tm, 128), lambda i, pos: (i, 0)),
                      pl.BlockSpec((1, d), c2),
                      pl.BlockSpec((s, d), c2),
                      pl.BlockSpec((s, 128), c2),
                      pl.BlockSpec((N_EXPERTS, s, d), lambda i, pos: (0, 0, 0)),
                      pl.BlockSpec(memory_space=pl.ANY)],
            out_specs=[pl.BlockSpec((tm, d), lambda i, pos: (i, 0)), pl.BlockSpec((s, d), c2)],
            scratch_shapes=[pltpu.VMEM((2, 2, tm, d), F32), pltpu.SemaphoreType.DMA((2,))]),
        out_shape=[jax.ShapeDtypeStruct((m, d), F32), jax.ShapeDtypeStruct((s, d), F32)],
        compiler_params=_cparams(("arbitrary",)), name="moe_combine",
    )(pos_flat, h, gate, g_final, hs, gs, ys_s, ys)


def _rope_tables(pos):
    half = ATT_HD // 2
    inv = ROPE_THETA ** (-jnp.arange(half, dtype=F32) / half)
    ang = pos.astype(F32)[:, None] * inv[None, :]
    cos, sin = jnp.cos(ang), jnp.sin(ang)
    return jnp.concatenate([cos, cos], axis=1), jnp.concatenate([-sin, sin], axis=1)


def kernel(x_prompt, x_sample, state_mlstm_C, state_mlstm_n, state_mlstm_m, cache_kv_w128, cache_kv_w512, cache_kv_w2048, g_mix, g_ffn, w_ml_in, b_ml_gates, b_ml_o, g_ml_hnorm, w_ml_out, g_kv, w_kv, w_q, w_o, w_ffn_gate, w_ffn_up, w_ffn_down, w_router, b_router, w_exp_gate, w_exp_up, w_exp_down, g_final):
    bp, seq, d = x_prompt.shape
    ns = x_sample.shape[0]
    caches = (cache_kv_w128, cache_kv_w512, cache_kv_w2048)
    assert bp == 1 and x_sample.shape[1] == 1 and d == D_MODEL and ns % 8 == 0
    assert seq % (ATT_STEPS * max(ATT_DILS)) == 0 and seq % 1024 == 0
    assert all(c.shape[1] == w for c, w in zip(caches, ATT_WINDOWS))
    tm = 512

    h0 = x_prompt[0]
    h0_s = x_sample[:, 0]

    w_gates = w_ml_in[0][:, ML_MAIN:]
    w_gates_pad = jnp.zeros((d, 256), F32).at[:, 0:ML_HEADS].set(w_gates[:, :ML_HEADS])
    w_gates_pad = w_gates_pad.at[:, 128:128 + ML_HEADS].set(w_gates[:, ML_HEADS:])
    bi = jnp.zeros((1, 128), F32).at[0, :ML_HEADS].set(b_ml_gates[0, :ML_HEADS])
    bf = jnp.zeros((1, 128), F32).at[0, :ML_HEADS].set(b_ml_gates[0, ML_HEADS:])
    (xn0, gates), (xn0_s, gates_s) = _rmsnorm(h0, h0_s, g_mix[0:1], proj=w_gates_pad)
    k_scale = jnp.concatenate([jnp.ones((1, ML_NQK), F32), jnp.full((1, ML_NQK), ML_DQK ** -0.5, F32),
                               jnp.ones((1, ML_NV + D_MODEL), F32)], axis=1)
    z, z_s = _matmul(xn0, [w_ml_in], ML_MAIN, BF16, xn0_s[None], tm=tm, tn=1024, col_scale=k_scale, name="ml_in")
    bo = b_ml_o[0:1]
    ghn = g_ml_hnorm[0:1]
    hg, p_c, p_n, p_m = _mlstm_prompt(z[0], gates, bi, bf, bo, ghn)
    m0 = jnp.zeros((ns, 1, 128), F32).at[:, 0, :ML_HEADS].set(state_mlstm_m[0])
    hg_s, s_c, s_n, s_m = _mlstm_sample(z_s[0].reshape(ns, 1, ML_MAIN), gates_s.reshape(ns, 1, 256),
                                        state_mlstm_C[0], state_mlstm_n[0], m0, bi, bf, bo, ghn)
    h1, h1_s = _matmul(hg, [w_ml_out], d, F32, hg_s[None], tm=tm, tn=1024, res=h0, res_s=h0_s, name="ml_out")
    h1, h1_s = h1[0], h1_s[0]

    (xf0,), (xf0_s,) = _rmsnorm(h1, h1_s, g_ffn[0:1])
    ffn_dense = w_ffn_gate.shape[2]
    hid, hid_s = _matmul(xf0, [w_ffn_gate, w_ffn_up], ffn_dense, BF16, xf0_s[None], tm=tm, tn=512, name="ffn_up")
    h2, h2_s = _matmul(hid[0], [w_ffn_down], d, F32, hid_s, tm=ROW_TILE, tn=512, res=h1, res_s=h1_s,
                       name="ffn_down")
    h2, h2_s = h2[0], h2_s[0]

    (xq, xkv), (xq_s, xkv_s) = _rmsnorm(h2, h2_s, jnp.stack([g_mix[1], g_kv]))
    cos, sin = _rope_tables(jnp.arange(seq))
    cos_s, sin_s = _rope_tables(jnp.full((ns,), PAST_LEN, I32))
    rope_args = dict(cos=cos, sin=sin, cos_s=cos_s, sin_s=sin_s)
    parts, kv_nat, kv_new, q_new = [], [], [], []
    for g in range(ATT_G):
        dil = ATT_DILS[g]
        kvd, kvn, kv_s = _matmul(xkv, [w_kv[None]], 2 * ATT_GW, BF16, xkv_s[None], tm=tm, tn=ATT_GW,
                                 col_off=2 * g, rope="even", dil=dil, natural=True, name=f"kv_proj_g{g}",
                                 **rope_args)
        qd, q_s = _matmul(xq, [w_q], ATT_GW, BF16, xq_s[None], tm=tm, tn=ATT_GW, col_off=g, rope="all",
                          dil=dil, name=f"q_proj_g{g}", **rope_args)
        parts.append(_attn_prompt(qd, kvd, g))
        kv_nat.append(kvn)
        kv_new.append(kv_s[0])
        q_new.append(q_s[0])
    att = _merge_groups(parts, seq)
    att_s = _attn_sample(jnp.stack(q_new, axis=1).reshape(ns, ATT_G, ATT_H, ATT_HD),
                         jnp.stack(kv_new, axis=1).reshape(ns, ATT_G, 2, ATT_H, ATT_HD), caches)
    h3, h3_s = _matmul(att, [w_o], d, F32, att_s.reshape(1, ns, ATT_GW), tm=tm, tn=1024, res=h2, res_s=h2_s,
                       name="attn_out")
    h3, h3_s = h3[0], h3_s[0]

    wr_pad = jnp.zeros((d, 128), F32).at[:, :N_EXPERTS].set(w_router[0])
    br_pad = jnp.zeros((1, 128), F32).at[0, :N_EXPERTS].set(b_router[0])
    xn2, eid, gate, rank, cnt, xn2_s, gates_moe_s = _router(h3, h3_s, g_ffn[1:2], wr_pad, br_pad)
    tg = MOE_TILE
    n_tiles = -(-(2 * seq + N_EXPERTS * (tg - 1)) // tg)
    counts = cnt[0, :N_EXPERTS].astype(I32)
    padded = jnp.maximum((counts + tg - 1) // tg, 1) * tg
    gend = jnp.cumsum(padded)
    gstart = gend - padded
    pos = (gstart[eid[:, :2]] + rank[:, :2]).astype(I32).reshape(-1)
    nvalid = (gend[-1] // tg).astype(I32)
    tile_start = jnp.arange(n_tiles, dtype=I32) * tg
    te = jnp.minimum(jnp.sum((tile_start[:, None] >= gend[None, :]).astype(I32), axis=1), N_EXPERTS - 1)
    te = te.astype(I32)
    nvalid = nvalid.reshape(1)
    xs = _dispatch(pos, nvalid, xn2, n_tiles, tg)
    ffn_e = w_exp_gate.shape[3]
    hs, hs_s = _matmul(xs, [w_exp_gate[0], w_exp_up[0]], ffn_e, BF16, xn2_s[None], tm=tg, tn=1024, te=te,
                       nvalid=nvalid, side_precise=False, name="moe_up")
    ys, ys_s = _matmul(hs[0], [w_exp_down[0]], d, F32, hs_s, tm=tg, tn=512, te=te, nvalid=nvalid,
                       xs_per_expert=True, side_precise=False, name="moe_down")
    y_p, y_s = _combine(pos, h3, gate, g_final.reshape(1, d), h3_s, gates_moe_s, ys_s, ys[0])

    p_bufs, s_bufs = [], []
    for g in range(ATT_G):
        keep = min(ATT_WINDOWS[g], seq)
        p_bufs.append(kv_nat[g][seq - keep:].reshape(1, keep, 2, ATT_H, ATT_HD))
        s_bufs.append(_kv_shift(caches[g], kv_new[g]))
    return (y_p.reshape(1, seq, d), y_s.reshape(ns, 1, d),
            p_c[None, None], p_n[None, None], p_m[:, :ML_HEADS][None],
            s_c[None], s_n[None], s_m[:, 0, :ML_HEADS][None],
            p_bufs[0], p_bufs[1], p_bufs[2], s_bufs[0], s_bufs[1], s_bufs[2])
```

```python
import functools

import jax
import jax.numpy as jnp
from jax import lax
from jax.experimental import pallas as pl
from jax.experimental.pallas import tpu as pltpu

F32 = jnp.float32
BF16 = jnp.bfloat16
I32 = jnp.int32
HIGHEST = lax.Precision.HIGHEST

D_MODEL = 2048
ML_HEADS = 8
ML_DQK = 128
ML_DV = 256
ML_NQK = ML_HEADS * ML_DQK
ML_NV = ML_HEADS * ML_DV
ML_MAIN = 2 * ML_NQK + ML_NV + D_MODEL
ML_CHUNK = 128
ATT_HD = 128
ATT_H = 8
ATT_G = 3
ATT_WINDOWS = (128, 512, 2048)
ATT_DILS = (1, 4, 16)
ATT_STEPS = 128
ATT_GW = ATT_H * ATT_HD
ROPE_THETA = 10000.0
PAST_LEN = 8192
N_EXPERTS = 8
RMS_EPS = 1e-6
NEG_BIG = -1e30

VMEM_LIMIT_BYTES = 58 * 1024 * 1024
ROW_TILE = 256
MOE_TILE = 256


def _cparams(sem):
    return pltpu.CompilerParams(dimension_semantics=sem, vmem_limit_bytes=VMEM_LIMIT_BYTES)


def _rms(x):
    return x * lax.rsqrt(jnp.mean(x * x, axis=-1, keepdims=True) + RMS_EPS)


def _norm_kernel(x_ref, xs_ref, g_ref, *refs, n_out, has_proj):
    ins = 1 if has_proj else 0
    n_each = n_out + ins
    main = refs[ins:ins + n_each]
    side = refs[ins + n_each:ins + 2 * n_each]

    def emit(x, outs):
        y = _rms(x)
        for i in range(n_out):
            outs[i][...] = (y * g_ref[i:i + 1, :]).astype(outs[i].dtype)
        if has_proj:
            outs[n_out][...] = jnp.dot(y * g_ref[0:1, :], refs[0][...], precision=HIGHEST,
                                       preferred_element_type=F32)

    emit(x_ref[...], main)

    @pl.when(pl.program_id(0) == 0)
    def _():
        emit(xs_ref[...], side)


def _rmsnorm(x, xs, gains, proj=None, tm=ROW_TILE):
    m, d = x.shape
    s = xs.shape[0]
    n_out = gains.shape[0]
    const = lambda i: (0, 0)
    in_specs = [pl.BlockSpec((tm, d), lambda i: (i, 0)), pl.BlockSpec((s, d), const),
                pl.BlockSpec((n_out, d), const)]
    args = [x, xs, gains]
    main_shape = [jax.ShapeDtypeStruct((m, d), BF16)] * n_out
    main_specs = [pl.BlockSpec((tm, d), lambda i: (i, 0))] * n_out
    side_shape = [jax.ShapeDtypeStruct((s, d), F32)] * n_out
    side_specs = [pl.BlockSpec((s, d), const)] * n_out
    if proj is not None:
        p = proj.shape[1]
        in_specs.append(pl.BlockSpec((d, p), const))
        args.append(proj)
        main_shape.append(jax.ShapeDtypeStruct((m, p), F32))
        main_specs.append(pl.BlockSpec((tm, p), lambda i: (i, 0)))
        side_shape.append(jax.ShapeDtypeStruct((s, p), F32))
        side_specs.append(pl.BlockSpec((s, p), const))
    outs = pl.pallas_call(
        functools.partial(_norm_kernel, n_out=n_out, has_proj=proj is not None),
        grid=(m // tm,), in_specs=in_specs, out_specs=main_specs + side_specs,
        out_shape=main_shape + side_shape,
        compiler_params=_cparams(("arbitrary",)), name="rmsnorm",
    )(*args)
    k = len(main_shape)
    return outs[:k], outs[k:]


def _rope_heads(acc, cos, sin):
    outs = []
    for h in range(acc.shape[1] // ATT_HD):
        a = acc[:, h * ATT_HD:(h + 1) * ATT_HD]
        outs.append(a * cos + pltpu.roll(a, ATT_HD // 2, 1) * sin)
    return jnp.concatenate(outs, axis=1)


def _mm_kernel(te_ref, nv_ref, x_ref, *refs, n_w, has_scale, has_res, rope, dil, has_nat, side_precise):
    n = pl.program_id(0)
    m = pl.program_id(1)
    it = iter(refs)
    w_refs = [next(it) for _ in range(n_w)]
    scale_ref = next(it) if has_scale else None
    res_ref = next(it) if has_res else None
    cos_ref, sin_ref = (next(it), next(it)) if rope else (None, None)
    xs_ref = next(it)
    res_s_ref = next(it) if has_res else None
    cos_s_ref, sin_s_ref = (next(it), next(it)) if rope else (None, None)
    o_ref = next(it)
    nat_ref = next(it) if has_nat else None
    os_ref = next(it)
    wb_refs = [next(it) for _ in range(n_w)]
    deint = next(it) if dil > 1 else None

    def finish(acc, up, res, cos, sin, store):
        if n_w == 2:
            acc = (acc * jax.nn.sigmoid(acc)) * up
        if has_scale:
            acc = acc * scale_ref[...]
        if has_res:
            acc = acc + res
        if rope == "all":
            store(_rope_heads(acc, cos, sin))
        elif rope == "even":
            @pl.when(n % 2 == 0)
            def _():
                store(_rope_heads(acc, cos, sin))

            @pl.when(n % 2 == 1)
            def _():
                store(acc)
        else:
            store(acc)

    def store_side(val):
        os_ref[0] = val

    def store_main(val):
        if has_nat:
            nat_ref[...] = val
        if dil == 1:
            o_ref[0] = val.astype(o_ref.dtype)
        else:
            rows = deint.shape[1] // dil
            for c in range(deint.shape[0]):
                lanes = slice(c * 128, (c + 1) * 128)
                deint[c] = val[:, lanes]
                for r in range(dil):
                    o_ref[r, :, lanes] = deint[c, pl.ds(r, rows, stride=dil), :].astype(o_ref.dtype)

    prev = jnp.maximum(m - 1, 0)
    new_weights = jnp.logical_or(m == 0, te_ref[m] != te_ref[prev])

    @pl.when(new_weights)
    def _():
        for w_ref, wb_ref in zip(w_refs, wb_refs):
            wb_ref[...] = w_ref[0].astype(BF16)
        if side_precise:
            xs = xs_ref[0]
            acc = jnp.dot(xs, w_refs[0][0], precision=HIGHEST, preferred_element_type=F32)
            up = jnp.dot(xs, w_refs[1][0], precision=HIGHEST, preferred_element_type=F32) if n_w == 2 else None
        else:
            xs = xs_ref[0].astype(BF16)
            acc = jnp.dot(xs, wb_refs[0][...], preferred_element_type=F32)
            up = jnp.dot(xs, wb_refs[1][...], preferred_element_type=F32) if n_w == 2 else None
        finish(acc, up, res_s_ref[...] if has_res else None,
               cos_s_ref[...] if rope else None, sin_s_ref[...] if rope else None, store_side)

    @pl.when(m >= nv_ref[0])
    def _():
        o_ref[...] = jnp.zeros_like(o_ref)

    @pl.when(m < nv_ref[0])
    def _():
        xb = x_ref[...].astype(BF16)
        acc = jnp.dot(xb, wb_refs[0][...], preferred_element_type=F32)
        up = jnp.dot(xb, wb_refs[1][...], preferred_element_type=F32) if n_w == 2 else None
        finish(acc, up, res_ref[...] if has_res else None,
               cos_ref[...] if rope else None, sin_ref[...] if rope else None, store_main)


def _matmul(x, ws, n_cols, out_dtype, xs, *, tm, tn, col_off=0, te=None, nvalid=None, xs_per_expert=False,
            col_scale=None, res=None, res_s=None, rope=None, cos=None, sin=None, cos_s=None, sin_s=None,
            dil=1, natural=False, side_precise=True, name="matmul"):
    m, k = x.shape
    s = xs.shape[1]
    n_m = m // tm
    n_n = n_cols // tn
    n_e = ws[0].shape[0]
    if te is None:
        te = jnp.zeros((n_m,), I32)
        nvalid = jnp.full((1,), n_m, I32)

    def row(mi, nv):
        return jnp.minimum(mi, nv[0] - 1)

    def exp(mi, te, nv):
        return te[row(mi, nv)]

    in_specs = [pl.BlockSpec((tm, k), lambda n, mi, te, nv: (row(mi, nv), 0))]
    args = [x]
    for w in ws:
        in_specs.append(pl.BlockSpec((1, k, tn), lambda n, mi, te, nv: (exp(mi, te, nv), 0, n + col_off)))
        args.append(w)
    if col_scale is not None:
        in_specs.append(pl.BlockSpec((1, tn), lambda n, mi, te, nv: (0, n)))
        args.append(col_scale)
    if res is not None:
        in_specs.append(pl.BlockSpec((tm, tn), lambda n, mi, te, nv: (row(mi, nv), n)))
        args.append(res)
    if rope is not None:
        for t in (cos, sin):
            in_specs.append(pl.BlockSpec((tm, ATT_HD), lambda n, mi, te, nv: (row(mi, nv), 0)))
            args.append(t)
    if xs_per_expert:
        in_specs.append(pl.BlockSpec((1, s, k), lambda n, mi, te, nv: (exp(mi, te, nv), 0, 0)))
    else:
        in_specs.append(pl.BlockSpec((1, s, k), lambda n, mi, te, nv: (0, 0, 0)))
    args.append(xs)
    if res is not None:
        in_specs.append(pl.BlockSpec((s, tn), lambda n, mi, te, nv: (0, n)))
        args.append(res_s)
    if rope is not None:
        for t in (cos_s, sin_s):
            in_specs.append(pl.BlockSpec((s, ATT_HD), lambda n, mi, te, nv: (0, 0)))
            args.append(t)

    out_shape = [jax.ShapeDtypeStruct((dil, m // dil, n_cols), out_dtype)]
    out_specs = [pl.BlockSpec((dil, tm // dil, tn), lambda n, mi, te, nv: (0, mi, n))]
    if natural:
        out_shape.append(jax.ShapeDtypeStruct((m, n_cols), F32))
        out_specs.append(pl.BlockSpec((tm, tn), lambda n, mi, te, nv: (mi, n)))
    out_shape.append(jax.ShapeDtypeStruct((n_e, s, n_cols), F32))
    out_specs.append(pl.BlockSpec((1, s, tn), lambda n, mi, te, nv: (exp(mi, te, nv), 0, n)))
    scratch = [pltpu.VMEM((k, tn), BF16) for _ in ws]
    if dil > 1:
        scratch.append(pltpu.VMEM((tn // 128, tm, 128), F32))
    kern = functools.partial(_mm_kernel, n_w=len(ws), has_scale=col_scale is not None, has_res=res is not None,
                             rope=rope, dil=dil, has_nat=natural, side_precise=side_precise)
    return pl.pallas_call(
        kern,
        grid_spec=pltpu.PrefetchScalarGridSpec(
            num_scalar_prefetch=2, grid=(n_n, n_m), in_specs=in_specs, out_specs=out_specs,
            scratch_shapes=scratch),
        out_shape=out_shape,
        compiler_params=_cparams(("arbitrary", "arbitrary")), name=name,
    )(te, nvalid, *args)


def _moe_mm_kernel(t0_ref, cnt_ref, nv_ref, x_hbm, *refs, n_w, tm, n_tiles):
    n = pl.program_id(0)
    e = pl.program_id(1)
    n_e = pl.num_programs(1)
    w_refs = refs[:n_w]
    xs_ref, o_hbm, os_ref = refs[n_w:n_w + 3]
    wb_refs = refs[n_w + 3:2 * n_w + 3]
    xbuf, obuf, sem_in, sem_out = refs[2 * n_w + 3:]
    tn = os_ref.shape[2]
    t0 = t0_ref[e]
    cnt = cnt_ref[e]

    def x_copy(tile, slot):
        return pltpu.make_async_copy(x_hbm.at[pl.ds(tile * tm, tm)], xbuf.at[slot], sem_in.at[slot])

    def o_copy(tile, slot):
        return pltpu.make_async_copy(obuf.at[slot], o_hbm.at[pl.ds(tile * tm, tm), pl.ds(n * tn, tn)],
                                     sem_out.at[slot])

    def product(xb):
        acc = jnp.dot(xb, wb_refs[0][...], preferred_element_type=F32)
        if n_w == 2:
            acc = (acc * jax.nn.sigmoid(acc)) * jnp.dot(xb, wb_refs[1][...], preferred_element_type=F32)
        return acc

    @pl.when(jnp.logical_and(n == 0, e == 0))
    def _():
        x_copy(t0, 0).start()

    for w_ref, wb_ref in zip(w_refs, wb_refs):
        wb_ref[...] = w_ref[0].astype(BF16)
    os_ref[0] = product(xs_ref[0].astype(BF16))

    def body(t, carry):
        slot = t % 2

        @pl.when(t + 1 < cnt)
        def _():
            x_copy(t0 + t + 1, 1 - slot).start()

        x_copy(t0 + t, slot).wait()

        @pl.when(t >= 2)
        def _():
            o_copy(t0 + t - 2, slot).wait()

        obuf[slot] = product(xbuf[slot]).astype(obuf.dtype)
        o_copy(t0 + t, slot).start()
        return carry

    lax.fori_loop(0, cnt, body, 0)

    last_step = jnp.logical_and(n == pl.num_programs(0) - 1, e == n_e - 1)

    @pl.when(jnp.logical_not(last_step))
    def _():
        x_copy(t0_ref[jnp.where(e == n_e - 1, 0, e + 1)], 0).start()

    @pl.when(cnt >= 2)
    def _():
        o_copy(t0 + cnt - 2, cnt % 2).wait()

    o_copy(t0 + cnt - 1, (cnt - 1) % 2).wait()

    @pl.when(e == n_e - 1)
    def _():
        obuf[0] = jnp.zeros(obuf.shape[1:], obuf.dtype)

        def fill(tile, carry):
            o_copy(tile, 0).start()
            o_copy(tile, 0).wait()
            return carry

        lax.fori_loop(nv_ref[0], n_tiles, fill, 0)


def _moe_matmul(x, ws, out_dtype, xs, t0, cnt, nvalid, *, tm, tn, xs_per_expert, name):
    r, k = x.shape
    n_e, _, n_cols = ws[0].shape
    s = xs.shape[1]
    n_w = len(ws)
    any_spec = pl.BlockSpec(memory_space=pl.ANY)
    in_specs = [any_spec]
    in_specs += [pl.BlockSpec((1, k, tn), lambda n, e, *_: (e, 0, n)) for _ in ws]
    if xs_per_expert:
        in_specs.append(pl.BlockSpec((1, s, k), lambda n, e, *_: (e, 0, 0)))
    else:
        in_specs.append(pl.BlockSpec((1, s, k), lambda n, e, *_: (0, 0, 0)))
    scratch = [pltpu.VMEM((k, tn), BF16) for _ in ws]
    scratch += [pltpu.VMEM((2, tm, k), x.dtype), pltpu.VMEM((2, tm, tn), out_dtype),
                pltpu.SemaphoreType.DMA((2,)), pltpu.SemaphoreType.DMA((2,))]
    return pl.pallas_call(
        functools.partial(_moe_mm_kernel, n_w=n_w, tm=tm, n_tiles=r // tm),
        grid_spec=pltpu.PrefetchScalarGridSpec(
            num_scalar_prefetch=3, grid=(n_cols // tn, n_e), in_specs=in_specs,
            out_specs=[any_spec, pl.BlockSpec((1, s, tn), lambda n, e, *_: (e, 0, n))],
            scratch_shapes=scratch),
        out_shape=[jax.ShapeDtypeStruct((r, n_cols), out_dtype), jax.ShapeDtypeStruct((n_e, s, n_cols), F32)],
        compiler_params=_cparams(("arbitrary", "arbitrary")), name=name,
    )(t0, cnt, nvalid, x, *ws, xs)


def _log_sigmoid(x):
    return jnp.minimum(x, 0.0) - jnp.log1p(jnp.exp(-jnp.abs(x)))


def _mlstm_prompt_kernel(q_ref, k_ref, v_ref, o_ref, gi_ref, gf_ref, bi_ref, bf_ref, bo_ref, ghn_ref,
                         h_ref, c_out_ref, n_out_ref, m_out_ref, ct_s, n_s, m_s):
    c = pl.program_id(0)
    L = ML_CHUNK

    @pl.when(c == 0)
    def _():
        ct_s[...] = jnp.zeros_like(ct_s)
        n_s[...] = jnp.zeros_like(n_s)
        m_s[...] = jnp.zeros_like(m_s)

    ig = gi_ref[...] + bi_ref[...]
    lf = _log_sigmoid(gf_ref[...] + bf_ref[...])
    r = lax.broadcasted_iota(I32, (L, L), 0)
    s = lax.broadcasted_iota(I32, (L, L), 1)
    causal = r >= s
    tril = causal.astype(F32)
    b = jnp.dot(tril, lf, precision=HIGHEST, preferred_element_type=F32)
    b_t = b.T
    ig_t = ig.T
    m_all = m_s[...]
    m_new_all = m_all
    lane = lax.broadcasted_iota(I32, (1, 128), 1)

    for h in range(ML_HEADS):
        qh = q_ref[:, h * ML_DQK:(h + 1) * ML_DQK]
        kh = k_ref[:, h * ML_DQK:(h + 1) * ML_DQK]
        vh = v_ref[:, h * ML_DV:(h + 1) * ML_DV]
        bc = b[:, h:h + 1]
        ic = ig[:, h:h + 1]
        br = b_t[h:h + 1, :]
        ir = ig_t[h:h + 1, :]
        m_h = m_all[:, h:h + 1]
        logd = jnp.where(causal, bc - br + ir, -jnp.inf)
        inter = bc + m_h
        mt = jnp.maximum(inter, jnp.max(logd, axis=1, keepdims=True))
        sc = lax.dot_general(qh, kh, (((1,), (1,)), ((), ())), preferred_element_type=F32)
        a = sc * jnp.exp(logd - mt)
        w_inter = jnp.exp(inter - mt)
        ct_h = ct_s[h]
        qc = jnp.dot(qh, ct_h.astype(BF16), preferred_element_type=F32)
        num = jnp.dot(a.astype(BF16), vh, preferred_element_type=F32) + w_inter * qc
        n_h = n_s[h:h + 1, :]
        qn = jnp.sum(qh.astype(F32) * n_h, axis=1, keepdims=True)
        den = jnp.sum(a, axis=1, keepdims=True) + w_inter * qn
        hh = num / jnp.maximum(jnp.abs(den), jnp.exp(-mt))
        hn = _rms(hh) * ghn_ref[:, h * ML_DV:(h + 1) * ML_DV]
        og = jax.nn.sigmoid(o_ref[:, h * ML_DV:(h + 1) * ML_DV].astype(F32) + bo_ref[:, h * ML_DV:(h + 1) * ML_DV])
        h_ref[:, h * ML_DV:(h + 1) * ML_DV] = (hn * og).astype(h_ref.dtype)
        m_new = mt[L - 1:L, :]
        b_last = bc[L - 1:L, :]
        decay = jnp.exp(b_last + m_h - m_new)
        wj = jnp.exp(b_last - bc + ic - m_new)
        kw = (kh.astype(F32) * wj).astype(BF16)
        upd = lax.dot_general(kw, vh, (((0,), (0,)), ((), ())), preferred_element_type=F32)
        ct_s[h] = decay * ct_h + upd
        n_s[h:h + 1, :] = decay * n_h + jnp.sum(kh.astype(F32) * wj, axis=0, keepdims=True)
        m_new_all = jnp.where(lane == h, m_new, m_new_all)

    m_s[...] = m_new_all

    @pl.when(c == pl.num_programs(0) - 1)
    def _():
        for h in range(ML_HEADS):
            c_out_ref[h] = ct_s[h].T
        n_out_ref[...] = n_s[...]
        m_out_ref[...] = m_s[...]


def _mlstm_prompt(z, gates, bi, bf, bo, ghn):
    seq = z.shape[0]
    nc = seq // ML_CHUNK
    L = ML_CHUNK
    const2 = lambda c: (0, 0)
    in_specs = [
        pl.BlockSpec((L, ML_NQK), lambda c: (c, 0)),
        pl.BlockSpec((L, ML_NQK), lambda c: (c, 1)),
        pl.BlockSpec((L, ML_NV), lambda c: (c, 1)),
        pl.BlockSpec((L, D_MODEL), lambda c: (c, 2)),
        pl.BlockSpec((L, 128), lambda c: (c, 0)),
        pl.BlockSpec((L, 128), lambda c: (c, 1)),
        pl.BlockSpec((1, 128), const2),
        pl.BlockSpec((1, 128), const2),
        pl.BlockSpec((1, D_MODEL), const2),
        pl.BlockSpec((1, ML_NV), const2),
    ]
    out_shape = [
        jax.ShapeDtypeStruct((seq, ML_NV), BF16),
        jax.ShapeDtypeStruct((ML_HEADS, ML_DV, ML_DQK), F32),
        jax.ShapeDtypeStruct((ML_HEADS, ML_DQK), F32),
        jax.ShapeDtypeStruct((1, 128), F32),
    ]
    out_specs = [
        pl.BlockSpec((L, ML_NV), lambda c: (c, 0)),
        pl.BlockSpec((ML_HEADS, ML_DV, ML_DQK), lambda c: (0, 0, 0)),
        pl.BlockSpec((ML_HEADS, ML_DQK), const2),
        pl.BlockSpec((1, 128), const2),
    ]
    return pl.pallas_call(
        _mlstm_prompt_kernel, grid=(nc,), in_specs=in_specs, out_specs=out_specs, out_shape=out_shape,
        scratch_shapes=[pltpu.VMEM((ML_HEADS, ML_DQK, ML_DV), F32), pltpu.VMEM((ML_HEADS, ML_DQK), F32),
                        pltpu.VMEM((1, 128), F32)],
        compiler_params=_cparams(("arbitrary",)), name="mlstm_prompt",
    )(z, z, z, z, gates, gates, bi, bf, bo, ghn)


def _mlstm_sample_kernel(z_ref, g_ref, c_ref, n_ref, m_ref, bi_ref, bf_ref, bo_ref, ghn_ref,
                         h_ref, c_out_ref, n_out_ref, m_out_ref):
    i = pl.program_id(0)
    z = z_ref[0]
    g = g_ref[0]
    ig_all = g[:, 0:128] + bi_ref[...]
    lf_all = _log_sigmoid(g[:, 128:256] + bf_ref[...])
    m_all = m_ref[0]
    mt_all = jnp.maximum(lf_all + m_all, ig_all)
    m_out_ref[0] = mt_all
    outs = []
    for h in range(ML_HEADS):
        q = z[:, h * ML_DQK:(h + 1) * ML_DQK]
        k = z[:, ML_NQK + h * ML_DQK:ML_NQK + (h + 1) * ML_DQK]
        v = z[:, 2 * ML_NQK + h * ML_DV:2 * ML_NQK + (h + 1) * ML_DV]
        ig = ig_all[:, h:h + 1]
        lf = lf_all[:, h:h + 1]
        m0 = m_all[:, h:h + 1]
        mt = mt_all[:, h:h + 1]
        w_inter = jnp.exp(lf + m0 - mt)
        wj = jnp.exp(ig - mt)
        a = jnp.sum(q * k, axis=1, keepdims=True) * wj
        c_h = c_ref[0, h]
        n_h = n_ref[0, h:h + 1, :]
        q8 = jnp.broadcast_to(q, (8, ML_DQK))
        cq = lax.dot_general(q8, c_h, (((1,), (1,)), ((), ())), precision=HIGHEST,
                             preferred_element_type=F32)[0:1, :]
        num = a * v + w_inter * cq
        den = a + w_inter * jnp.sum(n_h * q, axis=1, keepdims=True)
        hh = num / jnp.maximum(jnp.abs(den), jnp.exp(-mt))
        hn = _rms(hh) * ghn_ref[:, h * ML_DV:(h + 1) * ML_DV]
        og = jax.nn.sigmoid(z[:, 2 * ML_NQK + ML_NV + h * ML_DV:2 * ML_NQK + ML_NV + (h + 1) * ML_DV]
                            + bo_ref[:, h * ML_DV:(h + 1) * ML_DV])
        outs.append(hn * og)
        v_col = jnp.broadcast_to(v, (8, ML_DV)).T[:, 0:1]
        c_out_ref[0, h] = w_inter * c_h + wj * (v_col * k)
        n_out_ref[0, h:h + 1, :] = w_inter * n_h + wj * k
    h_ref[pl.ds(i, 1), :] = jnp.concatenate(outs, axis=1)


def _mlstm_sample(z_s, gates_s, c0, n0, m0, bi, bf, bo, ghn):
    ns = z_s.shape[0]
    const2 = lambda i: (0, 0)
    in_specs = [
        pl.BlockSpec((1, 1, ML_MAIN), lambda i: (i, 0, 0)),
        pl.BlockSpec((1, 1, 256), lambda i: (i, 0, 0)),
        pl.BlockSpec((1, ML_HEADS, ML_DV, ML_DQK), lambda i: (i, 0, 0, 0)),
        pl.BlockSpec((1, ML_HEADS, ML_DQK), lambda i: (i, 0, 0)),
        pl.BlockSpec((1, 1, 128), lambda i: (i, 0, 0)),
        pl.BlockSpec((1, 128), const2),
        pl.BlockSpec((1, 128), const2),
        pl.BlockSpec((1, D_MODEL), const2),
        pl.BlockSpec((1, ML_NV), const2),
    ]
    out_shape = [
        jax.ShapeDtypeStruct((ns, ML_NV), F32),
        jax.ShapeDtypeStruct((ns, ML_HEADS, ML_DV, ML_DQK), F32),
        jax.ShapeDtypeStruct((ns, ML_HEADS, ML_DQK), F32),
        jax.ShapeDtypeStruct((ns, 1, 128), F32),
    ]
    out_specs = [
        pl.BlockSpec((ns, ML_NV), const2),
        pl.BlockSpec((1, ML_HEADS, ML_DV, ML_DQK), lambda i: (i, 0, 0, 0)),
        pl.BlockSpec((1, ML_HEADS, ML_DQK), lambda i: (i, 0, 0)),
        pl.BlockSpec((1, 1, 128), lambda i: (i, 0, 0)),
    ]
    return pl.pallas_call(
        _mlstm_sample_kernel, grid=(ns,), in_specs=in_specs, out_specs=out_specs, out_shape=out_shape,
        compiler_params=_cparams(("arbitrary",)), name="mlstm_sample",
    )(z_s, gates_s, c0, n0, m0, bi, bf, bo, ghn)


def _attn_prompt_kernel(q_ref, kp_ref, kc_ref, vp_ref, vc_ref, o_ref, lse_ref):
    blk = pl.program_id(1)
    T = ATT_STEPS
    qi = lax.broadcasted_iota(I32, (T, T), 0)
    kj = lax.broadcasted_iota(I32, (T, T), 1)
    valid_prev = jnp.logical_and(kj >= qi, blk > 0)
    valid_cur = kj <= qi
    scale = ATT_HD ** -0.5
    nt = (((1,), (1,)), ((), ()))
    for h in range(ATT_H):
        sl = slice(h * ATT_HD, (h + 1) * ATT_HD)
        qh = q_ref[:, sl]
        s1 = jnp.where(valid_prev, lax.dot_general(qh, kp_ref[:, sl], nt, preferred_element_type=F32) * scale,
                       -jnp.inf)
        s2 = jnp.where(valid_cur, lax.dot_general(qh, kc_ref[:, sl], nt, preferred_element_type=F32) * scale,
                       -jnp.inf)
        mx = jnp.maximum(jnp.max(s1, axis=1, keepdims=True), jnp.max(s2, axis=1, keepdims=True))
        p1 = jnp.exp(s1 - mx)
        p2 = jnp.exp(s2 - mx)
        den = jnp.sum(p1, axis=1, keepdims=True) + jnp.sum(p2, axis=1, keepdims=True)
        acc = jnp.dot(p1.astype(BF16), vp_ref[:, sl], preferred_element_type=F32)
        acc = acc + jnp.dot(p2.astype(BF16), vc_ref[:, sl], preferred_element_type=F32)
        o_ref[:, sl] = acc / den
        lse_ref[:, sl] = jnp.broadcast_to(mx + jnp.log(den), (T, ATT_HD))


def _attn_prompt(q, kv, g):
    dil, L, _ = q.shape
    nb = L // ATT_STEPS
    T = ATT_STEPS
    blk = (None, T, ATT_GW)
    in_specs = [
        pl.BlockSpec(blk, lambda r, b: (r, b, 0)),
        pl.BlockSpec(blk, lambda r, b: (r, jnp.maximum(b - 1, 0), 0)),
        pl.BlockSpec(blk, lambda r, b: (r, b, 0)),
        pl.BlockSpec(blk, lambda r, b: (r, jnp.maximum(b - 1, 0), 1)),
        pl.BlockSpec(blk, lambda r, b: (r, b, 1)),
    ]
    out_spec = pl.BlockSpec(blk, lambda r, b: (r, b, 0))
    return pl.pallas_call(
        _attn_prompt_kernel, grid=(dil, nb), in_specs=in_specs, out_specs=[out_spec, out_spec],
        out_shape=[jax.ShapeDtypeStruct((dil, L, ATT_GW), F32)] * 2,
        compiler_params=_cparams(("arbitrary", "arbitrary")), name=f"attn_prompt_g{g}",
    )(q, kv, kv, kv, kv)


def _merge_kernel(*refs):
    in_refs, out_ref, scratch = refs[:2 * ATT_G], refs[2 * ATT_G], refs[2 * ATT_G + 1:]
    tm = out_ref.shape[0]
    for c in range(ATT_H):
        lanes = slice(c * ATT_HD, (c + 1) * ATT_HD)
        vals = []
        si = 0
        for g in range(ATT_G):
            dil = ATT_DILS[g]
            pair = []
            for ref in in_refs[2 * g:2 * g + 2]:
                if dil == 1:
                    pair.append(ref[0, :, lanes])
                else:
                    buf = scratch[si]
                    si += 1
                    for r in range(dil):
                        buf[pl.ds(r, tm // dil, stride=dil), :] = ref[r, :, lanes]
                    pair.append(buf[...])
            vals.append(pair)
        lses = [p[1] for p in vals]
        mx = jnp.maximum(jnp.maximum(lses[0], lses[1]), lses[2])
        es = [jnp.exp(l - mx) for l in lses]
        tot = es[0] + es[1] + es[2]
        out_ref[:, lanes] = ((es[0] / tot) * vals[0][0] + (es[1] / tot) * vals[1][0]
                             + (es[2] / tot) * vals[2][0]).astype(out_ref.dtype)


def _merge_groups(parts, seq, tm=512):
    in_specs, args, scratch = [], [], []
    for g, pair in enumerate(parts):
        dil = ATT_DILS[g]
        for a in pair:
            in_specs.append(pl.BlockSpec((dil, tm // dil, ATT_GW), lambda i: (0, i, 0)))
            args.append(a)
            if dil > 1:
                scratch.append(pltpu.VMEM((tm, ATT_HD), F32))
    return pl.pallas_call(
        _merge_kernel, grid=(seq // tm,), in_specs=in_specs,
        out_specs=pl.BlockSpec((tm, ATT_GW), lambda i: (i, 0)),
        out_shape=jax.ShapeDtypeStruct((seq, ATT_GW), BF16), scratch_shapes=scratch,
        compiler_params=_cparams(("arbitrary",)), name="attn_merge",
    )(*args)


def _attn_sample_kernel(q_ref, kvn_ref, b0_ref, b1_ref, b2_ref, out_ref):
    scale = ATT_HD ** -0.5
    outs, lses = [], []
    for g, b_ref in enumerate((b0_ref, b1_ref, b2_ref)):
        qg = q_ref[0, g]
        kn = kvn_ref[0, g, 0]
        vn = kvn_ref[0, g, 1]
        kb = b_ref[:, 0]
        vb = b_ref[:, 1]
        s = jnp.sum(kb * qg[None], axis=2, keepdims=True) * scale
        s_new = jnp.sum(kn * qg, axis=1, keepdims=True) * scale
        mx = jnp.maximum(jnp.max(s, axis=0), s_new)
        p = jnp.exp(s - mx[None])
        p_new = jnp.exp(s_new - mx)
        den = jnp.sum(p, axis=0) + p_new
        o = jnp.sum(p * vb, axis=0) + p_new * vn
        outs.append(o / den)
        lses.append(mx + jnp.log(den))
    mxl = jnp.maximum(jnp.maximum(lses[0], lses[1]), lses[2])
    es = [jnp.exp(l - mxl) for l in lses]
    tot = es[0] + es[1] + es[2]
    out_ref[0] = (es[0] / tot) * outs[0] + (es[1] / tot) * outs[1] + (es[2] / tot) * outs[2]


def _attn_sample(q_s, kv_s, caches):
    ns = q_s.shape[0]
    views, specs = [], []
    for g, cbuf in enumerate(caches):
        lb = cbuf.shape[1]
        dil = ATT_DILS[g]
        views.append(cbuf.reshape(ns, lb // dil, dil, 2, ATT_H, ATT_HD))
        specs.append(pl.BlockSpec((None, ATT_STEPS, None, 2, ATT_H, ATT_HD), lambda i: (i, 0, 0, 0, 0, 0)))
    in_specs = [
        pl.BlockSpec((1, ATT_G, ATT_H, ATT_HD), lambda i: (i, 0, 0, 0)),
        pl.BlockSpec((1, ATT_G, 2, ATT_H, ATT_HD), lambda i: (i, 0, 0, 0, 0)),
    ] + specs
    return pl.pallas_call(
        _attn_sample_kernel, grid=(ns,), in_specs=in_specs,
        out_specs=pl.BlockSpec((1, ATT_H, ATT_HD), lambda i: (i, 0, 0)),
        out_shape=jax.ShapeDtypeStruct((ns, ATT_H, ATT_HD), F32),
        compiler_params=_cparams(("arbitrary",)), name="attn_sample",
    )(q_s, kv_s, *views)


KV_ROW = 2 * ATT_H
KV_SHIFT_BLOCK = 8192


def _kv_shift_kernel(cur_ref, nxt_ref, new_ref, out_ref):
    blk = out_ref.shape[1]
    out_ref[0, :blk - KV_ROW] = cur_ref[0, KV_ROW:]
    last = pl.program_id(1) == pl.num_programs(1) - 1
    out_ref[0, blk - KV_ROW:] = jnp.where(last, new_ref[0], nxt_ref[0])


def _kv_shift(cache, new):
    ns, lb = cache.shape[0], cache.shape[1]
    rows = lb * KV_ROW
    blk = min(KV_SHIFT_BLOCK, rows)
    nb = rows // blk
    per = blk // KV_ROW
    flat = cache.reshape(ns, rows, ATT_HD)
    out = pl.pallas_call(
        _kv_shift_kernel, grid=(ns, nb),
        in_specs=[pl.BlockSpec((1, blk, ATT_HD), lambda i, j: (i, j, 0)),
                  pl.BlockSpec((1, KV_ROW, ATT_HD), lambda i, j: (i, jnp.minimum((j + 1) * per, lb - 1), 0)),
                  pl.BlockSpec((1, KV_ROW, ATT_HD), lambda i, j: (i, 0, 0))],
        out_specs=pl.BlockSpec((1, blk, ATT_HD), lambda i, j: (i, j, 0)),
        out_shape=jax.ShapeDtypeStruct(flat.shape, flat.dtype),
        compiler_params=_cparams(("arbitrary", "arbitrary")), name="kv_shift",
    )(flat, flat, new.reshape(ns, KV_ROW, ATT_HD))
    return out.reshape(cache.shape)


def _top2(y, wr_ref, br_ref):
    rows = y.shape[0]
    lane = lax.broadcasted_iota(I32, (rows, 128), 1)
    logits = jnp.dot(y, wr_ref[...], precision=HIGHEST, preferred_element_type=F32) + br_ref[...]
    logits = jnp.where(lane < N_EXPERTS, logits, NEG_BIG)
    e = jnp.exp(logits - jnp.max(logits, axis=1, keepdims=True))
    probs = e / jnp.sum(e, axis=1, keepdims=True)
    p1 = jnp.max(probs, axis=1, keepdims=True)
    i1 = jnp.min(jnp.where(probs == p1, lane, 128), axis=1, keepdims=True)
    probs2 = jnp.where(lane == i1, -1.0, probs)
    p2 = jnp.max(probs2, axis=1, keepdims=True)
    i2 = jnp.min(jnp.where(probs2 == p2, lane, 128), axis=1, keepdims=True)
    tot = p1 + p2
    return lane, i1, i2, p1 / tot, p2 / tot


def _router_kernel(x_ref, xs_ref, g_ref, wr_ref, br_ref, xn_ref, eid_ref, gate_ref, rank_ref, cnt_ref,
                   xns_ref, gs_ref, carry):
    i = pl.program_id(0)
    tm = x_ref.shape[0]

    @pl.when(i == 0)
    def _():
        carry[...] = jnp.zeros_like(carry)
        ys = _rms(xs_ref[...]) * g_ref[...]
        xns_ref[...] = ys
        lane, i1, i2, g1, g2 = _top2(ys, wr_ref, br_ref)
        gs_ref[...] = jnp.where(lane == i1, g1, jnp.where(lane == i2, g2, 0.0))

    y = _rms(x_ref[...]) * g_ref[...]
    xn_ref[...] = y
    lane, i1, i2, g1, g2 = _top2(y, wr_ref, br_ref)
    sel1 = lane == i1
    sel2 = lane == i2
    onehot = jnp.where(jnp.logical_or(sel1, sel2), 1.0, 0.0)
    rr = lax.broadcasted_iota(I32, (tm, tm), 0)
    cc = lax.broadcasted_iota(I32, (tm, tm), 1)
    before = (cc < rr).astype(BF16)
    prefix = jnp.dot(before, onehot.astype(BF16), preferred_element_type=F32) + carry[...]
    r1 = jnp.sum(jnp.where(sel1, prefix, 0.0), axis=1, keepdims=True)
    r2 = jnp.sum(jnp.where(sel2, prefix, 0.0), axis=1, keepdims=True)
    carry[...] = carry[...] + jnp.sum(onehot, axis=0, keepdims=True)
    eid_ref[...] = jnp.where(lane == 0, i1, jnp.where(lane == 1, i2, 0))
    gate_ref[...] = jnp.where(lane == 0, g1, jnp.where(lane == 1, g2, 0.0))
    rank_ref[...] = jnp.where(lane == 0, r1, jnp.where(lane == 1, r2, 0.0)).astype(I32)
    cnt_ref[...] = jnp.broadcast_to(carry[...], cnt_ref.shape)


def _router(h, hs, gain, w_router_pad, b_router_pad, tm=ROW_TILE):
    m, d = h.shape
    s = hs.shape[0]
    const = lambda i: (0, 0)
    row_spec = pl.BlockSpec((tm, 128), lambda i: (i, 0))
    return pl.pallas_call(
        _router_kernel, grid=(m // tm,),
        in_specs=[pl.BlockSpec((tm, d), lambda i: (i, 0)), pl.BlockSpec((s, d), const), pl.BlockSpec((1, d), const),
                  pl.BlockSpec((d, 128), const), pl.BlockSpec((1, 128), const)],
        out_specs=[pl.BlockSpec((tm, d), lambda i: (i, 0)), row_spec, row_spec, row_spec,
                   pl.BlockSpec((8, 128), const), pl.BlockSpec((s, d), const), pl.BlockSpec((s, 128), const)],
        out_shape=[jax.ShapeDtypeStruct((m, d), F32), jax.ShapeDtypeStruct((m, 128), I32),
                   jax.ShapeDtypeStruct((m, 128), F32), jax.ShapeDtypeStruct((m, 128), I32),
                   jax.ShapeDtypeStruct((8, 128), F32), jax.ShapeDtypeStruct((s, d), F32),
                   jax.ShapeDtypeStruct((s, 128), F32)],
        scratch_shapes=[pltpu.VMEM((1, 128), F32)],
        compiler_params=_cparams(("arbitrary",)), name="router",
    )(h, hs, gain, w_router_pad, b_router_pad)


def _dispatch_kernel(pos_ref, nv_ref, x_hbm, out_ref, inv, buf, sem, *, n_tok):
    i = pl.program_id(0)
    tg = out_ref.shape[0]

    @pl.when(i == 0)
    def _():
        def clear(s, c):
            inv[s] = 0
            return c

        lax.fori_loop(0, inv.shape[0], clear, 0, unroll=8)

        def fill(t, c):
            inv[pos_ref[2 * t]] = t
            inv[pos_ref[2 * t + 1]] = t
            return c

        lax.fori_loop(0, n_tok, fill, 0, unroll=8)

    def start_gather(tile):
        slot = tile % 2

        def issue(r, c):
            pltpu.make_async_copy(x_hbm.at[pl.ds(inv[tile * tg + r], 1)], buf.at[slot, pl.ds(r, 1)],
                                  sem.at[slot]).start()
            return c

        lax.fori_loop(0, tg, issue, 0, unroll=8)

    @pl.when(i == 0)
    def _():
        start_gather(i)

    @pl.when(i + 1 < nv_ref[0])
    def _():
        start_gather(i + 1)

    @pl.when(i < nv_ref[0])
    def _():
        slot = i % 2
        pltpu.make_async_copy(x_hbm.at[pl.ds(0, tg)], buf.at[slot], sem.at[slot]).wait()
        out_ref[...] = buf[slot].astype(out_ref.dtype)

    @pl.when(i >= nv_ref[0])
    def _():
        out_ref[...] = jnp.zeros_like(out_ref)


def _dispatch(pos_flat, nvalid, xn, n_tiles, tg):
    n_tok, d = xn.shape
    return pl.pallas_call(
        functools.partial(_dispatch_kernel, n_tok=n_tok),
        grid_spec=pltpu.PrefetchScalarGridSpec(
            num_scalar_prefetch=2, grid=(n_tiles,),
            in_specs=[pl.BlockSpec(memory_space=pl.ANY)],
            out_specs=pl.BlockSpec((tg, d), lambda i, pos, nv: (i, 0)),
            scratch_shapes=[pltpu.SMEM((n_tiles * tg,), I32), pltpu.VMEM((2, tg, d), xn.dtype),
                            pltpu.SemaphoreType.DMA((2,))]),
        out_shape=jax.ShapeDtypeStruct((n_tiles * tg, d), BF16),
        compiler_params=_cparams(("arbitrary",)), name="moe_dispatch",
    )(pos_flat, nvalid, xn)


def _combine_kernel(pos_ref, h_ref, gate_ref, g_ref, hs_ref, gs_ref, ys_s_ref, ys_hbm, out_ref, outs_ref, ybuf, sem):
    i = pl.program_id(0)
    tm = h_ref.shape[0]

    def start_gather(tile):
        slot = tile % 2

        def issue(r, carry):
            for j in range(2):
                pltpu.make_async_copy(ys_hbm.at[pl.ds(pos_ref[2 * (tile * tm + r) + j], 1)],
                                      ybuf.at[slot, j, pl.ds(r, 1)], sem.at[slot]).start()
            return carry

        lax.fori_loop(0, tm, issue, 0, unroll=4)

    @pl.when(i == 0)
    def _():
        start_gather(i)

    @pl.when(i + 1 < pl.num_programs(0))
    def _():
        start_gather(i + 1)

    @pl.when(i == 0)
    def _():
        gs = gs_ref[...]
        y = jnp.zeros(hs_ref.shape, F32)
        for e in range(N_EXPERTS):
            y = y + gs[:, e:e + 1] * ys_s_ref[e]
        outs_ref[...] = _rms(hs_ref[...] + y) * g_ref[...]

    slot = i % 2
    for j in range(2):
        pltpu.make_async_copy(ys_hbm.at[pl.ds(0, tm)], ybuf.at[slot, j], sem.at[slot]).wait()
    gate = gate_ref[...]
    y = h_ref[...] + (gate[:, 0:1] * ybuf[slot, 0] + gate[:, 1:2] * ybuf[slot, 1])
    out_ref[...] = _rms(y) * g_ref[...]


def _combine(pos_flat, h, gate, g_final, hs, gs, ys_s, ys, tm=ROW_TILE):
    m, d = h.shape
    s = hs.shape[0]
    c2 = lambda i, pos: (0, 0)
    return pl.pallas_call(
        _combine_kernel,
        grid_spec=pltpu.PrefetchScalarGridSpec(
            num_scalar_prefetch=1, grid=(m // tm,),
            in_specs=[pl.BlockSpec((tm, d), lambda i, pos: (i, 0)),
                      pl.BlockSpec((tm, 128), lambda i, pos: (i, 0)),
                      pl.BlockSpec((1, d), c2),
                      pl.BlockSpec((s, d), c2),
                      pl.BlockSpec((s, 128), c2),
                      pl.BlockSpec((N_EXPERTS, s, d), lambda i, pos: (0, 0, 0)),
                      pl.BlockSpec(memory_space=pl.ANY)],
            out_specs=[pl.BlockSpec((tm, d), lambda i, pos: (i, 0)), pl.BlockSpec((s, d), c2)],
            scratch_shapes=[pltpu.VMEM((2, 2, tm, d), F32), pltpu.SemaphoreType.DMA((2,))]),
        out_shape=[jax.ShapeDtypeStruct((m, d), F32), jax.ShapeDtypeStruct((s, d), F32)],
        compiler_params=_cparams(("arbitrary",)), name="moe_combine",
    )(pos_flat, h, gate, g_final, hs, gs, ys_s, ys)


def _rope_tables(pos):
    half = ATT_HD // 2
    inv = ROPE_THETA ** (-jnp.arange(half, dtype=F32) / half)
    ang = pos.astype(F32)[:, None] * inv[None, :]
    cos, sin = jnp.cos(ang), jnp.sin(ang)
    return jnp.concatenate([cos, cos], axis=1), jnp.concatenate([-sin, sin], axis=1)


def kernel(x_prompt, x_sample, state_mlstm_C, state_mlstm_n, state_mlstm_m, cache_kv_w128, cache_kv_w512, cache_kv_w2048, g_mix, g_ffn, w_ml_in, b_ml_gates, b_ml_o, g_ml_hnorm, w_ml_out, g_kv, w_kv, w_q, w_o, w_ffn_gate, w_ffn_up, w_ffn_down, w_router, b_router, w_exp_gate, w_exp_up, w_exp_down, g_final):
    bp, seq, d = x_prompt.shape
    ns = x_sample.shape[0]
    caches = (cache_kv_w128, cache_kv_w512, cache_kv_w2048)
    assert bp == 1 and x_sample.shape[1] == 1 and d == D_MODEL and ns % 8 == 0
    assert seq % (ATT_STEPS * max(ATT_DILS)) == 0 and seq % 1024 == 0
    assert all(c.shape[1] == w for c, w in zip(caches, ATT_WINDOWS))
    tm = 512

    h0 = x_prompt.reshape(seq, d)
    h0_s = x_sample.reshape(ns, d)

    w_gates = w_ml_in[0][:, ML_MAIN:]
    w_gates_pad = jnp.zeros((d, 256), F32).at[:, 0:ML_HEADS].set(w_gates[:, :ML_HEADS])
    w_gates_pad = w_gates_pad.at[:, 128:128 + ML_HEADS].set(w_gates[:, ML_HEADS:])
    bi = jnp.zeros((1, 128), F32).at[0, :ML_HEADS].set(b_ml_gates[0, :ML_HEADS])
    bf = jnp.zeros((1, 128), F32).at[0, :ML_HEADS].set(b_ml_gates[0, ML_HEADS:])
    (xn0, gates), (xn0_s, gates_s) = _rmsnorm(h0, h0_s, g_mix[0:1], proj=w_gates_pad)
    k_scale = jnp.concatenate([jnp.ones((1, ML_NQK), F32), jnp.full((1, ML_NQK), ML_DQK ** -0.5, F32),
                               jnp.ones((1, ML_NV + D_MODEL), F32)], axis=1)
    z, z_s = _matmul(xn0, [w_ml_in], ML_MAIN, BF16, xn0_s[None], tm=1024, tn=1024, col_scale=k_scale, name="ml_in")
    bo = b_ml_o[0:1]
    ghn = g_ml_hnorm[0:1]
    hg, p_c, p_n, p_m = _mlstm_prompt(z[0], gates, bi, bf, bo, ghn)
    m0 = jnp.zeros((ns, 1, 128), F32).at[:, 0, :ML_HEADS].set(state_mlstm_m[0])
    hg_s, s_c, s_n, s_m = _mlstm_sample(z_s[0].reshape(ns, 1, ML_MAIN), gates_s.reshape(ns, 1, 256),
                                        state_mlstm_C[0], state_mlstm_n[0], m0, bi, bf, bo, ghn)
    h1, h1_s = _matmul(hg, [w_ml_out], d, F32, hg_s[None], tm=tm, tn=1024, res=h0, res_s=h0_s, name="ml_out")
    h1, h1_s = h1[0], h1_s[0]

    (xf0,), (xf0_s,) = _rmsnorm(h1, h1_s, g_ffn[0:1])
    ffn_dense = w_ffn_gate.shape[2]
    hid, hid_s = _matmul(xf0, [w_ffn_gate, w_ffn_up], ffn_dense, BF16, xf0_s[None], tm=1024, tn=512, name="ffn_up")
    h2, h2_s = _matmul(hid[0], [w_ffn_down], d, F32, hid_s, tm=ROW_TILE, tn=512, res=h1, res_s=h1_s,
                       name="ffn_down")
    h2, h2_s = h2[0], h2_s[0]

    (xq, xkv), (xq_s, xkv_s) = _rmsnorm(h2, h2_s, jnp.stack([g_mix[1], g_kv]))
    cos, sin = _rope_tables(jnp.arange(seq))
    cos_s, sin_s = _rope_tables(jnp.full((ns,), PAST_LEN, I32))
    rope_args = dict(cos=cos, sin=sin, cos_s=cos_s, sin_s=sin_s)
    parts, kv_nat, kv_new, q_new = [], [], [], []
    for g in range(ATT_G):
        dil = ATT_DILS[g]
        kvd, kvn, kv_s = _matmul(xkv, [w_kv[None]], 2 * ATT_GW, BF16, xkv_s[None], tm=tm, tn=ATT_GW,
                                 col_off=2 * g, rope="even", dil=dil, natural=True, name=f"kv_proj_g{g}",
                                 **rope_args)
        qd, q_s = _matmul(xq, [w_q], ATT_GW, BF16, xq_s[None], tm=tm, tn=ATT_GW, col_off=g, rope="all",
                          dil=dil, name=f"q_proj_g{g}", **rope_args)
        parts.append(_attn_prompt(qd, kvd, g))
        kv_nat.append(kvn)
        kv_new.append(kv_s[0])
        q_new.append(q_s[0])
    att = _merge_groups(parts, seq)
    att_s = _attn_sample(jnp.stack(q_new, axis=1).reshape(ns, ATT_G, ATT_H, ATT_HD),
                         jnp.stack(kv_new, axis=1).reshape(ns, ATT_G, 2, ATT_H, ATT_HD), caches)
    h3, h3_s = _matmul(att, [w_o], d, F32, att_s.reshape(1, ns, ATT_GW), tm=tm, tn=1024, res=h2, res_s=h2_s,
                       name="attn_out")
    h3, h3_s = h3[0], h3_s[0]

    wr_pad = jnp.zeros((d, 128), F32).at[:, :N_EXPERTS].set(w_router[0])
    br_pad = jnp.zeros((1, 128), F32).at[0, :N_EXPERTS].set(b_router[0])
    xn2, eid, gate, rank, cnt, xn2_s, gates_moe_s = _router(h3, h3_s, g_ffn[1:2], wr_pad, br_pad)
    tg = MOE_TILE
    n_tiles = -(-(2 * seq + N_EXPERTS * (tg - 1)) // tg)
    counts = cnt[0, :N_EXPERTS].astype(I32)
    padded = jnp.maximum((counts + tg - 1) // tg, 1) * tg
    gend = jnp.cumsum(padded)
    gstart = gend - padded
    pos = (gstart[eid[:, :2]] + rank[:, :2]).astype(I32).reshape(-1)
    nvalid = (gend[-1] // tg).astype(I32).reshape(1)
    xs = _dispatch(pos, nvalid, xn2, n_tiles, tg)
    t0 = (gstart // tg).astype(I32)
    tcnt = (padded // tg).astype(I32)
    hs, hs_s = _moe_matmul(xs, [w_exp_gate[0], w_exp_up[0]], BF16, xn2_s[None], t0, tcnt, nvalid, tm=tg, tn=1024,
                           xs_per_expert=False, name="moe_up")
    ys, ys_s = _moe_matmul(hs, [w_exp_down[0]], F32, hs_s, t0, tcnt, nvalid, tm=tg, tn=512,
                           xs_per_expert=True, name="moe_down")
    y_p, y_s = _combine(pos, h3, gate, g_final.reshape(1, d), h3_s, gates_moe_s, ys_s, ys)

    p_bufs, s_bufs = [], []
    for g in range(ATT_G):
        keep = min(ATT_WINDOWS[g], seq)
        p_bufs.append(kv_nat[g][seq - keep:].reshape(1, keep, 2, ATT_H, ATT_HD))
        s_bufs.append(_kv_shift(caches[g], kv_new[g]))
    return (y_p.reshape(1, seq, d), y_s.reshape(ns, 1, d),
            p_c[None, None], p_n[None, None], p_m[:, :ML_HEADS][None],
            s_c[None], s_n[None], s_m[:, 0, :ML_HEADS][None],
            p_bufs[0], p_bufs[1], p_bufs[2], s_bufs[0], s_bufs[1], s_bufs[2])
```

```python
import functools

import jax
import jax.numpy as jnp
from jax import lax
from jax.experimental import pallas as pl
from jax.experimental.pallas import tpu as pltpu

F32 = jnp.float32
BF16 = jnp.bfloat16
I32 = jnp.int32
HIGHEST = lax.Precision.HIGHEST

D_MODEL = 2048
ML_HEADS = 8
ML_DQK = 128
ML_DV = 256
ML_NQK = ML_HEADS * ML_DQK
ML_NV = ML_HEADS * ML_DV
ML_MAIN = 2 * ML_NQK + ML_NV + D_MODEL
ML_CHUNK = 128
ATT_HD = 128
ATT_H = 8
ATT_G = 3
ATT_WINDOWS = (128, 512, 2048)
ATT_DILS = (1, 4, 16)
ATT_STEPS = 128
ATT_GW = ATT_H * ATT_HD
ROPE_THETA = 10000.0
PAST_LEN = 8192
N_EXPERTS = 8
RMS_EPS = 1e-6
NEG_BIG = -1e30

VMEM_LIMIT_BYTES = 58 * 1024 * 1024
ROW_TILE = 256
MOE_TILE = 256


def _cparams(sem):
    return pltpu.CompilerParams(dimension_semantics=sem, vmem_limit_bytes=VMEM_LIMIT_BYTES)


def _rms(x):
    return x * lax.rsqrt(jnp.mean(x * x, axis=-1, keepdims=True) + RMS_EPS)


def _norm_kernel(x_ref, xs_ref, g_ref, *refs, n_out, has_proj):
    ins = 1 if has_proj else 0
    n_each = n_out + ins
    main = refs[ins:ins + n_each]
    side = refs[ins + n_each:ins + 2 * n_each]

    def emit(x, outs):
        y = _rms(x)
        for i in range(n_out):
            outs[i][...] = (y * g_ref[i:i + 1, :]).astype(outs[i].dtype)
        if has_proj:
            outs[n_out][...] = jnp.dot(y * g_ref[0:1, :], refs[0][...], precision=HIGHEST,
                                       preferred_element_type=F32)

    emit(x_ref[...], main)

    @pl.when(pl.program_id(0) == 0)
    def _():
        emit(xs_ref[...], side)


def _rmsnorm(x, xs, gains, proj=None, tm=ROW_TILE):
    m, d = x.shape
    s = xs.shape[0]
    n_out = gains.shape[0]
    const = lambda i: (0, 0)
    in_specs = [pl.BlockSpec((tm, d), lambda i: (i, 0)), pl.BlockSpec((s, d), const),
                pl.BlockSpec((n_out, d), const)]
    args = [x, xs, gains]
    main_shape = [jax.ShapeDtypeStruct((m, d), BF16)] * n_out
    main_specs = [pl.BlockSpec((tm, d), lambda i: (i, 0))] * n_out
    side_shape = [jax.ShapeDtypeStruct((s, d), F32)] * n_out
    side_specs = [pl.BlockSpec((s, d), const)] * n_out
    if proj is not None:
        p = proj.shape[1]
        in_specs.append(pl.BlockSpec((d, p), const))
        args.append(proj)
        main_shape.append(jax.ShapeDtypeStruct((m, p), F32))
        main_specs.append(pl.BlockSpec((tm, p), lambda i: (i, 0)))
        side_shape.append(jax.ShapeDtypeStruct((s, p), F32))
        side_specs.append(pl.BlockSpec((s, p), const))
    outs = pl.pallas_call(
        functools.partial(_norm_kernel, n_out=n_out, has_proj=proj is not None),
        grid=(m // tm,), in_specs=in_specs, out_specs=main_specs + side_specs,
        out_shape=main_shape + side_shape,
        compiler_params=_cparams(("arbitrary",)), name="rmsnorm",
    )(*args)
    k = len(main_shape)
    return outs[:k], outs[k:]


def _rope_heads(acc, cos, sin):
    outs = []
    for h in range(acc.shape[1] // ATT_HD):
        a = acc[:, h * ATT_HD:(h + 1) * ATT_HD]
        outs.append(a * cos + pltpu.roll(a, ATT_HD // 2, 1) * sin)
    return jnp.concatenate(outs, axis=1)


def _mm_kernel(te_ref, nv_ref, x_ref, *refs, n_w, has_scale, has_res, rope, dil, has_nat, side_precise):
    n = pl.program_id(0)
    m = pl.program_id(1)
    it = iter(refs)
    w_refs = [next(it) for _ in range(n_w)]
    scale_ref = next(it) if has_scale else None
    res_ref = next(it) if has_res else None
    cos_ref, sin_ref = (next(it), next(it)) if rope else (None, None)
    xs_ref = next(it)
    res_s_ref = next(it) if has_res else None
    cos_s_ref, sin_s_ref = (next(it), next(it)) if rope else (None, None)
    o_ref = next(it)
    nat_ref = next(it) if has_nat else None
    os_ref = next(it)
    wb_refs = [next(it) for _ in range(n_w)]
    deint = next(it) if dil > 1 else None

    def finish(acc, up, res, cos, sin, store):
        if n_w == 2:
            acc = (acc * jax.nn.sigmoid(acc)) * up
        if has_scale:
            acc = acc * scale_ref[...]
        if has_res:
            acc = acc + res
        if rope == "all":
            store(_rope_heads(acc, cos, sin))
        elif rope == "even":
            @pl.when(n % 2 == 0)
            def _():
                store(_rope_heads(acc, cos, sin))

            @pl.when(n % 2 == 1)
            def _():
                store(acc)
        else:
            store(acc)

    def store_side(val):
        os_ref[0] = val

    def store_main(val):
        if has_nat:
            nat_ref[...] = val
        if dil == 1:
            o_ref[0] = val.astype(o_ref.dtype)
        else:
            rows = deint.shape[1] // dil
            for c in range(deint.shape[0]):
                lanes = slice(c * 128, (c + 1) * 128)
                deint[c] = val[:, lanes]
                for r in range(dil):
                    o_ref[r, :, lanes] = deint[c, pl.ds(r, rows, stride=dil), :].astype(o_ref.dtype)

    prev = jnp.maximum(m - 1, 0)
    new_weights = jnp.logical_or(m == 0, te_ref[m] != te_ref[prev])

    @pl.when(new_weights)
    def _():
        for w_ref, wb_ref in zip(w_refs, wb_refs):
            wb_ref[...] = w_ref[0].astype(BF16)
        if side_precise:
            xs = xs_ref[0]
            acc = jnp.dot(xs, w_refs[0][0], precision=HIGHEST, preferred_element_type=F32)
            up = jnp.dot(xs, w_refs[1][0], precision=HIGHEST, preferred_element_type=F32) if n_w == 2 else None
        else:
            xs = xs_ref[0].astype(BF16)
            acc = jnp.dot(xs, wb_refs[0][...], preferred_element_type=F32)
            up = jnp.dot(xs, wb_refs[1][...], preferred_element_type=F32) if n_w == 2 else None
        finish(acc, up, res_s_ref[...] if has_res else None,
               cos_s_ref[...] if rope else None, sin_s_ref[...] if rope else None, store_side)

    @pl.when(m >= nv_ref[0])
    def _():
        o_ref[...] = jnp.zeros_like(o_ref)

    @pl.when(m < nv_ref[0])
    def _():
        xb = x_ref[...].astype(BF16)
        acc = jnp.dot(xb, wb_refs[0][...], preferred_element_type=F32)
        up = jnp.dot(xb, wb_refs[1][...], preferred_element_type=F32) if n_w == 2 else None
        finish(acc, up, res_ref[...] if has_res else None,
               cos_ref[...] if rope else None, sin_ref[...] if rope else None, store_main)


def _matmul(x, ws, n_cols, out_dtype, xs, *, tm, tn, col_off=0, te=None, nvalid=None, xs_per_expert=False,
            col_scale=None, res=None, res_s=None, rope=None, cos=None, sin=None, cos_s=None, sin_s=None,
            dil=1, natural=False, side_precise=True, name="matmul"):
    m, k = x.shape
    s = xs.shape[1]
    n_m = m // tm
    n_n = n_cols // tn
    n_e = ws[0].shape[0]
    if te is None:
        te = jnp.zeros((n_m,), I32)
        nvalid = jnp.full((1,), n_m, I32)

    def row(mi, nv):
        return jnp.minimum(mi, nv[0] - 1)

    def exp(mi, te, nv):
        return te[row(mi, nv)]

    in_specs = [pl.BlockSpec((tm, k), lambda n, mi, te, nv: (row(mi, nv), 0))]
    args = [x]
    for w in ws:
        in_specs.append(pl.BlockSpec((1, k, tn), lambda n, mi, te, nv: (exp(mi, te, nv), 0, n + col_off)))
        args.append(w)
    if col_scale is not None:
        in_specs.append(pl.BlockSpec((1, tn), lambda n, mi, te, nv: (0, n)))
        args.append(col_scale)
    if res is not None:
        in_specs.append(pl.BlockSpec((tm, tn), lambda n, mi, te, nv: (row(mi, nv), n)))
        args.append(res)
    if rope is not None:
        for t in (cos, sin):
            in_specs.append(pl.BlockSpec((tm, ATT_HD), lambda n, mi, te, nv: (row(mi, nv), 0)))
            args.append(t)
    if xs_per_expert:
        in_specs.append(pl.BlockSpec((1, s, k), lambda n, mi, te, nv: (exp(mi, te, nv), 0, 0)))
    else:
        in_specs.append(pl.BlockSpec((1, s, k), lambda n, mi, te, nv: (0, 0, 0)))
    args.append(xs)
    if res is not None:
        in_specs.append(pl.BlockSpec((s, tn), lambda n, mi, te, nv: (0, n)))
        args.append(res_s)
    if rope is not None:
        for t in (cos_s, sin_s):
            in_specs.append(pl.BlockSpec((s, ATT_HD), lambda n, mi, te, nv: (0, 0)))
            args.append(t)

    out_shape = [jax.ShapeDtypeStruct((dil, m // dil, n_cols), out_dtype)]
    out_specs = [pl.BlockSpec((dil, tm // dil, tn), lambda n, mi, te, nv: (0, mi, n))]
    if natural:
        out_shape.append(jax.ShapeDtypeStruct((m, n_cols), F32))
        out_specs.append(pl.BlockSpec((tm, tn), lambda n, mi, te, nv: (mi, n)))
    out_shape.append(jax.ShapeDtypeStruct((n_e, s, n_cols), F32))
    out_specs.append(pl.BlockSpec((1, s, tn), lambda n, mi, te, nv: (exp(mi, te, nv), 0, n)))
    scratch = [pltpu.VMEM((k, tn), BF16) for _ in ws]
    if dil > 1:
        scratch.append(pltpu.VMEM((tn // 128, tm, 128), F32))
    kern = functools.partial(_mm_kernel, n_w=len(ws), has_scale=col_scale is not None, has_res=res is not None,
                             rope=rope, dil=dil, has_nat=natural, side_precise=side_precise)
    return pl.pallas_call(
        kern,
        grid_spec=pltpu.PrefetchScalarGridSpec(
            num_scalar_prefetch=2, grid=(n_n, n_m), in_specs=in_specs, out_specs=out_specs,
            scratch_shapes=scratch),
        out_shape=out_shape,
        compiler_params=_cparams(("arbitrary", "arbitrary")), name=name,
    )(te, nvalid, *args)


KV_ROW = 2 * ATT_H
SHIFT_SEQS = 2
SHIFT_MAX_ROWS = 128


def _shift_chunk_rows(window):
    moved = window - 1
    return max(d for d in range(1, SHIFT_MAX_ROWS + 1) if moved % d == 0)


class _WindowShift:
    def __init__(self, buf_hbm, new_hbm, out_hbm, stage, sem_in, sem_out, sem_new, next_chunk):
        self.buf, self.new, self.out, self.stage = buf_hbm, new_hbm, out_hbm, stage
        self.sem_in, self.sem_out, self.sem_new, self.next_chunk = sem_in, sem_out, sem_new, next_chunk
        ns, self.rows = buf_hbm.shape[0], buf_hbm.shape[1]
        self.chunk = stage.shape[2]
        self.per_group = (self.rows - KV_ROW) // self.chunk
        self.total = (ns // SHIFT_SEQS) * self.per_group

    def _load(self, k, slot):
        seqs = pl.ds((k // self.per_group) * SHIFT_SEQS, SHIFT_SEQS)
        src = self.buf.at[seqs, pl.ds(KV_ROW + (k % self.per_group) * self.chunk, self.chunk)]
        return pltpu.make_async_copy(src, self.stage.at[slot], self.sem_in.at[slot])

    def _store(self, k, slot):
        seqs = pl.ds((k // self.per_group) * SHIFT_SEQS, SHIFT_SEQS)
        dst = self.out.at[seqs, pl.ds((k % self.per_group) * self.chunk, self.chunk)]
        return pltpu.make_async_copy(self.stage.at[slot], dst, self.sem_out.at[slot])

    def _append(self):
        return pltpu.make_async_copy(self.new, self.out.at[:, pl.ds(self.rows - KV_ROW, KV_ROW)], self.sem_new.at[0])

    def begin(self):
        self.next_chunk[0] = 0
        self._load(0, 0).start(priority=1)
        self._append().start()

    def advance(self):
        k = self.next_chunk[0]

        @pl.when(k < self.total)
        def _():
            slot = k % 2
            self._load(k, slot).wait()

            @pl.when(k >= 1)
            def _():
                self._store(k - 1, 1 - slot).wait()

            self._store(k, slot).start()

            @pl.when(k + 1 < self.total)
            def _():
                self._load(k + 1, 1 - slot).start(priority=1)

            self.next_chunk[0] = k + 1

    def finish(self):
        def rest(i, carry):
            self.advance()
            return carry

        lax.fori_loop(self.next_chunk[0], self.total, rest, 0)
        self._store(self.total - 1, (self.total - 1) % 2).wait()
        self._append().wait()


def _moe_mm_kernel(t0_ref, cnt_ref, nv_ref, x_hbm, *refs, n_w, tm, n_tiles, n_bg):
    n = pl.program_id(0)
    e = pl.program_id(1)
    n_e = pl.num_programs(1)
    it = iter(refs)
    w_refs = [next(it) for _ in range(n_w)]
    xs_ref = next(it)
    bg_in = [next(it) for _ in range(2 * n_bg)]
    o_hbm, os_ref = next(it), next(it)
    bg_out = [next(it) for _ in range(n_bg)]
    wb_refs = [next(it) for _ in range(n_w)]
    xbuf, obuf, sem_in, sem_out = next(it), next(it), next(it), next(it)
    tn = os_ref.shape[2]
    t0 = t0_ref[e]
    cnt = cnt_ref[e]
    first_step = jnp.logical_and(n == 0, e == 0)
    last_step = jnp.logical_and(n == pl.num_programs(0) - 1, e == n_e - 1)

    shift = _WindowShift(bg_in[0], bg_in[1], bg_out[0], *[next(it) for _ in range(5)]) if n_bg else None
    if shift:
        pl.when(first_step)(shift.begin)

    def x_copy(tile, slot):
        return pltpu.make_async_copy(x_hbm.at[pl.ds(tile * tm, tm)], xbuf.at[slot], sem_in.at[slot])

    def o_copy(tile, slot):
        return pltpu.make_async_copy(obuf.at[slot], o_hbm.at[pl.ds(tile * tm, tm), pl.ds(n * tn, tn)],
                                     sem_out.at[slot])

    def product(xb):
        acc = jnp.dot(xb, wb_refs[0][...], preferred_element_type=F32)
        if n_w == 2:
            acc = (acc * jax.nn.sigmoid(acc)) * jnp.dot(xb, wb_refs[1][...], preferred_element_type=F32)
        return acc

    @pl.when(first_step)
    def _():
        x_copy(t0, 0).start(priority=1)

    for w_ref, wb_ref in zip(w_refs, wb_refs):
        wb_ref[...] = w_ref[0].astype(BF16)
    os_ref[0] = product(xs_ref[0].astype(BF16))

    def body(t, carry):
        slot = t % 2

        @pl.when(t + 1 < cnt)
        def _():
            x_copy(t0 + t + 1, 1 - slot).start(priority=1)

        x_copy(t0 + t, slot).wait()

        @pl.when(t >= 2)
        def _():
            o_copy(t0 + t - 2, slot).wait()

        obuf[slot] = product(xbuf[slot]).astype(obuf.dtype)
        o_copy(t0 + t, slot).start()
        if shift:
            shift.advance()
        return carry

    lax.fori_loop(0, cnt, body, 0)

    @pl.when(jnp.logical_not(last_step))
    def _():
        x_copy(t0_ref[jnp.where(e == n_e - 1, 0, e + 1)], 0).start(priority=1)

    @pl.when(cnt >= 2)
    def _():
        o_copy(t0 + cnt - 2, cnt % 2).wait()

    o_copy(t0 + cnt - 1, (cnt - 1) % 2).wait()

    @pl.when(e == n_e - 1)
    def _():
        obuf[0] = jnp.zeros(obuf.shape[1:], obuf.dtype)

        def fill(tile, carry):
            o_copy(tile, 0).start()
            o_copy(tile, 0).wait()
            return carry

        lax.fori_loop(nv_ref[0], n_tiles, fill, 0)

    if shift:
        pl.when(last_step)(shift.finish)


def _moe_matmul(x, ws, out_dtype, xs, t0, cnt, nvalid, *, tm, tn, xs_per_expert, name, window=None):
    r, k = x.shape
    n_e, _, n_cols = ws[0].shape
    s = xs.shape[1]
    n_w = len(ws)
    windows = []
    if window is not None:
        cache, new = window
        ns, lb = cache.shape[0], cache.shape[1]
        assert ns % SHIFT_SEQS == 0
        windows = [(cache.reshape(ns, lb * KV_ROW, ATT_HD), new.reshape(ns, KV_ROW, ATT_HD))]
        chunk = _shift_chunk_rows(lb) * KV_ROW
    n_bg = len(windows)
    any_spec = pl.BlockSpec(memory_space=pl.ANY)
    in_specs = [any_spec]
    in_specs += [pl.BlockSpec((1, k, tn), lambda n, e, *_: (e, 0, n)) for _ in ws]
    if xs_per_expert:
        in_specs.append(pl.BlockSpec((1, s, k), lambda n, e, *_: (e, 0, 0)))
    else:
        in_specs.append(pl.BlockSpec((1, s, k), lambda n, e, *_: (0, 0, 0)))
    in_specs += [any_spec] * (2 * n_bg)
    scratch = [pltpu.VMEM((k, tn), BF16) for _ in ws]
    scratch += [pltpu.VMEM((2, tm, k), x.dtype), pltpu.VMEM((2, tm, tn), out_dtype),
                pltpu.SemaphoreType.DMA((2,)), pltpu.SemaphoreType.DMA((2,))]
    if n_bg:
        scratch += [pltpu.VMEM((2, SHIFT_SEQS, chunk, ATT_HD), F32), pltpu.SemaphoreType.DMA((2,)),
                    pltpu.SemaphoreType.DMA((2,)), pltpu.SemaphoreType.DMA((1,)), pltpu.SMEM((1,), I32)]
    out_shape = [jax.ShapeDtypeStruct((r, n_cols), out_dtype), jax.ShapeDtypeStruct((n_e, s, n_cols), F32)]
    out_shape += [jax.ShapeDtypeStruct(buf.shape, buf.dtype) for buf, _ in windows]
    outs = pl.pallas_call(
        functools.partial(_moe_mm_kernel, n_w=n_w, tm=tm, n_tiles=r // tm, n_bg=n_bg),
        grid_spec=pltpu.PrefetchScalarGridSpec(
            num_scalar_prefetch=3, grid=(n_cols // tn, n_e), in_specs=in_specs,
            out_specs=[any_spec, pl.BlockSpec((1, s, tn), lambda n, e, *_: (e, 0, n))] + [any_spec] * n_bg,
            scratch_shapes=scratch),
        out_shape=out_shape,
        compiler_params=_cparams(("arbitrary", "arbitrary")), name=name,
    )(t0, cnt, nvalid, x, *ws, xs, *[a for pair in windows for a in pair])
    if window is not None:
        return outs[0], outs[1], outs[2].reshape(window[0].shape)
    return outs


def _log_sigmoid(x):
    return jnp.minimum(x, 0.0) - jnp.log1p(jnp.exp(-jnp.abs(x)))


def _mlstm_prompt_kernel(q_ref, k_ref, v_ref, o_ref, gi_ref, gf_ref, bi_ref, bf_ref, bo_ref, ghn_ref,
                         h_ref, c_out_ref, n_out_ref, m_out_ref, ct_s, n_s, m_s):
    c = pl.program_id(0)
    L = ML_CHUNK

    @pl.when(c == 0)
    def _():
        ct_s[...] = jnp.zeros_like(ct_s)
        n_s[...] = jnp.zeros_like(n_s)
        m_s[...] = jnp.zeros_like(m_s)

    ig = gi_ref[...] + bi_ref[...]
    lf = _log_sigmoid(gf_ref[...] + bf_ref[...])
    r = lax.broadcasted_iota(I32, (L, L), 0)
    s = lax.broadcasted_iota(I32, (L, L), 1)
    causal = r >= s
    tril = causal.astype(F32)
    b = jnp.dot(tril, lf, precision=HIGHEST, preferred_element_type=F32)
    b_t = b.T
    ig_t = ig.T
    m_all = m_s[...]
    m_new_all = m_all
    lane = lax.broadcasted_iota(I32, (1, 128), 1)

    for h in range(ML_HEADS):
        qh = q_ref[:, h * ML_DQK:(h + 1) * ML_DQK]
        kh = k_ref[:, h * ML_DQK:(h + 1) * ML_DQK]
        vh = v_ref[:, h * ML_DV:(h + 1) * ML_DV]
        bc = b[:, h:h + 1]
        ic = ig[:, h:h + 1]
        br = b_t[h:h + 1, :]
        ir = ig_t[h:h + 1, :]
        m_h = m_all[:, h:h + 1]
        logd = jnp.where(causal, bc - br + ir, -jnp.inf)
        inter = bc + m_h
        mt = jnp.maximum(inter, jnp.max(logd, axis=1, keepdims=True))
        sc = lax.dot_general(qh, kh, (((1,), (1,)), ((), ())), preferred_element_type=F32)
        a = sc * jnp.exp(logd - mt)
        w_inter = jnp.exp(inter - mt)
        ct_h = ct_s[h]
        qc = jnp.dot(qh, ct_h.astype(BF16), preferred_element_type=F32)
        num = jnp.dot(a.astype(BF16), vh, preferred_element_type=F32) + w_inter * qc
        n_h = n_s[h:h + 1, :]
        qn = jnp.sum(qh.astype(F32) * n_h, axis=1, keepdims=True)
        den = jnp.sum(a, axis=1, keepdims=True) + w_inter * qn
        hh = num / jnp.maximum(jnp.abs(den), jnp.exp(-mt))
        hn = _rms(hh) * ghn_ref[:, h * ML_DV:(h + 1) * ML_DV]
        og = jax.nn.sigmoid(o_ref[:, h * ML_DV:(h + 1) * ML_DV].astype(F32) + bo_ref[:, h * ML_DV:(h + 1) * ML_DV])
        h_ref[:, h * ML_DV:(h + 1) * ML_DV] = (hn * og).astype(h_ref.dtype)
        m_new = mt[L - 1:L, :]
        b_last = bc[L - 1:L, :]
        decay = jnp.exp(b_last + m_h - m_new)
        wj = jnp.exp(b_last - bc + ic - m_new)
        kw = (kh.astype(F32) * wj).astype(BF16)
        upd = lax.dot_general(kw, vh, (((0,), (0,)), ((), ())), preferred_element_type=F32)
        ct_s[h] = decay * ct_h + upd
        n_s[h:h + 1, :] = decay * n_h + jnp.sum(kh.astype(F32) * wj, axis=0, keepdims=True)
        m_new_all = jnp.where(lane == h, m_new, m_new_all)

    m_s[...] = m_new_all

    @pl.when(c == pl.num_programs(0) - 1)
    def _():
        for h in range(ML_HEADS):
            c_out_ref[h] = ct_s[h].T
        n_out_ref[...] = n_s[...]
        m_out_ref[...] = m_s[...]


def _mlstm_prompt(z, gates, bi, bf, bo, ghn):
    seq = z.shape[0]
    nc = seq // ML_CHUNK
    L = ML_CHUNK
    const2 = lambda c: (0, 0)
    in_specs = [
        pl.BlockSpec((L, ML_NQK), lambda c: (c, 0)),
        pl.BlockSpec((L, ML_NQK), lambda c: (c, 1)),
        pl.BlockSpec((L, ML_NV), lambda c: (c, 1)),
        pl.BlockSpec((L, D_MODEL), lambda c: (c, 2)),
        pl.BlockSpec((L, 128), lambda c: (c, 0)),
        pl.BlockSpec((L, 128), lambda c: (c, 1)),
        pl.BlockSpec((1, 128), const2),
        pl.BlockSpec((1, 128), const2),
        pl.BlockSpec((1, D_MODEL), const2),
        pl.BlockSpec((1, ML_NV), const2),
    ]
    out_shape = [
        jax.ShapeDtypeStruct((seq, ML_NV), BF16),
        jax.ShapeDtypeStruct((ML_HEADS, ML_DV, ML_DQK), F32),
        jax.ShapeDtypeStruct((ML_HEADS, ML_DQK), F32),
        jax.ShapeDtypeStruct((1, 128), F32),
    ]
    out_specs = [
        pl.BlockSpec((L, ML_NV), lambda c: (c, 0)),
        pl.BlockSpec((ML_HEADS, ML_DV, ML_DQK), lambda c: (0, 0, 0)),
        pl.BlockSpec((ML_HEADS, ML_DQK), const2),
        pl.BlockSpec((1, 128), const2),
    ]
    return pl.pallas_call(
        _mlstm_prompt_kernel, grid=(nc,), in_specs=in_specs, out_specs=out_specs, out_shape=out_shape,
        scratch_shapes=[pltpu.VMEM((ML_HEADS, ML_DQK, ML_DV), F32), pltpu.VMEM((ML_HEADS, ML_DQK), F32),
                        pltpu.VMEM((1, 128), F32)],
        compiler_params=_cparams(("arbitrary",)), name="mlstm_prompt",
    )(z, z, z, z, gates, gates, bi, bf, bo, ghn)


def _mlstm_sample_kernel(z_ref, g_ref, c_ref, n_ref, m_ref, bi_ref, bf_ref, bo_ref, ghn_ref,
                         h_ref, c_out_ref, n_out_ref, m_out_ref):
    i = pl.program_id(0)
    z = z_ref[0]
    g = g_ref[0]
    ig_all = g[:, 0:128] + bi_ref[...]
    lf_all = _log_sigmoid(g[:, 128:256] + bf_ref[...])
    m_all = m_ref[0]
    mt_all = jnp.maximum(lf_all + m_all, ig_all)
    m_out_ref[0] = mt_all
    outs = []
    for h in range(ML_HEADS):
        q = z[:, h * ML_DQK:(h + 1) * ML_DQK]
        k = z[:, ML_NQK + h * ML_DQK:ML_NQK + (h + 1) * ML_DQK]
        v = z[:, 2 * ML_NQK + h * ML_DV:2 * ML_NQK + (h + 1) * ML_DV]
        ig = ig_all[:, h:h + 1]
        lf = lf_all[:, h:h + 1]
        m0 = m_all[:, h:h + 1]
        mt = mt_all[:, h:h + 1]
        w_inter = jnp.exp(lf + m0 - mt)
        wj = jnp.exp(ig - mt)
        a = jnp.sum(q * k, axis=1, keepdims=True) * wj
        c_h = c_ref[0, h]
        n_h = n_ref[0, h:h + 1, :]
        q8 = jnp.broadcast_to(q, (8, ML_DQK))
        cq = lax.dot_general(q8, c_h, (((1,), (1,)), ((), ())), precision=HIGHEST,
                             preferred_element_type=F32)[0:1, :]
        num = a * v + w_inter * cq
        den = a + w_inter * jnp.sum(n_h * q, axis=1, keepdims=True)
        hh = num / jnp.maximum(jnp.abs(den), jnp.exp(-mt))
        hn = _rms(hh) * ghn_ref[:, h * ML_DV:(h + 1) * ML_DV]
        og = jax.nn.sigmoid(z[:, 2 * ML_NQK + ML_NV + h * ML_DV:2 * ML_NQK + ML_NV + (h + 1) * ML_DV]
                            + bo_ref[:, h * ML_DV:(h + 1) * ML_DV])
        outs.append(hn * og)
        v_col = jnp.broadcast_to(v, (8, ML_DV)).T[:, 0:1]
        c_out_ref[0, h] = w_inter * c_h + wj * (v_col * k)
        n_out_ref[0, h:h + 1, :] = w_inter * n_h + wj * k
    h_ref[pl.ds(i, 1), :] = jnp.concatenate(outs, axis=1)


def _mlstm_sample(z_s, gates_s, c0, n0, m0, bi, bf, bo, ghn):
    ns = z_s.shape[0]
    const2 = lambda i: (0, 0)
    in_specs = [
        pl.BlockSpec((1, 1, ML_MAIN), lambda i: (i, 0, 0)),
        pl.BlockSpec((1, 1, 256), lambda i: (i, 0, 0)),
        pl.BlockSpec((1, ML_HEADS, ML_DV, ML_DQK), lambda i: (i, 0, 0, 0)),
        pl.BlockSpec((1, ML_HEADS, ML_DQK), lambda i: (i, 0, 0)),
        pl.BlockSpec((1, 1, 128), lambda i: (i, 0, 0)),
        pl.BlockSpec((1, 128), const2),
        pl.BlockSpec((1, 128), const2),
        pl.BlockSpec((1, D_MODEL), const2),
        pl.BlockSpec((1, ML_NV), const2),
    ]
    out_shape = [
        jax.ShapeDtypeStruct((ns, ML_NV), F32),
        jax.ShapeDtypeStruct((ns, ML_HEADS, ML_DV, ML_DQK), F32),
        jax.ShapeDtypeStruct((ns, ML_HEADS, ML_DQK), F32),
        jax.ShapeDtypeStruct((ns, 1, 128), F32),
    ]
    out_specs = [
        pl.BlockSpec((ns, ML_NV), const2),
        pl.BlockSpec((1, ML_HEADS, ML_DV, ML_DQK), lambda i: (i, 0, 0, 0)),
        pl.BlockSpec((1, ML_HEADS, ML_DQK), lambda i: (i, 0, 0)),
        pl.BlockSpec((1, 1, 128), lambda i: (i, 0, 0)),
    ]
    return pl.pallas_call(
        _mlstm_sample_kernel, grid=(ns,), in_specs=in_specs, out_specs=out_specs, out_shape=out_shape,
        compiler_params=_cparams(("arbitrary",)), name="mlstm_sample",
    )(z_s, gates_s, c0, n0, m0, bi, bf, bo, ghn)


def _attn_prompt_kernel(q_ref, kp_ref, kc_ref, vp_ref, vc_ref, o_ref, lse_ref):
    blk = pl.program_id(1)
    T = ATT_STEPS
    qi = lax.broadcasted_iota(I32, (T, T), 0)
    kj = lax.broadcasted_iota(I32, (T, T), 1)
    valid_prev = jnp.logical_and(kj >= qi, blk > 0)
    valid_cur = kj <= qi
    scale = ATT_HD ** -0.5
    nt = (((1,), (1,)), ((), ()))
    for h in range(ATT_H):
        sl = slice(h * ATT_HD, (h + 1) * ATT_HD)
        qh = q_ref[:, sl]
        s1 = jnp.where(valid_prev, lax.dot_general(qh, kp_ref[:, sl], nt, preferred_element_type=F32) * scale,
                       -jnp.inf)
        s2 = jnp.where(valid_cur, lax.dot_general(qh, kc_ref[:, sl], nt, preferred_element_type=F32) * scale,
                       -jnp.inf)
        mx = jnp.maximum(jnp.max(s1, axis=1, keepdims=True), jnp.max(s2, axis=1, keepdims=True))
        p1 = jnp.exp(s1 - mx)
        p2 = jnp.exp(s2 - mx)
        den = jnp.sum(p1, axis=1, keepdims=True) + jnp.sum(p2, axis=1, keepdims=True)
        acc = jnp.dot(p1.astype(BF16), vp_ref[:, sl], preferred_element_type=F32)
        acc = acc + jnp.dot(p2.astype(BF16), vc_ref[:, sl], preferred_element_type=F32)
        o_ref[:, sl] = acc / den
        lse_ref[:, sl] = jnp.broadcast_to(mx + jnp.log(den), (T, ATT_HD))


def _attn_prompt(q, kv, g):
    dil, L, _ = q.shape
    nb = L // ATT_STEPS
    T = ATT_STEPS
    blk = (None, T, ATT_GW)
    in_specs = [
        pl.BlockSpec(blk, lambda r, b: (r, b, 0)),
        pl.BlockSpec(blk, lambda r, b: (r, jnp.maximum(b - 1, 0), 0)),
        pl.BlockSpec(blk, lambda r, b: (r, b, 0)),
        pl.BlockSpec(blk, lambda r, b: (r, jnp.maximum(b - 1, 0), 1)),
        pl.BlockSpec(blk, lambda r, b: (r, b, 1)),
    ]
    out_spec = pl.BlockSpec(blk, lambda r, b: (r, b, 0))
    return pl.pallas_call(
        _attn_prompt_kernel, grid=(dil, nb), in_specs=in_specs, out_specs=[out_spec, out_spec],
        out_shape=[jax.ShapeDtypeStruct((dil, L, ATT_GW), F32)] * 2,
        compiler_params=_cparams(("arbitrary", "arbitrary")), name=f"attn_prompt_g{g}",
    )(q, kv, kv, kv, kv)


def _merge_kernel(*refs):
    in_refs, out_ref, scratch = refs[:2 * ATT_G], refs[2 * ATT_G], refs[2 * ATT_G + 1:]
    tm = out_ref.shape[0]
    for c in range(ATT_H):
        lanes = slice(c * ATT_HD, (c + 1) * ATT_HD)
        vals = []
        si = 0
        for g in range(ATT_G):
            dil = ATT_DILS[g]
            pair = []
            for ref in in_refs[2 * g:2 * g + 2]:
                if dil == 1:
                    pair.append(ref[0, :, lanes])
                else:
                    buf = scratch[si]
                    si += 1
                    for r in range(dil):
                        buf[pl.ds(r, tm // dil, stride=dil), :] = ref[r, :, lanes]
                    pair.append(buf[...])
            vals.append(pair)
        lses = [p[1] for p in vals]
        mx = jnp.maximum(jnp.maximum(lses[0], lses[1]), lses[2])
        es = [jnp.exp(l - mx) for l in lses]
        tot = es[0] + es[1] + es[2]
        out_ref[:, lanes] = ((es[0] / tot) * vals[0][0] + (es[1] / tot) * vals[1][0]
                             + (es[2] / tot) * vals[2][0]).astype(out_ref.dtype)


def _merge_groups(parts, seq, tm=512):
    in_specs, args, scratch = [], [], []
    for g, pair in enumerate(parts):
        dil = ATT_DILS[g]
        for a in pair:
            in_specs.append(pl.BlockSpec((dil, tm // dil, ATT_GW), lambda i: (0, i, 0)))
            args.append(a)
            if dil > 1:
                scratch.append(pltpu.VMEM((tm, ATT_HD), F32))
    return pl.pallas_call(
        _merge_kernel, grid=(seq // tm,), in_specs=in_specs,
        out_specs=pl.BlockSpec((tm, ATT_GW), lambda i: (i, 0)),
        out_shape=jax.ShapeDtypeStruct((seq, ATT_GW), BF16), scratch_shapes=scratch,
        compiler_params=_cparams(("arbitrary",)), name="attn_merge",
    )(*args)


def _attn_sample_kernel(q_ref, kvn_ref, b0_ref, b1_ref, b2_ref, out_ref):
    scale = ATT_HD ** -0.5
    outs, lses = [], []
    for g, b_ref in enumerate((b0_ref, b1_ref, b2_ref)):
        qg = q_ref[0, g]
        kn = kvn_ref[0, g, 0]
        vn = kvn_ref[0, g, 1]
        kb = b_ref[:, 0]
        vb = b_ref[:, 1]
        s = jnp.sum(kb * qg[None], axis=2, keepdims=True) * scale
        s_new = jnp.sum(kn * qg, axis=1, keepdims=True) * scale
        mx = jnp.maximum(jnp.max(s, axis=0), s_new)
        p = jnp.exp(s - mx[None])
        p_new = jnp.exp(s_new - mx)
        den = jnp.sum(p, axis=0) + p_new
        o = jnp.sum(p * vb, axis=0) + p_new * vn
        outs.append(o / den)
        lses.append(mx + jnp.log(den))
    mxl = jnp.maximum(jnp.maximum(lses[0], lses[1]), lses[2])
    es = [jnp.exp(l - mxl) for l in lses]
    tot = es[0] + es[1] + es[2]
    out_ref[0] = (es[0] / tot) * outs[0] + (es[1] / tot) * outs[1] + (es[2] / tot) * outs[2]


def _attn_sample(q_s, kv_s, caches):
    ns = q_s.shape[0]
    views, specs = [], []
    for g, cbuf in enumerate(caches):
        lb = cbuf.shape[1]
        dil = ATT_DILS[g]
        views.append(cbuf.reshape(ns, lb // dil, dil, 2, ATT_H, ATT_HD))
        specs.append(pl.BlockSpec((None, ATT_STEPS, None, 2, ATT_H, ATT_HD), lambda i: (i, 0, 0, 0, 0, 0)))
    in_specs = [
        pl.BlockSpec((1, ATT_G, ATT_H, ATT_HD), lambda i: (i, 0, 0, 0)),
        pl.BlockSpec((1, ATT_G, 2, ATT_H, ATT_HD), lambda i: (i, 0, 0, 0, 0)),
    ] + specs
    return pl.pallas_call(
        _attn_sample_kernel, grid=(ns,), in_specs=in_specs,
        out_specs=pl.BlockSpec((1, ATT_H, ATT_HD), lambda i: (i, 0, 0)),
        out_shape=jax.ShapeDtypeStruct((ns, ATT_H, ATT_HD), F32),
        compiler_params=_cparams(("arbitrary",)), name="attn_sample",
    )(q_s, kv_s, *views)


KV_SHIFT_BLOCK = 8192


def _kv_shift_kernel(cur_ref, nxt_ref, new_ref, out_ref):
    blk = out_ref.shape[1]
    out_ref[0, :blk - KV_ROW] = cur_ref[0, KV_ROW:]
    last = pl.program_id(1) == pl.num_programs(1) - 1
    out_ref[0, blk - KV_ROW:] = jnp.where(last, new_ref[0], nxt_ref[0])


def _kv_shift(cache, new):
    ns, lb = cache.shape[0], cache.shape[1]
    rows = lb * KV_ROW
    blk = min(KV_SHIFT_BLOCK, rows)
    nb = rows // blk
    per = blk // KV_ROW
    flat = cache.reshape(ns, rows, ATT_HD)
    out = pl.pallas_call(
        _kv_shift_kernel, grid=(ns, nb),
        in_specs=[pl.BlockSpec((1, blk, ATT_HD), lambda i, j: (i, j, 0)),
                  pl.BlockSpec((1, KV_ROW, ATT_HD), lambda i, j: (i, jnp.minimum((j + 1) * per, lb - 1), 0)),
                  pl.BlockSpec((1, KV_ROW, ATT_HD), lambda i, j: (i, 0, 0))],
        out_specs=pl.BlockSpec((1, blk, ATT_HD), lambda i, j: (i, j, 0)),
        out_shape=jax.ShapeDtypeStruct(flat.shape, flat.dtype),
        compiler_params=_cparams(("arbitrary", "arbitrary")), name="kv_shift",
    )(flat, flat, new.reshape(ns, KV_ROW, ATT_HD))
    return out.reshape(cache.shape)


def _top2(y, wr_ref, br_ref):
    rows = y.shape[0]
    lane = lax.broadcasted_iota(I32, (rows, 128), 1)
    logits = jnp.dot(y, wr_ref[...], precision=HIGHEST, preferred_element_type=F32) + br_ref[...]
    logits = jnp.where(lane < N_EXPERTS, logits, NEG_BIG)
    e = jnp.exp(logits - jnp.max(logits, axis=1, keepdims=True))
    probs = e / jnp.sum(e, axis=1, keepdims=True)
    p1 = jnp.max(probs, axis=1, keepdims=True)
    i1 = jnp.min(jnp.where(probs == p1, lane, 128), axis=1, keepdims=True)
    probs2 = jnp.where(lane == i1, -1.0, probs)
    p2 = jnp.max(probs2, axis=1, keepdims=True)
    i2 = jnp.min(jnp.where(probs2 == p2, lane, 128), axis=1, keepdims=True)
    tot = p1 + p2
    return lane, i1, i2, p1 / tot, p2 / tot


def _router_kernel(x_ref, xs_ref, g_ref, wr_ref, br_ref, xn_ref, eid_ref, gate_ref, rank_ref, cnt_ref,
                   xns_ref, gs_ref, carry):
    i = pl.program_id(0)
    tm = x_ref.shape[0]

    @pl.when(i == 0)
    def _():
        carry[...] = jnp.zeros_like(carry)
        ys = _rms(xs_ref[...]) * g_ref[...]
        xns_ref[...] = ys
        lane, i1, i2, g1, g2 = _top2(ys, wr_ref, br_ref)
        gs_ref[...] = jnp.where(lane == i1, g1, jnp.where(lane == i2, g2, 0.0))

    y = _rms(x_ref[...]) * g_ref[...]
    xn_ref[...] = y
    lane, i1, i2, g1, g2 = _top2(y, wr_ref, br_ref)
    sel1 = lane == i1
    sel2 = lane == i2
    onehot = jnp.where(jnp.logical_or(sel1, sel2), 1.0, 0.0)
    rr = lax.broadcasted_iota(I32, (tm, tm), 0)
    cc = lax.broadcasted_iota(I32, (tm, tm), 1)
    before = (cc < rr).astype(BF16)
    prefix = jnp.dot(before, onehot.astype(BF16), preferred_element_type=F32) + carry[...]
    r1 = jnp.sum(jnp.where(sel1, prefix, 0.0), axis=1, keepdims=True)
    r2 = jnp.sum(jnp.where(sel2, prefix, 0.0), axis=1, keepdims=True)
    carry[...] = carry[...] + jnp.sum(onehot, axis=0, keepdims=True)
    eid_ref[...] = jnp.where(lane == 0, i1, jnp.where(lane == 1, i2, 0))
    gate_ref[...] = jnp.where(lane == 0, g1, jnp.where(lane == 1, g2, 0.0))
    rank_ref[...] = jnp.where(lane == 0, r1, jnp.where(lane == 1, r2, 0.0)).astype(I32)
    cnt_ref[...] = jnp.broadcast_to(carry[...], cnt_ref.shape)


def _router(h, hs, gain, w_router_pad, b_router_pad, tm=ROW_TILE):
    m, d = h.shape
    s = hs.shape[0]
    const = lambda i: (0, 0)
    row_spec = pl.BlockSpec((tm, 128), lambda i: (i, 0))
    return pl.pallas_call(
        _router_kernel, grid=(m // tm,),
        in_specs=[pl.BlockSpec((tm, d), lambda i: (i, 0)), pl.BlockSpec((s, d), const), pl.BlockSpec((1, d), const),
                  pl.BlockSpec((d, 128), const), pl.BlockSpec((1, 128), const)],
        out_specs=[pl.BlockSpec((tm, d), lambda i: (i, 0)), row_spec, row_spec, row_spec,
                   pl.BlockSpec((8, 128), const), pl.BlockSpec((s, d), const), pl.BlockSpec((s, 128), const)],
        out_shape=[jax.ShapeDtypeStruct((m, d), F32), jax.ShapeDtypeStruct((m, 128), I32),
                   jax.ShapeDtypeStruct((m, 128), F32), jax.ShapeDtypeStruct((m, 128), I32),
                   jax.ShapeDtypeStruct((8, 128), F32), jax.ShapeDtypeStruct((s, d), F32),
                   jax.ShapeDtypeStruct((s, 128), F32)],
        scratch_shapes=[pltpu.VMEM((1, 128), F32)],
        compiler_params=_cparams(("arbitrary",)), name="router",
    )(h, hs, gain, w_router_pad, b_router_pad)


def _dispatch_kernel(pos_ref, nv_ref, x_hbm, out_ref, inv, buf, sem, *, n_tok):
    i = pl.program_id(0)
    tg = out_ref.shape[0]

    @pl.when(i == 0)
    def _():
        def clear(s, c):
            inv[s] = 0
            return c

        lax.fori_loop(0, inv.shape[0], clear, 0, unroll=8)

        def fill(t, c):
            inv[pos_ref[2 * t]] = t
            inv[pos_ref[2 * t + 1]] = t
            return c

        lax.fori_loop(0, n_tok, fill, 0, unroll=8)

    def start_gather(tile):
        slot = tile % 2

        def issue(r2, c):
            for j in range(2):
                r = 2 * r2 + j
                pltpu.make_async_copy(x_hbm.at[pl.ds(inv[tile * tg + r], 1)], buf.at[slot, pl.ds(r, 1)],
                                      sem.at[slot]).start(priority=j)
            return c

        lax.fori_loop(0, tg // 2, issue, 0, unroll=4)

    @pl.when(i == 0)
    def _():
        start_gather(i)

    @pl.when(i + 1 < nv_ref[0])
    def _():
        start_gather(i + 1)

    @pl.when(i < nv_ref[0])
    def _():
        slot = i % 2
        pltpu.make_async_copy(x_hbm.at[pl.ds(0, tg)], buf.at[slot], sem.at[slot]).wait()
        out_ref[...] = buf[slot].astype(out_ref.dtype)

    @pl.when(i >= nv_ref[0])
    def _():
        out_ref[...] = jnp.zeros_like(out_ref)


def _dispatch(pos_flat, nvalid, xn, n_tiles, tg):
    n_tok, d = xn.shape
    return pl.pallas_call(
        functools.partial(_dispatch_kernel, n_tok=n_tok),
        grid_spec=pltpu.PrefetchScalarGridSpec(
            num_scalar_prefetch=2, grid=(n_tiles,),
            in_specs=[pl.BlockSpec(memory_space=pl.ANY)],
            out_specs=pl.BlockSpec((tg, d), lambda i, pos, nv: (i, 0)),
            scratch_shapes=[pltpu.SMEM((n_tiles * tg,), I32), pltpu.VMEM((2, tg, d), xn.dtype),
                            pltpu.SemaphoreType.DMA((2,))]),
        out_shape=jax.ShapeDtypeStruct((n_tiles * tg, d), BF16),
        compiler_params=_cparams(("arbitrary",)), name="moe_dispatch",
    )(pos_flat, nvalid, xn)


def _combine_kernel(pos_ref, h_ref, gate_ref, g_ref, hs_ref, gs_ref, ys_s_ref, ys_hbm, out_ref, outs_ref, ybuf, sem):
    i = pl.program_id(0)
    tm = h_ref.shape[0]

    def start_gather(tile):
        slot = tile % 2

        def issue(r, carry):
            for j in range(2):
                pltpu.make_async_copy(ys_hbm.at[pl.ds(pos_ref[2 * (tile * tm + r) + j], 1)],
                                      ybuf.at[slot, j, pl.ds(r, 1)], sem.at[slot]).start(priority=j)
            return carry

        lax.fori_loop(0, tm, issue, 0, unroll=4)

    @pl.when(i == 0)
    def _():
        start_gather(i)

    @pl.when(i + 1 < pl.num_programs(0))
    def _():
        start_gather(i + 1)

    @pl.when(i == 0)
    def _():
        gs = gs_ref[...]
        y = jnp.zeros(hs_ref.shape, F32)
        for e in range(N_EXPERTS):
            y = y + gs[:, e:e + 1] * ys_s_ref[e]
        outs_ref[...] = _rms(hs_ref[...] + y) * g_ref[...]

    slot = i % 2
    for j in range(2):
        pltpu.make_async_copy(ys_hbm.at[pl.ds(0, tm)], ybuf.at[slot, j], sem.at[slot]).wait()
    gate = gate_ref[...]
    y = h_ref[...] + (gate[:, 0:1] * ybuf[slot, 0] + gate[:, 1:2] * ybuf[slot, 1])
    out_ref[...] = _rms(y) * g_ref[...]


def _combine(pos_flat, h, gate, g_final, hs, gs, ys_s, ys, tm=ROW_TILE):
    m, d = h.shape
    s = hs.shape[0]
    c2 = lambda i, pos: (0, 0)
    return pl.pallas_call(
        _combine_kernel,
        grid_spec=pltpu.PrefetchScalarGridSpec(
            num_scalar_prefetch=1, grid=(m // tm,),
            in_specs=[pl.BlockSpec((tm, d), lambda i, pos: (i, 0)),
                      pl.BlockSpec((tm, 128), lambda i, pos: (i, 0)),
                      pl.BlockSpec((1, d), c2),
                      pl.BlockSpec((s, d), c2),
                      pl.BlockSpec((s, 128), c2),
                      pl.BlockSpec((N_EXPERTS, s, d), lambda i, pos: (0, 0, 0)),
                      pl.BlockSpec(memory_space=pl.ANY)],
            out_specs=[pl.BlockSpec((tm, d), lambda i, pos: (i, 0)), pl.BlockSpec((s, d), c2)],
            scratch_shapes=[pltpu.VMEM((2, 2, tm, d), F32), pltpu.SemaphoreType.DMA((2,))]),
        out_shape=[jax.ShapeDtypeStruct((m, d), F32), jax.ShapeDtypeStruct((s, d), F32)],
        compiler_params=_cparams(("arbitrary",)), name="moe_combine",
    )(pos_flat, h, gate, g_final, hs, gs, ys_s, ys)


def _rope_tables(pos):
    half = ATT_HD // 2
    inv = ROPE_THETA ** (-jnp.arange(half, dtype=F32) / half)
    ang = pos.astype(F32)[:, None] * inv[None, :]
    cos, sin = jnp.cos(ang), jnp.sin(ang)
    return jnp.concatenate([cos, cos], axis=1), jnp.concatenate([-sin, sin], axis=1)


def kernel(x_prompt, x_sample, state_mlstm_C, state_mlstm_n, state_mlstm_m, cache_kv_w128, cache_kv_w512, cache_kv_w2048, g_mix, g_ffn, w_ml_in, b_ml_gates, b_ml_o, g_ml_hnorm, w_ml_out, g_kv, w_kv, w_q, w_o, w_ffn_gate, w_ffn_up, w_ffn_down, w_router, b_router, w_exp_gate, w_exp_up, w_exp_down, g_final):
    bp, seq, d = x_prompt.shape
    ns = x_sample.shape[0]
    caches = (cache_kv_w128, cache_kv_w512, cache_kv_w2048)
    assert bp == 1 and x_sample.shape[1] == 1 and d == D_MODEL and ns % 8 == 0
    assert seq % (ATT_STEPS * max(ATT_DILS)) == 0 and seq % 1024 == 0
    assert all(c.shape[1] == w for c, w in zip(caches, ATT_WINDOWS))
    tm = 512

    h0 = x_prompt.reshape(seq, d)
    h0_s = x_sample.reshape(ns, d)

    w_gates = w_ml_in[0][:, ML_MAIN:]
    w_gates_pad = jnp.zeros((d, 256), F32).at[:, 0:ML_HEADS].set(w_gates[:, :ML_HEADS])
    w_gates_pad = w_gates_pad.at[:, 128:128 + ML_HEADS].set(w_gates[:, ML_HEADS:])
    bi = jnp.zeros((1, 128), F32).at[0, :ML_HEADS].set(b_ml_gates[0, :ML_HEADS])
    bf = jnp.zeros((1, 128), F32).at[0, :ML_HEADS].set(b_ml_gates[0, ML_HEADS:])
    (xn0, gates), (xn0_s, gates_s) = _rmsnorm(h0, h0_s, g_mix[0:1], proj=w_gates_pad)
    k_scale = jnp.concatenate([jnp.ones((1, ML_NQK), F32), jnp.full((1, ML_NQK), ML_DQK ** -0.5, F32),
                               jnp.ones((1, ML_NV + D_MODEL), F32)], axis=1)
    z, z_s = _matmul(xn0, [w_ml_in], ML_MAIN, BF16, xn0_s[None], tm=1024, tn=1024, col_scale=k_scale, name="ml_in")
    bo = b_ml_o[0:1]
    ghn = g_ml_hnorm[0:1]
    hg, p_c, p_n, p_m = _mlstm_prompt(z[0], gates, bi, bf, bo, ghn)
    m0 = jnp.zeros((ns, 1, 128), F32).at[:, 0, :ML_HEADS].set(state_mlstm_m[0])
    hg_s, s_c, s_n, s_m = _mlstm_sample(z_s[0].reshape(ns, 1, ML_MAIN), gates_s.reshape(ns, 1, 256),
                                        state_mlstm_C[0], state_mlstm_n[0], m0, bi, bf, bo, ghn)
    h1, h1_s = _matmul(hg, [w_ml_out], d, F32, hg_s[None], tm=tm, tn=1024, res=h0, res_s=h0_s, name="ml_out")
    h1, h1_s = h1[0], h1_s[0]

    (xf0,), (xf0_s,) = _rmsnorm(h1, h1_s, g_ffn[0:1])
    ffn_dense = w_ffn_gate.shape[2]
    hid, hid_s = _matmul(xf0, [w_ffn_gate, w_ffn_up], ffn_dense, BF16, xf0_s[None], tm=1024, tn=512, name="ffn_up")
    h2, h2_s = _matmul(hid[0], [w_ffn_down], d, F32, hid_s, tm=tm, tn=512, res=h1, res_s=h1_s, name="ffn_down")
    h2, h2_s = h2[0], h2_s[0]

    (xq, xkv), (xq_s, xkv_s) = _rmsnorm(h2, h2_s, jnp.stack([g_mix[1], g_kv]))
    cos, sin = _rope_tables(jnp.arange(seq))
    cos_s, sin_s = _rope_tables(jnp.full((ns,), PAST_LEN, I32))
    rope_args = dict(cos=cos, sin=sin, cos_s=cos_s, sin_s=sin_s)
    parts, kv_nat, kv_new, q_new = [], [], [], []
    for g in range(ATT_G):
        dil = ATT_DILS[g]
        kvd, kvn, kv_s = _matmul(xkv, [w_kv[None]], 2 * ATT_GW, BF16, xkv_s[None], tm=tm, tn=ATT_GW,
                                 col_off=2 * g, rope="even", dil=dil, natural=True, name=f"kv_proj_g{g}",
                                 **rope_args)
        qd, q_s = _matmul(xq, [w_q], ATT_GW, BF16, xq_s[None], tm=tm, tn=ATT_GW, col_off=g, rope="all",
                          dil=dil, name=f"q_proj_g{g}", **rope_args)
        parts.append(_attn_prompt(qd, kvd, g))
        kv_nat.append(kvn)
        kv_new.append(kv_s[0])
        q_new.append(q_s[0])
    att = _merge_groups(parts, seq)
    att_s = _attn_sample(jnp.stack(q_new, axis=1).reshape(ns, ATT_G, ATT_H, ATT_HD),
                         jnp.stack(kv_new, axis=1).reshape(ns, ATT_G, 2, ATT_H, ATT_HD), caches)
    h3, h3_s = _matmul(att, [w_o], d, F32, att_s.reshape(1, ns, ATT_GW), tm=tm, tn=1024, res=h2, res_s=h2_s,
                       name="attn_out")
    h3, h3_s = h3[0], h3_s[0]

    wr_pad = jnp.zeros((d, 128), F32).at[:, :N_EXPERTS].set(w_router[0])
    br_pad = jnp.zeros((1, 128), F32).at[0, :N_EXPERTS].set(b_router[0])
    xn2, eid, gate, rank, cnt, xn2_s, gates_moe_s = _router(h3, h3_s, g_ffn[1:2], wr_pad, br_pad)
    tg = MOE_TILE
    n_tiles = -(-(2 * seq + N_EXPERTS * (tg - 1)) // tg)
    counts = cnt[0, :N_EXPERTS].astype(I32)
    padded = jnp.maximum((counts + tg - 1) // tg, 1) * tg
    gend = jnp.cumsum(padded)
    gstart = gend - padded
    pos = (gstart[eid[:, :2]] + rank[:, :2]).astype(I32).reshape(-1)
    nvalid = (gend[-1] // tg).astype(I32).reshape(1)
    xs = _dispatch(pos, nvalid, xn2, n_tiles, tg)
    t0 = (gstart // tg).astype(I32)
    tcnt = (padded // tg).astype(I32)
    hs, hs_s, sbuf2 = _moe_matmul(xs, [w_exp_gate[0], w_exp_up[0]], BF16, xn2_s[None], t0, tcnt, nvalid, tm=tg,
                                  tn=1024, xs_per_expert=False, name="moe_up", window=(caches[2], kv_new[2]))
    ys, ys_s, sbuf1 = _moe_matmul(hs, [w_exp_down[0]], F32, hs_s, t0, tcnt, nvalid, tm=tg, tn=512,
                                  xs_per_expert=True, name="moe_down", window=(caches[1], kv_new[1]))
    s_bufs = [_kv_shift(caches[0], kv_new[0]), sbuf1, sbuf2]
    y_p, y_s = _combine(pos, h3, gate, g_final.reshape(1, d), h3_s, gates_moe_s, ys_s, ys)

    p_bufs = []
    for g in range(ATT_G):
        keep = min(ATT_WINDOWS[g], seq)
        p_bufs.append(kv_nat[g][seq - keep:].reshape(1, keep, 2, ATT_H, ATT_HD))
    return (y_p.reshape(1, seq, d), y_s.reshape(ns, 1, d),
            p_c[None, None], p_n[None, None], p_m[:, :ML_HEADS][None],
            s_c[None], s_n[None], s_m[:, 0, :ML_HEADS][None],
            p_bufs[0], p_bufs[1], p_bufs[2], s_bufs[0], s_bufs[1], s_bufs[2])
```

```python
import functools

import jax
import jax.numpy as jnp
from jax import lax
from jax.experimental import pallas as pl
from jax.experimental.pallas import tpu as pltpu

F32 = jnp.float32
BF16 = jnp.bfloat16
I32 = jnp.int32
HIGHEST = lax.Precision.HIGHEST

D_MODEL = 2048
ML_HEADS = 8
ML_DQK = 128
ML_DV = 256
ML_NQK = ML_HEADS * ML_DQK
ML_NV = ML_HEADS * ML_DV
ML_MAIN = 2 * ML_NQK + ML_NV + D_MODEL
ML_CHUNK = 128
ATT_HD = 128
ATT_H = 8
ATT_G = 3
ATT_WINDOWS = (128, 512, 2048)
ATT_DILS = (1, 4, 16)
ATT_STEPS = 128
ATT_GW = ATT_H * ATT_HD
ROPE_THETA = 10000.0
PAST_LEN = 8192
N_EXPERTS = 8
RMS_EPS = 1e-6
NEG_BIG = -1e30

VMEM_LIMIT_BYTES = 58 * 1024 * 1024
ROW_TILE = 512
MOE_TILE = 256


def _cparams(sem):
    return pltpu.CompilerParams(dimension_semantics=sem, vmem_limit_bytes=VMEM_LIMIT_BYTES)


def _rms(x):
    return x * lax.rsqrt(jnp.mean(x * x, axis=-1, keepdims=True) + RMS_EPS)


def _norm_kernel(x_ref, xs_ref, g_ref, *refs, n_out, has_proj):
    ins = 1 if has_proj else 0
    n_each = n_out + ins
    main = refs[ins:ins + n_each]
    side = refs[ins + n_each:ins + 2 * n_each]

    def emit(x, outs):
        y = _rms(x)
        for i in range(n_out):
            outs[i][...] = (y * g_ref[i:i + 1, :]).astype(outs[i].dtype)
        if has_proj:
            outs[n_out][...] = jnp.dot(y * g_ref[0:1, :], refs[0][...], precision=HIGHEST,
                                       preferred_element_type=F32)

    emit(x_ref[...], main)

    @pl.when(pl.program_id(0) == 0)
    def _():
        emit(xs_ref[...], side)


def _rmsnorm(x, xs, gains, proj=None, tm=ROW_TILE):
    m, d = x.shape
    s = xs.shape[0]
    n_out = gains.shape[0]
    const = lambda i: (0, 0)
    in_specs = [pl.BlockSpec((tm, d), lambda i: (i, 0)), pl.BlockSpec((s, d), const),
                pl.BlockSpec((n_out, d), const)]
    args = [x, xs, gains]
    main_shape = [jax.ShapeDtypeStruct((m, d), BF16)] * n_out
    main_specs = [pl.BlockSpec((tm, d), lambda i: (i, 0))] * n_out
    side_shape = [jax.ShapeDtypeStruct((s, d), F32)] * n_out
    side_specs = [pl.BlockSpec((s, d), const)] * n_out
    if proj is not None:
        p = proj.shape[1]
        in_specs.append(pl.BlockSpec((d, p), const))
        args.append(proj)
        main_shape.append(jax.ShapeDtypeStruct((m, p), F32))
        main_specs.append(pl.BlockSpec((tm, p), lambda i: (i, 0)))
        side_shape.append(jax.ShapeDtypeStruct((s, p), F32))
        side_specs.append(pl.BlockSpec((s, p), const))
    outs = pl.pallas_call(
        functools.partial(_norm_kernel, n_out=n_out, has_proj=proj is not None),
        grid=(m // tm,), in_specs=in_specs, out_specs=main_specs + side_specs,
        out_shape=main_shape + side_shape,
        compiler_params=_cparams(("arbitrary",)), name="rmsnorm",
    )(*args)
    k = len(main_shape)
    return outs[:k], outs[k:]


def _rope_heads(acc, cos, sin):
    outs = []
    for h in range(acc.shape[1] // ATT_HD):
        a = acc[:, h * ATT_HD:(h + 1) * ATT_HD]
        outs.append(a * cos + pltpu.roll(a, ATT_HD // 2, 1) * sin)
    return jnp.concatenate(outs, axis=1)


def _mm_kernel(te_ref, nv_ref, x_ref, *refs, n_w, has_scale, has_res, rope, dil, nat_first, side_precise):
    n = pl.program_id(0)
    m = pl.program_id(1)
    it = iter(refs)
    w_refs = [next(it) for _ in range(n_w)]
    scale_ref = next(it) if has_scale else None
    res_ref = next(it) if has_res else None
    cos_ref, sin_ref = (next(it), next(it)) if rope else (None, None)
    xs_ref = next(it)
    res_s_ref = next(it) if has_res else None
    cos_s_ref, sin_s_ref = (next(it), next(it)) if rope else (None, None)
    o_ref = next(it)
    nat_ref = next(it) if nat_first is not None else None
    os_ref = next(it)
    wb_refs = [next(it) for _ in range(n_w)]
    deint = next(it) if dil > 1 else None

    def finish(acc, up, res, cos, sin, store):
        if n_w == 2:
            acc = (acc * jax.nn.sigmoid(acc)) * up
        if has_scale:
            acc = acc * scale_ref[...]
        if has_res:
            acc = acc + res
        if rope == "all":
            store(_rope_heads(acc, cos, sin))
        elif rope == "even":
            @pl.when(n % 2 == 0)
            def _():
                store(_rope_heads(acc, cos, sin))

            @pl.when(n % 2 == 1)
            def _():
                store(acc)
        else:
            store(acc)

    def store_side(val):
        os_ref[0] = val

    def store_main(val):
        if nat_first is not None:
            @pl.when(m >= nat_first)
            def _():
                nat_ref[...] = val
        if dil == 1:
            o_ref[0] = val.astype(o_ref.dtype)
        else:
            rows = deint.shape[1] // dil
            for c in range(deint.shape[0]):
                lanes = slice(c * 128, (c + 1) * 128)
                deint[c] = val[:, lanes]
                for r in range(dil):
                    o_ref[r, :, lanes] = deint[c, pl.ds(r, rows, stride=dil), :].astype(o_ref.dtype)

    prev = jnp.maximum(m - 1, 0)
    new_weights = jnp.logical_or(m == 0, te_ref[m] != te_ref[prev])

    @pl.when(new_weights)
    def _():
        for w_ref, wb_ref in zip(w_refs, wb_refs):
            wb_ref[...] = w_ref[0].astype(BF16)
        if side_precise:
            xs = xs_ref[0]
            acc = jnp.dot(xs, w_refs[0][0], precision=HIGHEST, preferred_element_type=F32)
            up = jnp.dot(xs, w_refs[1][0], precision=HIGHEST, preferred_element_type=F32) if n_w == 2 else None
        else:
            xs = xs_ref[0].astype(BF16)
            acc = jnp.dot(xs, wb_refs[0][...], preferred_element_type=F32)
            up = jnp.dot(xs, wb_refs[1][...], preferred_element_type=F32) if n_w == 2 else None
        finish(acc, up, res_s_ref[...] if has_res else None,
               cos_s_ref[...] if rope else None, sin_s_ref[...] if rope else None, store_side)

    @pl.when(m >= nv_ref[0])
    def _():
        o_ref[...] = jnp.zeros_like(o_ref)

    @pl.when(m < nv_ref[0])
    def _():
        xb = x_ref[...].astype(BF16)
        acc = jnp.dot(xb, wb_refs[0][...], preferred_element_type=F32)
        up = jnp.dot(xb, wb_refs[1][...], preferred_element_type=F32) if n_w == 2 else None
        finish(acc, up, res_ref[...] if has_res else None,
               cos_ref[...] if rope else None, sin_ref[...] if rope else None, store_main)


def _matmul(x, ws, n_cols, out_dtype, xs, *, tm, tn, col_off=0, te=None, nvalid=None, xs_per_expert=False,
            col_scale=None, res=None, res_s=None, rope=None, cos=None, sin=None, cos_s=None, sin_s=None,
            dil=1, natural_tail=0, side_precise=True, name="matmul"):
    m, k = x.shape
    s = xs.shape[1]
    n_m = m // tm
    n_n = n_cols // tn
    n_e = ws[0].shape[0]
    if te is None:
        te = jnp.zeros((n_m,), I32)
        nvalid = jnp.full((1,), n_m, I32)

    def row(mi, nv):
        return jnp.minimum(mi, nv[0] - 1)

    def exp(mi, te, nv):
        return te[row(mi, nv)]

    in_specs = [pl.BlockSpec((tm, k), lambda n, mi, te, nv: (row(mi, nv), 0))]
    args = [x]
    for w in ws:
        in_specs.append(pl.BlockSpec((1, k, tn), lambda n, mi, te, nv: (exp(mi, te, nv), 0, n + col_off)))
        args.append(w)
    if col_scale is not None:
        in_specs.append(pl.BlockSpec((1, tn), lambda n, mi, te, nv: (0, n)))
        args.append(col_scale)
    if res is not None:
        in_specs.append(pl.BlockSpec((tm, tn), lambda n, mi, te, nv: (row(mi, nv), n)))
        args.append(res)
    if rope is not None:
        for t in (cos, sin):
            in_specs.append(pl.BlockSpec((tm, ATT_HD), lambda n, mi, te, nv: (row(mi, nv), 0)))
            args.append(t)
    if xs_per_expert:
        in_specs.append(pl.BlockSpec((1, s, k), lambda n, mi, te, nv: (exp(mi, te, nv), 0, 0)))
    else:
        in_specs.append(pl.BlockSpec((1, s, k), lambda n, mi, te, nv: (0, 0, 0)))
    args.append(xs)
    if res is not None:
        in_specs.append(pl.BlockSpec((s, tn), lambda n, mi, te, nv: (0, n)))
        args.append(res_s)
    if rope is not None:
        for t in (cos_s, sin_s):
            in_specs.append(pl.BlockSpec((s, ATT_HD), lambda n, mi, te, nv: (0, 0)))
            args.append(t)

    out_shape = [jax.ShapeDtypeStruct((dil, m // dil, n_cols), out_dtype)]
    out_specs = [pl.BlockSpec((dil, tm // dil, tn), lambda n, mi, te, nv: (0, mi, n))]
    nat_first = None
    if natural_tail:
        nat_first = n_m - natural_tail // tm
        out_shape.append(jax.ShapeDtypeStruct((natural_tail, n_cols), F32))
        out_specs.append(pl.BlockSpec((tm, tn), lambda n, mi, te, nv: (jnp.maximum(mi - nat_first, 0), n)))
    out_shape.append(jax.ShapeDtypeStruct((n_e, s, n_cols), F32))
    out_specs.append(pl.BlockSpec((1, s, tn), lambda n, mi, te, nv: (exp(mi, te, nv), 0, n)))
    scratch = [pltpu.VMEM((k, tn), BF16) for _ in ws]
    if dil > 1:
        scratch.append(pltpu.VMEM((tn // 128, tm, 128), F32))
    kern = functools.partial(_mm_kernel, n_w=len(ws), has_scale=col_scale is not None, has_res=res is not None,
                             rope=rope, dil=dil, nat_first=nat_first, side_precise=side_precise)
    return pl.pallas_call(
        kern,
        grid_spec=pltpu.PrefetchScalarGridSpec(
            num_scalar_prefetch=2, grid=(n_n, n_m), in_specs=in_specs, out_specs=out_specs,
            scratch_shapes=scratch),
        out_shape=out_shape,
        compiler_params=_cparams(("arbitrary", "arbitrary")), name=name,
    )(te, nvalid, *args)


KV_ROW = 2 * ATT_H
SHIFT_SEQS = 2
SHIFT_MAX_ROWS = 128


def _shift_chunk_rows(window):
    moved = window - 1
    return max(d for d in range(1, SHIFT_MAX_ROWS + 1) if moved % d == 0)


class _WindowShift:
    def __init__(self, buf_hbm, new_hbm, out_hbm, stage, sem_in, sem_out, sem_new, next_chunk):
        self.buf, self.new, self.out, self.stage = buf_hbm, new_hbm, out_hbm, stage
        self.sem_in, self.sem_out, self.sem_new, self.next_chunk = sem_in, sem_out, sem_new, next_chunk
        ns, self.rows = buf_hbm.shape[0], buf_hbm.shape[1]
        self.chunk = stage.shape[2]
        self.per_group = (self.rows - KV_ROW) // self.chunk
        self.total = (ns // SHIFT_SEQS) * self.per_group

    def _load(self, k, slot):
        seqs = pl.ds((k // self.per_group) * SHIFT_SEQS, SHIFT_SEQS)
        src = self.buf.at[seqs, pl.ds(KV_ROW + (k % self.per_group) * self.chunk, self.chunk)]
        return pltpu.make_async_copy(src, self.stage.at[slot], self.sem_in.at[slot])

    def _store(self, k, slot):
        seqs = pl.ds((k // self.per_group) * SHIFT_SEQS, SHIFT_SEQS)
        dst = self.out.at[seqs, pl.ds((k % self.per_group) * self.chunk, self.chunk)]
        return pltpu.make_async_copy(self.stage.at[slot], dst, self.sem_out.at[slot])

    def _append(self):
        return pltpu.make_async_copy(self.new, self.out.at[:, pl.ds(self.rows - KV_ROW, KV_ROW)], self.sem_new.at[0])

    def begin(self):
        self.next_chunk[0] = 0
        self._load(0, 0).start(priority=1)
        self._append().start()

    def advance(self):
        k = self.next_chunk[0]

        @pl.when(k < self.total)
        def _():
            slot = k % 2
            self._load(k, slot).wait()

            @pl.when(k >= 1)
            def _():
                self._store(k - 1, 1 - slot).wait()

            self._store(k, slot).start()

            @pl.when(k + 1 < self.total)
            def _():
                self._load(k + 1, 1 - slot).start(priority=1)

            self.next_chunk[0] = k + 1

    def finish(self):
        def rest(i, carry):
            self.advance()
            return carry

        lax.fori_loop(self.next_chunk[0], self.total, rest, 0)
        self._store(self.total - 1, (self.total - 1) % 2).wait()
        self._append().wait()


def _moe_mm_kernel(t0_ref, cnt_ref, nv_ref, x_hbm, *refs, n_w, tm, n_tiles, n_bg):
    n = pl.program_id(0)
    e = pl.program_id(1)
    n_e = pl.num_programs(1)
    it = iter(refs)
    w_refs = [next(it) for _ in range(n_w)]
    xs_ref = next(it)
    bg_in = [next(it) for _ in range(2 * n_bg)]
    o_hbm, os_ref = next(it), next(it)
    bg_out = [next(it) for _ in range(n_bg)]
    wb_refs = [next(it) for _ in range(n_w)]
    xbuf, obuf, sem_in, sem_out = next(it), next(it), next(it), next(it)
    tn = os_ref.shape[2]
    t0 = t0_ref[e]
    cnt = cnt_ref[e]
    first_step = jnp.logical_and(n == 0, e == 0)
    last_step = jnp.logical_and(n == pl.num_programs(0) - 1, e == n_e - 1)

    shift = _WindowShift(bg_in[0], bg_in[1], bg_out[0], *[next(it) for _ in range(5)]) if n_bg else None
    if shift:
        pl.when(first_step)(shift.begin)

    def x_copy(tile, slot):
        return pltpu.make_async_copy(x_hbm.at[pl.ds(tile * tm, tm)], xbuf.at[slot], sem_in.at[slot])

    def o_copy(tile, slot):
        return pltpu.make_async_copy(obuf.at[slot], o_hbm.at[pl.ds(tile * tm, tm), pl.ds(n * tn, tn)],
                                     sem_out.at[slot])

    def product(xb):
        acc = jnp.dot(xb, wb_refs[0][...], preferred_element_type=F32)
        if n_w == 2:
            acc = (acc * jax.nn.sigmoid(acc)) * jnp.dot(xb, wb_refs[1][...], preferred_element_type=F32)
        return acc

    @pl.when(first_step)
    def _():
        x_copy(t0, 0).start(priority=1)

    for w_ref, wb_ref in zip(w_refs, wb_refs):
        wb_ref[...] = w_ref[0].astype(BF16)
    os_ref[0] = product(xs_ref[0].astype(BF16))

    def body(t, carry):
        slot = t % 2

        @pl.when(t + 1 < cnt)
        def _():
            x_copy(t0 + t + 1, 1 - slot).start(priority=1)

        x_copy(t0 + t, slot).wait()

        @pl.when(t >= 2)
        def _():
            o_copy(t0 + t - 2, slot).wait()

        obuf[slot] = product(xbuf[slot]).astype(obuf.dtype)
        o_copy(t0 + t, slot).start()
        if shift:
            shift.advance()
        return carry

    lax.fori_loop(0, cnt, body, 0)

    @pl.when(jnp.logical_not(last_step))
    def _():
        x_copy(t0_ref[jnp.where(e == n_e - 1, 0, e + 1)], 0).start(priority=1)

    @pl.when(cnt >= 2)
    def _():
        o_copy(t0 + cnt - 2, cnt % 2).wait()

    o_copy(t0 + cnt - 1, (cnt - 1) % 2).wait()

    @pl.when(e == n_e - 1)
    def _():
        obuf[0] = jnp.zeros(obuf.shape[1:], obuf.dtype)

        def fill(tile, carry):
            o_copy(tile, 0).start()
            o_copy(tile, 0).wait()
            return carry

        lax.fori_loop(nv_ref[0], n_tiles, fill, 0)

    if shift:
        pl.when(last_step)(shift.finish)


def _moe_matmul(x, ws, out_dtype, xs, t0, cnt, nvalid, *, tm, tn, xs_per_expert, name, window=None):
    r, k = x.shape
    n_e, _, n_cols = ws[0].shape
    s = xs.shape[1]
    n_w = len(ws)
    windows = []
    if window is not None:
        cache, new = window
        ns, lb = cache.shape[0], cache.shape[1]
        assert ns % SHIFT_SEQS == 0
        windows = [(cache.reshape(ns, lb * KV_ROW, ATT_HD), new.reshape(ns, KV_ROW, ATT_HD))]
        chunk = _shift_chunk_rows(lb) * KV_ROW
    n_bg = len(windows)
    any_spec = pl.BlockSpec(memory_space=pl.ANY)
    in_specs = [any_spec]
    in_specs += [pl.BlockSpec((1, k, tn), lambda n, e, *_: (e, 0, n)) for _ in ws]
    if xs_per_expert:
        in_specs.append(pl.BlockSpec((1, s, k), lambda n, e, *_: (e, 0, 0)))
    else:
        in_specs.append(pl.BlockSpec((1, s, k), lambda n, e, *_: (0, 0, 0)))
    in_specs += [any_spec] * (2 * n_bg)
    scratch = [pltpu.VMEM((k, tn), BF16) for _ in ws]
    scratch += [pltpu.VMEM((2, tm, k), x.dtype), pltpu.VMEM((2, tm, tn), out_dtype),
                pltpu.SemaphoreType.DMA((2,)), pltpu.SemaphoreType.DMA((2,))]
    if n_bg:
        scratch += [pltpu.VMEM((2, SHIFT_SEQS, chunk, ATT_HD), F32), pltpu.SemaphoreType.DMA((2,)),
                    pltpu.SemaphoreType.DMA((2,)), pltpu.SemaphoreType.DMA((1,)), pltpu.SMEM((1,), I32)]
    out_shape = [jax.ShapeDtypeStruct((r, n_cols), out_dtype), jax.ShapeDtypeStruct((n_e, s, n_cols), F32)]
    out_shape += [jax.ShapeDtypeStruct(buf.shape, buf.dtype) for buf, _ in windows]
    outs = pl.pallas_call(
        functools.partial(_moe_mm_kernel, n_w=n_w, tm=tm, n_tiles=r // tm, n_bg=n_bg),
        grid_spec=pltpu.PrefetchScalarGridSpec(
            num_scalar_prefetch=3, grid=(n_cols // tn, n_e), in_specs=in_specs,
            out_specs=[any_spec, pl.BlockSpec((1, s, tn), lambda n, e, *_: (e, 0, n))] + [any_spec] * n_bg,
            scratch_shapes=scratch),
        out_shape=out_shape,
        compiler_params=_cparams(("arbitrary", "arbitrary")), name=name,
    )(t0, cnt, nvalid, x, *ws, xs, *[a for pair in windows for a in pair])
    if window is not None:
        return outs[0], outs[1], outs[2].reshape(window[0].shape)
    return outs


def _log_sigmoid(x):
    return jnp.minimum(x, 0.0) - jnp.log1p(jnp.exp(-jnp.abs(x)))


def _mlstm_prompt_kernel(q_ref, k_ref, v_ref, o_ref, gi_ref, gf_ref, bi_ref, bf_ref, bo_ref, ghn_ref,
                         h_ref, c_out_ref, n_out_ref, m_out_ref, ct_s, n_s, m_s):
    c = pl.program_id(0)
    L = ML_CHUNK

    @pl.when(c == 0)
    def _():
        ct_s[...] = jnp.zeros_like(ct_s)
        n_s[...] = jnp.zeros_like(n_s)
        m_s[...] = jnp.zeros_like(m_s)

    ig = gi_ref[...] + bi_ref[...]
    lf = _log_sigmoid(gf_ref[...] + bf_ref[...])
    r = lax.broadcasted_iota(I32, (L, L), 0)
    s = lax.broadcasted_iota(I32, (L, L), 1)
    causal = r >= s
    tril = causal.astype(F32)
    b = jnp.dot(tril, lf, precision=HIGHEST, preferred_element_type=F32)
    b_t = b.T
    ig_t = ig.T
    m_all = m_s[...]
    m_new_all = m_all
    lane = lax.broadcasted_iota(I32, (1, 128), 1)

    for h in range(ML_HEADS):
        qh = q_ref[:, h * ML_DQK:(h + 1) * ML_DQK]
        kh = k_ref[:, h * ML_DQK:(h + 1) * ML_DQK]
        vh = v_ref[:, h * ML_DV:(h + 1) * ML_DV]
        bc = b[:, h:h + 1]
        ic = ig[:, h:h + 1]
        br = b_t[h:h + 1, :]
        ir = ig_t[h:h + 1, :]
        m_h = m_all[:, h:h + 1]
        logd = jnp.where(causal, bc - br + ir, -jnp.inf)
        inter = bc + m_h
        mt = jnp.maximum(inter, jnp.max(logd, axis=1, keepdims=True))
        sc = lax.dot_general(qh, kh, (((1,), (1,)), ((), ())), preferred_element_type=F32)
        a = sc * jnp.exp(logd - mt)
        w_inter = jnp.exp(inter - mt)
        ct_h = ct_s[h]
        qc = jnp.dot(qh, ct_h.astype(BF16), preferred_element_type=F32)
        num = jnp.dot(a.astype(BF16), vh, preferred_element_type=F32) + w_inter * qc
        n_h = n_s[h:h + 1, :]
        qn = jnp.sum(qh.astype(F32) * n_h, axis=1, keepdims=True)
        den = jnp.sum(a, axis=1, keepdims=True) + w_inter * qn
        hh = num / jnp.maximum(jnp.abs(den), jnp.exp(-mt))
        hn = _rms(hh) * ghn_ref[:, h * ML_DV:(h + 1) * ML_DV]
        og = jax.nn.sigmoid(o_ref[:, h * ML_DV:(h + 1) * ML_DV].astype(F32) + bo_ref[:, h * ML_DV:(h + 1) * ML_DV])
        h_ref[:, h * ML_DV:(h + 1) * ML_DV] = (hn * og).astype(h_ref.dtype)
        m_new = mt[L - 1:L, :]
        b_last = bc[L - 1:L, :]
        decay = jnp.exp(b_last + m_h - m_new)
        wj = jnp.exp(b_last - bc + ic - m_new)
        kw = (kh.astype(F32) * wj).astype(BF16)
        upd = lax.dot_general(kw, vh, (((0,), (0,)), ((), ())), preferred_element_type=F32)
        ct_s[h] = decay * ct_h + upd
        n_s[h:h + 1, :] = decay * n_h + jnp.sum(kh.astype(F32) * wj, axis=0, keepdims=True)
        m_new_all = jnp.where(lane == h, m_new, m_new_all)

    m_s[...] = m_new_all

    @pl.when(c == pl.num_programs(0) - 1)
    def _():
        for h in range(ML_HEADS):
            c_out_ref[h] = ct_s[h].T
        n_out_ref[...] = n_s[...]
        m_out_ref[...] = m_s[...]


def _mlstm_prompt(z, gates, bi, bf, bo, ghn):
    seq = z.shape[0]
    nc = seq // ML_CHUNK
    L = ML_CHUNK
    const2 = lambda c: (0, 0)
    in_specs = [
        pl.BlockSpec((L, ML_NQK), lambda c: (c, 0)),
        pl.BlockSpec((L, ML_NQK), lambda c: (c, 1)),
        pl.BlockSpec((L, ML_NV), lambda c: (c, 1)),
        pl.BlockSpec((L, D_MODEL), lambda c: (c, 2)),
        pl.BlockSpec((L, 128), lambda c: (c, 0)),
        pl.BlockSpec((L, 128), lambda c: (c, 1)),
        pl.BlockSpec((1, 128), const2),
        pl.BlockSpec((1, 128), const2),
        pl.BlockSpec((1, D_MODEL), const2),
        pl.BlockSpec((1, ML_NV), const2),
    ]
    out_shape = [
        jax.ShapeDtypeStruct((seq, ML_NV), BF16),
        jax.ShapeDtypeStruct((ML_HEADS, ML_DV, ML_DQK), F32),
        jax.ShapeDtypeStruct((ML_HEADS, ML_DQK), F32),
        jax.ShapeDtypeStruct((1, 128), F32),
    ]
    out_specs = [
        pl.BlockSpec((L, ML_NV), lambda c: (c, 0)),
        pl.BlockSpec((ML_HEADS, ML_DV, ML_DQK), lambda c: (0, 0, 0)),
        pl.BlockSpec((ML_HEADS, ML_DQK), const2),
        pl.BlockSpec((1, 128), const2),
    ]
    return pl.pallas_call(
        _mlstm_prompt_kernel, grid=(nc,), in_specs=in_specs, out_specs=out_specs, out_shape=out_shape,
        scratch_shapes=[pltpu.VMEM((ML_HEADS, ML_DQK, ML_DV), F32), pltpu.VMEM((ML_HEADS, ML_DQK), F32),
                        pltpu.VMEM((1, 128), F32)],
        compiler_params=_cparams(("arbitrary",)), name="mlstm_prompt",
    )(z, z, z, z, gates, gates, bi, bf, bo, ghn)


def _mlstm_sample_kernel(z_ref, g_ref, c_ref, n_ref, m_ref, bi_ref, bf_ref, bo_ref, ghn_ref,
                         h_ref, c_out_ref, n_out_ref, m_out_ref):
    i = pl.program_id(0)
    z = z_ref[0]
    g = g_ref[0]
    ig_all = g[:, 0:128] + bi_ref[...]
    lf_all = _log_sigmoid(g[:, 128:256] + bf_ref[...])
    m_all = m_ref[0]
    mt_all = jnp.maximum(lf_all + m_all, ig_all)
    m_out_ref[0] = mt_all
    outs = []
    for h in range(ML_HEADS):
        q = z[:, h * ML_DQK:(h + 1) * ML_DQK]
        k = z[:, ML_NQK + h * ML_DQK:ML_NQK + (h + 1) * ML_DQK]
        v = z[:, 2 * ML_NQK + h * ML_DV:2 * ML_NQK + (h + 1) * ML_DV]
        ig = ig_all[:, h:h + 1]
        lf = lf_all[:, h:h + 1]
        m0 = m_all[:, h:h + 1]
        mt = mt_all[:, h:h + 1]
        w_inter = jnp.exp(lf + m0 - mt)
        wj = jnp.exp(ig - mt)
        a = jnp.sum(q * k, axis=1, keepdims=True) * wj
        c_h = c_ref[0, h]
        n_h = n_ref[0, h:h + 1, :]
        q8 = jnp.broadcast_to(q, (8, ML_DQK))
        cq = lax.dot_general(q8, c_h, (((1,), (1,)), ((), ())), precision=HIGHEST,
                             preferred_element_type=F32)[0:1, :]
        num = a * v + w_inter * cq
        den = a + w_inter * jnp.sum(n_h * q, axis=1, keepdims=True)
        hh = num / jnp.maximum(jnp.abs(den), jnp.exp(-mt))
        hn = _rms(hh) * ghn_ref[:, h * ML_DV:(h + 1) * ML_DV]
        og = jax.nn.sigmoid(z[:, 2 * ML_NQK + ML_NV + h * ML_DV:2 * ML_NQK + ML_NV + (h + 1) * ML_DV]
                            + bo_ref[:, h * ML_DV:(h + 1) * ML_DV])
        outs.append(hn * og)
        v_col = jnp.broadcast_to(v, (8, ML_DV)).T[:, 0:1]
        c_out_ref[0, h] = w_inter * c_h + wj * (v_col * k)
        n_out_ref[0, h:h + 1, :] = w_inter * n_h + wj * k
    h_ref[pl.ds(i, 1), :] = jnp.concatenate(outs, axis=1)


def _mlstm_sample(z_s, gates_s, c0, n0, m0, bi, bf, bo, ghn):
    ns = z_s.shape[0]
    const2 = lambda i: (0, 0)
    in_specs = [
        pl.BlockSpec((1, 1, ML_MAIN), lambda i: (i, 0, 0)),
        pl.BlockSpec((1, 1, 256), lambda i: (i, 0, 0)),
        pl.BlockSpec((1, ML_HEADS, ML_DV, ML_DQK), lambda i: (i, 0, 0, 0)),
        pl.BlockSpec((1, ML_HEADS, ML_DQK), lambda i: (i, 0, 0)),
        pl.BlockSpec((1, 1, 128), lambda i: (i, 0, 0)),
        pl.BlockSpec((1, 128), const2),
        pl.BlockSpec((1, 128), const2),
        pl.BlockSpec((1, D_MODEL), const2),
        pl.BlockSpec((1, ML_NV), const2),
    ]
    out_shape = [
        jax.ShapeDtypeStruct((ns, ML_NV), F32),
        jax.ShapeDtypeStruct((ns, ML_HEADS, ML_DV, ML_DQK), F32),
        jax.ShapeDtypeStruct((ns, ML_HEADS, ML_DQK), F32),
        jax.ShapeDtypeStruct((ns, 1, 128), F32),
    ]
    out_specs = [
        pl.BlockSpec((ns, ML_NV), const2),
        pl.BlockSpec((1, ML_HEADS, ML_DV, ML_DQK), lambda i: (i, 0, 0, 0)),
        pl.BlockSpec((1, ML_HEADS, ML_DQK), lambda i: (i, 0, 0)),
        pl.BlockSpec((1, 1, 128), lambda i: (i, 0, 0)),
    ]
    return pl.pallas_call(
        _mlstm_sample_kernel, grid=(ns,), in_specs=in_specs, out_specs=out_specs, out_shape=out_shape,
        compiler_params=_cparams(("arbitrary",)), name="mlstm_sample",
    )(z_s, gates_s, c0, n0, m0, bi, bf, bo, ghn)


def _attn_prompt_kernel(q_ref, kp_ref, kc_ref, vp_ref, vc_ref, o_ref, lse_ref):
    blk = pl.program_id(1)
    T = ATT_STEPS
    qi = lax.broadcasted_iota(I32, (T, T), 0)
    kj = lax.broadcasted_iota(I32, (T, T), 1)
    valid_prev = jnp.logical_and(kj >= qi, blk > 0)
    valid_cur = kj <= qi
    scale = ATT_HD ** -0.5
    nt = (((1,), (1,)), ((), ()))
    lane = lax.broadcasted_iota(I32, (T, ATT_HD), 1)
    lse_all = jnp.zeros((T, ATT_HD), F32)
    for h in range(ATT_H):
        sl = slice(h * ATT_HD, (h + 1) * ATT_HD)
        qh = q_ref[:, sl]
        s1 = jnp.where(valid_prev, lax.dot_general(qh, kp_ref[:, sl], nt, preferred_element_type=F32) * scale,
                       -jnp.inf)
        s2 = jnp.where(valid_cur, lax.dot_general(qh, kc_ref[:, sl], nt, preferred_element_type=F32) * scale,
                       -jnp.inf)
        mx = jnp.maximum(jnp.max(s1, axis=1, keepdims=True), jnp.max(s2, axis=1, keepdims=True))
        p1 = jnp.exp(s1 - mx)
        p2 = jnp.exp(s2 - mx)
        den = jnp.sum(p1, axis=1, keepdims=True) + jnp.sum(p2, axis=1, keepdims=True)
        acc = jnp.dot(p1.astype(BF16), vp_ref[:, sl], preferred_element_type=F32)
        acc = acc + jnp.dot(p2.astype(BF16), vc_ref[:, sl], preferred_element_type=F32)
        o_ref[:, sl] = acc / den
        lse_all = jnp.where(lane == h, mx + jnp.log(den), lse_all)
    lse_ref[...] = lse_all


def _attn_prompt(q, kv, g):
    dil, L, _ = q.shape
    nb = L // ATT_STEPS
    T = ATT_STEPS
    blk = (None, T, ATT_GW)
    in_specs = [
        pl.BlockSpec(blk, lambda r, b: (r, b, 0)),
        pl.BlockSpec(blk, lambda r, b: (r, jnp.maximum(b - 1, 0), 0)),
        pl.BlockSpec(blk, lambda r, b: (r, b, 0)),
        pl.BlockSpec(blk, lambda r, b: (r, jnp.maximum(b - 1, 0), 1)),
        pl.BlockSpec(blk, lambda r, b: (r, b, 1)),
    ]
    out_specs = [pl.BlockSpec(blk, lambda r, b: (r, b, 0)), pl.BlockSpec((None, T, ATT_HD), lambda r, b: (r, b, 0))]
    return pl.pallas_call(
        _attn_prompt_kernel, grid=(dil, nb), in_specs=in_specs, out_specs=out_specs,
        out_shape=[jax.ShapeDtypeStruct((dil, L, ATT_GW), F32), jax.ShapeDtypeStruct((dil, L, ATT_HD), F32)],
        compiler_params=_cparams(("arbitrary", "arbitrary")), name=f"attn_prompt_g{g}",
    )(q, kv, kv, kv, kv)


def _merge_kernel(*refs):
    in_refs, out_ref, scratch = refs[:2 * ATT_G], refs[2 * ATT_G], refs[2 * ATT_G + 1:]
    tm = out_ref.shape[0]

    def position_order(ref, lanes, buf, dil):
        if dil == 1:
            return ref[0, :, lanes]
        for r in range(dil):
            buf[pl.ds(r, tm // dil, stride=dil), :] = ref[r, :, lanes]
        return buf[...]

    all128 = slice(0, ATT_HD)
    lses = [position_order(in_refs[2 * g + 1], all128, scratch[2 * g + 1], ATT_DILS[g]) for g in range(ATT_G)]
    mx = jnp.maximum(jnp.maximum(lses[0], lses[1]), lses[2])
    es = [jnp.exp(l - mx) for l in lses]
    tot = es[0] + es[1] + es[2]
    wgt = [e / tot for e in es]
    for c in range(ATT_H):
        lanes = slice(c * ATT_HD, (c + 1) * ATT_HD)
        acc = None
        for g in range(ATT_G):
            o = position_order(in_refs[2 * g], lanes, scratch[2 * g], ATT_DILS[g])
            term = wgt[g][:, c:c + 1] * o
            acc = term if acc is None else acc + term
        out_ref[:, lanes] = acc.astype(out_ref.dtype)


def _merge_groups(parts, seq, tm=512):
    in_specs, args, scratch = [], [], []
    for g, pair in enumerate(parts):
        dil = ATT_DILS[g]
        for a in pair:
            in_specs.append(pl.BlockSpec((dil, tm // dil, a.shape[2]), lambda i: (0, i, 0)))
            args.append(a)
            scratch.append(pltpu.VMEM((tm, ATT_HD), F32))
    return pl.pallas_call(
        _merge_kernel, grid=(seq // tm,), in_specs=in_specs,
        out_specs=pl.BlockSpec((tm, ATT_GW), lambda i: (i, 0)),
        out_shape=jax.ShapeDtypeStruct((seq, ATT_GW), BF16), scratch_shapes=scratch,
        compiler_params=_cparams(("arbitrary",)), name="attn_merge",
    )(*args)


def _attn_sample_kernel(q_ref, kvn_ref, b0_ref, b1_ref, b2_ref, out_ref):
    scale = ATT_HD ** -0.5
    outs, lses = [], []
    for g, b_ref in enumerate((b0_ref, b1_ref, b2_ref)):
        qg = q_ref[0, g]
        kn = kvn_ref[0, g, 0]
        vn = kvn_ref[0, g, 1]
        kb = b_ref[:, 0]
        vb = b_ref[:, 1]
        s = jnp.sum(kb * qg[None], axis=2, keepdims=True) * scale
        s_new = jnp.sum(kn * qg, axis=1, keepdims=True) * scale
        mx = jnp.maximum(jnp.max(s, axis=0), s_new)
        p = jnp.exp(s - mx[None])
        p_new = jnp.exp(s_new - mx)
        den = jnp.sum(p, axis=0) + p_new
        o = jnp.sum(p * vb, axis=0) + p_new * vn
        outs.append(o / den)
        lses.append(mx + jnp.log(den))
    mxl = jnp.maximum(jnp.maximum(lses[0], lses[1]), lses[2])
    es = [jnp.exp(l - mxl) for l in lses]
    tot = es[0] + es[1] + es[2]
    out_ref[0] = (es[0] / tot) * outs[0] + (es[1] / tot) * outs[1] + (es[2] / tot) * outs[2]


def _attn_sample(q_s, kv_s, caches):
    ns = q_s.shape[0]
    views, specs = [], []
    for g, cbuf in enumerate(caches):
        lb = cbuf.shape[1]
        dil = ATT_DILS[g]
        views.append(cbuf.reshape(ns, lb // dil, dil, 2, ATT_H, ATT_HD))
        specs.append(pl.BlockSpec((None, ATT_STEPS, None, 2, ATT_H, ATT_HD), lambda i: (i, 0, 0, 0, 0, 0)))
    in_specs = [
        pl.BlockSpec((1, ATT_G, ATT_H, ATT_HD), lambda i: (i, 0, 0, 0)),
        pl.BlockSpec((1, ATT_G, 2, ATT_H, ATT_HD), lambda i: (i, 0, 0, 0, 0)),
    ] + specs
    return pl.pallas_call(
        _attn_sample_kernel, grid=(ns,), in_specs=in_specs,
        out_specs=pl.BlockSpec((1, ATT_H, ATT_HD), lambda i: (i, 0, 0)),
        out_shape=jax.ShapeDtypeStruct((ns, ATT_H, ATT_HD), F32),
        compiler_params=_cparams(("arbitrary",)), name="attn_sample",
    )(q_s, kv_s, *views)


KV_SHIFT_BLOCK = 8192


def _kv_shift_kernel(cur_ref, nxt_ref, new_ref, out_ref):
    blk = out_ref.shape[1]
    out_ref[0, :blk - KV_ROW] = cur_ref[0, KV_ROW:]
    last = pl.program_id(1) == pl.num_programs(1) - 1
    out_ref[0, blk - KV_ROW:] = jnp.where(last, new_ref[0], nxt_ref[0])


def _kv_shift(cache, new):
    ns, lb = cache.shape[0], cache.shape[1]
    rows = lb * KV_ROW
    blk = min(KV_SHIFT_BLOCK, rows)
    nb = rows // blk
    per = blk // KV_ROW
    flat = cache.reshape(ns, rows, ATT_HD)
    out = pl.pallas_call(
        _kv_shift_kernel, grid=(ns, nb),
        in_specs=[pl.BlockSpec((1, blk, ATT_HD), lambda i, j: (i, j, 0)),
                  pl.BlockSpec((1, KV_ROW, ATT_HD), lambda i, j: (i, jnp.minimum((j + 1) * per, lb - 1), 0)),
                  pl.BlockSpec((1, KV_ROW, ATT_HD), lambda i, j: (i, 0, 0))],
        out_specs=pl.BlockSpec((1, blk, ATT_HD), lambda i, j: (i, j, 0)),
        out_shape=jax.ShapeDtypeStruct(flat.shape, flat.dtype),
        compiler_params=_cparams(("arbitrary", "arbitrary")), name="kv_shift",
    )(flat, flat, new.reshape(ns, KV_ROW, ATT_HD))
    return out.reshape(cache.shape)


def _top2(y, wr_ref, br_ref):
    rows = y.shape[0]
    lane = lax.broadcasted_iota(I32, (rows, 128), 1)
    logits = jnp.dot(y, wr_ref[...], precision=HIGHEST, preferred_element_type=F32) + br_ref[...]
    logits = jnp.where(lane < N_EXPERTS, logits, NEG_BIG)
    e = jnp.exp(logits - jnp.max(logits, axis=1, keepdims=True))
    probs = e / jnp.sum(e, axis=1, keepdims=True)
    p1 = jnp.max(probs, axis=1, keepdims=True)
    i1 = jnp.min(jnp.where(probs == p1, lane, 128), axis=1, keepdims=True)
    probs2 = jnp.where(lane == i1, -1.0, probs)
    p2 = jnp.max(probs2, axis=1, keepdims=True)
    i2 = jnp.min(jnp.where(probs2 == p2, lane, 128), axis=1, keepdims=True)
    tot = p1 + p2
    return lane, i1, i2, p1 / tot, p2 / tot


def _router_kernel(x_ref, xs_ref, g_ref, wr_ref, br_ref, xn_ref, eid_ref, gate_ref, rank_ref, cnt_ref,
                   xns_ref, gs_ref, carry):
    i = pl.program_id(0)
    tm = x_ref.shape[0]

    @pl.when(i == 0)
    def _():
        carry[...] = jnp.zeros_like(carry)
        ys = _rms(xs_ref[...]) * g_ref[...]
        xns_ref[...] = ys
        lane, i1, i2, g1, g2 = _top2(ys, wr_ref, br_ref)
        gs_ref[...] = jnp.where(lane == i1, g1, jnp.where(lane == i2, g2, 0.0))

    y = _rms(x_ref[...]) * g_ref[...]
    xn_ref[...] = y
    lane, i1, i2, g1, g2 = _top2(y, wr_ref, br_ref)
    sel1 = lane == i1
    sel2 = lane == i2
    onehot = jnp.where(jnp.logical_or(sel1, sel2), 1.0, 0.0)
    rr = lax.broadcasted_iota(I32, (tm, tm), 0)
    cc = lax.broadcasted_iota(I32, (tm, tm), 1)
    before = (cc < rr).astype(BF16)
    prefix = jnp.dot(before, onehot.astype(BF16), preferred_element_type=F32) + carry[...]
    r1 = jnp.sum(jnp.where(sel1, prefix, 0.0), axis=1, keepdims=True)
    r2 = jnp.sum(jnp.where(sel2, prefix, 0.0), axis=1, keepdims=True)
    carry[...] = carry[...] + jnp.sum(onehot, axis=0, keepdims=True)
    eid_ref[...] = jnp.where(lane == 0, i1, jnp.where(lane == 1, i2, 0))
    gate_ref[...] = jnp.where(lane == 0, g1, jnp.where(lane == 1, g2, 0.0))
    rank_ref[...] = jnp.where(lane == 0, r1, jnp.where(lane == 1, r2, 0.0)).astype(I32)
    cnt_ref[...] = jnp.broadcast_to(carry[...], cnt_ref.shape)


def _router(h, hs, gain, w_router_pad, b_router_pad, tm=ROW_TILE):
    m, d = h.shape
    s = hs.shape[0]
    const = lambda i: (0, 0)
    row_spec = pl.BlockSpec((tm, 128), lambda i: (i, 0))
    return pl.pallas_call(
        _router_kernel, grid=(m // tm,),
        in_specs=[pl.BlockSpec((tm, d), lambda i: (i, 0)), pl.BlockSpec((s, d), const), pl.BlockSpec((1, d), const),
                  pl.BlockSpec((d, 128), const), pl.BlockSpec((1, 128), const)],
        out_specs=[pl.BlockSpec((tm, d), lambda i: (i, 0)), row_spec, row_spec, row_spec,
                   pl.BlockSpec((8, 128), const), pl.BlockSpec((s, d), const), pl.BlockSpec((s, 128), const)],
        out_shape=[jax.ShapeDtypeStruct((m, d), F32), jax.ShapeDtypeStruct((m, 128), I32),
                   jax.ShapeDtypeStruct((m, 128), F32), jax.ShapeDtypeStruct((m, 128), I32),
                   jax.ShapeDtypeStruct((8, 128), F32), jax.ShapeDtypeStruct((s, d), F32),
                   jax.ShapeDtypeStruct((s, 128), F32)],
        scratch_shapes=[pltpu.VMEM((1, 128), F32)],
        compiler_params=_cparams(("arbitrary",)), name="router",
    )(h, hs, gain, w_router_pad, b_router_pad)


def _dispatch_kernel(pos_ref, nv_ref, x_hbm, out_ref, inv, buf, sem, *, n_tok):
    i = pl.program_id(0)
    tg = out_ref.shape[0]

    @pl.when(i == 0)
    def _():
        def clear(s, c):
            inv[s] = 0
            return c

        lax.fori_loop(0, inv.shape[0], clear, 0, unroll=8)

        def fill(t, c):
            inv[pos_ref[2 * t]] = t
            inv[pos_ref[2 * t + 1]] = t
            return c

        lax.fori_loop(0, n_tok, fill, 0, unroll=8)

    def start_gather(tile):
        slot = tile % 2

        def issue(r2, c):
            for j in range(2):
                r = 2 * r2 + j
                pltpu.make_async_copy(x_hbm.at[pl.ds(inv[tile * tg + r], 1)], buf.at[slot, pl.ds(r, 1)],
                                      sem.at[slot]).start(priority=j)
            return c

        lax.fori_loop(0, tg // 2, issue, 0, unroll=4)

    @pl.when(i == 0)
    def _():
        start_gather(i)

    @pl.when(i + 1 < nv_ref[0])
    def _():
        start_gather(i + 1)

    @pl.when(i < nv_ref[0])
    def _():
        slot = i % 2
        pltpu.make_async_copy(x_hbm.at[pl.ds(0, tg)], buf.at[slot], sem.at[slot]).wait()
        out_ref[...] = buf[slot].astype(out_ref.dtype)

    @pl.when(i >= nv_ref[0])
    def _():
        out_ref[...] = jnp.zeros_like(out_ref)


def _dispatch(pos_flat, nvalid, xn, n_tiles, tg):
    n_tok, d = xn.shape
    return pl.pallas_call(
        functools.partial(_dispatch_kernel, n_tok=n_tok),
        grid_spec=pltpu.PrefetchScalarGridSpec(
            num_scalar_prefetch=2, grid=(n_tiles,),
            in_specs=[pl.BlockSpec(memory_space=pl.ANY)],
            out_specs=pl.BlockSpec((tg, d), lambda i, pos, nv: (i, 0)),
            scratch_shapes=[pltpu.SMEM((n_tiles * tg,), I32), pltpu.VMEM((2, tg, d), xn.dtype),
                            pltpu.SemaphoreType.DMA((2,))]),
        out_shape=jax.ShapeDtypeStruct((n_tiles * tg, d), BF16),
        compiler_params=_cparams(("arbitrary",)), name="moe_dispatch",
    )(pos_flat, nvalid, xn)


def _combine_kernel(pos_ref, h_ref, gate_ref, g_ref, hs_ref, gs_ref, ys_s_ref, ys_hbm, out_ref, outs_ref, ybuf, sem):
    i = pl.program_id(0)
    tm = h_ref.shape[0]

    def start_gather(tile):
        slot = tile % 2

        def issue(r, carry):
            for j in range(2):
                pltpu.make_async_copy(ys_hbm.at[pl.ds(pos_ref[2 * (tile * tm + r) + j], 1)],
                                      ybuf.at[slot, j, pl.ds(r, 1)], sem.at[slot]).start(priority=j)
            return carry

        lax.fori_loop(0, tm, issue, 0, unroll=4)

    @pl.when(i == 0)
    def _():
        start_gather(i)

    @pl.when(i + 1 < pl.num_programs(0))
    def _():
        start_gather(i + 1)

    @pl.when(i == 0)
    def _():
        gs = gs_ref[...]
        y = jnp.zeros(hs_ref.shape, F32)
        for e in range(N_EXPERTS):
            y = y + gs[:, e:e + 1] * ys_s_ref[e]
        outs_ref[...] = _rms(hs_ref[...] + y) * g_ref[...]

    slot = i % 2
    for j in range(2):
        pltpu.make_async_copy(ys_hbm.at[pl.ds(0, tm)], ybuf.at[slot, j], sem.at[slot]).wait()
    gate = gate_ref[...]
    y = h_ref[...] + (gate[:, 0:1] * ybuf[slot, 0] + gate[:, 1:2] * ybuf[slot, 1])
    out_ref[...] = _rms(y) * g_ref[...]


def _combine(pos_flat, h, gate, g_final, hs, gs, ys_s, ys, tm=ROW_TILE):
    m, d = h.shape
    s = hs.shape[0]
    c2 = lambda i, pos: (0, 0)
    return pl.pallas_call(
        _combine_kernel,
        grid_spec=pltpu.PrefetchScalarGridSpec(
            num_scalar_prefetch=1, grid=(m // tm,),
            in_specs=[pl.BlockSpec((tm, d), lambda i, pos: (i, 0)),
                      pl.BlockSpec((tm, 128), lambda i, pos: (i, 0)),
                      pl.BlockSpec((1, d), c2),
                      pl.BlockSpec((s, d), c2),
                      pl.BlockSpec((s, 128), c2),
                      pl.BlockSpec((N_EXPERTS, s, d), lambda i, pos: (0, 0, 0)),
                      pl.BlockSpec(memory_space=pl.ANY)],
            out_specs=[pl.BlockSpec((tm, d), lambda i, pos: (i, 0)), pl.BlockSpec((s, d), c2)],
            scratch_shapes=[pltpu.VMEM((2, 2, tm, d), F32), pltpu.SemaphoreType.DMA((2,))]),
        out_shape=[jax.ShapeDtypeStruct((m, d), F32), jax.ShapeDtypeStruct((s, d), F32)],
        compiler_params=_cparams(("arbitrary",)), name="moe_combine",
    )(pos_flat, h, gate, g_final, hs, gs, ys_s, ys)


def _rope_tables(pos):
    half = ATT_HD // 2
    inv = ROPE_THETA ** (-jnp.arange(half, dtype=F32) / half)
    ang = pos.astype(F32)[:, None] * inv[None, :]
    cos, sin = jnp.cos(ang), jnp.sin(ang)
    return jnp.concatenate([cos, cos], axis=1), jnp.concatenate([-sin, sin], axis=1)


def kernel(x_prompt, x_sample, state_mlstm_C, state_mlstm_n, state_mlstm_m, cache_kv_w128, cache_kv_w512, cache_kv_w2048, g_mix, g_ffn, w_ml_in, b_ml_gates, b_ml_o, g_ml_hnorm, w_ml_out, g_kv, w_kv, w_q, w_o, w_ffn_gate, w_ffn_up, w_ffn_down, w_router, b_router, w_exp_gate, w_exp_up, w_exp_down, g_final):
    bp, seq, d = x_prompt.shape
    ns = x_sample.shape[0]
    caches = (cache_kv_w128, cache_kv_w512, cache_kv_w2048)
    assert bp == 1 and x_sample.shape[1] == 1 and d == D_MODEL and ns % 8 == 0
    assert seq % (ATT_STEPS * max(ATT_DILS)) == 0 and seq % 1024 == 0
    assert all(c.shape[1] == w for c, w in zip(caches, ATT_WINDOWS))
    tm = 512

    h0 = x_prompt.reshape(seq, d)
    h0_s = x_sample.reshape(ns, d)

    w_gates = lax.slice_in_dim(w_ml_in, ML_MAIN, ML_MAIN + 2 * ML_HEADS, axis=2)[0]
    w_gates_pad = jnp.zeros((d, 256), F32).at[:, 0:ML_HEADS].set(w_gates[:, :ML_HEADS])
    w_gates_pad = w_gates_pad.at[:, 128:128 + ML_HEADS].set(w_gates[:, ML_HEADS:])
    bi = jnp.zeros((1, 128), F32).at[0, :ML_HEADS].set(b_ml_gates[0, :ML_HEADS])
    bf = jnp.zeros((1, 128), F32).at[0, :ML_HEADS].set(b_ml_gates[0, ML_HEADS:])
    (xn0, gates), (xn0_s, gates_s) = _rmsnorm(h0, h0_s, g_mix[0:1], proj=w_gates_pad)
    k_scale = jnp.concatenate([jnp.ones((1, ML_NQK), F32), jnp.full((1, ML_NQK), ML_DQK ** -0.5, F32),
                               jnp.ones((1, ML_NV + D_MODEL), F32)], axis=1)
    z, z_s = _matmul(xn0, [w_ml_in], ML_MAIN, BF16, xn0_s[None], tm=1024, tn=1024, col_scale=k_scale, name="ml_in")
    bo = b_ml_o[0:1]
    ghn = g_ml_hnorm[0:1]
    hg, p_c, p_n, p_m = _mlstm_prompt(z[0], gates, bi, bf, bo, ghn)
    m0 = jnp.zeros((ns, 1, 128), F32).at[:, 0, :ML_HEADS].set(state_mlstm_m[0])
    hg_s, s_c, s_n, s_m = _mlstm_sample(z_s[0].reshape(ns, 1, ML_MAIN), gates_s.reshape(ns, 1, 256),
                                        state_mlstm_C[0], state_mlstm_n[0], m0, bi, bf, bo, ghn)
    h1, h1_s = _matmul(hg, [w_ml_out], d, F32, hg_s[None], tm=tm, tn=1024, res=h0, res_s=h0_s, name="ml_out")
    h1, h1_s = h1[0], h1_s[0]

    (xf0,), (xf0_s,) = _rmsnorm(h1, h1_s, g_ffn[0:1])
    ffn_dense = w_ffn_gate.shape[2]
    hid, hid_s = _matmul(xf0, [w_ffn_gate, w_ffn_up], ffn_dense, BF16, xf0_s[None], tm=1024, tn=512, name="ffn_up")
    h2, h2_s = _matmul(hid[0], [w_ffn_down], d, F32, hid_s, tm=tm, tn=512, res=h1, res_s=h1_s, name="ffn_down")
    h2, h2_s = h2[0], h2_s[0]

    (xq, xkv), (xq_s, xkv_s) = _rmsnorm(h2, h2_s, jnp.stack([g_mix[1], g_kv]))
    cos, sin = _rope_tables(jnp.arange(seq))
    cos_s, sin_s = _rope_tables(jnp.full((ns,), PAST_LEN, I32))
    rope_args = dict(cos=cos, sin=sin, cos_s=cos_s, sin_s=sin_s)
    parts, kv_nat, kv_new, q_new = [], [], [], []
    tail = min(max(ATT_WINDOWS), seq)
    for g in range(ATT_G):
        dil = ATT_DILS[g]
        kvd, kvn, kv_s = _matmul(xkv, [w_kv[None]], 2 * ATT_GW, BF16, xkv_s[None], tm=tm, tn=ATT_GW,
                                 col_off=2 * g, rope="even", dil=dil, natural_tail=tail, name=f"kv_proj_g{g}",
                                 **rope_args)
        qd, q_s = _matmul(xq, [w_q], ATT_GW, BF16, xq_s[None], tm=tm, tn=ATT_GW, col_off=g, rope="all",
                          dil=dil, name=f"q_proj_g{g}", **rope_args)
        parts.append(_attn_prompt(qd, kvd, g))
        kv_nat.append(kvn)
        kv_new.append(kv_s[0])
        q_new.append(q_s[0])
    att = _merge_groups(parts, seq)
    att_s = _attn_sample(jnp.stack(q_new, axis=1).reshape(ns, ATT_G, ATT_H, ATT_HD),
                         jnp.stack(kv_new, axis=1).reshape(ns, ATT_G, 2, ATT_H, ATT_HD), caches)
    h3, h3_s = _matmul(att, [w_o], d, F32, att_s.reshape(1, ns, ATT_GW), tm=tm, tn=1024, res=h2, res_s=h2_s,
                       name="attn_out")
    h3, h3_s = h3[0], h3_s[0]

    wr_pad = jnp.zeros((d, 128), F32).at[:, :N_EXPERTS].set(w_router[0])
    br_pad = jnp.zeros((1, 128), F32).at[0, :N_EXPERTS].set(b_router[0])
    xn2, eid, gate, rank, cnt, xn2_s, gates_moe_s = _router(h3, h3_s, g_ffn[1:2], wr_pad, br_pad)
    tg = MOE_TILE
    n_tiles = -(-(2 * seq + N_EXPERTS * (tg - 1)) // tg)
    counts = cnt[0, :N_EXPERTS].astype(I32)
    padded = jnp.maximum((counts + tg - 1) // tg, 1) * tg
    gend = jnp.cumsum(padded)
    gstart = gend - padded
    pos = (gstart[eid[:, :2]] + rank[:, :2]).astype(I32).reshape(-1)
    nvalid = (gend[-1] // tg).astype(I32).reshape(1)
    xs = _dispatch(pos, nvalid, xn2, n_tiles, tg)
    t0 = (gstart // tg).astype(I32)
    tcnt = (padded // tg).astype(I32)
    hs, hs_s, sbuf2 = _moe_matmul(xs, [w_exp_gate[0], w_exp_up[0]], BF16, xn2_s[None], t0, tcnt, nvalid, tm=tg,
                                  tn=1024, xs_per_expert=False, name="moe_up", window=(caches[2], kv_new[2]))
    ys, ys_s, sbuf1 = _moe_matmul(hs, [w_exp_down[0]], F32, hs_s, t0, tcnt, nvalid, tm=tg, tn=512,
                                  xs_per_expert=True, name="moe_down", window=(caches[1], kv_new[1]))
    s_bufs = [_kv_shift(caches[0], kv_new[0]), sbuf1, sbuf2]
    y_p, y_s = _combine(pos, h3, gate, g_final.reshape(1, d), h3_s, gates_moe_s, ys_s, ys)

    p_bufs = []
    for g in range(ATT_G):
        keep = min(ATT_WINDOWS[g], seq)
        p_bufs.append(kv_nat[g][tail - keep:].reshape(1, keep, 2, ATT_H, ATT_HD))
    return (y_p.reshape(1, seq, d), y_s.reshape(ns, 1, d),
            p_c[None, None], p_n[None, None], p_m[:, :ML_HEADS][None],
            s_c[None], s_n[None], s_m[:, 0, :ML_HEADS][None],
            p_bufs[0], p_bufs[1], p_bufs[2], s_bufs[0], s_bufs[1], s_bufs[2])
```

```python
import functools

import jax
import jax.numpy as jnp
from jax import lax
from jax.experimental import pallas as pl
from jax.experimental.pallas import tpu as pltpu

F32 = jnp.float32
BF16 = jnp.bfloat16
I32 = jnp.int32
HIGHEST = lax.Precision.HIGHEST

D_MODEL = 2048
ML_HEADS = 8
ML_DQK = 128
ML_DV = 256
ML_NQK = ML_HEADS * ML_DQK
ML_NV = ML_HEADS * ML_DV
ML_MAIN = 2 * ML_NQK + ML_NV + D_MODEL
ML_CHUNK = 128
ATT_HD = 128
ATT_H = 8
ATT_G = 3
ATT_WINDOWS = (128, 512, 2048)
ATT_DILS = (1, 4, 16)
ATT_STEPS = 128
ATT_GW = ATT_H * ATT_HD
ROPE_THETA = 10000.0
PAST_LEN = 8192
N_EXPERTS = 8
RMS_EPS = 1e-6
NEG_BIG = -1e30

VMEM_LIMIT_BYTES = 58 * 1024 * 1024
ROW_TILE = 512
MOE_TILE = 256


def _cparams(sem):
    return pltpu.CompilerParams(dimension_semantics=sem, vmem_limit_bytes=VMEM_LIMIT_BYTES)


def _rms(x):
    return x * lax.rsqrt(jnp.mean(x * x, axis=-1, keepdims=True) + RMS_EPS)


def _norm_kernel(x_ref, xs_ref, g_ref, *refs, n_out, has_proj):
    ins = 1 if has_proj else 0
    n_each = n_out + ins
    main = refs[ins:ins + n_each]
    side = refs[ins + n_each:ins + 2 * n_each]

    def emit(x, outs):
        y = _rms(x)
        for i in range(n_out):
            outs[i][...] = (y * g_ref[i:i + 1, :]).astype(outs[i].dtype)
        if has_proj:
            outs[n_out][...] = jnp.dot(y * g_ref[0:1, :], refs[0][...], precision=HIGHEST,
                                       preferred_element_type=F32)

    emit(x_ref[...], main)

    @pl.when(pl.program_id(0) == 0)
    def _():
        emit(xs_ref[...], side)


def _rmsnorm(x, xs, gains, proj=None, tm=ROW_TILE):
    m, d = x.shape
    s = xs.shape[0]
    n_out = gains.shape[0]
    const = lambda i: (0, 0)
    in_specs = [pl.BlockSpec((tm, d), lambda i: (i, 0)), pl.BlockSpec((s, d), const),
                pl.BlockSpec((n_out, d), const)]
    args = [x, xs, gains]
    main_shape = [jax.ShapeDtypeStruct((m, d), BF16)] * n_out
    main_specs = [pl.BlockSpec((tm, d), lambda i: (i, 0))] * n_out
    side_shape = [jax.ShapeDtypeStruct((s, d), F32)] * n_out
    side_specs = [pl.BlockSpec((s, d), const)] * n_out
    if proj is not None:
        p = proj.shape[1]
        in_specs.append(pl.BlockSpec((d, p), const))
        args.append(proj)
        main_shape.append(jax.ShapeDtypeStruct((m, p), F32))
        main_specs.append(pl.BlockSpec((tm, p), lambda i: (i, 0)))
        side_shape.append(jax.ShapeDtypeStruct((s, p), F32))
        side_specs.append(pl.BlockSpec((s, p), const))
    outs = pl.pallas_call(
        functools.partial(_norm_kernel, n_out=n_out, has_proj=proj is not None),
        grid=(m // tm,), in_specs=in_specs, out_specs=main_specs + side_specs,
        out_shape=main_shape + side_shape,
        compiler_params=_cparams(("arbitrary",)), name="rmsnorm",
    )(*args)
    k = len(main_shape)
    return outs[:k], outs[k:]


def _rope_heads(acc, cos, sin):
    outs = []
    for h in range(acc.shape[1] // ATT_HD):
        a = acc[:, h * ATT_HD:(h + 1) * ATT_HD]
        outs.append(a * cos + pltpu.roll(a, ATT_HD // 2, 1) * sin)
    return jnp.concatenate(outs, axis=1)


def _mm_kernel(te_ref, nv_ref, x_ref, *refs, n_w, has_scale, has_res, rope, dil, nat_first, side_precise):
    n = pl.program_id(0)
    m = pl.program_id(1)
    it = iter(refs)
    w_refs = [next(it) for _ in range(n_w)]
    scale_ref = next(it) if has_scale else None
    res_ref = next(it) if has_res else None
    cos_ref, sin_ref = (next(it), next(it)) if rope else (None, None)
    xs_ref = next(it)
    res_s_ref = next(it) if has_res else None
    cos_s_ref, sin_s_ref = (next(it), next(it)) if rope else (None, None)
    o_ref = next(it)
    nat_ref = next(it) if nat_first is not None else None
    os_ref = next(it)
    wb_refs = [next(it) for _ in range(n_w)]
    deint = next(it) if dil > 1 else None

    def finish(acc, up, res, cos, sin, store):
        if n_w == 2:
            acc = (acc * jax.nn.sigmoid(acc)) * up
        if has_scale:
            acc = acc * scale_ref[...]
        if has_res:
            acc = acc + res
        if rope == "all":
            store(_rope_heads(acc, cos, sin))
        elif rope == "even":
            @pl.when(n % 2 == 0)
            def _():
                store(_rope_heads(acc, cos, sin))

            @pl.when(n % 2 == 1)
            def _():
                store(acc)
        else:
            store(acc)

    def store_side(val):
        os_ref[0] = val

    def store_main(val):
        if nat_first is not None:
            @pl.when(m >= nat_first)
            def _():
                nat_ref[...] = val
        if dil == 1:
            o_ref[0] = val.astype(o_ref.dtype)
        else:
            rows = deint.shape[1] // dil
            for c in range(deint.shape[0]):
                lanes = slice(c * 128, (c + 1) * 128)
                deint[c] = val[:, lanes]
                for r in range(dil):
                    o_ref[r, :, lanes] = deint[c, pl.ds(r, rows, stride=dil), :].astype(o_ref.dtype)

    prev = jnp.maximum(m - 1, 0)
    new_weights = jnp.logical_or(m == 0, te_ref[m] != te_ref[prev])

    @pl.when(new_weights)
    def _():
        for w_ref, wb_ref in zip(w_refs, wb_refs):
            wb_ref[...] = w_ref[0].astype(BF16)
        if side_precise:
            xs = xs_ref[0]
            acc = jnp.dot(xs, w_refs[0][0], precision=HIGHEST, preferred_element_type=F32)
            up = jnp.dot(xs, w_refs[1][0], precision=HIGHEST, preferred_element_type=F32) if n_w == 2 else None
        else:
            xs = xs_ref[0].astype(BF16)
            acc = jnp.dot(xs, wb_refs[0][...], preferred_element_type=F32)
            up = jnp.dot(xs, wb_refs[1][...], preferred_element_type=F32) if n_w == 2 else None
        finish(acc, up, res_s_ref[...] if has_res else None,
               cos_s_ref[...] if rope else None, sin_s_ref[...] if rope else None, store_side)

    @pl.when(m >= nv_ref[0])
    def _():
        o_ref[...] = jnp.zeros_like(o_ref)

    @pl.when(m < nv_ref[0])
    def _():
        xb = x_ref[...].astype(BF16)
        acc = jnp.dot(xb, wb_refs[0][...], preferred_element_type=F32)
        up = jnp.dot(xb, wb_refs[1][...], preferred_element_type=F32) if n_w == 2 else None
        finish(acc, up, res_ref[...] if has_res else None,
               cos_ref[...] if rope else None, sin_ref[...] if rope else None, store_main)


def _matmul(x, ws, n_cols, out_dtype, xs, *, tm, tn, col_off=0, te=None, nvalid=None, xs_per_expert=False,
            col_scale=None, res=None, res_s=None, rope=None, cos=None, sin=None, cos_s=None, sin_s=None,
            dil=1, natural_tail=0, side_precise=True, name="matmul"):
    m, k = x.shape
    s = xs.shape[1]
    n_m = m // tm
    n_n = n_cols // tn
    n_e = ws[0].shape[0]
    if te is None:
        te = jnp.zeros((n_m,), I32)
        nvalid = jnp.full((1,), n_m, I32)

    def row(mi, nv):
        return jnp.minimum(mi, nv[0] - 1)

    def exp(mi, te, nv):
        return te[row(mi, nv)]

    in_specs = [pl.BlockSpec((tm, k), lambda n, mi, te, nv: (row(mi, nv), 0))]
    args = [x]
    for w in ws:
        in_specs.append(pl.BlockSpec((1, k, tn), lambda n, mi, te, nv: (exp(mi, te, nv), 0, n + col_off)))
        args.append(w)
    if col_scale is not None:
        in_specs.append(pl.BlockSpec((1, tn), lambda n, mi, te, nv: (0, n)))
        args.append(col_scale)
    if res is not None:
        in_specs.append(pl.BlockSpec((tm, tn), lambda n, mi, te, nv: (row(mi, nv), n)))
        args.append(res)
    if rope is not None:
        for t in (cos, sin):
            in_specs.append(pl.BlockSpec((tm, ATT_HD), lambda n, mi, te, nv: (row(mi, nv), 0)))
            args.append(t)
    if xs_per_expert:
        in_specs.append(pl.BlockSpec((1, s, k), lambda n, mi, te, nv: (exp(mi, te, nv), 0, 0)))
    else:
        in_specs.append(pl.BlockSpec((1, s, k), lambda n, mi, te, nv: (0, 0, 0)))
    args.append(xs)
    if res is not None:
        in_specs.append(pl.BlockSpec((s, tn), lambda n, mi, te, nv: (0, n)))
        args.append(res_s)
    if rope is not None:
        for t in (cos_s, sin_s):
            in_specs.append(pl.BlockSpec((s, ATT_HD), lambda n, mi, te, nv: (0, 0)))
            args.append(t)

    out_shape = [jax.ShapeDtypeStruct((dil, m // dil, n_cols), out_dtype)]
    out_specs = [pl.BlockSpec((dil, tm // dil, tn), lambda n, mi, te, nv: (0, mi, n))]
    nat_first = None
    if natural_tail:
        nat_first = n_m - natural_tail // tm
        out_shape.append(jax.ShapeDtypeStruct((natural_tail, n_cols), F32))
        out_specs.append(pl.BlockSpec((tm, tn), lambda n, mi, te, nv: (jnp.maximum(mi - nat_first, 0), n)))
    out_shape.append(jax.ShapeDtypeStruct((n_e, s, n_cols), F32))
    out_specs.append(pl.BlockSpec((1, s, tn), lambda n, mi, te, nv: (exp(mi, te, nv), 0, n)))
    scratch = [pltpu.VMEM((k, tn), BF16) for _ in ws]
    if dil > 1:
        scratch.append(pltpu.VMEM((tn // 128, tm, 128), F32))
    kern = functools.partial(_mm_kernel, n_w=len(ws), has_scale=col_scale is not None, has_res=res is not None,
                             rope=rope, dil=dil, nat_first=nat_first, side_precise=side_precise)
    return pl.pallas_call(
        kern,
        grid_spec=pltpu.PrefetchScalarGridSpec(
            num_scalar_prefetch=2, grid=(n_n, n_m), in_specs=in_specs, out_specs=out_specs,
            scratch_shapes=scratch),
        out_shape=out_shape,
        compiler_params=_cparams(("arbitrary", "arbitrary")), name=name,
    )(te, nvalid, *args)


KV_ROW = 2 * ATT_H
SHIFT_SEQS = 2
SHIFT_MAX_ROWS = 128


def _shift_chunk_rows(window):
    moved = window - 1
    return max(d for d in range(1, SHIFT_MAX_ROWS + 1) if moved % d == 0)


class _WindowShift:
    def __init__(self, buf_hbm, new_hbm, out_hbm, stage, sem_in, sem_out, sem_new, next_chunk):
        self.buf, self.new, self.out, self.stage = buf_hbm, new_hbm, out_hbm, stage
        self.sem_in, self.sem_out, self.sem_new, self.next_chunk = sem_in, sem_out, sem_new, next_chunk
        ns, self.rows = buf_hbm.shape[0], buf_hbm.shape[1]
        self.chunk = stage.shape[2]
        self.per_group = (self.rows - KV_ROW) // self.chunk
        self.total = (ns // SHIFT_SEQS) * self.per_group

    def _load(self, k, slot):
        seqs = pl.ds((k // self.per_group) * SHIFT_SEQS, SHIFT_SEQS)
        src = self.buf.at[seqs, pl.ds(KV_ROW + (k % self.per_group) * self.chunk, self.chunk)]
        return pltpu.make_async_copy(src, self.stage.at[slot], self.sem_in.at[slot])

    def _store(self, k, slot):
        seqs = pl.ds((k // self.per_group) * SHIFT_SEQS, SHIFT_SEQS)
        dst = self.out.at[seqs, pl.ds((k % self.per_group) * self.chunk, self.chunk)]
        return pltpu.make_async_copy(self.stage.at[slot], dst, self.sem_out.at[slot])

    def _append(self):
        return pltpu.make_async_copy(self.new, self.out.at[:, pl.ds(self.rows - KV_ROW, KV_ROW)], self.sem_new.at[0])

    def begin(self):
        self.next_chunk[0] = 0
        self._load(0, 0).start(priority=1)
        self._append().start()

    def advance(self):
        k = self.next_chunk[0]

        @pl.when(k < self.total)
        def _():
            slot = k % 2
            self._load(k, slot).wait()

            @pl.when(k >= 1)
            def _():
                self._store(k - 1, 1 - slot).wait()

            self._store(k, slot).start()

            @pl.when(k + 1 < self.total)
            def _():
                self._load(k + 1, 1 - slot).start(priority=1)

            self.next_chunk[0] = k + 1

    def finish(self):
        def rest(i, carry):
            self.advance()
            return carry

        lax.fori_loop(self.next_chunk[0], self.total, rest, 0)
        self._store(self.total - 1, (self.total - 1) % 2).wait()
        self._append().wait()


def _moe_mm_kernel(t0_ref, cnt_ref, nv_ref, x_hbm, *refs, n_w, tm, n_tiles, n_bg):
    n = pl.program_id(0)
    e = pl.program_id(1)
    n_e = pl.num_programs(1)
    it = iter(refs)
    w_refs = [next(it) for _ in range(n_w)]
    xs_ref = next(it)
    bg_in = [next(it) for _ in range(2 * n_bg)]
    o_hbm, os_ref = next(it), next(it)
    bg_out = [next(it) for _ in range(n_bg)]
    wb_refs = [next(it) for _ in range(n_w)]
    xbuf, obuf, sem_in, sem_out = next(it), next(it), next(it), next(it)
    tn = os_ref.shape[2]
    t0 = t0_ref[e]
    cnt = cnt_ref[e]
    first_step = jnp.logical_and(n == 0, e == 0)
    last_step = jnp.logical_and(n == pl.num_programs(0) - 1, e == n_e - 1)

    shift = _WindowShift(bg_in[0], bg_in[1], bg_out[0], *[next(it) for _ in range(5)]) if n_bg else None
    if shift:
        pl.when(first_step)(shift.begin)

    def x_copy(tile, slot):
        return pltpu.make_async_copy(x_hbm.at[pl.ds(tile * tm, tm)], xbuf.at[slot], sem_in.at[slot])

    def o_copy(tile, slot):
        return pltpu.make_async_copy(obuf.at[slot], o_hbm.at[pl.ds(tile * tm, tm), pl.ds(n * tn, tn)],
                                     sem_out.at[slot])

    def product(xb):
        acc = jnp.dot(xb, wb_refs[0][...], preferred_element_type=F32)
        if n_w == 2:
            acc = (acc * jax.nn.sigmoid(acc)) * jnp.dot(xb, wb_refs[1][...], preferred_element_type=F32)
        return acc

    @pl.when(first_step)
    def _():
        x_copy(t0, 0).start(priority=1)

    for w_ref, wb_ref in zip(w_refs, wb_refs):
        wb_ref[...] = w_ref[0].astype(BF16)
    os_ref[0] = product(xs_ref[0].astype(BF16))

    def body(t, carry):
        slot = t % 2

        @pl.when(t + 1 < cnt)
        def _():
            x_copy(t0 + t + 1, 1 - slot).start(priority=1)

        x_copy(t0 + t, slot).wait()

        @pl.when(t >= 2)
        def _():
            o_copy(t0 + t - 2, slot).wait()

        obuf[slot] = product(xbuf[slot]).astype(obuf.dtype)
        o_copy(t0 + t, slot).start()
        if shift:
            shift.advance()
        return carry

    lax.fori_loop(0, cnt, body, 0)

    @pl.when(jnp.logical_not(last_step))
    def _():
        x_copy(t0_ref[jnp.where(e == n_e - 1, 0, e + 1)], 0).start(priority=1)

    @pl.when(cnt >= 2)
    def _():
        o_copy(t0 + cnt - 2, cnt % 2).wait()

    o_copy(t0 + cnt - 1, (cnt - 1) % 2).wait()

    @pl.when(e == n_e - 1)
    def _():
        obuf[0] = jnp.zeros(obuf.shape[1:], obuf.dtype)

        def fill(tile, carry):
            o_copy(tile, 0).start()
            o_copy(tile, 0).wait()
            return carry

        lax.fori_loop(nv_ref[0], n_tiles, fill, 0)

    if shift:
        pl.when(last_step)(shift.finish)


def _moe_matmul(x, ws, out_dtype, xs, t0, cnt, nvalid, *, tm, tn, xs_per_expert, name, window=None):
    r, k = x.shape
    n_e, _, n_cols = ws[0].shape
    s = xs.shape[1]
    n_w = len(ws)
    windows = []
    if window is not None:
        cache, new = window
        ns, lb = cache.shape[0], cache.shape[1]
        assert ns % SHIFT_SEQS == 0
        windows = [(cache.reshape(ns, lb * KV_ROW, ATT_HD), new.reshape(ns, KV_ROW, ATT_HD))]
        chunk = _shift_chunk_rows(lb) * KV_ROW
    n_bg = len(windows)
    any_spec = pl.BlockSpec(memory_space=pl.ANY)
    in_specs = [any_spec]
    in_specs += [pl.BlockSpec((1, k, tn), lambda n, e, *_: (e, 0, n)) for _ in ws]
    if xs_per_expert:
        in_specs.append(pl.BlockSpec((1, s, k), lambda n, e, *_: (e, 0, 0)))
    else:
        in_specs.append(pl.BlockSpec((1, s, k), lambda n, e, *_: (0, 0, 0)))
    in_specs += [any_spec] * (2 * n_bg)
    scratch = [pltpu.VMEM((k, tn), BF16) for _ in ws]
    scratch += [pltpu.VMEM((2, tm, k), x.dtype), pltpu.VMEM((2, tm, tn), out_dtype),
                pltpu.SemaphoreType.DMA((2,)), pltpu.SemaphoreType.DMA((2,))]
    if n_bg:
        scratch += [pltpu.VMEM((2, SHIFT_SEQS, chunk, ATT_HD), F32), pltpu.SemaphoreType.DMA((2,)),
                    pltpu.SemaphoreType.DMA((2,)), pltpu.SemaphoreType.DMA((1,)), pltpu.SMEM((1,), I32)]
    out_shape = [jax.ShapeDtypeStruct((r, n_cols), out_dtype), jax.ShapeDtypeStruct((n_e, s, n_cols), F32)]
    out_shape += [jax.ShapeDtypeStruct(buf.shape, buf.dtype) for buf, _ in windows]
    outs = pl.pallas_call(
        functools.partial(_moe_mm_kernel, n_w=n_w, tm=tm, n_tiles=r // tm, n_bg=n_bg),
        grid_spec=pltpu.PrefetchScalarGridSpec(
            num_scalar_prefetch=3, grid=(n_cols // tn, n_e), in_specs=in_specs,
            out_specs=[any_spec, pl.BlockSpec((1, s, tn), lambda n, e, *_: (e, 0, n))] + [any_spec] * n_bg,
            scratch_shapes=scratch),
        out_shape=out_shape,
        compiler_params=_cparams(("arbitrary", "arbitrary")), name=name,
    )(t0, cnt, nvalid, x, *ws, xs, *[a for pair in windows for a in pair])
    if window is not None:
        return outs[0], outs[1], outs[2].reshape(window[0].shape)
    return outs


def _log_sigmoid(x):
    return jnp.minimum(x, 0.0) - jnp.log1p(jnp.exp(-jnp.abs(x)))


def _mlstm_prompt_kernel(q_ref, k_ref, v_ref, o_ref, gi_ref, gf_ref, bi_ref, bf_ref, bo_ref, ghn_ref,
                         h_ref, c_out_ref, n_out_ref, m_out_ref, ct_s, n_s, m_s):
    c = pl.program_id(0)
    L = ML_CHUNK

    @pl.when(c == 0)
    def _():
        ct_s[...] = jnp.zeros_like(ct_s)
        n_s[...] = jnp.zeros_like(n_s)
        m_s[...] = jnp.zeros_like(m_s)

    ig = gi_ref[...] + bi_ref[...]
    lf = _log_sigmoid(gf_ref[...] + bf_ref[...])
    r = lax.broadcasted_iota(I32, (L, L), 0)
    s = lax.broadcasted_iota(I32, (L, L), 1)
    causal = r >= s
    tril = causal.astype(F32)
    b = jnp.dot(tril, lf, precision=HIGHEST, preferred_element_type=F32)
    b_t = b.T
    ig_t = ig.T
    m_all = m_s[...]
    m_new_all = m_all
    lane = lax.broadcasted_iota(I32, (1, 128), 1)

    H = range(ML_HEADS)
    q = [q_ref[:, h * ML_DQK:(h + 1) * ML_DQK] for h in H]
    k = [k_ref[:, h * ML_DQK:(h + 1) * ML_DQK] for h in H]
    v = [v_ref[:, h * ML_DV:(h + 1) * ML_DV] for h in H]
    ct = [ct_s[h] for h in H]
    n_old = [n_s[h:h + 1, :] for h in H]
    bc = [b[:, h:h + 1] for h in H]
    m_old = [m_all[:, h:h + 1] for h in H]

    sc = [lax.dot_general(q[h], k[h], (((1,), (1,)), ((), ())), preferred_element_type=F32) for h in H]
    qc = [jnp.dot(q[h], ct[h].astype(BF16), preferred_element_type=F32) for h in H]

    logd = [jnp.where(causal, bc[h] - b_t[h:h + 1, :] + ig_t[h:h + 1, :], -jnp.inf) for h in H]
    inter = [bc[h] + m_old[h] for h in H]
    mt = [jnp.maximum(inter[h], jnp.max(logd[h], axis=1, keepdims=True)) for h in H]
    w_inter = [jnp.exp(inter[h] - mt[h]) for h in H]
    m_new = [mt[h][L - 1:L, :] for h in H]
    b_last = [bc[h][L - 1:L, :] for h in H]
    decay = [jnp.exp(b_last[h] + m_old[h] - m_new[h]) for h in H]
    wj = [jnp.exp(b_last[h] - bc[h] + ig[:, h:h + 1] - m_new[h]) for h in H]

    a = [sc[h] * jnp.exp(logd[h] - mt[h]) for h in H]
    qn = [jnp.sum(q[h].astype(F32) * n_old[h], axis=1, keepdims=True) for h in H]
    den = [jnp.sum(a[h], axis=1, keepdims=True) + w_inter[h] * qn[h] for h in H]
    num = [jnp.dot(a[h].astype(BF16), v[h], preferred_element_type=F32) + w_inter[h] * qc[h] for h in H]

    hh = [num[h] / jnp.maximum(jnp.abs(den[h]), jnp.exp(-mt[h])) for h in H]
    for h in H:
        cols = slice(h * ML_DV, (h + 1) * ML_DV)
        og = jax.nn.sigmoid(o_ref[:, cols].astype(F32) + bo_ref[:, cols])
        h_ref[:, cols] = (_rms(hh[h]) * ghn_ref[:, cols] * og).astype(h_ref.dtype)

    kf = [k[h].astype(F32) * wj[h] for h in H]
    for h in H:
        upd = lax.dot_general(kf[h].astype(BF16), v[h], (((0,), (0,)), ((), ())), preferred_element_type=F32)
        ct_s[h] = decay[h] * ct[h] + upd
        n_s[h:h + 1, :] = decay[h] * n_old[h] + jnp.sum(kf[h], axis=0, keepdims=True)
        m_new_all = jnp.where(lane == h, m_new[h], m_new_all)

    m_s[...] = m_new_all

    @pl.when(c == pl.num_programs(0) - 1)
    def _():
        for h in range(ML_HEADS):
            c_out_ref[h] = ct_s[h].T
        n_out_ref[...] = n_s[...]
        m_out_ref[...] = m_s[...]


def _mlstm_prompt(z, gates, bi, bf, bo, ghn):
    seq = z.shape[0]
    nc = seq // ML_CHUNK
    L = ML_CHUNK
    const2 = lambda c: (0, 0)
    in_specs = [
        pl.BlockSpec((L, ML_NQK), lambda c: (c, 0)),
        pl.BlockSpec((L, ML_NQK), lambda c: (c, 1)),
        pl.BlockSpec((L, ML_NV), lambda c: (c, 1)),
        pl.BlockSpec((L, D_MODEL), lambda c: (c, 2)),
        pl.BlockSpec((L, 128), lambda c: (c, 0)),
        pl.BlockSpec((L, 128), lambda c: (c, 1)),
        pl.BlockSpec((1, 128), const2),
        pl.BlockSpec((1, 128), const2),
        pl.BlockSpec((1, D_MODEL), const2),
        pl.BlockSpec((1, ML_NV), const2),
    ]
    out_shape = [
        jax.ShapeDtypeStruct((seq, ML_NV), BF16),
        jax.ShapeDtypeStruct((ML_HEADS, ML_DV, ML_DQK), F32),
        jax.ShapeDtypeStruct((ML_HEADS, ML_DQK), F32),
        jax.ShapeDtypeStruct((1, 128), F32),
    ]
    out_specs = [
        pl.BlockSpec((L, ML_NV), lambda c: (c, 0)),
        pl.BlockSpec((ML_HEADS, ML_DV, ML_DQK), lambda c: (0, 0, 0)),
        pl.BlockSpec((ML_HEADS, ML_DQK), const2),
        pl.BlockSpec((1, 128), const2),
    ]
    return pl.pallas_call(
        _mlstm_prompt_kernel, grid=(nc,), in_specs=in_specs, out_specs=out_specs, out_shape=out_shape,
        scratch_shapes=[pltpu.VMEM((ML_HEADS, ML_DQK, ML_DV), F32), pltpu.VMEM((ML_HEADS, ML_DQK), F32),
                        pltpu.VMEM((1, 128), F32)],
        compiler_params=_cparams(("arbitrary",)), name="mlstm_prompt",
    )(z, z, z, z, gates, gates, bi, bf, bo, ghn)


def _mlstm_sample_kernel(z_ref, g_ref, c_ref, n_ref, m_ref, bi_ref, bf_ref, bo_ref, ghn_ref,
                         h_ref, c_out_ref, n_out_ref, m_out_ref):
    i = pl.program_id(0)
    z = z_ref[0]
    g = g_ref[0]
    ig_all = g[:, 0:128] + bi_ref[...]
    lf_all = _log_sigmoid(g[:, 128:256] + bf_ref[...])
    m_all = m_ref[0]
    mt_all = jnp.maximum(lf_all + m_all, ig_all)
    m_out_ref[0] = mt_all
    outs = []
    for h in range(ML_HEADS):
        q = z[:, h * ML_DQK:(h + 1) * ML_DQK]
        k = z[:, ML_NQK + h * ML_DQK:ML_NQK + (h + 1) * ML_DQK]
        v = z[:, 2 * ML_NQK + h * ML_DV:2 * ML_NQK + (h + 1) * ML_DV]
        ig = ig_all[:, h:h + 1]
        lf = lf_all[:, h:h + 1]
        m0 = m_all[:, h:h + 1]
        mt = mt_all[:, h:h + 1]
        w_inter = jnp.exp(lf + m0 - mt)
        wj = jnp.exp(ig - mt)
        a = jnp.sum(q * k, axis=1, keepdims=True) * wj
        c_h = c_ref[0, h]
        n_h = n_ref[0, h:h + 1, :]
        q8 = jnp.broadcast_to(q, (8, ML_DQK))
        cq = lax.dot_general(q8, c_h, (((1,), (1,)), ((), ())), precision=HIGHEST,
                             preferred_element_type=F32)[0:1, :]
        num = a * v + w_inter * cq
        den = a + w_inter * jnp.sum(n_h * q, axis=1, keepdims=True)
        hh = num / jnp.maximum(jnp.abs(den), jnp.exp(-mt))
        hn = _rms(hh) * ghn_ref[:, h * ML_DV:(h + 1) * ML_DV]
        og = jax.nn.sigmoid(z[:, 2 * ML_NQK + ML_NV + h * ML_DV:2 * ML_NQK + ML_NV + (h + 1) * ML_DV]
                            + bo_ref[:, h * ML_DV:(h + 1) * ML_DV])
        outs.append(hn * og)
        v_col = jnp.broadcast_to(v, (8, ML_DV)).T[:, 0:1]
        c_out_ref[0, h] = w_inter * c_h + wj * (v_col * k)
        n_out_ref[0, h:h + 1, :] = w_inter * n_h + wj * k
    h_ref[pl.ds(i, 1), :] = jnp.concatenate(outs, axis=1)


def _mlstm_sample(z_s, gates_s, c0, n0, m0, bi, bf, bo, ghn):
    ns = z_s.shape[0]
    const2 = lambda i: (0, 0)
    in_specs = [
        pl.BlockSpec((1, 1, ML_MAIN), lambda i: (i, 0, 0)),
        pl.BlockSpec((1, 1, 256), lambda i: (i, 0, 0)),
        pl.BlockSpec((1, ML_HEADS, ML_DV, ML_DQK), lambda i: (i, 0, 0, 0)),
        pl.BlockSpec((1, ML_HEADS, ML_DQK), lambda i: (i, 0, 0)),
        pl.BlockSpec((1, 1, 128), lambda i: (i, 0, 0)),
        pl.BlockSpec((1, 128), const2),
        pl.BlockSpec((1, 128), const2),
        pl.BlockSpec((1, D_MODEL), const2),
        pl.BlockSpec((1, ML_NV), const2),
    ]
    out_shape = [
        jax.ShapeDtypeStruct((ns, ML_NV), F32),
        jax.ShapeDtypeStruct((ns, ML_HEADS, ML_DV, ML_DQK), F32),
        jax.ShapeDtypeStruct((ns, ML_HEADS, ML_DQK), F32),
        jax.ShapeDtypeStruct((ns, 1, 128), F32),
    ]
    out_specs = [
        pl.BlockSpec((ns, ML_NV), const2),
        pl.BlockSpec((1, ML_HEADS, ML_DV, ML_DQK), lambda i: (i, 0, 0, 0)),
        pl.BlockSpec((1, ML_HEADS, ML_DQK), lambda i: (i, 0, 0)),
        pl.BlockSpec((1, 1, 128), lambda i: (i, 0, 0)),
    ]
    return pl.pallas_call(
        _mlstm_sample_kernel, grid=(ns,), in_specs=in_specs, out_specs=out_specs, out_shape=out_shape,
        compiler_params=_cparams(("arbitrary",)), name="mlstm_sample",
    )(z_s, gates_s, c0, n0, m0, bi, bf, bo, ghn)


def _attn_prompt_kernel(q_ref, kp_ref, kc_ref, vp_ref, vc_ref, o_ref, lse_ref):
    blk = pl.program_id(1)
    T = ATT_STEPS
    qi = lax.broadcasted_iota(I32, (T, T), 0)
    kj = lax.broadcasted_iota(I32, (T, T), 1)
    valid_prev = jnp.logical_and(kj >= qi, blk > 0)
    valid_cur = kj <= qi
    scale = ATT_HD ** -0.5
    nt = (((1,), (1,)), ((), ()))
    lane = lax.broadcasted_iota(I32, (T, ATT_HD), 1)
    lse_all = jnp.zeros((T, ATT_HD), F32)
    raw = []
    for h in range(ATT_H):
        sl = slice(h * ATT_HD, (h + 1) * ATT_HD)
        qh = q_ref[:, sl]
        raw.append((lax.dot_general(qh, kp_ref[:, sl], nt, preferred_element_type=F32),
                    lax.dot_general(qh, kc_ref[:, sl], nt, preferred_element_type=F32)))
    for h in range(ATT_H):
        sl = slice(h * ATT_HD, (h + 1) * ATT_HD)
        s1 = jnp.where(valid_prev, raw[h][0] * scale, -jnp.inf)
        s2 = jnp.where(valid_cur, raw[h][1] * scale, -jnp.inf)
        mx = jnp.max(jnp.maximum(s1, s2), axis=1, keepdims=True)
        p1 = jnp.exp(s1 - mx)
        p2 = jnp.exp(s2 - mx)
        den = jnp.sum(p1 + p2, axis=1, keepdims=True)
        acc = jnp.dot(p1.astype(BF16), vp_ref[:, sl], preferred_element_type=F32)
        acc = acc + jnp.dot(p2.astype(BF16), vc_ref[:, sl], preferred_element_type=F32)
        o_ref[:, sl] = acc / den
        lse_all = jnp.where(lane == h, mx + jnp.log(den), lse_all)
    lse_ref[...] = lse_all


def _attn_prompt(q, kv, g):
    dil, L, _ = q.shape
    nb = L // ATT_STEPS
    T = ATT_STEPS
    blk = (None, T, ATT_GW)
    in_specs = [
        pl.BlockSpec(blk, lambda r, b: (r, b, 0)),
        pl.BlockSpec(blk, lambda r, b: (r, jnp.maximum(b - 1, 0), 0)),
        pl.BlockSpec(blk, lambda r, b: (r, b, 0)),
        pl.BlockSpec(blk, lambda r, b: (r, jnp.maximum(b - 1, 0), 1)),
        pl.BlockSpec(blk, lambda r, b: (r, b, 1)),
    ]
    out_specs = [pl.BlockSpec(blk, lambda r, b: (r, b, 0)), pl.BlockSpec((None, T, ATT_HD), lambda r, b: (r, b, 0))]
    return pl.pallas_call(
        _attn_prompt_kernel, grid=(dil, nb), in_specs=in_specs, out_specs=out_specs,
        out_shape=[jax.ShapeDtypeStruct((dil, L, ATT_GW), F32), jax.ShapeDtypeStruct((dil, L, ATT_HD), F32)],
        compiler_params=_cparams(("arbitrary", "arbitrary")), name=f"attn_prompt_g{g}",
    )(q, kv, kv, kv, kv)


def _merge_kernel(*refs):
    in_refs, out_ref, scratch = refs[:2 * ATT_G], refs[2 * ATT_G], refs[2 * ATT_G + 1:]
    tm = out_ref.shape[0]

    def position_order(ref, lanes, buf, dil):
        if dil == 1:
            return ref[0, :, lanes]
        for r in range(dil):
            buf[pl.ds(r, tm // dil, stride=dil), :] = ref[r, :, lanes]
        return buf[...]

    all128 = slice(0, ATT_HD)
    lses = [position_order(in_refs[2 * g + 1], all128, scratch[2 * g + 1], ATT_DILS[g]) for g in range(ATT_G)]
    mx = jnp.maximum(jnp.maximum(lses[0], lses[1]), lses[2])
    es = [jnp.exp(l - mx) for l in lses]
    tot = es[0] + es[1] + es[2]
    wgt = [e / tot for e in es]
    for c in range(ATT_H):
        lanes = slice(c * ATT_HD, (c + 1) * ATT_HD)
        acc = None
        for g in range(ATT_G):
            o = position_order(in_refs[2 * g], lanes, scratch[2 * g], ATT_DILS[g])
            term = wgt[g][:, c:c + 1] * o
            acc = term if acc is None else acc + term
        out_ref[:, lanes] = acc.astype(out_ref.dtype)


def _merge_groups(parts, seq, tm=512):
    in_specs, args, scratch = [], [], []
    for g, pair in enumerate(parts):
        dil = ATT_DILS[g]
        for a in pair:
            in_specs.append(pl.BlockSpec((dil, tm // dil, a.shape[2]), lambda i: (0, i, 0)))
            args.append(a)
            scratch.append(pltpu.VMEM((tm, ATT_HD), F32))
    return pl.pallas_call(
        _merge_kernel, grid=(seq // tm,), in_specs=in_specs,
        out_specs=pl.BlockSpec((tm, ATT_GW), lambda i: (i, 0)),
        out_shape=jax.ShapeDtypeStruct((seq, ATT_GW), BF16), scratch_shapes=scratch,
        compiler_params=_cparams(("arbitrary",)), name="attn_merge",
    )(*args)


def _attn_sample_kernel(q_ref, kvn_ref, b0_ref, b1_ref, b2_ref, out_ref):
    scale = ATT_HD ** -0.5
    outs, lses = [], []
    for g, b_ref in enumerate((b0_ref, b1_ref, b2_ref)):
        qg = q_ref[0, g]
        kn = kvn_ref[0, g, 0]
        vn = kvn_ref[0, g, 1]
        kb = b_ref[:, 0]
        vb = b_ref[:, 1]
        s = jnp.sum(kb * qg[None], axis=2, keepdims=True) * scale
        s_new = jnp.sum(kn * qg, axis=1, keepdims=True) * scale
        mx = jnp.maximum(jnp.max(s, axis=0), s_new)
        p = jnp.exp(s - mx[None])
        p_new = jnp.exp(s_new - mx)
        den = jnp.sum(p, axis=0) + p_new
        o = jnp.sum(p * vb, axis=0) + p_new * vn
        outs.append(o / den)
        lses.append(mx + jnp.log(den))
    mxl = jnp.maximum(jnp.maximum(lses[0], lses[1]), lses[2])
    es = [jnp.exp(l - mxl) for l in lses]
    tot = es[0] + es[1] + es[2]
    out_ref[0] = (es[0] / tot) * outs[0] + (es[1] / tot) * outs[1] + (es[2] / tot) * outs[2]


def _attn_sample(q_s, kv_s, caches):
    ns = q_s.shape[0]
    views, specs = [], []
    for g, cbuf in enumerate(caches):
        lb = cbuf.shape[1]
        dil = ATT_DILS[g]
        views.append(cbuf.reshape(ns, lb // dil, dil, 2, ATT_H, ATT_HD))
        specs.append(pl.BlockSpec((None, ATT_STEPS, None, 2, ATT_H, ATT_HD), lambda i: (i, 0, 0, 0, 0, 0)))
    in_specs = [
        pl.BlockSpec((1, ATT_G, ATT_H, ATT_HD), lambda i: (i, 0, 0, 0)),
        pl.BlockSpec((1, ATT_G, 2, ATT_H, ATT_HD), lambda i: (i, 0, 0, 0, 0)),
    ] + specs
    return pl.pallas_call(
        _attn_sample_kernel, grid=(ns,), in_specs=in_specs,
        out_specs=pl.BlockSpec((1, ATT_H, ATT_HD), lambda i: (i, 0, 0)),
        out_shape=jax.ShapeDtypeStruct((ns, ATT_H, ATT_HD), F32),
        compiler_params=_cparams(("arbitrary",)), name="attn_sample",
    )(q_s, kv_s, *views)


KV_SHIFT_BLOCK = 8192


def _kv_shift_kernel(cur_ref, nxt_ref, new_ref, out_ref):
    blk = out_ref.shape[1]
    out_ref[0, :blk - KV_ROW] = cur_ref[0, KV_ROW:]
    last = pl.program_id(1) == pl.num_programs(1) - 1
    out_ref[0, blk - KV_ROW:] = jnp.where(last, new_ref[0], nxt_ref[0])


def _kv_shift(cache, new):
    ns, lb = cache.shape[0], cache.shape[1]
    rows = lb * KV_ROW
    blk = min(KV_SHIFT_BLOCK, rows)
    nb = rows // blk
    per = blk // KV_ROW
    flat = cache.reshape(ns, rows, ATT_HD)
    out = pl.pallas_call(
        _kv_shift_kernel, grid=(ns, nb),
        in_specs=[pl.BlockSpec((1, blk, ATT_HD), lambda i, j: (i, j, 0)),
                  pl.BlockSpec((1, KV_ROW, ATT_HD), lambda i, j: (i, jnp.minimum((j + 1) * per, lb - 1), 0)),
                  pl.BlockSpec((1, KV_ROW, ATT_HD), lambda i, j: (i, 0, 0))],
        out_specs=pl.BlockSpec((1, blk, ATT_HD), lambda i, j: (i, j, 0)),
        out_shape=jax.ShapeDtypeStruct(flat.shape, flat.dtype),
        compiler_params=_cparams(("arbitrary", "arbitrary")), name="kv_shift",
    )(flat, flat, new.reshape(ns, KV_ROW, ATT_HD))
    return out.reshape(cache.shape)


def _top2(y, wr_ref, br_ref):
    rows = y.shape[0]
    lane = lax.broadcasted_iota(I32, (rows, 128), 1)
    logits = jnp.dot(y, wr_ref[...], precision=HIGHEST, preferred_element_type=F32) + br_ref[...]
    logits = jnp.where(lane < N_EXPERTS, logits, NEG_BIG)
    e = jnp.exp(logits - jnp.max(logits, axis=1, keepdims=True))
    probs = e / jnp.sum(e, axis=1, keepdims=True)
    p1 = jnp.max(probs, axis=1, keepdims=True)
    i1 = jnp.min(jnp.where(probs == p1, lane, 128), axis=1, keepdims=True)
    probs2 = jnp.where(lane == i1, -1.0, probs)
    p2 = jnp.max(probs2, axis=1, keepdims=True)
    i2 = jnp.min(jnp.where(probs2 == p2, lane, 128), axis=1, keepdims=True)
    tot = p1 + p2
    return lane, i1, i2, p1 / tot, p2 / tot


def _router_kernel(x_ref, xs_ref, g_ref, wr_ref, br_ref, xn_ref, eid_ref, gate_ref, rank_ref, cnt_ref,
                   xns_ref, gs_ref, carry):
    i = pl.program_id(0)
    tm = x_ref.shape[0]

    @pl.when(i == 0)
    def _():
        carry[...] = jnp.zeros_like(carry)
        ys = _rms(xs_ref[...]) * g_ref[...]
        xns_ref[...] = ys
        lane, i1, i2, g1, g2 = _top2(ys, wr_ref, br_ref)
        gs_ref[...] = jnp.where(lane == i1, g1, jnp.where(lane == i2, g2, 0.0))

    y = _rms(x_ref[...]) * g_ref[...]
    xn_ref[...] = y
    lane, i1, i2, g1, g2 = _top2(y, wr_ref, br_ref)
    sel1 = lane == i1
    sel2 = lane == i2
    onehot = jnp.where(jnp.logical_or(sel1, sel2), 1.0, 0.0)
    rr = lax.broadcasted_iota(I32, (tm, tm), 0)
    cc = lax.broadcasted_iota(I32, (tm, tm), 1)
    before = (cc < rr).astype(BF16)
    prefix = jnp.dot(before, onehot.astype(BF16), preferred_element_type=F32) + carry[...]
    r1 = jnp.sum(jnp.where(sel1, prefix, 0.0), axis=1, keepdims=True)
    r2 = jnp.sum(jnp.where(sel2, prefix, 0.0), axis=1, keepdims=True)
    carry[...] = carry[...] + jnp.sum(onehot, axis=0, keepdims=True)
    eid_ref[...] = jnp.where(lane == 0, i1, jnp.where(lane == 1, i2, 0))
    gate_ref[...] = jnp.where(lane == 0, g1, jnp.where(lane == 1, g2, 0.0))
    rank_ref[...] = jnp.where(lane == 0, r1, jnp.where(lane == 1, r2, 0.0)).astype(I32)
    cnt_ref[...] = jnp.broadcast_to(carry[...], cnt_ref.shape)


def _router(h, hs, gain, w_router_pad, b_router_pad, tm=ROW_TILE):
    m, d = h.shape
    s = hs.shape[0]
    const = lambda i: (0, 0)
    row_spec = pl.BlockSpec((tm, 128), lambda i: (i, 0))
    return pl.pallas_call(
        _router_kernel, grid=(m // tm,),
        in_specs=[pl.BlockSpec((tm, d), lambda i: (i, 0)), pl.BlockSpec((s, d), const), pl.BlockSpec((1, d), const),
                  pl.BlockSpec((d, 128), const), pl.BlockSpec((1, 128), const)],
        out_specs=[pl.BlockSpec((tm, d), lambda i: (i, 0)), row_spec, row_spec, row_spec,
                   pl.BlockSpec((8, 128), const), pl.BlockSpec((s, d), const), pl.BlockSpec((s, 128), const)],
        out_shape=[jax.ShapeDtypeStruct((m, d), F32), jax.ShapeDtypeStruct((m, 128), I32),
                   jax.ShapeDtypeStruct((m, 128), F32), jax.ShapeDtypeStruct((m, 128), I32),
                   jax.ShapeDtypeStruct((8, 128), F32), jax.ShapeDtypeStruct((s, d), F32),
                   jax.ShapeDtypeStruct((s, 128), F32)],
        scratch_shapes=[pltpu.VMEM((1, 128), F32)],
        compiler_params=_cparams(("arbitrary",)), name="router",
    )(h, hs, gain, w_router_pad, b_router_pad)


def _dispatch_kernel(pos_ref, nv_ref, x_hbm, out_ref, inv, buf, sem, *, n_tok):
    i = pl.program_id(0)
    tg = out_ref.shape[0]

    @pl.when(i == 0)
    def _():
        def clear(s, c):
            inv[s] = 0
            return c

        lax.fori_loop(0, inv.shape[0], clear, 0, unroll=8)

        def fill(t, c):
            inv[pos_ref[2 * t]] = t
            inv[pos_ref[2 * t + 1]] = t
            return c

        lax.fori_loop(0, n_tok, fill, 0, unroll=8)

    def start_gather(tile):
        slot = tile % 2

        def issue(r2, c):
            for j in range(2):
                r = 2 * r2 + j
                pltpu.make_async_copy(x_hbm.at[pl.ds(inv[tile * tg + r], 1)], buf.at[slot, pl.ds(r, 1)],
                                      sem.at[slot]).start(priority=j)
            return c

        lax.fori_loop(0, tg // 2, issue, 0, unroll=4)

    @pl.when(i == 0)
    def _():
        start_gather(i)

    @pl.when(i + 1 < nv_ref[0])
    def _():
        start_gather(i + 1)

    @pl.when(i < nv_ref[0])
    def _():
        slot = i % 2
        pltpu.make_async_copy(x_hbm.at[pl.ds(0, tg)], buf.at[slot], sem.at[slot]).wait()
        out_ref[...] = buf[slot].astype(out_ref.dtype)

    @pl.when(i >= nv_ref[0])
    def _():
        out_ref[...] = jnp.zeros_like(out_ref)


def _dispatch(pos_flat, nvalid, xn, n_tiles, tg):
    n_tok, d = xn.shape
    return pl.pallas_call(
        functools.partial(_dispatch_kernel, n_tok=n_tok),
        grid_spec=pltpu.PrefetchScalarGridSpec(
            num_scalar_prefetch=2, grid=(n_tiles,),
            in_specs=[pl.BlockSpec(memory_space=pl.ANY)],
            out_specs=pl.BlockSpec((tg, d), lambda i, pos, nv: (i, 0)),
            scratch_shapes=[pltpu.SMEM((n_tiles * tg,), I32), pltpu.VMEM((2, tg, d), xn.dtype),
                            pltpu.SemaphoreType.DMA((2,))]),
        out_shape=jax.ShapeDtypeStruct((n_tiles * tg, d), BF16),
        compiler_params=_cparams(("arbitrary",)), name="moe_dispatch",
    )(pos_flat, nvalid, xn)


def _combine_kernel(pos_ref, h_ref, gate_ref, g_ref, hs_ref, gs_ref, ys_s_ref, ys_hbm, out_ref, outs_ref, ybuf, sem):
    i = pl.program_id(0)
    tm = h_ref.shape[0]

    def start_gather(tile):
        slot = tile % 2

        def issue(r, carry):
            for j in range(2):
                pltpu.make_async_copy(ys_hbm.at[pl.ds(pos_ref[2 * (tile * tm + r) + j], 1)],
                                      ybuf.at[slot, j, pl.ds(r, 1)], sem.at[slot]).start(priority=j)
            return carry

        lax.fori_loop(0, tm, issue, 0, unroll=4)

    @pl.when(i == 0)
    def _():
        start_gather(i)

    @pl.when(i + 1 < pl.num_programs(0))
    def _():
        start_gather(i + 1)

    @pl.when(i == 0)
    def _():
        gs = gs_ref[...]
        y = jnp.zeros(hs_ref.shape, F32)
        for e in range(N_EXPERTS):
            y = y + gs[:, e:e + 1] * ys_s_ref[e]
        outs_ref[...] = _rms(hs_ref[...] + y) * g_ref[...]

    slot = i % 2
    for j in range(2):
        pltpu.make_async_copy(ys_hbm.at[pl.ds(0, tm)], ybuf.at[slot, j], sem.at[slot]).wait()
    gate = gate_ref[...]
    y = h_ref[...] + (gate[:, 0:1] * ybuf[slot, 0] + gate[:, 1:2] * ybuf[slot, 1])
    out_ref[...] = _rms(y) * g_ref[...]


def _combine(pos_flat, h, gate, g_final, hs, gs, ys_s, ys, tm=ROW_TILE):
    m, d = h.shape
    s = hs.shape[0]
    c2 = lambda i, pos: (0, 0)
    return pl.pallas_call(
        _combine_kernel,
        grid_spec=pltpu.PrefetchScalarGridSpec(
            num_scalar_prefetch=1, grid=(m // tm,),
            in_specs=[pl.BlockSpec((tm, d), lambda i, pos: (i, 0)),
                      pl.BlockSpec((tm, 128), lambda i, pos: (i, 0)),
                      pl.BlockSpec((1, d), c2),
                      pl.BlockSpec((s, d), c2),
                      pl.BlockSpec((s, 128), c2),
                      pl.BlockSpec((N_EXPERTS, s, d), lambda i, pos: (0, 0, 0)),
                      pl.BlockSpec(memory_space=pl.ANY)],
            out_specs=[pl.BlockSpec((tm, d), lambda i, pos: (i, 0)), pl.BlockSpec((s, d), c2)],
            scratch_shapes=[pltpu.VMEM((2, 2, tm, d), F32), pltpu.SemaphoreType.DMA((2,))]),
        out_shape=[jax.ShapeDtypeStruct((m, d), F32), jax.ShapeDtypeStruct((s, d), F32)],
        compiler_params=_cparams(("arbitrary",)), name="moe_combine",
    )(pos_flat, h, gate, g_final, hs, gs, ys_s, ys)


def _rope_tables(pos):
    half = ATT_HD // 2
    inv = ROPE_THETA ** (-jnp.arange(half, dtype=F32) / half)
    ang = pos.astype(F32)[:, None] * inv[None, :]
    cos, sin = jnp.cos(ang), jnp.sin(ang)
    return jnp.concatenate([cos, cos], axis=1), jnp.concatenate([-sin, sin], axis=1)


def kernel(x_prompt, x_sample, state_mlstm_C, state_mlstm_n, state_mlstm_m, cache_kv_w128, cache_kv_w512, cache_kv_w2048, g_mix, g_ffn, w_ml_in, b_ml_gates, b_ml_o, g_ml_hnorm, w_ml_out, g_kv, w_kv, w_q, w_o, w_ffn_gate, w_ffn_up, w_ffn_down, w_router, b_router, w_exp_gate, w_exp_up, w_exp_down, g_final):
    bp, seq, d = x_prompt.shape
    ns = x_sample.shape[0]
    caches = (cache_kv_w128, cache_kv_w512, cache_kv_w2048)
    assert bp == 1 and x_sample.shape[1] == 1 and d == D_MODEL and ns % 8 == 0
    assert seq % (ATT_STEPS * max(ATT_DILS)) == 0 and seq % 1024 == 0
    assert all(c.shape[1] == w for c, w in zip(caches, ATT_WINDOWS))
    tm = 512

    h0 = x_prompt.reshape(seq, d)
    h0_s = x_sample.reshape(ns, d)

    w_gates = lax.slice_in_dim(w_ml_in, ML_MAIN, ML_MAIN + 2 * ML_HEADS, axis=2)[0]
    w_gates_pad = jnp.zeros((d, 256), F32).at[:, 0:ML_HEADS].set(w_gates[:, :ML_HEADS])
    w_gates_pad = w_gates_pad.at[:, 128:128 + ML_HEADS].set(w_gates[:, ML_HEADS:])
    bi = jnp.zeros((1, 128), F32).at[0, :ML_HEADS].set(b_ml_gates[0, :ML_HEADS])
    bf = jnp.zeros((1, 128), F32).at[0, :ML_HEADS].set(b_ml_gates[0, ML_HEADS:])
    (xn0, gates), (xn0_s, gates_s) = _rmsnorm(h0, h0_s, g_mix[0:1], proj=w_gates_pad)
    k_scale = jnp.concatenate([jnp.ones((1, ML_NQK), F32), jnp.full((1, ML_NQK), ML_DQK ** -0.5, F32),
                               jnp.ones((1, ML_NV + D_MODEL), F32)], axis=1)
    z, z_s = _matmul(xn0, [w_ml_in], ML_MAIN, BF16, xn0_s[None], tm=1024, tn=1024, col_scale=k_scale, name="ml_in")
    bo = b_ml_o[0:1]
    ghn = g_ml_hnorm[0:1]
    hg, p_c, p_n, p_m = _mlstm_prompt(z[0], gates, bi, bf, bo, ghn)
    m0 = jnp.zeros((ns, 1, 128), F32).at[:, 0, :ML_HEADS].set(state_mlstm_m[0])
    hg_s, s_c, s_n, s_m = _mlstm_sample(z_s[0].reshape(ns, 1, ML_MAIN), gates_s.reshape(ns, 1, 256),
                                        state_mlstm_C[0], state_mlstm_n[0], m0, bi, bf, bo, ghn)
    h1, h1_s = _matmul(hg, [w_ml_out], d, F32, hg_s[None], tm=tm, tn=1024, res=h0, res_s=h0_s, name="ml_out")
    h1, h1_s = h1[0], h1_s[0]

    (xf0,), (xf0_s,) = _rmsnorm(h1, h1_s, g_ffn[0:1])
    ffn_dense = w_ffn_gate.shape[2]
    hid, hid_s = _matmul(xf0, [w_ffn_gate, w_ffn_up], ffn_dense, BF16, xf0_s[None], tm=1024, tn=512, name="ffn_up")
    h2, h2_s = _matmul(hid[0], [w_ffn_down], d, F32, hid_s, tm=tm, tn=512, res=h1, res_s=h1_s, name="ffn_down")
    h2, h2_s = h2[0], h2_s[0]

    (xq, xkv), (xq_s, xkv_s) = _rmsnorm(h2, h2_s, jnp.stack([g_mix[1], g_kv]))
    cos, sin = _rope_tables(jnp.arange(seq))
    cos_s, sin_s = _rope_tables(jnp.full((ns,), PAST_LEN, I32))
    rope_args = dict(cos=cos, sin=sin, cos_s=cos_s, sin_s=sin_s)
    parts, kv_nat, kv_new, q_new = [], [], [], []
    tail = min(max(ATT_WINDOWS), seq)
    for g in range(ATT_G):
        dil = ATT_DILS[g]
        kvd, kvn, kv_s = _matmul(xkv, [w_kv[None]], 2 * ATT_GW, BF16, xkv_s[None], tm=tm, tn=ATT_GW,
                                 col_off=2 * g, rope="even", dil=dil, natural_tail=tail, name=f"kv_proj_g{g}",
                                 **rope_args)
        qd, q_s = _matmul(xq, [w_q], ATT_GW, BF16, xq_s[None], tm=tm, tn=ATT_GW, col_off=g, rope="all",
                          dil=dil, name=f"q_proj_g{g}", **rope_args)
        parts.append(_attn_prompt(qd, kvd, g))
        kv_nat.append(kvn)
        kv_new.append(kv_s[0])
        q_new.append(q_s[0])
    att = _merge_groups(parts, seq)
    att_s = _attn_sample(jnp.stack(q_new, axis=1).reshape(ns, ATT_G, ATT_H, ATT_HD),
                         jnp.stack(kv_new, axis=1).reshape(ns, ATT_G, 2, ATT_H, ATT_HD), caches)
    h3, h3_s = _matmul(att, [w_o], d, F32, att_s.reshape(1, ns, ATT_GW), tm=tm, tn=1024, res=h2, res_s=h2_s,
                       name="attn_out")
    h3, h3_s = h3[0], h3_s[0]

    wr_pad = jnp.zeros((d, 128), F32).at[:, :N_EXPERTS].set(w_router[0])
    br_pad = jnp.zeros((1, 128), F32).at[0, :N_EXPERTS].set(b_router[0])
    xn2, eid, gate, rank, cnt, xn2_s, gates_moe_s = _router(h3, h3_s, g_ffn[1:2], wr_pad, br_pad)
    tg = MOE_TILE
    n_tiles = -(-(2 * seq + N_EXPERTS * (tg - 1)) // tg)
    counts = cnt[0, :N_EXPERTS].astype(I32)
    padded = jnp.maximum((counts + tg - 1) // tg, 1) * tg
    gend = jnp.cumsum(padded)
    gstart = gend - padded
    pos = (gstart[eid[:, :2]] + rank[:, :2]).astype(I32).reshape(-1)
    nvalid = (gend[-1] // tg).astype(I32).reshape(1)
    xs = _dispatch(pos, nvalid, xn2, n_tiles, tg)
    t0 = (gstart // tg).astype(I32)
    tcnt = (padded // tg).astype(I32)
    hs, hs_s, sbuf2 = _moe_matmul(xs, [w_exp_gate[0], w_exp_up[0]], BF16, xn2_s[None], t0, tcnt, nvalid, tm=tg,
                                  tn=1024, xs_per_expert=False, name="moe_up", window=(caches[2], kv_new[2]))
    ys, ys_s, sbuf1 = _moe_matmul(hs, [w_exp_down[0]], F32, hs_s, t0, tcnt, nvalid, tm=tg, tn=512,
                                  xs_per_expert=True, name="moe_down", window=(caches[1], kv_new[1]))
    s_bufs = [_kv_shift(caches[0], kv_new[0]), sbuf1, sbuf2]
    y_p, y_s = _combine(pos, h3, gate, g_final.reshape(1, d), h3_s, gates_moe_s, ys_s, ys)

    p_bufs = []
    for g in range(ATT_G):
        keep = min(ATT_WINDOWS[g], seq)
        p_bufs.append(kv_nat[g][tail - keep:].reshape(1, keep, 2, ATT_H, ATT_HD))
    return (y_p.reshape(1, seq, d), y_s.reshape(ns, 1, d),
            p_c[None, None], p_n[None, None], p_m[:, :ML_HEADS][None],
            s_c[None], s_n[None], s_m[:, 0, :ML_HEADS][None],
            p_bufs[0], p_bufs[1], p_bufs[2], s_bufs[0], s_bufs[1], s_bufs[2])
```

```python
import functools

import jax
import jax.numpy as jnp
from jax import lax
from jax.experimental import pallas as pl
from jax.experimental.pallas import tpu as pltpu

F32 = jnp.float32
BF16 = jnp.bfloat16
I32 = jnp.int32
HIGHEST = lax.Precision.HIGHEST

D_MODEL = 2048
ML_HEADS = 8
ML_DQK = 128
ML_DV = 256
ML_NQK = ML_HEADS * ML_DQK
ML_NV = ML_HEADS * ML_DV
ML_MAIN = 2 * ML_NQK + ML_NV + D_MODEL
ML_CHUNK = 128
ATT_HD = 128
ATT_H = 8
ATT_G = 3
ATT_WINDOWS = (128, 512, 2048)
ATT_DILS = (1, 4, 16)
ATT_STEPS = 128
ATT_GW = ATT_H * ATT_HD
ROPE_THETA = 10000.0
PAST_LEN = 8192
N_EXPERTS = 8
RMS_EPS = 1e-6
NEG_BIG = -1e30

VMEM_LIMIT_BYTES = 58 * 1024 * 1024
ROW_TILE = 512
MOE_TILE = 256


def _cparams(sem):
    return pltpu.CompilerParams(dimension_semantics=sem, vmem_limit_bytes=VMEM_LIMIT_BYTES)


def _rms(x):
    return x * lax.rsqrt(jnp.mean(x * x, axis=-1, keepdims=True) + RMS_EPS)


def _norm_kernel(x_ref, xs_ref, g_ref, *refs, n_out, has_proj):
    ins = 1 if has_proj else 0
    n_each = n_out + ins
    main = refs[ins:ins + n_each]
    side = refs[ins + n_each:ins + 2 * n_each]

    def emit(x, outs):
        y = _rms(x)
        for i in range(n_out):
            outs[i][...] = (y * g_ref[i:i + 1, :]).astype(outs[i].dtype)
        if has_proj:
            outs[n_out][...] = jnp.dot(y * g_ref[0:1, :], refs[0][...], precision=HIGHEST,
                                       preferred_element_type=F32)

    emit(x_ref[...], main)

    @pl.when(pl.program_id(0) == 0)
    def _():
        emit(xs_ref[...], side)


def _rmsnorm(x, xs, gains, proj=None, tm=ROW_TILE):
    m, d = x.shape
    s = xs.shape[0]
    n_out = gains.shape[0]
    const = lambda i: (0, 0)
    in_specs = [pl.BlockSpec((tm, d), lambda i: (i, 0)), pl.BlockSpec((s, d), const),
                pl.BlockSpec((n_out, d), const)]
    args = [x, xs, gains]
    main_shape = [jax.ShapeDtypeStruct((m, d), BF16)] * n_out
    main_specs = [pl.BlockSpec((tm, d), lambda i: (i, 0))] * n_out
    side_shape = [jax.ShapeDtypeStruct((s, d), F32)] * n_out
    side_specs = [pl.BlockSpec((s, d), const)] * n_out
    if proj is not None:
        p = proj.shape[1]
        in_specs.append(pl.BlockSpec((d, p), const))
        args.append(proj)
        main_shape.append(jax.ShapeDtypeStruct((m, p), F32))
        main_specs.append(pl.BlockSpec((tm, p), lambda i: (i, 0)))
        side_shape.append(jax.ShapeDtypeStruct((s, p), F32))
        side_specs.append(pl.BlockSpec((s, p), const))
    outs = pl.pallas_call(
        functools.partial(_norm_kernel, n_out=n_out, has_proj=proj is not None),
        grid=(m // tm,), in_specs=in_specs, out_specs=main_specs + side_specs,
        out_shape=main_shape + side_shape,
        compiler_params=_cparams(("arbitrary",)), name="rmsnorm",
    )(*args)
    k = len(main_shape)
    return outs[:k], outs[k:]


def _rope_heads(acc, cos, sin):
    outs = []
    for h in range(acc.shape[1] // ATT_HD):
        a = acc[:, h * ATT_HD:(h + 1) * ATT_HD]
        outs.append(a * cos + pltpu.roll(a, ATT_HD // 2, 1) * sin)
    return jnp.concatenate(outs, axis=1)


def _mm_kernel(te_ref, nv_ref, x_ref, *refs, n_w, has_scale, has_res, rope, dil, nat_first, side_precise):
    n = pl.program_id(0)
    m = pl.program_id(1)
    it = iter(refs)
    w_refs = [next(it) for _ in range(n_w)]
    scale_ref = next(it) if has_scale else None
    res_ref = next(it) if has_res else None
    cos_ref, sin_ref = (next(it), next(it)) if rope else (None, None)
    xs_ref = next(it)
    res_s_ref = next(it) if has_res else None
    cos_s_ref, sin_s_ref = (next(it), next(it)) if rope else (None, None)
    o_ref = next(it)
    nat_ref = next(it) if nat_first is not None else None
    os_ref = next(it)
    wb_refs = [next(it) for _ in range(n_w)]
    deint = next(it) if dil > 1 else None

    def finish(acc, up, res, cos, sin, store):
        if n_w == 2:
            acc = (acc * jax.nn.sigmoid(acc)) * up
        if has_scale:
            acc = acc * scale_ref[...]
        if has_res:
            acc = acc + res
        if rope == "all":
            store(_rope_heads(acc, cos, sin))
        elif rope == "even":
            @pl.when(n % 2 == 0)
            def _():
                store(_rope_heads(acc, cos, sin))

            @pl.when(n % 2 == 1)
            def _():
                store(acc)
        else:
            store(acc)

    def store_side(val):
        os_ref[0] = val

    def store_main(val):
        if nat_first is not None:
            @pl.when(m >= nat_first)
            def _():
                nat_ref[...] = val
        if dil == 1:
            o_ref[0] = val.astype(o_ref.dtype)
        else:
            rows = deint.shape[1] // dil
            for c in range(deint.shape[0]):
                lanes = slice(c * 128, (c + 1) * 128)
                deint[c] = val[:, lanes]
                for r in range(dil):
                    o_ref[r, :, lanes] = deint[c, pl.ds(r, rows, stride=dil), :].astype(o_ref.dtype)

    prev = jnp.maximum(m - 1, 0)
    new_weights = jnp.logical_or(m == 0, te_ref[m] != te_ref[prev])

    @pl.when(new_weights)
    def _():
        for w_ref, wb_ref in zip(w_refs, wb_refs):
            wb_ref[...] = w_ref[0].astype(BF16)
        if side_precise:
            xs = xs_ref[0]
            acc = jnp.dot(xs, w_refs[0][0], precision=HIGHEST, preferred_element_type=F32)
            up = jnp.dot(xs, w_refs[1][0], precision=HIGHEST, preferred_element_type=F32) if n_w == 2 else None
        else:
            xs = xs_ref[0].astype(BF16)
            acc = jnp.dot(xs, wb_refs[0][...], preferred_element_type=F32)
            up = jnp.dot(xs, wb_refs[1][...], preferred_element_type=F32) if n_w == 2 else None
        finish(acc, up, res_s_ref[...] if has_res else None,
               cos_s_ref[...] if rope else None, sin_s_ref[...] if rope else None, store_side)

    @pl.when(m >= nv_ref[0])
    def _():
        o_ref[...] = jnp.zeros_like(o_ref)

    @pl.when(m < nv_ref[0])
    def _():
        xb = x_ref[...].astype(BF16)
        acc = jnp.dot(xb, wb_refs[0][...], preferred_element_type=F32)
        up = jnp.dot(xb, wb_refs[1][...], preferred_element_type=F32) if n_w == 2 else None
        finish(acc, up, res_ref[...] if has_res else None,
               cos_ref[...] if rope else None, sin_ref[...] if rope else None, store_main)


def _matmul(x, ws, n_cols, out_dtype, xs, *, tm, tn, col_off=0, te=None, nvalid=None, xs_per_expert=False,
            col_scale=None, res=None, res_s=None, rope=None, cos=None, sin=None, cos_s=None, sin_s=None,
            dil=1, natural_tail=0, side_precise=True, name="matmul"):
    m, k = x.shape
    s = xs.shape[1]
    n_m = m // tm
    n_n = n_cols // tn
    n_e = ws[0].shape[0]
    if te is None:
        te = jnp.zeros((n_m,), I32)
        nvalid = jnp.full((1,), n_m, I32)

    def row(mi, nv):
        return jnp.minimum(mi, nv[0] - 1)

    def exp(mi, te, nv):
        return te[row(mi, nv)]

    in_specs = [pl.BlockSpec((tm, k), lambda n, mi, te, nv: (row(mi, nv), 0))]
    args = [x]
    for w in ws:
        in_specs.append(pl.BlockSpec((1, k, tn), lambda n, mi, te, nv: (exp(mi, te, nv), 0, n + col_off)))
        args.append(w)
    if col_scale is not None:
        in_specs.append(pl.BlockSpec((1, tn), lambda n, mi, te, nv: (0, n)))
        args.append(col_scale)
    if res is not None:
        in_specs.append(pl.BlockSpec((tm, tn), lambda n, mi, te, nv: (row(mi, nv), n)))
        args.append(res)
    if rope is not None:
        for t in (cos, sin):
            in_specs.append(pl.BlockSpec((tm, ATT_HD), lambda n, mi, te, nv: (row(mi, nv), 0)))
            args.append(t)
    if xs_per_expert:
        in_specs.append(pl.BlockSpec((1, s, k), lambda n, mi, te, nv: (exp(mi, te, nv), 0, 0)))
    else:
        in_specs.append(pl.BlockSpec((1, s, k), lambda n, mi, te, nv: (0, 0, 0)))
    args.append(xs)
    if res is not None:
        in_specs.append(pl.BlockSpec((s, tn), lambda n, mi, te, nv: (0, n)))
        args.append(res_s)
    if rope is not None:
        for t in (cos_s, sin_s):
            in_specs.append(pl.BlockSpec((s, ATT_HD), lambda n, mi, te, nv: (0, 0)))
            args.append(t)

    out_shape = [jax.ShapeDtypeStruct((dil, m // dil, n_cols), out_dtype)]
    out_specs = [pl.BlockSpec((dil, tm // dil, tn), lambda n, mi, te, nv: (0, mi, n))]
    nat_first = None
    if natural_tail:
        nat_first = n_m - natural_tail // tm
        out_shape.append(jax.ShapeDtypeStruct((natural_tail, n_cols), F32))
        out_specs.append(pl.BlockSpec((tm, tn), lambda n, mi, te, nv: (jnp.maximum(mi - nat_first, 0), n)))
    out_shape.append(jax.ShapeDtypeStruct((n_e, s, n_cols), F32))
    out_specs.append(pl.BlockSpec((1, s, tn), lambda n, mi, te, nv: (exp(mi, te, nv), 0, n)))
    scratch = [pltpu.VMEM((k, tn), BF16) for _ in ws]
    if dil > 1:
        scratch.append(pltpu.VMEM((tn // 128, tm, 128), F32))
    kern = functools.partial(_mm_kernel, n_w=len(ws), has_scale=col_scale is not None, has_res=res is not None,
                             rope=rope, dil=dil, nat_first=nat_first, side_precise=side_precise)
    return pl.pallas_call(
        kern,
        grid_spec=pltpu.PrefetchScalarGridSpec(
            num_scalar_prefetch=2, grid=(n_n, n_m), in_specs=in_specs, out_specs=out_specs,
            scratch_shapes=scratch),
        out_shape=out_shape,
        compiler_params=_cparams(("arbitrary", "arbitrary")), name=name,
    )(te, nvalid, *args)


KV_ROW = 2 * ATT_H
RING_SLOTS = 3
SHIFT_SEQS = 2
SHIFT_MAX_ROWS = 128


def _shift_chunk_rows(window):
    moved = window - 1
    return max(d for d in range(1, SHIFT_MAX_ROWS + 1) if moved % d == 0)


class _WindowShift:
    def __init__(self, buf_hbm, new_hbm, out_hbm, stage, sem_in, sem_out, sem_new, next_chunk):
        self.buf, self.new, self.out, self.stage = buf_hbm, new_hbm, out_hbm, stage
        self.sem_in, self.sem_out, self.sem_new, self.next_chunk = sem_in, sem_out, sem_new, next_chunk
        ns, self.rows = buf_hbm.shape[0], buf_hbm.shape[1]
        self.chunk = stage.shape[2]
        self.per_group = (self.rows - KV_ROW) // self.chunk
        self.total = (ns // SHIFT_SEQS) * self.per_group

    def _load(self, k, slot):
        seqs = pl.ds((k // self.per_group) * SHIFT_SEQS, SHIFT_SEQS)
        src = self.buf.at[seqs, pl.ds(KV_ROW + (k % self.per_group) * self.chunk, self.chunk)]
        return pltpu.make_async_copy(src, self.stage.at[slot], self.sem_in.at[slot])

    def _store(self, k, slot):
        seqs = pl.ds((k // self.per_group) * SHIFT_SEQS, SHIFT_SEQS)
        dst = self.out.at[seqs, pl.ds((k % self.per_group) * self.chunk, self.chunk)]
        return pltpu.make_async_copy(self.stage.at[slot], dst, self.sem_out.at[slot])

    def _append(self):
        return pltpu.make_async_copy(self.new, self.out.at[:, pl.ds(self.rows - KV_ROW, KV_ROW)], self.sem_new.at[0])

    def begin(self):
        self.next_chunk[0] = 0
        for k in range(RING_SLOTS - 1):
            self._load(k, k).start(priority=1)
        self._append().start()

    def advance(self):
        k = self.next_chunk[0]

        @pl.when(k < self.total)
        def _():
            self._load(k, k % RING_SLOTS).wait()

            @pl.when(k >= 1)
            def _():
                self._store(k - 1, (k - 1) % RING_SLOTS).wait()

            self._store(k, k % RING_SLOTS).start()
            ahead = k + RING_SLOTS - 1

            @pl.when(ahead < self.total)
            def _():
                self._load(ahead, ahead % RING_SLOTS).start(priority=1)

            self.next_chunk[0] = k + 1

    def finish(self):
        def rest(i, carry):
            self.advance()
            return carry

        lax.fori_loop(self.next_chunk[0], self.total, rest, 0)
        self._store(self.total - 1, (self.total - 1) % RING_SLOTS).wait()
        self._append().wait()


def _moe_mm_kernel(t0_ref, cnt_ref, nv_ref, x_hbm, *refs, n_w, tm, n_tiles, n_bg):
    n = pl.program_id(0)
    e = pl.program_id(1)
    n_e = pl.num_programs(1)
    it = iter(refs)
    w_refs = [next(it) for _ in range(n_w)]
    xs_ref = next(it)
    bg_in = [next(it) for _ in range(2 * n_bg)]
    o_hbm, os_ref = next(it), next(it)
    bg_out = [next(it) for _ in range(n_bg)]
    wb_refs = [next(it) for _ in range(n_w)]
    xbuf, obuf, sem_in, sem_out = next(it), next(it), next(it), next(it)
    tn = os_ref.shape[2]
    t0 = t0_ref[e]
    cnt = cnt_ref[e]
    first_step = jnp.logical_and(n == 0, e == 0)
    last_step = jnp.logical_and(n == pl.num_programs(0) - 1, e == n_e - 1)

    shift = _WindowShift(bg_in[0], bg_in[1], bg_out[0], *[next(it) for _ in range(5)]) if n_bg else None
    if shift:
        pl.when(first_step)(shift.begin)

    def x_copy(tile, slot):
        return pltpu.make_async_copy(x_hbm.at[pl.ds(tile * tm, tm)], xbuf.at[slot], sem_in.at[slot])

    def o_copy(tile, slot):
        return pltpu.make_async_copy(obuf.at[slot], o_hbm.at[pl.ds(tile * tm, tm), pl.ds(n * tn, tn)],
                                     sem_out.at[slot])

    def product(xb):
        acc = jnp.dot(xb, wb_refs[0][...], preferred_element_type=F32)
        if n_w == 2:
            acc = (acc * jax.nn.sigmoid(acc)) * jnp.dot(xb, wb_refs[1][...], preferred_element_type=F32)
        return acc

    def start_first_tiles(first_tile, n_tiles_here):
        for j in range(RING_SLOTS - 1):
            @pl.when(j < n_tiles_here)
            def _():
                x_copy(first_tile + j, j).start(priority=1)

    @pl.when(first_step)
    def _():
        start_first_tiles(t0, cnt)

    for w_ref, wb_ref in zip(w_refs, wb_refs):
        wb_ref[...] = w_ref[0].astype(BF16)
    os_ref[0] = product(xs_ref[0].astype(BF16))

    def body(t, carry):
        slot = t % 2
        ahead = t + RING_SLOTS - 1

        @pl.when(ahead < cnt)
        def _():
            x_copy(t0 + ahead, ahead % RING_SLOTS).start(priority=1)

        x_copy(t0 + t, t % RING_SLOTS).wait()

        @pl.when(t >= 2)
        def _():
            o_copy(t0 + t - 2, slot).wait()

        obuf[slot] = product(xbuf[t % RING_SLOTS]).astype(obuf.dtype)
        o_copy(t0 + t, slot).start()
        if shift:
            shift.advance()
        return carry

    lax.fori_loop(0, cnt, body, 0)

    @pl.when(jnp.logical_not(last_step))
    def _():
        e_next = jnp.where(e == n_e - 1, 0, e + 1)
        start_first_tiles(t0_ref[e_next], cnt_ref[e_next])

    @pl.when(cnt >= 2)
    def _():
        o_copy(t0 + cnt - 2, cnt % 2).wait()

    o_copy(t0 + cnt - 1, (cnt - 1) % 2).wait()

    @pl.when(e == n_e - 1)
    def _():
        obuf[0] = jnp.zeros(obuf.shape[1:], obuf.dtype)

        def fill(tile, carry):
            o_copy(tile, 0).start()
            o_copy(tile, 0).wait()
            return carry

        lax.fori_loop(nv_ref[0], n_tiles, fill, 0)

    if shift:
        pl.when(last_step)(shift.finish)


def _moe_matmul(x, ws, out_dtype, xs, t0, cnt, nvalid, *, tm, tn, xs_per_expert, name, window=None):
    r, k = x.shape
    n_e, _, n_cols = ws[0].shape
    s = xs.shape[1]
    n_w = len(ws)
    windows = []
    if window is not None:
        cache, new = window
        ns, lb = cache.shape[0], cache.shape[1]
        assert ns % SHIFT_SEQS == 0
        windows = [(cache.reshape(ns, lb * KV_ROW, ATT_HD), new.reshape(ns, KV_ROW, ATT_HD))]
        chunk = _shift_chunk_rows(lb) * KV_ROW
    n_bg = len(windows)
    any_spec = pl.BlockSpec(memory_space=pl.ANY)
    in_specs = [any_spec]
    in_specs += [pl.BlockSpec((1, k, tn), lambda n, e, *_: (e, 0, n)) for _ in ws]
    if xs_per_expert:
        in_specs.append(pl.BlockSpec((1, s, k), lambda n, e, *_: (e, 0, 0)))
    else:
        in_specs.append(pl.BlockSpec((1, s, k), lambda n, e, *_: (0, 0, 0)))
    in_specs += [any_spec] * (2 * n_bg)
    scratch = [pltpu.VMEM((k, tn), BF16) for _ in ws]
    scratch += [pltpu.VMEM((RING_SLOTS, tm, k), x.dtype), pltpu.VMEM((2, tm, tn), out_dtype),
                pltpu.SemaphoreType.DMA((RING_SLOTS,)), pltpu.SemaphoreType.DMA((2,))]
    if n_bg:
        scratch += [pltpu.VMEM((RING_SLOTS, SHIFT_SEQS, chunk, ATT_HD), F32),
                    pltpu.SemaphoreType.DMA((RING_SLOTS,)), pltpu.SemaphoreType.DMA((RING_SLOTS,)),
                    pltpu.SemaphoreType.DMA((1,)), pltpu.SMEM((1,), I32)]
    out_shape = [jax.ShapeDtypeStruct((r, n_cols), out_dtype), jax.ShapeDtypeStruct((n_e, s, n_cols), F32)]
    out_shape += [jax.ShapeDtypeStruct(buf.shape, buf.dtype) for buf, _ in windows]
    outs = pl.pallas_call(
        functools.partial(_moe_mm_kernel, n_w=n_w, tm=tm, n_tiles=r // tm, n_bg=n_bg),
        grid_spec=pltpu.PrefetchScalarGridSpec(
            num_scalar_prefetch=3, grid=(n_cols // tn, n_e), in_specs=in_specs,
            out_specs=[any_spec, pl.BlockSpec((1, s, tn), lambda n, e, *_: (e, 0, n))] + [any_spec] * n_bg,
            scratch_shapes=scratch),
        out_shape=out_shape,
        compiler_params=_cparams(("arbitrary", "arbitrary")), name=name,
    )(t0, cnt, nvalid, x, *ws, xs, *[a for pair in windows for a in pair])
    if window is not None:
        return outs[0], outs[1], outs[2].reshape(window[0].shape)
    return outs


def _log_sigmoid(x):
    return jnp.minimum(x, 0.0) - jnp.log1p(jnp.exp(-jnp.abs(x)))


def _mlstm_prompt_kernel(q_ref, k_ref, v_ref, o_ref, gi_ref, gf_ref, bi_ref, bf_ref, bo_ref, ghn_ref,
                         h_ref, c_out_ref, n_out_ref, m_out_ref, ct_s, n_s, m_s):
    c = pl.program_id(0)
    L = ML_CHUNK

    @pl.when(c == 0)
    def _():
        ct_s[...] = jnp.zeros_like(ct_s)
        n_s[...] = jnp.zeros_like(n_s)
        m_s[...] = jnp.zeros_like(m_s)

    ig = gi_ref[...] + bi_ref[...]
    lf = _log_sigmoid(gf_ref[...] + bf_ref[...])
    r = lax.broadcasted_iota(I32, (L, L), 0)
    s = lax.broadcasted_iota(I32, (L, L), 1)
    causal = r >= s
    tril = causal.astype(F32)
    b = jnp.dot(tril, lf, precision=HIGHEST, preferred_element_type=F32)
    b_t = b.T
    ig_t = ig.T
    m_all = m_s[...]
    m_new_all = m_all
    lane = lax.broadcasted_iota(I32, (1, 128), 1)

    H = range(ML_HEADS)
    q = [q_ref[:, h * ML_DQK:(h + 1) * ML_DQK] for h in H]
    k = [k_ref[:, h * ML_DQK:(h + 1) * ML_DQK] for h in H]
    v = [v_ref[:, h * ML_DV:(h + 1) * ML_DV] for h in H]
    ct = [ct_s[h] for h in H]
    n_old = [n_s[h:h + 1, :] for h in H]
    bc = [b[:, h:h + 1] for h in H]
    m_old = [m_all[:, h:h + 1] for h in H]

    sc = [lax.dot_general(q[h], k[h], (((1,), (1,)), ((), ())), preferred_element_type=F32) for h in H]
    qc = [jnp.dot(q[h], ct[h].astype(BF16), preferred_element_type=F32) for h in H]

    logd = [jnp.where(causal, bc[h] - b_t[h:h + 1, :] + ig_t[h:h + 1, :], -jnp.inf) for h in H]
    inter = [bc[h] + m_old[h] for h in H]
    mt = [jnp.maximum(inter[h], jnp.max(logd[h], axis=1, keepdims=True)) for h in H]
    w_inter = [jnp.exp(inter[h] - mt[h]) for h in H]
    m_new = [mt[h][L - 1:L, :] for h in H]
    b_last = [bc[h][L - 1:L, :] for h in H]
    decay = [jnp.exp(b_last[h] + m_old[h] - m_new[h]) for h in H]
    wj = [jnp.exp(b_last[h] - bc[h] + ig[:, h:h + 1] - m_new[h]) for h in H]

    a = [sc[h] * jnp.exp(logd[h] - mt[h]) for h in H]
    qn = [jnp.sum(q[h].astype(F32) * n_old[h], axis=1, keepdims=True) for h in H]
    den = [jnp.sum(a[h], axis=1, keepdims=True) + w_inter[h] * qn[h] for h in H]
    num = [jnp.dot(a[h].astype(BF16), v[h], preferred_element_type=F32) + w_inter[h] * qc[h] for h in H]

    hh = [num[h] / jnp.maximum(jnp.abs(den[h]), jnp.exp(-mt[h])) for h in H]
    for h in H:
        cols = slice(h * ML_DV, (h + 1) * ML_DV)
        og = jax.nn.sigmoid(o_ref[:, cols].astype(F32) + bo_ref[:, cols])
        h_ref[:, cols] = (_rms(hh[h]) * ghn_ref[:, cols] * og).astype(h_ref.dtype)

    kf = [k[h].astype(F32) * wj[h] for h in H]
    for h in H:
        upd = lax.dot_general(kf[h].astype(BF16), v[h], (((0,), (0,)), ((), ())), preferred_element_type=F32)
        ct_s[h] = decay[h] * ct[h] + upd
        n_s[h:h + 1, :] = decay[h] * n_old[h] + jnp.sum(kf[h], axis=0, keepdims=True)
        m_new_all = jnp.where(lane == h, m_new[h], m_new_all)

    m_s[...] = m_new_all

    @pl.when(c == pl.num_programs(0) - 1)
    def _():
        for h in range(ML_HEADS):
            c_out_ref[h] = ct_s[h].T
        n_out_ref[...] = n_s[...]
        m_out_ref[...] = m_s[...]


def _mlstm_prompt(z, gates, bi, bf, bo, ghn):
    seq = z.shape[0]
    nc = seq // ML_CHUNK
    L = ML_CHUNK
    const2 = lambda c: (0, 0)
    in_specs = [
        pl.BlockSpec((L, ML_NQK), lambda c: (c, 0)),
        pl.BlockSpec((L, ML_NQK), lambda c: (c, 1)),
        pl.BlockSpec((L, ML_NV), lambda c: (c, 1)),
        pl.BlockSpec((L, D_MODEL), lambda c: (c, 2)),
        pl.BlockSpec((L, 128), lambda c: (c, 0)),
        pl.BlockSpec((L, 128), lambda c: (c, 1)),
        pl.BlockSpec((1, 128), const2),
        pl.BlockSpec((1, 128), const2),
        pl.BlockSpec((1, D_MODEL), const2),
        pl.BlockSpec((1, ML_NV), const2),
    ]
    out_shape = [
        jax.ShapeDtypeStruct((seq, ML_NV), BF16),
        jax.ShapeDtypeStruct((ML_HEADS, ML_DV, ML_DQK), F32),
        jax.ShapeDtypeStruct((ML_HEADS, ML_DQK), F32),
        jax.ShapeDtypeStruct((1, 128), F32),
    ]
    out_specs = [
        pl.BlockSpec((L, ML_NV), lambda c: (c, 0)),
        pl.BlockSpec((ML_HEADS, ML_DV, ML_DQK), lambda c: (0, 0, 0)),
        pl.BlockSpec((ML_HEADS, ML_DQK), const2),
        pl.BlockSpec((1, 128), const2),
    ]
    return pl.pallas_call(
        _mlstm_prompt_kernel, grid=(nc,), in_specs=in_specs, out_specs=out_specs, out_shape=out_shape,
        scratch_shapes=[pltpu.VMEM((ML_HEADS, ML_DQK, ML_DV), F32), pltpu.VMEM((ML_HEADS, ML_DQK), F32),
                        pltpu.VMEM((1, 128), F32)],
        compiler_params=_cparams(("arbitrary",)), name="mlstm_prompt",
    )(z, z, z, z, gates, gates, bi, bf, bo, ghn)


def _mlstm_sample_kernel(z_ref, g_ref, c_ref, n_ref, m_ref, bi_ref, bf_ref, bo_ref, ghn_ref,
                         h_ref, c_out_ref, n_out_ref, m_out_ref):
    i = pl.program_id(0)
    z = z_ref[0]
    g = g_ref[0]
    ig_all = g[:, 0:128] + bi_ref[...]
    lf_all = _log_sigmoid(g[:, 128:256] + bf_ref[...])
    m_all = m_ref[0]
    mt_all = jnp.maximum(lf_all + m_all, ig_all)
    m_out_ref[0] = mt_all
    outs = []
    for h in range(ML_HEADS):
        q = z[:, h * ML_DQK:(h + 1) * ML_DQK]
        k = z[:, ML_NQK + h * ML_DQK:ML_NQK + (h + 1) * ML_DQK]
        v = z[:, 2 * ML_NQK + h * ML_DV:2 * ML_NQK + (h + 1) * ML_DV]
        ig = ig_all[:, h:h + 1]
        lf = lf_all[:, h:h + 1]
        m0 = m_all[:, h:h + 1]
        mt = mt_all[:, h:h + 1]
        w_inter = jnp.exp(lf + m0 - mt)
        wj = jnp.exp(ig - mt)
        a = jnp.sum(q * k, axis=1, keepdims=True) * wj
        c_h = c_ref[0, h]
        n_h = n_ref[0, h:h + 1, :]
        q8 = jnp.broadcast_to(q, (8, ML_DQK))
        cq = lax.dot_general(q8, c_h, (((1,), (1,)), ((), ())), precision=HIGHEST,
                             preferred_element_type=F32)[0:1, :]
        num = a * v + w_inter * cq
        den = a + w_inter * jnp.sum(n_h * q, axis=1, keepdims=True)
        hh = num / jnp.maximum(jnp.abs(den), jnp.exp(-mt))
        hn = _rms(hh) * ghn_ref[:, h * ML_DV:(h + 1) * ML_DV]
        og = jax.nn.sigmoid(z[:, 2 * ML_NQK + ML_NV + h * ML_DV:2 * ML_NQK + ML_NV + (h + 1) * ML_DV]
                            + bo_ref[:, h * ML_DV:(h + 1) * ML_DV])
        outs.append(hn * og)
        v_col = jnp.broadcast_to(v, (8, ML_DV)).T[:, 0:1]
        c_out_ref[0, h] = w_inter * c_h + wj * (v_col * k)
        n_out_ref[0, h:h + 1, :] = w_inter * n_h + wj * k
    h_ref[pl.ds(i, 1), :] = jnp.concatenate(outs, axis=1)


def _mlstm_sample(z_s, gates_s, c0, n0, m0, bi, bf, bo, ghn):
    ns = z_s.shape[0]
    const2 = lambda i: (0, 0)
    in_specs = [
        pl.BlockSpec((1, 1, ML_MAIN), lambda i: (i, 0, 0)),
        pl.BlockSpec((1, 1, 256), lambda i: (i, 0, 0)),
        pl.BlockSpec((1, ML_HEADS, ML_DV, ML_DQK), lambda i: (i, 0, 0, 0)),
        pl.BlockSpec((1, ML_HEADS, ML_DQK), lambda i: (i, 0, 0)),
        pl.BlockSpec((1, 1, 128), lambda i: (i, 0, 0)),
        pl.BlockSpec((1, 128), const2),
        pl.BlockSpec((1, 128), const2),
        pl.BlockSpec((1, D_MODEL), const2),
        pl.BlockSpec((1, ML_NV), const2),
    ]
    out_shape = [
        jax.ShapeDtypeStruct((ns, ML_NV), F32),
        jax.ShapeDtypeStruct((ns, ML_HEADS, ML_DV, ML_DQK), F32),
        jax.ShapeDtypeStruct((ns, ML_HEADS, ML_DQK), F32),
        jax.ShapeDtypeStruct((ns, 1, 128), F32),
    ]
    out_specs = [
        pl.BlockSpec((ns, ML_NV), const2),
        pl.BlockSpec((1, ML_HEADS, ML_DV, ML_DQK), lambda i: (i, 0, 0, 0)),
        pl.BlockSpec((1, ML_HEADS, ML_DQK), lambda i: (i, 0, 0)),
        pl.BlockSpec((1, 1, 128), lambda i: (i, 0, 0)),
    ]
    return pl.pallas_call(
        _mlstm_sample_kernel, grid=(ns,), in_specs=in_specs, out_specs=out_specs, out_shape=out_shape,
        compiler_params=_cparams(("arbitrary",)), name="mlstm_sample",
    )(z_s, gates_s, c0, n0, m0, bi, bf, bo, ghn)


def _attn_prompt_kernel(q_ref, kp_ref, kc_ref, vp_ref, vc_ref, o_ref, lse_ref):
    blk = pl.program_id(1)
    T = ATT_STEPS
    qi = lax.broadcasted_iota(I32, (T, T), 0)
    kj = lax.broadcasted_iota(I32, (T, T), 1)
    valid_prev = jnp.logical_and(kj >= qi, blk > 0)
    valid_cur = kj <= qi
    scale = ATT_HD ** -0.5
    nt = (((1,), (1,)), ((), ()))
    lane = lax.broadcasted_iota(I32, (T, ATT_HD), 1)
    lse_all = jnp.zeros((T, ATT_HD), F32)
    raw = []
    for h in range(ATT_H):
        sl = slice(h * ATT_HD, (h + 1) * ATT_HD)
        qh = q_ref[:, sl]
        raw.append((lax.dot_general(qh, kp_ref[:, sl], nt, preferred_element_type=F32),
                    lax.dot_general(qh, kc_ref[:, sl], nt, preferred_element_type=F32)))
    for h in range(ATT_H):
        sl = slice(h * ATT_HD, (h + 1) * ATT_HD)
        s1 = jnp.where(valid_prev, raw[h][0] * scale, -jnp.inf)
        s2 = jnp.where(valid_cur, raw[h][1] * scale, -jnp.inf)
        mx = jnp.max(jnp.maximum(s1, s2), axis=1, keepdims=True)
        p1 = jnp.exp(s1 - mx)
        p2 = jnp.exp(s2 - mx)
        den = jnp.sum(p1 + p2, axis=1, keepdims=True)
        acc = jnp.dot(p1.astype(BF16), vp_ref[:, sl], preferred_element_type=F32)
        acc = acc + jnp.dot(p2.astype(BF16), vc_ref[:, sl], preferred_element_type=F32)
        o_ref[:, sl] = acc / den
        lse_all = jnp.where(lane == h, mx + jnp.log(den), lse_all)
    lse_ref[...] = lse_all


def _attn_prompt(q, kv, g):
    dil, L, _ = q.shape
    nb = L // ATT_STEPS
    T = ATT_STEPS
    blk = (None, T, ATT_GW)
    in_specs = [
        pl.BlockSpec(blk, lambda r, b: (r, b, 0)),
        pl.BlockSpec(blk, lambda r, b: (r, jnp.maximum(b - 1, 0), 0)),
        pl.BlockSpec(blk, lambda r, b: (r, b, 0)),
        pl.BlockSpec(blk, lambda r, b: (r, jnp.maximum(b - 1, 0), 1)),
        pl.BlockSpec(blk, lambda r, b: (r, b, 1)),
    ]
    out_specs = [pl.BlockSpec(blk, lambda r, b: (r, b, 0)), pl.BlockSpec((None, T, ATT_HD), lambda r, b: (r, b, 0))]
    return pl.pallas_call(
        _attn_prompt_kernel, grid=(dil, nb), in_specs=in_specs, out_specs=out_specs,
        out_shape=[jax.ShapeDtypeStruct((dil, L, ATT_GW), F32), jax.ShapeDtypeStruct((dil, L, ATT_HD), F32)],
        compiler_params=_cparams(("arbitrary", "arbitrary")), name=f"attn_prompt_g{g}",
    )(q, kv, kv, kv, kv)


def _merge_kernel(*refs):
    in_refs, out_ref, scratch = refs[:2 * ATT_G], refs[2 * ATT_G], refs[2 * ATT_G + 1:]
    tm = out_ref.shape[0]

    def position_order(ref, lanes, buf, dil):
        if dil == 1:
            return ref[0, :, lanes]
        for r in range(dil):
            buf[pl.ds(r, tm // dil, stride=dil), :] = ref[r, :, lanes]
        return buf[...]

    all128 = slice(0, ATT_HD)
    lses = [position_order(in_refs[2 * g + 1], all128, scratch[2 * g + 1], ATT_DILS[g]) for g in range(ATT_G)]
    mx = jnp.maximum(jnp.maximum(lses[0], lses[1]), lses[2])
    es = [jnp.exp(l - mx) for l in lses]
    tot = es[0] + es[1] + es[2]
    wgt = [e / tot for e in es]
    for c in range(ATT_H):
        lanes = slice(c * ATT_HD, (c + 1) * ATT_HD)
        acc = None
        for g in range(ATT_G):
            o = position_order(in_refs[2 * g], lanes, scratch[2 * g], ATT_DILS[g])
            term = wgt[g][:, c:c + 1] * o
            acc = term if acc is None else acc + term
        out_ref[:, lanes] = acc.astype(out_ref.dtype)


def _merge_groups(parts, seq, tm=512):
    in_specs, args, scratch = [], [], []
    for g, pair in enumerate(parts):
        dil = ATT_DILS[g]
        for a in pair:
            in_specs.append(pl.BlockSpec((dil, tm // dil, a.shape[2]), lambda i: (0, i, 0)))
            args.append(a)
            scratch.append(pltpu.VMEM((tm, ATT_HD), F32))
    return pl.pallas_call(
        _merge_kernel, grid=(seq // tm,), in_specs=in_specs,
        out_specs=pl.BlockSpec((tm, ATT_GW), lambda i: (i, 0)),
        out_shape=jax.ShapeDtypeStruct((seq, ATT_GW), BF16), scratch_shapes=scratch,
        compiler_params=_cparams(("arbitrary",)), name="attn_merge",
    )(*args)


def _attn_sample_kernel(q_ref, kvn_ref, b0_ref, b1_ref, b2_ref, out_ref):
    scale = ATT_HD ** -0.5
    outs, lses = [], []
    for g, b_ref in enumerate((b0_ref, b1_ref, b2_ref)):
        qg = q_ref[0, g]
        kn = kvn_ref[0, g, 0]
        vn = kvn_ref[0, g, 1]
        kb = b_ref[:, 0]
        vb = b_ref[:, 1]
        s = jnp.sum(kb * qg[None], axis=2, keepdims=True) * scale
        s_new = jnp.sum(kn * qg, axis=1, keepdims=True) * scale
        mx = jnp.maximum(jnp.max(s, axis=0), s_new)
        p = jnp.exp(s - mx[None])
        p_new = jnp.exp(s_new - mx)
        den = jnp.sum(p, axis=0) + p_new
        o = jnp.sum(p * vb, axis=0) + p_new * vn
        outs.append(o / den)
        lses.append(mx + jnp.log(den))
    mxl = jnp.maximum(jnp.maximum(lses[0], lses[1]), lses[2])
    es = [jnp.exp(l - mxl) for l in lses]
    tot = es[0] + es[1] + es[2]
    out_ref[0] = (es[0] / tot) * outs[0] + (es[1] / tot) * outs[1] + (es[2] / tot) * outs[2]


def _attn_sample(q_s, kv_s, caches):
    ns = q_s.shape[0]
    views, specs = [], []
    for g, cbuf in enumerate(caches):
        lb = cbuf.shape[1]
        dil = ATT_DILS[g]
        views.append(cbuf.reshape(ns, lb // dil, dil, 2, ATT_H, ATT_HD))
        specs.append(pl.BlockSpec((None, ATT_STEPS, None, 2, ATT_H, ATT_HD), lambda i: (i, 0, 0, 0, 0, 0)))
    in_specs = [
        pl.BlockSpec((1, ATT_G, ATT_H, ATT_HD), lambda i: (i, 0, 0, 0)),
        pl.BlockSpec((1, ATT_G, 2, ATT_H, ATT_HD), lambda i: (i, 0, 0, 0, 0)),
    ] + specs
    return pl.pallas_call(
        _attn_sample_kernel, grid=(ns,), in_specs=in_specs,
        out_specs=pl.BlockSpec((1, ATT_H, ATT_HD), lambda i: (i, 0, 0)),
        out_shape=jax.ShapeDtypeStruct((ns, ATT_H, ATT_HD), F32),
        compiler_params=_cparams(("arbitrary",)), name="attn_sample",
    )(q_s, kv_s, *views)


KV_SHIFT_BLOCK = 8192


def _kv_shift_kernel(cur_ref, nxt_ref, new_ref, out_ref):
    blk = out_ref.shape[1]
    out_ref[0, :blk - KV_ROW] = cur_ref[0, KV_ROW:]
    last = pl.program_id(1) == pl.num_programs(1) - 1
    out_ref[0, blk - KV_ROW:] = jnp.where(last, new_ref[0], nxt_ref[0])


def _kv_shift(cache, new):
    ns, lb = cache.shape[0], cache.shape[1]
    rows = lb * KV_ROW
    blk = min(KV_SHIFT_BLOCK, rows)
    nb = rows // blk
    per = blk // KV_ROW
    flat = cache.reshape(ns, rows, ATT_HD)
    out = pl.pallas_call(
        _kv_shift_kernel, grid=(ns, nb),
        in_specs=[pl.BlockSpec((1, blk, ATT_HD), lambda i, j: (i, j, 0)),
                  pl.BlockSpec((1, KV_ROW, ATT_HD), lambda i, j: (i, jnp.minimum((j + 1) * per, lb - 1), 0)),
                  pl.BlockSpec((1, KV_ROW, ATT_HD), lambda i, j: (i, 0, 0))],
        out_specs=pl.BlockSpec((1, blk, ATT_HD), lambda i, j: (i, j, 0)),
        out_shape=jax.ShapeDtypeStruct(flat.shape, flat.dtype),
        compiler_params=_cparams(("arbitrary", "arbitrary")), name="kv_shift",
    )(flat, flat, new.reshape(ns, KV_ROW, ATT_HD))
    return out.reshape(cache.shape)


def _top2(y, wr_ref, br_ref):
    rows = y.shape[0]
    lane = lax.broadcasted_iota(I32, (rows, 128), 1)
    logits = jnp.dot(y, wr_ref[...], precision=HIGHEST, preferred_element_type=F32) + br_ref[...]
    logits = jnp.where(lane < N_EXPERTS, logits, NEG_BIG)
    e = jnp.exp(logits - jnp.max(logits, axis=1, keepdims=True))
    probs = e / jnp.sum(e, axis=1, keepdims=True)
    p1 = jnp.max(probs, axis=1, keepdims=True)
    i1 = jnp.min(jnp.where(probs == p1, lane, 128), axis=1, keepdims=True)
    probs2 = jnp.where(lane == i1, -1.0, probs)
    p2 = jnp.max(probs2, axis=1, keepdims=True)
    i2 = jnp.min(jnp.where(probs2 == p2, lane, 128), axis=1, keepdims=True)
    tot = p1 + p2
    return lane, i1, i2, p1 / tot, p2 / tot


def _router_kernel(x_ref, xs_ref, g_ref, wr_ref, br_ref, xn_ref, eid_ref, gate_ref, rank_ref, cnt_ref,
                   xns_ref, gs_ref, carry):
    i = pl.program_id(0)
    tm = x_ref.shape[0]

    @pl.when(i == 0)
    def _():
        carry[...] = jnp.zeros_like(carry)
        ys = _rms(xs_ref[...]) * g_ref[...]
        xns_ref[...] = ys
        lane, i1, i2, g1, g2 = _top2(ys, wr_ref, br_ref)
        gs_ref[...] = jnp.where(lane == i1, g1, jnp.where(lane == i2, g2, 0.0))

    y = _rms(x_ref[...]) * g_ref[...]
    xn_ref[...] = y
    lane, i1, i2, g1, g2 = _top2(y, wr_ref, br_ref)
    sel1 = lane == i1
    sel2 = lane == i2
    onehot = jnp.where(jnp.logical_or(sel1, sel2), 1.0, 0.0)
    rr = lax.broadcasted_iota(I32, (tm, tm), 0)
    cc = lax.broadcasted_iota(I32, (tm, tm), 1)
    before = (cc < rr).astype(BF16)
    prefix = jnp.dot(before, onehot.astype(BF16), preferred_element_type=F32) + carry[...]
    r1 = jnp.sum(jnp.where(sel1, prefix, 0.0), axis=1, keepdims=True)
    r2 = jnp.sum(jnp.where(sel2, prefix, 0.0), axis=1, keepdims=True)
    carry[...] = carry[...] + jnp.sum(onehot, axis=0, keepdims=True)
    eid_ref[...] = jnp.where(lane == 0, i1, jnp.where(lane == 1, i2, 0))
    gate_ref[...] = jnp.where(lane == 0, g1, jnp.where(lane == 1, g2, 0.0))
    rank_ref[...] = jnp.where(lane == 0, r1, jnp.where(lane == 1, r2, 0.0)).astype(I32)
    cnt_ref[...] = jnp.broadcast_to(carry[...], cnt_ref.shape)


def _router(h, hs, gain, w_router_pad, b_router_pad, tm=ROW_TILE):
    m, d = h.shape
    s = hs.shape[0]
    const = lambda i: (0, 0)
    row_spec = pl.BlockSpec((tm, 128), lambda i: (i, 0))
    return pl.pallas_call(
        _router_kernel, grid=(m // tm,),
        in_specs=[pl.BlockSpec((tm, d), lambda i: (i, 0)), pl.BlockSpec((s, d), const), pl.BlockSpec((1, d), const),
                  pl.BlockSpec((d, 128), const), pl.BlockSpec((1, 128), const)],
        out_specs=[pl.BlockSpec((tm, d), lambda i: (i, 0)), row_spec, row_spec, row_spec,
                   pl.BlockSpec((8, 128), const), pl.BlockSpec((s, d), const), pl.BlockSpec((s, 128), const)],
        out_shape=[jax.ShapeDtypeStruct((m, d), F32), jax.ShapeDtypeStruct((m, 128), I32),
                   jax.ShapeDtypeStruct((m, 128), F32), jax.ShapeDtypeStruct((m, 128), I32),
                   jax.ShapeDtypeStruct((8, 128), F32), jax.ShapeDtypeStruct((s, d), F32),
                   jax.ShapeDtypeStruct((s, 128), F32)],
        scratch_shapes=[pltpu.VMEM((1, 128), F32)],
        compiler_params=_cparams(("arbitrary",)), name="router",
    )(h, hs, gain, w_router_pad, b_router_pad)


def _dispatch_kernel(pos_ref, nv_ref, x_hbm, out_ref, inv, buf, sem, *, n_tok):
    i = pl.program_id(0)
    tg = out_ref.shape[0]

    @pl.when(i == 0)
    def _():
        def clear(s, c):
            inv[s] = 0
            return c

        lax.fori_loop(0, inv.shape[0], clear, 0, unroll=8)

        def fill(t, c):
            inv[pos_ref[2 * t]] = t
            inv[pos_ref[2 * t + 1]] = t
            return c

        lax.fori_loop(0, n_tok, fill, 0, unroll=8)

    def start_gather(tile):
        slot = tile % 2

        def issue(r2, c):
            for j in range(2):
                r = 2 * r2 + j
                pltpu.make_async_copy(x_hbm.at[pl.ds(inv[tile * tg + r], 1)], buf.at[slot, pl.ds(r, 1)],
                                      sem.at[slot]).start(priority=j)
            return c

        lax.fori_loop(0, tg // 2, issue, 0, unroll=4)

    @pl.when(i == 0)
    def _():
        start_gather(i)

    @pl.when(i + 1 < nv_ref[0])
    def _():
        start_gather(i + 1)

    @pl.when(i < nv_ref[0])
    def _():
        slot = i % 2
        pltpu.make_async_copy(x_hbm.at[pl.ds(0, tg)], buf.at[slot], sem.at[slot]).wait()
        out_ref[...] = buf[slot].astype(out_ref.dtype)

    @pl.when(i >= nv_ref[0])
    def _():
        out_ref[...] = jnp.zeros_like(out_ref)


def _dispatch(pos_flat, nvalid, xn, n_tiles, tg):
    n_tok, d = xn.shape
    return pl.pallas_call(
        functools.partial(_dispatch_kernel, n_tok=n_tok),
        grid_spec=pltpu.PrefetchScalarGridSpec(
            num_scalar_prefetch=2, grid=(n_tiles,),
            in_specs=[pl.BlockSpec(memory_space=pl.ANY)],
            out_specs=pl.BlockSpec((tg, d), lambda i, pos, nv: (i, 0)),
            scratch_shapes=[pltpu.SMEM((n_tiles * tg,), I32), pltpu.VMEM((2, tg, d), xn.dtype),
                            pltpu.SemaphoreType.DMA((2,))]),
        out_shape=jax.ShapeDtypeStruct((n_tiles * tg, d), BF16),
        compiler_params=_cparams(("arbitrary",)), name="moe_dispatch",
    )(pos_flat, nvalid, xn)


def _combine_kernel(pos_ref, h_ref, gate_ref, g_ref, hs_ref, gs_ref, ys_s_ref, ys_hbm, out_ref, outs_ref, ybuf, sem):
    i = pl.program_id(0)
    tm = h_ref.shape[0]

    def start_gather(tile):
        slot = tile % 2

        def issue(r, carry):
            for j in range(2):
                pltpu.make_async_copy(ys_hbm.at[pl.ds(pos_ref[2 * (tile * tm + r) + j], 1)],
                                      ybuf.at[slot, j, pl.ds(r, 1)], sem.at[slot]).start(priority=j)
            return carry

        lax.fori_loop(0, tm, issue, 0, unroll=4)

    @pl.when(i == 0)
    def _():
        start_gather(i)

    @pl.when(i + 1 < pl.num_programs(0))
    def _():
        start_gather(i + 1)

    @pl.when(i == 0)
    def _():
        gs = gs_ref[...]
        y = jnp.zeros(hs_ref.shape, F32)
        for e in range(N_EXPERTS):
            y = y + gs[:, e:e + 1] * ys_s_ref[e]
        outs_ref[...] = _rms(hs_ref[...] + y) * g_ref[...]

    slot = i % 2
    for j in range(2):
        pltpu.make_async_copy(ys_hbm.at[pl.ds(0, tm)], ybuf.at[slot, j], sem.at[slot]).wait()
    gate = gate_ref[...]
    y = h_ref[...] + (gate[:, 0:1] * ybuf[slot, 0] + gate[:, 1:2] * ybuf[slot, 1])
    out_ref[...] = _rms(y) * g_ref[...]


def _combine(pos_flat, h, gate, g_final, hs, gs, ys_s, ys, tm=ROW_TILE):
    m, d = h.shape
    s = hs.shape[0]
    c2 = lambda i, pos: (0, 0)
    return pl.pallas_call(
        _combine_kernel,
        grid_spec=pltpu.PrefetchScalarGridSpec(
            num_scalar_prefetch=1, grid=(m // tm,),
            in_specs=[pl.BlockSpec((tm, d), lambda i, pos: (i, 0)),
                      pl.BlockSpec((tm, 128), lambda i, pos: (i, 0)),
                      pl.BlockSpec((1, d), c2),
                      pl.BlockSpec((s, d), c2),
                      pl.BlockSpec((s, 128), c2),
                      pl.BlockSpec((N_EXPERTS, s, d), lambda i, pos: (0, 0, 0)),
                      pl.BlockSpec(memory_space=pl.ANY)],
            out_specs=[pl.BlockSpec((tm, d), lambda i, pos: (i, 0)), pl.BlockSpec((s, d), c2)],
            scratch_shapes=[pltpu.VMEM((2, 2, tm, d), F32), pltpu.SemaphoreType.DMA((2,))]),
        out_shape=[jax.ShapeDtypeStruct((m, d), F32), jax.ShapeDtypeStruct((s, d), F32)],
        compiler_params=_cparams(("arbitrary",)), name="moe_combine",
    )(pos_flat, h, gate, g_final, hs, gs, ys_s, ys)


def _rope_tables(pos):
    half = ATT_HD // 2
    inv = ROPE_THETA ** (-jnp.arange(half, dtype=F32) / half)
    ang = pos.astype(F32)[:, None] * inv[None, :]
    cos, sin = jnp.cos(ang), jnp.sin(ang)
    return jnp.concatenate([cos, cos], axis=1), jnp.concatenate([-sin, sin], axis=1)


def kernel(x_prompt, x_sample, state_mlstm_C, state_mlstm_n, state_mlstm_m, cache_kv_w128, cache_kv_w512, cache_kv_w2048, g_mix, g_ffn, w_ml_in, b_ml_gates, b_ml_o, g_ml_hnorm, w_ml_out, g_kv, w_kv, w_q, w_o, w_ffn_gate, w_ffn_up, w_ffn_down, w_router, b_router, w_exp_gate, w_exp_up, w_exp_down, g_final):
    bp, seq, d = x_prompt.shape
    ns = x_sample.shape[0]
    caches = (cache_kv_w128, cache_kv_w512, cache_kv_w2048)
    assert bp == 1 and x_sample.shape[1] == 1 and d == D_MODEL and ns % 8 == 0
    assert seq % (ATT_STEPS * max(ATT_DILS)) == 0 and seq % 1024 == 0
    assert all(c.shape[1] == w for c, w in zip(caches, ATT_WINDOWS))
    tm = 512

    h0 = x_prompt.reshape(seq, d)
    h0_s = x_sample.reshape(ns, d)

    w_gates = lax.slice_in_dim(w_ml_in, ML_MAIN, ML_MAIN + 2 * ML_HEADS, axis=2)[0]
    w_gates_pad = jnp.zeros((d, 256), F32).at[:, 0:ML_HEADS].set(w_gates[:, :ML_HEADS])
    w_gates_pad = w_gates_pad.at[:, 128:128 + ML_HEADS].set(w_gates[:, ML_HEADS:])
    bi = jnp.zeros((1, 128), F32).at[0, :ML_HEADS].set(b_ml_gates[0, :ML_HEADS])
    bf = jnp.zeros((1, 128), F32).at[0, :ML_HEADS].set(b_ml_gates[0, ML_HEADS:])
    (xn0, gates), (xn0_s, gates_s) = _rmsnorm(h0, h0_s, g_mix[0:1], proj=w_gates_pad)
    k_scale = jnp.concatenate([jnp.ones((1, ML_NQK), F32), jnp.full((1, ML_NQK), ML_DQK ** -0.5, F32),
                               jnp.ones((1, ML_NV + D_MODEL), F32)], axis=1)
    z, z_s = _matmul(xn0, [w_ml_in], ML_MAIN, BF16, xn0_s[None], tm=1024, tn=1024, col_scale=k_scale, name="ml_in")
    bo = b_ml_o[0:1]
    ghn = g_ml_hnorm[0:1]
    hg, p_c, p_n, p_m = _mlstm_prompt(z[0], gates, bi, bf, bo, ghn)
    m0 = jnp.zeros((ns, 1, 128), F32).at[:, 0, :ML_HEADS].set(state_mlstm_m[0])
    hg_s, s_c, s_n, s_m = _mlstm_sample(z_s[0].reshape(ns, 1, ML_MAIN), gates_s.reshape(ns, 1, 256),
                                        state_mlstm_C[0], state_mlstm_n[0], m0, bi, bf, bo, ghn)
    h1, h1_s = _matmul(hg, [w_ml_out], d, F32, hg_s[None], tm=tm, tn=1024, res=h0, res_s=h0_s, name="ml_out")
    h1, h1_s = h1[0], h1_s[0]

    (xf0,), (xf0_s,) = _rmsnorm(h1, h1_s, g_ffn[0:1])
    ffn_dense = w_ffn_gate.shape[2]
    hid, hid_s = _matmul(xf0, [w_ffn_gate, w_ffn_up], ffn_dense, BF16, xf0_s[None], tm=1024, tn=512, name="ffn_up")
    h2, h2_s = _matmul(hid[0], [w_ffn_down], d, F32, hid_s, tm=tm, tn=512, res=h1, res_s=h1_s, name="ffn_down")
    h2, h2_s = h2[0], h2_s[0]

    (xq, xkv), (xq_s, xkv_s) = _rmsnorm(h2, h2_s, jnp.stack([g_mix[1], g_kv]))
    cos, sin = _rope_tables(jnp.arange(seq))
    cos_s, sin_s = _rope_tables(jnp.full((ns,), PAST_LEN, I32))
    rope_args = dict(cos=cos, sin=sin, cos_s=cos_s, sin_s=sin_s)
    parts, kv_nat, kv_new, q_new = [], [], [], []
    tail = min(max(ATT_WINDOWS), seq)
    for g in range(ATT_G):
        dil = ATT_DILS[g]
        kvd, kvn, kv_s = _matmul(xkv, [w_kv[None]], 2 * ATT_GW, BF16, xkv_s[None], tm=tm, tn=ATT_GW,
                                 col_off=2 * g, rope="even", dil=dil, natural_tail=tail, name=f"kv_proj_g{g}",
                                 **rope_args)
        qd, q_s = _matmul(xq, [w_q], ATT_GW, BF16, xq_s[None], tm=tm, tn=ATT_GW, col_off=g, rope="all",
                          dil=dil, name=f"q_proj_g{g}", **rope_args)
        parts.append(_attn_prompt(qd, kvd, g))
        kv_nat.append(kvn)
        kv_new.append(kv_s[0])
        q_new.append(q_s[0])
    att = _merge_groups(parts, seq)
    att_s = _attn_sample(jnp.stack(q_new, axis=1).reshape(ns, ATT_G, ATT_H, ATT_HD),
                         jnp.stack(kv_new, axis=1).reshape(ns, ATT_G, 2, ATT_H, ATT_HD), caches)
    h3, h3_s = _matmul(att, [w_o], d, F32, att_s.reshape(1, ns, ATT_GW), tm=tm, tn=1024, res=h2, res_s=h2_s,
                       name="attn_out")
    h3, h3_s = h3[0], h3_s[0]

    wr_pad = jnp.zeros((d, 128), F32).at[:, :N_EXPERTS].set(w_router[0])
    br_pad = jnp.zeros((1, 128), F32).at[0, :N_EXPERTS].set(b_router[0])
    xn2, eid, gate, rank, cnt, xn2_s, gates_moe_s = _router(h3, h3_s, g_ffn[1:2], wr_pad, br_pad)
    tg = MOE_TILE
    n_tiles = -(-(2 * seq + N_EXPERTS * (tg - 1)) // tg)
    counts = cnt[0, :N_EXPERTS].astype(I32)
    padded = jnp.maximum((counts + tg - 1) // tg, 1) * tg
    gend = jnp.cumsum(padded)
    gstart = gend - padded
    pos = (gstart[eid[:, :2]] + rank[:, :2]).astype(I32).reshape(-1)
    nvalid = (gend[-1] // tg).astype(I32).reshape(1)
    xs = _dispatch(pos, nvalid, xn2, n_tiles, tg)
    t0 = (gstart // tg).astype(I32)
    tcnt = (padded // tg).astype(I32)
    hs, hs_s, sbuf2 = _moe_matmul(xs, [w_exp_gate[0], w_exp_up[0]], BF16, xn2_s[None], t0, tcnt, nvalid, tm=tg,
                                  tn=1024, xs_per_expert=False, name="moe_up", window=(caches[2], kv_new[2]))
    ys, ys_s, sbuf1 = _moe_matmul(hs, [w_exp_down[0]], F32, hs_s, t0, tcnt, nvalid, tm=tg, tn=512,
                                  xs_per_expert=True, name="moe_down", window=(caches[1], kv_new[1]))
    s_bufs = [_kv_shift(caches[0], kv_new[0]), sbuf1, sbuf2]
    y_p, y_s = _combine(pos, h3, gate, g_final.reshape(1, d), h3_s, gates_moe_s, ys_s, ys)

    p_bufs = []
    for g in range(ATT_G):
        keep = min(ATT_WINDOWS[g], seq)
        p_bufs.append(kv_nat[g][tail - keep:].reshape(1, keep, 2, ATT_H, ATT_HD))
    return (y_p.reshape(1, seq, d), y_s.reshape(ns, 1, d),
            p_c[None, None], p_n[None, None], p_m[:, :ML_HEADS][None],
            s_c[None], s_n[None], s_m[:, 0, :ML_HEADS][None],
            p_bufs[0], p_bufs[1], p_bufs[2], s_bufs[0], s_bufs[1], s_bufs[2])
```

```python
import functools

import jax
import jax.numpy as jnp
from jax import lax
from jax.experimental import pallas as pl
from jax.experimental.pallas import tpu as pltpu

F32 = jnp.float32
BF16 = jnp.bfloat16
I32 = jnp.int32
HIGHEST = lax.Precision.HIGHEST

D_MODEL = 2048
ML_HEADS = 8
ML_DQK = 128
ML_DV = 256
ML_NQK = ML_HEADS * ML_DQK
ML_NV = ML_HEADS * ML_DV
ML_MAIN = 2 * ML_NQK + ML_NV + D_MODEL
ML_CHUNK = 128
ATT_HD = 128
ATT_H = 8
ATT_G = 3
ATT_WINDOWS = (128, 512, 2048)
ATT_DILS = (1, 4, 16)
ATT_STEPS = 128
ATT_GW = ATT_H * ATT_HD
ROPE_THETA = 10000.0
PAST_LEN = 8192
N_EXPERTS = 8
RMS_EPS = 1e-6
NEG_BIG = -1e30

VMEM_LIMIT_BYTES = 58 * 1024 * 1024
ROW_TILE = 512
MOE_TILE = 256


def _cparams(sem):
    return pltpu.CompilerParams(dimension_semantics=sem, vmem_limit_bytes=VMEM_LIMIT_BYTES)


def _rms(x):
    return x * lax.rsqrt(jnp.mean(x * x, axis=-1, keepdims=True) + RMS_EPS)


def _norm_kernel(x_ref, xs_ref, g_ref, *refs, n_out, has_proj):
    ins = 1 if has_proj else 0
    n_each = n_out + ins
    main = refs[ins:ins + n_each]
    side = refs[ins + n_each:ins + 2 * n_each]

    def emit(x, outs):
        y = _rms(x)
        for i in range(n_out):
            outs[i][...] = (y * g_ref[i:i + 1, :]).astype(outs[i].dtype)
        if has_proj:
            outs[n_out][...] = jnp.dot(y * g_ref[0:1, :], refs[0][...], precision=HIGHEST,
                                       preferred_element_type=F32)

    emit(x_ref[...], main)

    @pl.when(pl.program_id(0) == 0)
    def _():
        emit(xs_ref[...], side)


def _rmsnorm(x, xs, gains, proj=None, tm=ROW_TILE):
    m, d = x.shape
    s = xs.shape[0]
    n_out = gains.shape[0]
    const = lambda i: (0, 0)
    in_specs = [pl.BlockSpec((tm, d), lambda i: (i, 0)), pl.BlockSpec((s, d), const),
                pl.BlockSpec((n_out, d), const)]
    args = [x, xs, gains]
    main_shape = [jax.ShapeDtypeStruct((m, d), BF16)] * n_out
    main_specs = [pl.BlockSpec((tm, d), lambda i: (i, 0))] * n_out
    side_shape = [jax.ShapeDtypeStruct((s, d), F32)] * n_out
    side_specs = [pl.BlockSpec((s, d), const)] * n_out
    if proj is not None:
        p = proj.shape[1]
        in_specs.append(pl.BlockSpec((d, p), const))
        args.append(proj)
        main_shape.append(jax.ShapeDtypeStruct((m, p), F32))
        main_specs.append(pl.BlockSpec((tm, p), lambda i: (i, 0)))
        side_shape.append(jax.ShapeDtypeStruct((s, p), F32))
        side_specs.append(pl.BlockSpec((s, p), const))
    outs = pl.pallas_call(
        functools.partial(_norm_kernel, n_out=n_out, has_proj=proj is not None),
        grid=(m // tm,), in_specs=in_specs, out_specs=main_specs + side_specs,
        out_shape=main_shape + side_shape,
        compiler_params=_cparams(("arbitrary",)), name="rmsnorm",
    )(*args)
    k = len(main_shape)
    return outs[:k], outs[k:]


def _rope_heads(acc, cos, sin):
    outs = []
    for h in range(acc.shape[1] // ATT_HD):
        a = acc[:, h * ATT_HD:(h + 1) * ATT_HD]
        outs.append(a * cos + pltpu.roll(a, ATT_HD // 2, 1) * sin)
    return jnp.concatenate(outs, axis=1)


def _mm_kernel(te_ref, nv_ref, x_ref, *refs, n_w, has_scale, has_res, rope, dil, nat_first):
    n = pl.program_id(0)
    m = pl.program_id(1)
    it = iter(refs)
    w_refs = [next(it) for _ in range(n_w)]
    scale_ref = next(it) if has_scale else None
    res_ref = next(it) if has_res else None
    cos_ref, sin_ref = (next(it), next(it)) if rope else (None, None)
    xs_ref = next(it)
    res_s_ref = next(it) if has_res else None
    cos_s_ref, sin_s_ref = (next(it), next(it)) if rope else (None, None)
    o_ref = next(it)
    nat_ref = next(it) if nat_first is not None else None
    os_ref = next(it)
    wb_refs = [next(it) for _ in range(n_w)]
    deint = next(it) if dil > 1 else None

    def finish(acc, up, res, cos, sin, store):
        if n_w == 2:
            acc = (acc * jax.nn.sigmoid(acc)) * up
        if has_scale:
            acc = acc * scale_ref[...]
        if has_res:
            acc = acc + res
        if rope == "all":
            store(_rope_heads(acc, cos, sin))
        elif rope == "even":
            @pl.when(n % 2 == 0)
            def _():
                store(_rope_heads(acc, cos, sin))

            @pl.when(n % 2 == 1)
            def _():
                store(acc)
        else:
            store(acc)

    def store_side(val):
        os_ref[0] = val

    def store_main(val):
        if nat_first is not None:
            @pl.when(m >= nat_first)
            def _():
                nat_ref[...] = val
        if dil == 1:
            o_ref[0] = val.astype(o_ref.dtype)
        else:
            rows = deint.shape[1] // dil
            for c in range(deint.shape[0]):
                lanes = slice(c * 128, (c + 1) * 128)
                deint[c] = val[:, lanes]
                for r in range(dil):
                    o_ref[r, :, lanes] = deint[c, pl.ds(r, rows, stride=dil), :].astype(o_ref.dtype)

    prev = jnp.maximum(m - 1, 0)
    new_weights = jnp.logical_or(m == 0, te_ref[m] != te_ref[prev])

    @pl.when(new_weights)
    def _():
        for w_ref, wb_ref in zip(w_refs, wb_refs):
            wb_ref[...] = w_ref[0].astype(BF16)
        xs = xs_ref[0]
        s_rows = xs.shape[0]
        xh = xs.astype(BF16)
        xl = (xs - xh.astype(F32)).astype(BF16)
        x_hl = jnp.concatenate([xh, xl], axis=0)

        def side_product(i):
            wh = wb_refs[i][...]
            wl = (w_refs[i][0] - wh.astype(F32)).astype(BF16)
            both = jnp.dot(x_hl, wh, preferred_element_type=F32)
            return both[:s_rows] + both[s_rows:] + jnp.dot(xh, wl, preferred_element_type=F32)

        acc = side_product(0)
        up = side_product(1) if n_w == 2 else None
        finish(acc, up, res_s_ref[...] if has_res else None,
               cos_s_ref[...] if rope else None, sin_s_ref[...] if rope else None, store_side)

    @pl.when(m >= nv_ref[0])
    def _():
        o_ref[...] = jnp.zeros_like(o_ref)

    @pl.when(m < nv_ref[0])
    def _():
        xb = x_ref[...].astype(BF16)
        acc = jnp.dot(xb, wb_refs[0][...], preferred_element_type=F32)
        up = jnp.dot(xb, wb_refs[1][...], preferred_element_type=F32) if n_w == 2 else None
        finish(acc, up, res_ref[...] if has_res else None,
               cos_ref[...] if rope else None, sin_ref[...] if rope else None, store_main)


def _matmul(x, ws, n_cols, out_dtype, xs, *, tm, tn, col_off=0, te=None, nvalid=None, xs_per_expert=False,
            col_scale=None, res=None, res_s=None, rope=None, cos=None, sin=None, cos_s=None, sin_s=None,
            dil=1, natural_tail=0, name="matmul"):
    m, k = x.shape
    s = xs.shape[1]
    n_m = m // tm
    n_n = n_cols // tn
    n_e = ws[0].shape[0]
    if te is None:
        te = jnp.zeros((n_m,), I32)
        nvalid = jnp.full((1,), n_m, I32)

    def row(mi, nv):
        return jnp.minimum(mi, nv[0] - 1)

    def exp(mi, te, nv):
        return te[row(mi, nv)]

    in_specs = [pl.BlockSpec((tm, k), lambda n, mi, te, nv: (row(mi, nv), 0))]
    args = [x]
    for w in ws:
        in_specs.append(pl.BlockSpec((1, k, tn), lambda n, mi, te, nv: (exp(mi, te, nv), 0, n + col_off)))
        args.append(w)
    if col_scale is not None:
        in_specs.append(pl.BlockSpec((1, tn), lambda n, mi, te, nv: (0, n)))
        args.append(col_scale)
    if res is not None:
        in_specs.append(pl.BlockSpec((tm, tn), lambda n, mi, te, nv: (row(mi, nv), n)))
        args.append(res)
    if rope is not None:
        for t in (cos, sin):
            in_specs.append(pl.BlockSpec((tm, ATT_HD), lambda n, mi, te, nv: (row(mi, nv), 0)))
            args.append(t)
    if xs_per_expert:
        in_specs.append(pl.BlockSpec((1, s, k), lambda n, mi, te, nv: (exp(mi, te, nv), 0, 0)))
    else:
        in_specs.append(pl.BlockSpec((1, s, k), lambda n, mi, te, nv: (0, 0, 0)))
    args.append(xs)
    if res is not None:
        in_specs.append(pl.BlockSpec((s, tn), lambda n, mi, te, nv: (0, n)))
        args.append(res_s)
    if rope is not None:
        for t in (cos_s, sin_s):
            in_specs.append(pl.BlockSpec((s, ATT_HD), lambda n, mi, te, nv: (0, 0)))
            args.append(t)

    out_shape = [jax.ShapeDtypeStruct((dil, m // dil, n_cols), out_dtype)]
    out_specs = [pl.BlockSpec((dil, tm // dil, tn), lambda n, mi, te, nv: (0, mi, n))]
    nat_first = None
    if natural_tail:
        nat_first = n_m - natural_tail // tm
        out_shape.append(jax.ShapeDtypeStruct((natural_tail, n_cols), F32))
        out_specs.append(pl.BlockSpec((tm, tn), lambda n, mi, te, nv: (jnp.maximum(mi - nat_first, 0), n)))
    out_shape.append(jax.ShapeDtypeStruct((n_e, s, n_cols), F32))
    out_specs.append(pl.BlockSpec((1, s, tn), lambda n, mi, te, nv: (exp(mi, te, nv), 0, n)))
    scratch = [pltpu.VMEM((k, tn), BF16) for _ in ws]
    if dil > 1:
        scratch.append(pltpu.VMEM((tn // 128, tm, 128), F32))
    kern = functools.partial(_mm_kernel, n_w=len(ws), has_scale=col_scale is not None, has_res=res is not None,
                             rope=rope, dil=dil, nat_first=nat_first)
    return pl.pallas_call(
        kern,
        grid_spec=pltpu.PrefetchScalarGridSpec(
            num_scalar_prefetch=2, grid=(n_n, n_m), in_specs=in_specs, out_specs=out_specs,
            scratch_shapes=scratch),
        out_shape=out_shape,
        compiler_params=_cparams(("arbitrary", "arbitrary")), name=name,
    )(te, nvalid, *args)


KV_ROW = 2 * ATT_H
RING_SLOTS = 3
SHIFT_SEQS = 2
SHIFT_MAX_ROWS = 128


def _shift_chunk_rows(window):
    moved = window - 1
    return max(d for d in range(1, SHIFT_MAX_ROWS + 1) if moved % d == 0)


class _WindowShift:
    def __init__(self, buf_hbm, new_hbm, out_hbm, stage, sem_in, sem_out, sem_new, next_chunk):
        self.buf, self.new, self.out, self.stage = buf_hbm, new_hbm, out_hbm, stage
        self.sem_in, self.sem_out, self.sem_new, self.next_chunk = sem_in, sem_out, sem_new, next_chunk
        ns, self.rows = buf_hbm.shape[0], buf_hbm.shape[1]
        self.chunk = stage.shape[2]
        self.per_group = (self.rows - KV_ROW) // self.chunk
        self.total = (ns // SHIFT_SEQS) * self.per_group

    def _load(self, k, slot):
        seqs = pl.ds((k // self.per_group) * SHIFT_SEQS, SHIFT_SEQS)
        src = self.buf.at[seqs, pl.ds(KV_ROW + (k % self.per_group) * self.chunk, self.chunk)]
        return pltpu.make_async_copy(src, self.stage.at[slot], self.sem_in.at[slot])

    def _store(self, k, slot):
        seqs = pl.ds((k // self.per_group) * SHIFT_SEQS, SHIFT_SEQS)
        dst = self.out.at[seqs, pl.ds((k % self.per_group) * self.chunk, self.chunk)]
        return pltpu.make_async_copy(self.stage.at[slot], dst, self.sem_out.at[slot])

    def _append(self):
        return pltpu.make_async_copy(self.new, self.out.at[:, pl.ds(self.rows - KV_ROW, KV_ROW)], self.sem_new.at[0])

    def begin(self):
        self.next_chunk[0] = 0
        for k in range(RING_SLOTS - 1):
            self._load(k, k).start(priority=1)
        self._append().start()

    def advance(self):
        k = self.next_chunk[0]

        @pl.when(k < self.total)
        def _():
            self._load(k, k % RING_SLOTS).wait()

            @pl.when(k >= 1)
            def _():
                self._store(k - 1, (k - 1) % RING_SLOTS).wait()

            self._store(k, k % RING_SLOTS).start()
            ahead = k + RING_SLOTS - 1

            @pl.when(ahead < self.total)
            def _():
                self._load(ahead, ahead % RING_SLOTS).start(priority=1)

            self.next_chunk[0] = k + 1

    def finish(self):
        def rest(i, carry):
            self.advance()
            return carry

        lax.fori_loop(self.next_chunk[0], self.total, rest, 0)
        self._store(self.total - 1, (self.total - 1) % RING_SLOTS).wait()
        self._append().wait()


def _moe_mm_kernel(t0_ref, cnt_ref, nv_ref, x_hbm, *refs, n_w, tm, n_tiles, n_bg):
    n = pl.program_id(0)
    e = pl.program_id(1)
    n_e = pl.num_programs(1)
    it = iter(refs)
    w_refs = [next(it) for _ in range(n_w)]
    xs_ref = next(it)
    bg_in = [next(it) for _ in range(2 * n_bg)]
    o_hbm, os_ref = next(it), next(it)
    bg_out = [next(it) for _ in range(n_bg)]
    wb_refs = [next(it) for _ in range(n_w)]
    xbuf, obuf, sem_in, sem_out = next(it), next(it), next(it), next(it)
    tn = os_ref.shape[2]
    t0 = t0_ref[e]
    cnt = cnt_ref[e]
    first_step = jnp.logical_and(n == 0, e == 0)
    last_step = jnp.logical_and(n == pl.num_programs(0) - 1, e == n_e - 1)

    shift = _WindowShift(bg_in[0], bg_in[1], bg_out[0], *[next(it) for _ in range(5)]) if n_bg else None
    if shift:
        pl.when(first_step)(shift.begin)

    def x_copy(tile, slot):
        return pltpu.make_async_copy(x_hbm.at[pl.ds(tile * tm, tm)], xbuf.at[slot], sem_in.at[slot])

    def o_copy(tile, slot):
        return pltpu.make_async_copy(obuf.at[slot], o_hbm.at[pl.ds(tile * tm, tm), pl.ds(n * tn, tn)],
                                     sem_out.at[slot])

    def product(xb):
        acc = jnp.dot(xb, wb_refs[0][...], preferred_element_type=F32)
        if n_w == 2:
            acc = (acc * jax.nn.sigmoid(acc)) * jnp.dot(xb, wb_refs[1][...], preferred_element_type=F32)
        return acc

    def start_first_tiles(first_tile, n_tiles_here):
        for j in range(RING_SLOTS - 1):
            @pl.when(j < n_tiles_here)
            def _():
                x_copy(first_tile + j, j).start(priority=1)

    @pl.when(first_step)
    def _():
        start_first_tiles(t0, cnt)

    for w_ref, wb_ref in zip(w_refs, wb_refs):
        wb_ref[...] = w_ref[0].astype(BF16)
    os_ref[0] = product(xs_ref[0].astype(BF16))

    def body(t, carry):
        slot = t % 2
        ahead = t + RING_SLOTS - 1

        @pl.when(ahead < cnt)
        def _():
            x_copy(t0 + ahead, ahead % RING_SLOTS).start(priority=1)

        x_copy(t0 + t, t % RING_SLOTS).wait()

        @pl.when(t >= 2)
        def _():
            o_copy(t0 + t - 2, slot).wait()

        obuf[slot] = product(xbuf[t % RING_SLOTS]).astype(obuf.dtype)
        o_copy(t0 + t, slot).start()
        if shift:
            shift.advance()
        return carry

    lax.fori_loop(0, cnt, body, 0)

    @pl.when(jnp.logical_not(last_step))
    def _():
        e_next = jnp.where(e == n_e - 1, 0, e + 1)
        start_first_tiles(t0_ref[e_next], cnt_ref[e_next])

    @pl.when(cnt >= 2)
    def _():
        o_copy(t0 + cnt - 2, cnt % 2).wait()

    o_copy(t0 + cnt - 1, (cnt - 1) % 2).wait()

    @pl.when(e == n_e - 1)
    def _():
        obuf[0] = jnp.zeros(obuf.shape[1:], obuf.dtype)

        def fill(tile, carry):
            o_copy(tile, 0).start()
            o_copy(tile, 0).wait()
            return carry

        lax.fori_loop(nv_ref[0], n_tiles, fill, 0)

    if shift:
        pl.when(last_step)(shift.finish)


def _moe_matmul(x, ws, out_dtype, xs, t0, cnt, nvalid, *, tm, tn, xs_per_expert, name, window=None):
    r, k = x.shape
    n_e, _, n_cols = ws[0].shape
    s = xs.shape[1]
    n_w = len(ws)
    windows = []
    if window is not None:
        cache, new = window
        ns, lb = cache.shape[0], cache.shape[1]
        assert ns % SHIFT_SEQS == 0
        windows = [(cache.reshape(ns, lb * KV_ROW, ATT_HD), new.reshape(ns, KV_ROW, ATT_HD))]
        chunk = _shift_chunk_rows(lb) * KV_ROW
    n_bg = len(windows)
    any_spec = pl.BlockSpec(memory_space=pl.ANY)
    in_specs = [any_spec]
    in_specs += [pl.BlockSpec((1, k, tn), lambda n, e, *_: (e, 0, n)) for _ in ws]
    if xs_per_expert:
        in_specs.append(pl.BlockSpec((1, s, k), lambda n, e, *_: (e, 0, 0)))
    else:
        in_specs.append(pl.BlockSpec((1, s, k), lambda n, e, *_: (0, 0, 0)))
    in_specs += [any_spec] * (2 * n_bg)
    scratch = [pltpu.VMEM((k, tn), BF16) for _ in ws]
    scratch += [pltpu.VMEM((RING_SLOTS, tm, k), x.dtype), pltpu.VMEM((2, tm, tn), out_dtype),
                pltpu.SemaphoreType.DMA((RING_SLOTS,)), pltpu.SemaphoreType.DMA((2,))]
    if n_bg:
        scratch += [pltpu.VMEM((RING_SLOTS, SHIFT_SEQS, chunk, ATT_HD), F32),
                    pltpu.SemaphoreType.DMA((RING_SLOTS,)), pltpu.SemaphoreType.DMA((RING_SLOTS,)),
                    pltpu.SemaphoreType.DMA((1,)), pltpu.SMEM((1,), I32)]
    out_shape = [jax.ShapeDtypeStruct((r, n_cols), out_dtype), jax.ShapeDtypeStruct((n_e, s, n_cols), F32)]
    out_shape += [jax.ShapeDtypeStruct(buf.shape, buf.dtype) for buf, _ in windows]
    outs = pl.pallas_call(
        functools.partial(_moe_mm_kernel, n_w=n_w, tm=tm, n_tiles=r // tm, n_bg=n_bg),
        grid_spec=pltpu.PrefetchScalarGridSpec(
            num_scalar_prefetch=3, grid=(n_cols // tn, n_e), in_specs=in_specs,
            out_specs=[any_spec, pl.BlockSpec((1, s, tn), lambda n, e, *_: (e, 0, n))] + [any_spec] * n_bg,
            scratch_shapes=scratch),
        out_shape=out_shape,
        compiler_params=_cparams(("arbitrary", "arbitrary")), name=name,
    )(t0, cnt, nvalid, x, *ws, xs, *[a for pair in windows for a in pair])
    if window is not None:
        return outs[0], outs[1], outs[2].reshape(window[0].shape)
    return outs


def _log_sigmoid(x):
    return jnp.minimum(x, 0.0) - jnp.log1p(jnp.exp(-jnp.abs(x)))


def _mlstm_prompt_kernel(q_ref, k_ref, v_ref, o_ref, gi_ref, gf_ref, bi_ref, bf_ref, bo_ref, ghn_ref,
                         h_ref, c_out_ref, n_out_ref, m_out_ref, ct_s, n_s, m_s):
    c = pl.program_id(0)
    L = ML_CHUNK

    @pl.when(c == 0)
    def _():
        ct_s[...] = jnp.zeros_like(ct_s)
        n_s[...] = jnp.zeros_like(n_s)
        m_s[...] = jnp.zeros_like(m_s)

    ig = gi_ref[...] + bi_ref[...]
    lf = _log_sigmoid(gf_ref[...] + bf_ref[...])
    r = lax.broadcasted_iota(I32, (L, L), 0)
    s = lax.broadcasted_iota(I32, (L, L), 1)
    causal = r >= s
    tril = causal.astype(F32)
    b = jnp.dot(tril, lf, precision=HIGHEST, preferred_element_type=F32)
    b_t = b.T
    ig_t = ig.T
    m_all = m_s[...]
    m_new_all = m_all
    lane = lax.broadcasted_iota(I32, (1, 128), 1)

    H = range(ML_HEADS)
    q = [q_ref[:, h * ML_DQK:(h + 1) * ML_DQK] for h in H]
    k = [k_ref[:, h * ML_DQK:(h + 1) * ML_DQK] for h in H]
    v = [v_ref[:, h * ML_DV:(h + 1) * ML_DV] for h in H]
    ct = [ct_s[h] for h in H]
    n_old = [n_s[h:h + 1, :] for h in H]
    bc = [b[:, h:h + 1] for h in H]
    m_old = [m_all[:, h:h + 1] for h in H]

    sc = [lax.dot_general(q[h], k[h], (((1,), (1,)), ((), ())), preferred_element_type=F32) for h in H]
    qc = [jnp.dot(q[h], ct[h].astype(BF16), preferred_element_type=F32) for h in H]

    logd = [jnp.where(causal, bc[h] - b_t[h:h + 1, :] + ig_t[h:h + 1, :], -jnp.inf) for h in H]
    inter = [bc[h] + m_old[h] for h in H]
    mt = [jnp.maximum(inter[h], jnp.max(logd[h], axis=1, keepdims=True)) for h in H]
    w_inter = [jnp.exp(inter[h] - mt[h]) for h in H]
    m_new = [mt[h][L - 1:L, :] for h in H]
    b_last = [bc[h][L - 1:L, :] for h in H]
    decay = [jnp.exp(b_last[h] + m_old[h] - m_new[h]) for h in H]
    wj = [jnp.exp(b_last[h] - bc[h] + ig[:, h:h + 1] - m_new[h]) for h in H]

    a = [sc[h] * jnp.exp(logd[h] - mt[h]) for h in H]
    qn = [jnp.sum(q[h].astype(F32) * n_old[h], axis=1, keepdims=True) for h in H]
    den = [jnp.sum(a[h], axis=1, keepdims=True) + w_inter[h] * qn[h] for h in H]
    num = [jnp.dot(a[h].astype(BF16), v[h], preferred_element_type=F32) + w_inter[h] * qc[h] for h in H]

    hh = [num[h] / jnp.maximum(jnp.abs(den[h]), jnp.exp(-mt[h])) for h in H]
    for h in H:
        cols = slice(h * ML_DV, (h + 1) * ML_DV)
        og = jax.nn.sigmoid(o_ref[:, cols].astype(F32) + bo_ref[:, cols])
        h_ref[:, cols] = (_rms(hh[h]) * ghn_ref[:, cols] * og).astype(h_ref.dtype)

    kf = [k[h].astype(F32) * wj[h] for h in H]
    for h in H:
        upd = lax.dot_general(kf[h].astype(BF16), v[h], (((0,), (0,)), ((), ())), preferred_element_type=F32)
        ct_s[h] = decay[h] * ct[h] + upd
        n_s[h:h + 1, :] = decay[h] * n_old[h] + jnp.sum(kf[h], axis=0, keepdims=True)
        m_new_all = jnp.where(lane == h, m_new[h], m_new_all)

    m_s[...] = m_new_all

    @pl.when(c == pl.num_programs(0) - 1)
    def _():
        for h in range(ML_HEADS):
            c_out_ref[h] = ct_s[h].T
        n_out_ref[...] = n_s[...]
        m_out_ref[...] = m_s[...]


def _mlstm_prompt(z, gates, bi, bf, bo, ghn):
    seq = z.shape[0]
    nc = seq // ML_CHUNK
    L = ML_CHUNK
    const2 = lambda c: (0, 0)
    in_specs = [
        pl.BlockSpec((L, ML_NQK), lambda c: (c, 0)),
        pl.BlockSpec((L, ML_NQK), lambda c: (c, 1)),
        pl.BlockSpec((L, ML_NV), lambda c: (c, 1)),
        pl.BlockSpec((L, D_MODEL), lambda c: (c, 2)),
        pl.BlockSpec((L, 128), lambda c: (c, 0)),
        pl.BlockSpec((L, 128), lambda c: (c, 1)),
        pl.BlockSpec((1, 128), const2),
        pl.BlockSpec((1, 128), const2),
        pl.BlockSpec((1, D_MODEL), const2),
        pl.BlockSpec((1, ML_NV), const2),
    ]
    out_shape = [
        jax.ShapeDtypeStruct((seq, ML_NV), BF16),
        jax.ShapeDtypeStruct((ML_HEADS, ML_DV, ML_DQK), F32),
        jax.ShapeDtypeStruct((ML_HEADS, ML_DQK), F32),
        jax.ShapeDtypeStruct((1, 128), F32),
    ]
    out_specs = [
        pl.BlockSpec((L, ML_NV), lambda c: (c, 0)),
        pl.BlockSpec((ML_HEADS, ML_DV, ML_DQK), lambda c: (0, 0, 0)),
        pl.BlockSpec((ML_HEADS, ML_DQK), const2),
        pl.BlockSpec((1, 128), const2),
    ]
    return pl.pallas_call(
        _mlstm_prompt_kernel, grid=(nc,), in_specs=in_specs, out_specs=out_specs, out_shape=out_shape,
        scratch_shapes=[pltpu.VMEM((ML_HEADS, ML_DQK, ML_DV), F32), pltpu.VMEM((ML_HEADS, ML_DQK), F32),
                        pltpu.VMEM((1, 128), F32)],
        compiler_params=_cparams(("arbitrary",)), name="mlstm_prompt",
    )(z, z, z, z, gates, gates, bi, bf, bo, ghn)


def _mlstm_sample_kernel(z_ref, g_ref, c_ref, n_ref, m_ref, bi_ref, bf_ref, bo_ref, ghn_ref,
                         h_ref, c_out_ref, n_out_ref, m_out_ref):
    i = pl.program_id(0)
    z = z_ref[0]
    g = g_ref[0]
    ig_all = g[:, 0:128] + bi_ref[...]
    lf_all = _log_sigmoid(g[:, 128:256] + bf_ref[...])
    m_all = m_ref[0]
    mt_all = jnp.maximum(lf_all + m_all, ig_all)
    m_out_ref[0] = mt_all
    outs = []
    for h in range(ML_HEADS):
        q = z[:, h * ML_DQK:(h + 1) * ML_DQK]
        k = z[:, ML_NQK + h * ML_DQK:ML_NQK + (h + 1) * ML_DQK]
        v = z[:, 2 * ML_NQK + h * ML_DV:2 * ML_NQK + (h + 1) * ML_DV]
        ig = ig_all[:, h:h + 1]
        lf = lf_all[:, h:h + 1]
        m0 = m_all[:, h:h + 1]
        mt = mt_all[:, h:h + 1]
        w_inter = jnp.exp(lf + m0 - mt)
        wj = jnp.exp(ig - mt)
        a = jnp.sum(q * k, axis=1, keepdims=True) * wj
        c_h = c_ref[0, h]
        n_h = n_ref[0, h:h + 1, :]
        q8 = jnp.broadcast_to(q, (8, ML_DQK))
        cq = lax.dot_general(q8, c_h, (((1,), (1,)), ((), ())), precision=HIGHEST,
                             preferred_element_type=F32)[0:1, :]
        num = a * v + w_inter * cq
        den = a + w_inter * jnp.sum(n_h * q, axis=1, keepdims=True)
        hh = num / jnp.maximum(jnp.abs(den), jnp.exp(-mt))
        hn = _rms(hh) * ghn_ref[:, h * ML_DV:(h + 1) * ML_DV]
        og = jax.nn.sigmoid(z[:, 2 * ML_NQK + ML_NV + h * ML_DV:2 * ML_NQK + ML_NV + (h + 1) * ML_DV]
                            + bo_ref[:, h * ML_DV:(h + 1) * ML_DV])
        outs.append(hn * og)
        v_col = jnp.broadcast_to(v, (8, ML_DV)).T[:, 0:1]
        c_out_ref[0, h] = w_inter * c_h + wj * (v_col * k)
        n_out_ref[0, h:h + 1, :] = w_inter * n_h + wj * k
    h_ref[pl.ds(i, 1), :] = jnp.concatenate(outs, axis=1)


def _mlstm_sample(z_s, gates_s, c0, n0, m0, bi, bf, bo, ghn):
    ns = z_s.shape[0]
    const2 = lambda i: (0, 0)
    in_specs = [
        pl.BlockSpec((1, 1, ML_MAIN), lambda i: (i, 0, 0)),
        pl.BlockSpec((1, 1, 256), lambda i: (i, 0, 0)),
        pl.BlockSpec((1, ML_HEADS, ML_DV, ML_DQK), lambda i: (i, 0, 0, 0)),
        pl.BlockSpec((1, ML_HEADS, ML_DQK), lambda i: (i, 0, 0)),
        pl.BlockSpec((1, 1, 128), lambda i: (i, 0, 0)),
        pl.BlockSpec((1, 128), const2),
        pl.BlockSpec((1, 128), const2),
        pl.BlockSpec((1, D_MODEL), const2),
        pl.BlockSpec((1, ML_NV), const2),
    ]
    out_shape = [
        jax.ShapeDtypeStruct((ns, ML_NV), F32),
        jax.ShapeDtypeStruct((ns, ML_HEADS, ML_DV, ML_DQK), F32),
        jax.ShapeDtypeStruct((ns, ML_HEADS, ML_DQK), F32),
        jax.ShapeDtypeStruct((ns, 1, 128), F32),
    ]
    out_specs = [
        pl.BlockSpec((ns, ML_NV), const2),
        pl.BlockSpec((1, ML_HEADS, ML_DV, ML_DQK), lambda i: (i, 0, 0, 0)),
        pl.BlockSpec((1, ML_HEADS, ML_DQK), lambda i: (i, 0, 0)),
        pl.BlockSpec((1, 1, 128), lambda i: (i, 0, 0)),
    ]
    return pl.pallas_call(
        _mlstm_sample_kernel, grid=(ns,), in_specs=in_specs, out_specs=out_specs, out_shape=out_shape,
        compiler_params=_cparams(("arbitrary",)), name="mlstm_sample",
    )(z_s, gates_s, c0, n0, m0, bi, bf, bo, ghn)


def _attn_prompt_kernel(q_ref, kp_ref, kc_ref, vp_ref, vc_ref, o_ref, lse_ref):
    blk = pl.program_id(1)
    T = ATT_STEPS
    qi = lax.broadcasted_iota(I32, (T, T), 0)
    kj = lax.broadcasted_iota(I32, (T, T), 1)
    valid_prev = jnp.logical_and(kj >= qi, blk > 0)
    valid_cur = kj <= qi
    scale = ATT_HD ** -0.5
    nt = (((1,), (1,)), ((), ()))
    lane = lax.broadcasted_iota(I32, (T, ATT_HD), 1)
    lse_all = jnp.zeros((T, ATT_HD), F32)
    raw = []
    for h in range(ATT_H):
        sl = slice(h * ATT_HD, (h + 1) * ATT_HD)
        qh = q_ref[:, sl]
        raw.append((lax.dot_general(qh, kp_ref[:, sl], nt, preferred_element_type=F32),
                    lax.dot_general(qh, kc_ref[:, sl], nt, preferred_element_type=F32)))
    for h in range(ATT_H):
        sl = slice(h * ATT_HD, (h + 1) * ATT_HD)
        s1 = jnp.where(valid_prev, raw[h][0] * scale, -jnp.inf)
        s2 = jnp.where(valid_cur, raw[h][1] * scale, -jnp.inf)
        mx = jnp.max(jnp.maximum(s1, s2), axis=1, keepdims=True)
        p1 = jnp.exp(s1 - mx)
        p2 = jnp.exp(s2 - mx)
        den = jnp.sum(p1 + p2, axis=1, keepdims=True)
        acc = jnp.dot(p1.astype(BF16), vp_ref[:, sl], preferred_element_type=F32)
        acc = acc + jnp.dot(p2.astype(BF16), vc_ref[:, sl], preferred_element_type=F32)
        o_ref[:, sl] = acc / den
        lse_all = jnp.where(lane == h, mx + jnp.log(den), lse_all)
    lse_ref[...] = lse_all


def _attn_prompt(q, kv, g):
    dil, L, _ = q.shape
    nb = L // ATT_STEPS
    T = ATT_STEPS
    blk = (None, T, ATT_GW)
    in_specs = [
        pl.BlockSpec(blk, lambda r, b: (r, b, 0)),
        pl.BlockSpec(blk, lambda r, b: (r, jnp.maximum(b - 1, 0), 0)),
        pl.BlockSpec(blk, lambda r, b: (r, b, 0)),
        pl.BlockSpec(blk, lambda r, b: (r, jnp.maximum(b - 1, 0), 1)),
        pl.BlockSpec(blk, lambda r, b: (r, b, 1)),
    ]
    out_specs = [pl.BlockSpec(blk, lambda r, b: (r, b, 0)), pl.BlockSpec((None, T, ATT_HD), lambda r, b: (r, b, 0))]
    return pl.pallas_call(
        _attn_prompt_kernel, grid=(dil, nb), in_specs=in_specs, out_specs=out_specs,
        out_shape=[jax.ShapeDtypeStruct((dil, L, ATT_GW), F32), jax.ShapeDtypeStruct((dil, L, ATT_HD), F32)],
        compiler_params=_cparams(("arbitrary", "arbitrary")), name=f"attn_prompt_g{g}",
    )(q, kv, kv, kv, kv)


def _merge_kernel(*refs):
    in_refs, out_ref, scratch = refs[:2 * ATT_G], refs[2 * ATT_G], refs[2 * ATT_G + 1:]
    tm = out_ref.shape[0]

    def position_order(ref, lanes, buf, dil):
        if dil == 1:
            return ref[0, :, lanes]
        for r in range(dil):
            buf[pl.ds(r, tm // dil, stride=dil), :] = ref[r, :, lanes]
        return buf[...]

    all128 = slice(0, ATT_HD)
    lses = [position_order(in_refs[2 * g + 1], all128, scratch[2 * g + 1], ATT_DILS[g]) for g in range(ATT_G)]
    mx = jnp.maximum(jnp.maximum(lses[0], lses[1]), lses[2])
    es = [jnp.exp(l - mx) for l in lses]
    tot = es[0] + es[1] + es[2]
    wgt = [e / tot for e in es]
    for c in range(ATT_H):
        lanes = slice(c * ATT_HD, (c + 1) * ATT_HD)
        acc = None
        for g in range(ATT_G):
            o = position_order(in_refs[2 * g], lanes, scratch[2 * g], ATT_DILS[g])
            term = wgt[g][:, c:c + 1] * o
            acc = term if acc is None else acc + term
        out_ref[:, lanes] = acc.astype(out_ref.dtype)


def _merge_groups(parts, seq, tm=512):
    in_specs, args, scratch = [], [], []
    for g, pair in enumerate(parts):
        dil = ATT_DILS[g]
        for a in pair:
            in_specs.append(pl.BlockSpec((dil, tm // dil, a.shape[2]), lambda i: (0, i, 0)))
            args.append(a)
            scratch.append(pltpu.VMEM((tm, ATT_HD), F32))
    return pl.pallas_call(
        _merge_kernel, grid=(seq // tm,), in_specs=in_specs,
        out_specs=pl.BlockSpec((tm, ATT_GW), lambda i: (i, 0)),
        out_shape=jax.ShapeDtypeStruct((seq, ATT_GW), BF16), scratch_shapes=scratch,
        compiler_params=_cparams(("arbitrary",)), name="attn_merge",
    )(*args)


def _attn_sample_kernel(q_ref, kvn_ref, b0_ref, b1_ref, b2_ref, out_ref):
    scale = ATT_HD ** -0.5
    outs, lses = [], []
    for g, b_ref in enumerate((b0_ref, b1_ref, b2_ref)):
        qg = q_ref[0, g]
        kn = kvn_ref[0, g, 0]
        vn = kvn_ref[0, g, 1]
        kb = b_ref[:, 0]
        vb = b_ref[:, 1]
        s = jnp.sum(kb * qg[None], axis=2, keepdims=True) * scale
        s_new = jnp.sum(kn * qg, axis=1, keepdims=True) * scale
        mx = jnp.maximum(jnp.max(s, axis=0), s_new)
        p = jnp.exp(s - mx[None])
        p_new = jnp.exp(s_new - mx)
        den = jnp.sum(p, axis=0) + p_new
        o = jnp.sum(p * vb, axis=0) + p_new * vn
        outs.append(o / den)
        lses.append(mx + jnp.log(den))
    mxl = jnp.maximum(jnp.maximum(lses[0], lses[1]), lses[2])
    es = [jnp.exp(l - mxl) for l in lses]
    tot = es[0] + es[1] + es[2]
    out_ref[0] = (es[0] / tot) * outs[0] + (es[1] / tot) * outs[1] + (es[2] / tot) * outs[2]


def _attn_sample(q_s, kv_s, caches):
    ns = q_s.shape[0]
    views, specs = [], []
    for g, cbuf in enumerate(caches):
        lb = cbuf.shape[1]
        dil = ATT_DILS[g]
        views.append(cbuf.reshape(ns, lb // dil, dil, 2, ATT_H, ATT_HD))
        specs.append(pl.BlockSpec((None, ATT_STEPS, None, 2, ATT_H, ATT_HD), lambda i: (i, 0, 0, 0, 0, 0)))
    in_specs = [
        pl.BlockSpec((1, ATT_G, ATT_H, ATT_HD), lambda i: (i, 0, 0, 0)),
        pl.BlockSpec((1, ATT_G, 2, ATT_H, ATT_HD), lambda i: (i, 0, 0, 0, 0)),
    ] + specs
    return pl.pallas_call(
        _attn_sample_kernel, grid=(ns,), in_specs=in_specs,
        out_specs=pl.BlockSpec((1, ATT_H, ATT_HD), lambda i: (i, 0, 0)),
        out_shape=jax.ShapeDtypeStruct((ns, ATT_H, ATT_HD), F32),
        compiler_params=_cparams(("arbitrary",)), name="attn_sample",
    )(q_s, kv_s, *views)


KV_SHIFT_BLOCK = 8192


def _kv_shift_kernel(cur_ref, nxt_ref, new_ref, out_ref):
    blk = out_ref.shape[1]
    out_ref[0, :blk - KV_ROW] = cur_ref[0, KV_ROW:]
    last = pl.program_id(1) == pl.num_programs(1) - 1
    out_ref[0, blk - KV_ROW:] = jnp.where(last, new_ref[0], nxt_ref[0])


def _kv_shift(cache, new):
    ns, lb = cache.shape[0], cache.shape[1]
    rows = lb * KV_ROW
    blk = min(KV_SHIFT_BLOCK, rows)
    nb = rows // blk
    per = blk // KV_ROW
    flat = cache.reshape(ns, rows, ATT_HD)
    out = pl.pallas_call(
        _kv_shift_kernel, grid=(ns, nb),
        in_specs=[pl.BlockSpec((1, blk, ATT_HD), lambda i, j: (i, j, 0)),
                  pl.BlockSpec((1, KV_ROW, ATT_HD), lambda i, j: (i, jnp.minimum((j + 1) * per, lb - 1), 0)),
                  pl.BlockSpec((1, KV_ROW, ATT_HD), lambda i, j: (i, 0, 0))],
        out_specs=pl.BlockSpec((1, blk, ATT_HD), lambda i, j: (i, j, 0)),
        out_shape=jax.ShapeDtypeStruct(flat.shape, flat.dtype),
        compiler_params=_cparams(("arbitrary", "arbitrary")), name="kv_shift",
    )(flat, flat, new.reshape(ns, KV_ROW, ATT_HD))
    return out.reshape(cache.shape)


def _top2(y, wr_ref, br_ref):
    rows = y.shape[0]
    lane = lax.broadcasted_iota(I32, (rows, 128), 1)
    logits = jnp.dot(y, wr_ref[...], precision=HIGHEST, preferred_element_type=F32) + br_ref[...]
    logits = jnp.where(lane < N_EXPERTS, logits, NEG_BIG)
    e = jnp.exp(logits - jnp.max(logits, axis=1, keepdims=True))
    probs = e / jnp.sum(e, axis=1, keepdims=True)
    p1 = jnp.max(probs, axis=1, keepdims=True)
    i1 = jnp.min(jnp.where(probs == p1, lane, 128), axis=1, keepdims=True)
    probs2 = jnp.where(lane == i1, -1.0, probs)
    p2 = jnp.max(probs2, axis=1, keepdims=True)
    i2 = jnp.min(jnp.where(probs2 == p2, lane, 128), axis=1, keepdims=True)
    tot = p1 + p2
    return lane, i1, i2, p1 / tot, p2 / tot


def _router_kernel(x_ref, xs_ref, g_ref, wr_ref, br_ref, xn_ref, eid_ref, gate_ref, rank_ref, cnt_ref,
                   xns_ref, gs_ref, carry):
    i = pl.program_id(0)
    tm = x_ref.shape[0]

    @pl.when(i == 0)
    def _():
        carry[...] = jnp.zeros_like(carry)
        ys = _rms(xs_ref[...]) * g_ref[...]
        xns_ref[...] = ys
        lane, i1, i2, g1, g2 = _top2(ys, wr_ref, br_ref)
        gs_ref[...] = jnp.where(lane == i1, g1, jnp.where(lane == i2, g2, 0.0))

    y = _rms(x_ref[...]) * g_ref[...]
    xn_ref[...] = y
    lane, i1, i2, g1, g2 = _top2(y, wr_ref, br_ref)
    sel1 = lane == i1
    sel2 = lane == i2
    onehot = jnp.where(jnp.logical_or(sel1, sel2), 1.0, 0.0)
    rr = lax.broadcasted_iota(I32, (tm, tm), 0)
    cc = lax.broadcasted_iota(I32, (tm, tm), 1)
    before = (cc < rr).astype(BF16)
    prefix = jnp.dot(before, onehot.astype(BF16), preferred_element_type=F32) + carry[...]
    r1 = jnp.sum(jnp.where(sel1, prefix, 0.0), axis=1, keepdims=True)
    r2 = jnp.sum(jnp.where(sel2, prefix, 0.0), axis=1, keepdims=True)
    carry[...] = carry[...] + jnp.sum(onehot, axis=0, keepdims=True)
    eid_ref[...] = jnp.where(lane == 0, i1, jnp.where(lane == 1, i2, 0))
    gate_ref[...] = jnp.where(lane == 0, g1, jnp.where(lane == 1, g2, 0.0))
    rank_ref[...] = jnp.where(lane == 0, r1, jnp.where(lane == 1, r2, 0.0)).astype(I32)
    cnt_ref[...] = jnp.broadcast_to(carry[...], cnt_ref.shape)


def _router(h, hs, gain, w_router_pad, b_router_pad, tm=ROW_TILE):
    m, d = h.shape
    s = hs.shape[0]
    const = lambda i: (0, 0)
    row_spec = pl.BlockSpec((tm, 128), lambda i: (i, 0))
    return pl.pallas_call(
        _router_kernel, grid=(m // tm,),
        in_specs=[pl.BlockSpec((tm, d), lambda i: (i, 0)), pl.BlockSpec((s, d), const), pl.BlockSpec((1, d), const),
                  pl.BlockSpec((d, 128), const), pl.BlockSpec((1, 128), const)],
        out_specs=[pl.BlockSpec((tm, d), lambda i: (i, 0)), row_spec, row_spec, row_spec,
                   pl.BlockSpec((8, 128), const), pl.BlockSpec((s, d), const), pl.BlockSpec((s, 128), const)],
        out_shape=[jax.ShapeDtypeStruct((m, d), F32), jax.ShapeDtypeStruct((m, 128), I32),
                   jax.ShapeDtypeStruct((m, 128), F32), jax.ShapeDtypeStruct((m, 128), I32),
                   jax.ShapeDtypeStruct((8, 128), F32), jax.ShapeDtypeStruct((s, d), F32),
                   jax.ShapeDtypeStruct((s, 128), F32)],
        scratch_shapes=[pltpu.VMEM((1, 128), F32)],
        compiler_params=_cparams(("arbitrary",)), name="router",
    )(h, hs, gain, w_router_pad, b_router_pad)


def _dispatch_kernel(pos_ref, nv_ref, x_hbm, out_ref, inv, buf, sem, *, n_tok):
    i = pl.program_id(0)
    tg = out_ref.shape[0]

    @pl.when(i == 0)
    def _():
        def clear(s, c):
            inv[s] = 0
            return c

        lax.fori_loop(0, inv.shape[0], clear, 0, unroll=8)

        def fill(t, c):
            inv[pos_ref[2 * t]] = t
            inv[pos_ref[2 * t + 1]] = t
            return c

        lax.fori_loop(0, n_tok, fill, 0, unroll=8)

    def start_gather(tile):
        slot = tile % 2

        def issue(r2, c):
            for j in range(2):
                r = 2 * r2 + j
                pltpu.make_async_copy(x_hbm.at[pl.ds(inv[tile * tg + r], 1)], buf.at[slot, pl.ds(r, 1)],
                                      sem.at[slot]).start(priority=j)
            return c

        lax.fori_loop(0, tg // 2, issue, 0, unroll=4)

    @pl.when(i == 0)
    def _():
        start_gather(i)

    @pl.when(i + 1 < nv_ref[0])
    def _():
        start_gather(i + 1)

    @pl.when(i < nv_ref[0])
    def _():
        slot = i % 2
        pltpu.make_async_copy(x_hbm.at[pl.ds(0, tg)], buf.at[slot], sem.at[slot]).wait()
        out_ref[...] = buf[slot].astype(out_ref.dtype)

    @pl.when(i >= nv_ref[0])
    def _():
        out_ref[...] = jnp.zeros_like(out_ref)


def _dispatch(pos_flat, nvalid, xn, n_tiles, tg):
    n_tok, d = xn.shape
    return pl.pallas_call(
        functools.partial(_dispatch_kernel, n_tok=n_tok),
        grid_spec=pltpu.PrefetchScalarGridSpec(
            num_scalar_prefetch=2, grid=(n_tiles,),
            in_specs=[pl.BlockSpec(memory_space=pl.ANY)],
            out_specs=pl.BlockSpec((tg, d), lambda i, pos, nv: (i, 0)),
            scratch_shapes=[pltpu.SMEM((n_tiles * tg,), I32), pltpu.VMEM((2, tg, d), xn.dtype),
                            pltpu.SemaphoreType.DMA((2,))]),
        out_shape=jax.ShapeDtypeStruct((n_tiles * tg, d), BF16),
        compiler_params=_cparams(("arbitrary",)), name="moe_dispatch",
    )(pos_flat, nvalid, xn)


def _combine_kernel(pos_ref, h_ref, gate_ref, g_ref, hs_ref, gs_ref, ys_s_ref, ys_hbm, out_ref, outs_ref, ybuf, sem):
    i = pl.program_id(0)
    tm = h_ref.shape[0]

    def start_gather(tile):
        slot = tile % 2

        def issue(r, carry):
            for j in range(2):
                pltpu.make_async_copy(ys_hbm.at[pl.ds(pos_ref[2 * (tile * tm + r) + j], 1)],
                                      ybuf.at[slot, j, pl.ds(r, 1)], sem.at[slot]).start(priority=j)
            return carry

        lax.fori_loop(0, tm, issue, 0, unroll=4)

    @pl.when(i == 0)
    def _():
        start_gather(i)

    @pl.when(i + 1 < pl.num_programs(0))
    def _():
        start_gather(i + 1)

    @pl.when(i == 0)
    def _():
        gs = gs_ref[...]
        y = jnp.zeros(hs_ref.shape, F32)
        for e in range(N_EXPERTS):
            y = y + gs[:, e:e + 1] * ys_s_ref[e]
        outs_ref[...] = _rms(hs_ref[...] + y) * g_ref[...]

    slot = i % 2
    for j in range(2):
        pltpu.make_async_copy(ys_hbm.at[pl.ds(0, tm)], ybuf.at[slot, j], sem.at[slot]).wait()
    gate = gate_ref[...]
    y = h_ref[...] + (gate[:, 0:1] * ybuf[slot, 0] + gate[:, 1:2] * ybuf[slot, 1])
    out_ref[...] = _rms(y) * g_ref[...]


def _combine(pos_flat, h, gate, g_final, hs, gs, ys_s, ys, tm=ROW_TILE):
    m, d = h.shape
    s = hs.shape[0]
    c2 = lambda i, pos: (0, 0)
    return pl.pallas_call(
        _combine_kernel,
        grid_spec=pltpu.PrefetchScalarGridSpec(
            num_scalar_prefetch=1, grid=(m // tm,),
            in_specs=[pl.BlockSpec((tm, d), lambda i, pos: (i, 0)),
                      pl.BlockSpec((tm, 128), lambda i, pos: (i, 0)),
                      pl.BlockSpec((1, d), c2),
                      pl.BlockSpec((s, d), c2),
                      pl.BlockSpec((s, 128), c2),
                      pl.BlockSpec((N_EXPERTS, s, d), lambda i, pos: (0, 0, 0)),
                      pl.BlockSpec(memory_space=pl.ANY)],
            out_specs=[pl.BlockSpec((tm, d), lambda i, pos: (i, 0)), pl.BlockSpec((s, d), c2)],
            scratch_shapes=[pltpu.VMEM((2, 2, tm, d), F32), pltpu.SemaphoreType.DMA((2,))]),
        out_shape=[jax.ShapeDtypeStruct((m, d), F32), jax.ShapeDtypeStruct((s, d), F32)],
        compiler_params=_cparams(("arbitrary",)), name="moe_combine",
    )(pos_flat, h, gate, g_final, hs, gs, ys_s, ys)


def _rope_tables(pos):
    half = ATT_HD // 2
    inv = ROPE_THETA ** (-jnp.arange(half, dtype=F32) / half)
    ang = pos.astype(F32)[:, None] * inv[None, :]
    cos, sin = jnp.cos(ang), jnp.sin(ang)
    return jnp.concatenate([cos, cos], axis=1), jnp.concatenate([-sin, sin], axis=1)


def kernel(x_prompt, x_sample, state_mlstm_C, state_mlstm_n, state_mlstm_m, cache_kv_w128, cache_kv_w512, cache_kv_w2048, g_mix, g_ffn, w_ml_in, b_ml_gates, b_ml_o, g_ml_hnorm, w_ml_out, g_kv, w_kv, w_q, w_o, w_ffn_gate, w_ffn_up, w_ffn_down, w_router, b_router, w_exp_gate, w_exp_up, w_exp_down, g_final):
    bp, seq, d = x_prompt.shape
    ns = x_sample.shape[0]
    caches = (cache_kv_w128, cache_kv_w512, cache_kv_w2048)
    assert bp == 1 and x_sample.shape[1] == 1 and d == D_MODEL and ns % 8 == 0
    assert seq % (ATT_STEPS * max(ATT_DILS)) == 0 and seq % 1024 == 0
    assert all(c.shape[1] == w for c, w in zip(caches, ATT_WINDOWS))
    tm = 512

    h0 = x_prompt.reshape(seq, d)
    h0_s = x_sample.reshape(ns, d)

    w_gates = lax.slice_in_dim(w_ml_in, ML_MAIN, ML_MAIN + 2 * ML_HEADS, axis=2)[0]
    w_gates_pad = jnp.zeros((d, 256), F32).at[:, 0:ML_HEADS].set(w_gates[:, :ML_HEADS])
    w_gates_pad = w_gates_pad.at[:, 128:128 + ML_HEADS].set(w_gates[:, ML_HEADS:])
    bi = jnp.zeros((1, 128), F32).at[0, :ML_HEADS].set(b_ml_gates[0, :ML_HEADS])
    bf = jnp.zeros((1, 128), F32).at[0, :ML_HEADS].set(b_ml_gates[0, ML_HEADS:])
    (xn0, gates), (xn0_s, gates_s) = _rmsnorm(h0, h0_s, g_mix[0:1], proj=w_gates_pad)
    k_scale = jnp.concatenate([jnp.ones((1, ML_NQK), F32), jnp.full((1, ML_NQK), ML_DQK ** -0.5, F32),
                               jnp.ones((1, ML_NV + D_MODEL), F32)], axis=1)
    z, z_s = _matmul(xn0, [w_ml_in], ML_MAIN, BF16, xn0_s[None], tm=1024, tn=1024, col_scale=k_scale, name="ml_in")
    bo = b_ml_o[0:1]
    ghn = g_ml_hnorm[0:1]
    hg, p_c, p_n, p_m = _mlstm_prompt(z[0], gates, bi, bf, bo, ghn)
    m0 = jnp.zeros((ns, 1, 128), F32).at[:, 0, :ML_HEADS].set(state_mlstm_m[0])
    hg_s, s_c, s_n, s_m = _mlstm_sample(z_s[0].reshape(ns, 1, ML_MAIN), gates_s.reshape(ns, 1, 256),
                                        state_mlstm_C[0], state_mlstm_n[0], m0, bi, bf, bo, ghn)
    h1, h1_s = _matmul(hg, [w_ml_out], d, F32, hg_s[None], tm=tm, tn=1024, res=h0, res_s=h0_s, name="ml_out")
    h1, h1_s = h1[0], h1_s[0]

    (xf0,), (xf0_s,) = _rmsnorm(h1, h1_s, g_ffn[0:1])
    ffn_dense = w_ffn_gate.shape[2]
    hid, hid_s = _matmul(xf0, [w_ffn_gate, w_ffn_up], ffn_dense, BF16, xf0_s[None], tm=1024, tn=512, name="ffn_up")
    h2, h2_s = _matmul(hid[0], [w_ffn_down], d, F32, hid_s, tm=tm, tn=512, res=h1, res_s=h1_s, name="ffn_down")
    h2, h2_s = h2[0], h2_s[0]

    (xq, xkv), (xq_s, xkv_s) = _rmsnorm(h2, h2_s, jnp.stack([g_mix[1], g_kv]))
    cos, sin = _rope_tables(jnp.arange(seq))
    cos_s, sin_s = _rope_tables(jnp.full((ns,), PAST_LEN, I32))
    rope_args = dict(cos=cos, sin=sin, cos_s=cos_s, sin_s=sin_s)
    parts, kv_nat, kv_new, q_new = [], [], [], []
    tail = min(max(ATT_WINDOWS), seq)
    for g in range(ATT_G):
        dil = ATT_DILS[g]
        kvd, kvn, kv_s = _matmul(xkv, [w_kv[None]], 2 * ATT_GW, BF16, xkv_s[None], tm=tm, tn=ATT_GW,
                                 col_off=2 * g, rope="even", dil=dil, natural_tail=tail, name=f"kv_proj_g{g}",
                                 **rope_args)
        qd, q_s = _matmul(xq, [w_q], ATT_GW, BF16, xq_s[None], tm=tm, tn=ATT_GW, col_off=g, rope="all",
                          dil=dil, name=f"q_proj_g{g}", **rope_args)
        parts.append(_attn_prompt(qd, kvd, g))
        kv_nat.append(kvn)
        kv_new.append(kv_s[0])
        q_new.append(q_s[0])
    att = _merge_groups(parts, seq)
    att_s = _attn_sample(jnp.stack(q_new, axis=1).reshape(ns, ATT_G, ATT_H, ATT_HD),
                         jnp.stack(kv_new, axis=1).reshape(ns, ATT_G, 2, ATT_H, ATT_HD), caches)
    h3, h3_s = _matmul(att, [w_o], d, F32, att_s.reshape(1, ns, ATT_GW), tm=tm, tn=1024, res=h2, res_s=h2_s,
                       name="attn_out")
    h3, h3_s = h3[0], h3_s[0]

    wr_pad = jnp.zeros((d, 128), F32).at[:, :N_EXPERTS].set(w_router[0])
    br_pad = jnp.zeros((1, 128), F32).at[0, :N_EXPERTS].set(b_router[0])
    xn2, eid, gate, rank, cnt, xn2_s, gates_moe_s = _router(h3, h3_s, g_ffn[1:2], wr_pad, br_pad)
    tg = MOE_TILE
    n_tiles = -(-(2 * seq + N_EXPERTS * (tg - 1)) // tg)
    counts = cnt[0, :N_EXPERTS].astype(I32)
    padded = jnp.maximum((counts + tg - 1) // tg, 1) * tg
    gend = jnp.cumsum(padded)
    gstart = gend - padded
    pos = (gstart[eid[:, :2]] + rank[:, :2]).astype(I32).reshape(-1)
    nvalid = (gend[-1] // tg).astype(I32).reshape(1)
    xs = _dispatch(pos, nvalid, xn2, n_tiles, tg)
    t0 = (gstart // tg).astype(I32)
    tcnt = (padded // tg).astype(I32)
    hs, hs_s, sbuf2 = _moe_matmul(xs, [w_exp_gate[0], w_exp_up[0]], BF16, xn2_s[None], t0, tcnt, nvalid, tm=tg,
                                  tn=1024, xs_per_expert=False, name="moe_up", window=(caches[2], kv_new[2]))
    ys, ys_s, sbuf1 = _moe_matmul(hs, [w_exp_down[0]], F32, hs_s, t0, tcnt, nvalid, tm=tg, tn=512,
                                  xs_per_expert=True, name="moe_down", window=(caches[1], kv_new[1]))
    s_bufs = [_kv_shift(caches[0], kv_new[0]), sbuf1, sbuf2]
    y_p, y_s = _combine(pos, h3, gate, g_final.reshape(1, d), h3_s, gates_moe_s, ys_s, ys)

    p_bufs = []
    for g in range(ATT_G):
        keep = min(ATT_WINDOWS[g], seq)
        p_bufs.append(kv_nat[g][tail - keep:].reshape(1, keep, 2, ATT_H, ATT_HD))
    return (y_p.reshape(1, seq, d), y_s.reshape(ns, 1, d),
            p_c[None, None], p_n[None, None], p_m[:, :ML_HEADS][None],
            s_c[None], s_n[None], s_m[:, 0, :ML_HEADS][None],
            p_bufs[0], p_bufs[1], p_bufs[2], s_bufs[0], s_bufs[1], s_bufs[2])
```

```python
import functools

import jax
import jax.numpy as jnp
from jax import lax
from jax.experimental import pallas as pl
from jax.experimental.pallas import tpu as pltpu

F32 = jnp.float32
BF16 = jnp.bfloat16
I32 = jnp.int32
HIGHEST = lax.Precision.HIGHEST

D_MODEL = 2048
ML_HEADS = 8
ML_DQK = 128
ML_DV = 256
ML_NQK = ML_HEADS * ML_DQK
ML_NV = ML_HEADS * ML_DV
ML_MAIN = 2 * ML_NQK + ML_NV + D_MODEL
ML_CHUNK = 128
ATT_HD = 128
ATT_H = 8
ATT_G = 3
ATT_WINDOWS = (128, 512, 2048)
ATT_DILS = (1, 4, 16)
ATT_STEPS = 128
ATT_GW = ATT_H * ATT_HD
ROPE_THETA = 10000.0
PAST_LEN = 8192
N_EXPERTS = 8
RMS_EPS = 1e-6
NEG_BIG = -1e30

VMEM_LIMIT_BYTES = 58 * 1024 * 1024
ROW_TILE = 512
MOE_TILE = 256


def _cparams(sem):
    return pltpu.CompilerParams(dimension_semantics=sem, vmem_limit_bytes=VMEM_LIMIT_BYTES)


def _rms(x):
    return x * lax.rsqrt(jnp.mean(x * x, axis=-1, keepdims=True) + RMS_EPS)


def _norm_kernel(x_ref, xs_ref, g_ref, *refs, n_out, has_proj):
    ins = 1 if has_proj else 0
    n_each = n_out + ins
    main = refs[ins:ins + n_each]
    side = refs[ins + n_each:ins + 2 * n_each]

    def emit(x, outs):
        y = _rms(x)
        for i in range(n_out):
            outs[i][...] = (y * g_ref[i:i + 1, :]).astype(outs[i].dtype)
        if has_proj:
            outs[n_out][...] = jnp.dot(y * g_ref[0:1, :], refs[0][...], precision=HIGHEST,
                                       preferred_element_type=F32)

    emit(x_ref[...], main)

    @pl.when(pl.program_id(0) == 0)
    def _():
        emit(xs_ref[...], side)


def _rmsnorm(x, xs, gains, proj=None, tm=ROW_TILE):
    m, d = x.shape
    s = xs.shape[0]
    n_out = gains.shape[0]
    const = lambda i: (0, 0)
    in_specs = [pl.BlockSpec((tm, d), lambda i: (i, 0)), pl.BlockSpec((s, d), const),
                pl.BlockSpec((n_out, d), const)]
    args = [x, xs, gains]
    main_shape = [jax.ShapeDtypeStruct((m, d), BF16)] * n_out
    main_specs = [pl.BlockSpec((tm, d), lambda i: (i, 0))] * n_out
    side_shape = [jax.ShapeDtypeStruct((s, d), F32)] * n_out
    side_specs = [pl.BlockSpec((s, d), const)] * n_out
    if proj is not None:
        p = proj.shape[1]
        in_specs.append(pl.BlockSpec((d, p), const))
        args.append(proj)
        main_shape.append(jax.ShapeDtypeStruct((m, p), F32))
        main_specs.append(pl.BlockSpec((tm, p), lambda i: (i, 0)))
        side_shape.append(jax.ShapeDtypeStruct((s, p), F32))
        side_specs.append(pl.BlockSpec((s, p), const))
    outs = pl.pallas_call(
        functools.partial(_norm_kernel, n_out=n_out, has_proj=proj is not None),
        grid=(m // tm,), in_specs=in_specs, out_specs=main_specs + side_specs,
        out_shape=main_shape + side_shape,
        compiler_params=_cparams(("arbitrary",)), name="rmsnorm",
    )(*args)
    k = len(main_shape)
    return outs[:k], outs[k:]


def _rope_heads(acc, cos, sin):
    outs = []
    for h in range(acc.shape[1] // ATT_HD):
        a = acc[:, h * ATT_HD:(h + 1) * ATT_HD]
        outs.append(a * cos + pltpu.roll(a, ATT_HD // 2, 1) * sin)
    return jnp.concatenate(outs, axis=1)


def _mm_kernel(te_ref, nv_ref, x_ref, *refs, n_w, has_scale, has_res, rope, dil, nat_first, shift_chunks):
    n = pl.program_id(0)
    m = pl.program_id(1)
    it = iter(refs)
    w_refs = [next(it) for _ in range(n_w)]
    scale_ref = next(it) if has_scale else None
    res_ref = next(it) if has_res else None
    cos_ref, sin_ref = (next(it), next(it)) if rope else (None, None)
    xs_ref = next(it)
    res_s_ref = next(it) if has_res else None
    cos_s_ref, sin_s_ref = (next(it), next(it)) if rope else (None, None)
    shift_src = next(it) if shift_chunks else None
    o_ref = next(it)
    nat_ref = next(it) if nat_first is not None else None
    os_ref = next(it)
    shift_dst = next(it) if shift_chunks else None
    wb_refs = [next(it) for _ in range(n_w)]
    deint = next(it) if dil > 1 else None

    shift = None
    if shift_chunks:
        stage, ring_in, ring_out, next_chunk = next(it), next(it), next(it), next(it)
        shift = _WindowShift(shift_src, None, shift_dst, stage, ring_in, ring_out, None, next_chunk,
                             first=0, last=shift_chunks)
        pl.when(jnp.logical_and(n == 0, m == 0))(shift.begin)

    def finish(acc, up, res, cos, sin, store):
        if n_w == 2:
            acc = (acc * jax.nn.sigmoid(acc)) * up
        if has_scale:
            acc = acc * scale_ref[...]
        if has_res:
            acc = acc + res
        if rope == "all":
            store(_rope_heads(acc, cos, sin))
        elif rope == "even":
            @pl.when(n % 2 == 0)
            def _():
                store(_rope_heads(acc, cos, sin))

            @pl.when(n % 2 == 1)
            def _():
                store(acc)
        else:
            store(acc)

    def store_side(val):
        os_ref[0] = val

    def store_main(val):
        if nat_first is not None:
            @pl.when(m >= nat_first)
            def _():
                nat_ref[...] = val
        if dil == 1:
            o_ref[0] = val.astype(o_ref.dtype)
        else:
            rows = deint.shape[1] // dil
            for c in range(deint.shape[0]):
                lanes = slice(c * 128, (c + 1) * 128)
                deint[c] = val[:, lanes]
                for r in range(dil):
                    o_ref[r, :, lanes] = deint[c, pl.ds(r, rows, stride=dil), :].astype(o_ref.dtype)

    prev = jnp.maximum(m - 1, 0)
    new_weights = jnp.logical_or(m == 0, te_ref[m] != te_ref[prev])

    @pl.when(new_weights)
    def _():
        for w_ref, wb_ref in zip(w_refs, wb_refs):
            wb_ref[...] = w_ref[0].astype(BF16)
        xs = xs_ref[0]
        s_rows = xs.shape[0]
        xh = xs.astype(BF16)
        xl = (xs - xh.astype(F32)).astype(BF16)
        x_hl = jnp.concatenate([xh, xl], axis=0)

        def side_product(i):
            wh = wb_refs[i][...]
            wl = (w_refs[i][0] - wh.astype(F32)).astype(BF16)
            both = jnp.dot(x_hl, wh, preferred_element_type=F32)
            return both[:s_rows] + both[s_rows:] + jnp.dot(xh, wl, preferred_element_type=F32)

        acc = side_product(0)
        up = side_product(1) if n_w == 2 else None
        finish(acc, up, res_s_ref[...] if has_res else None,
               cos_s_ref[...] if rope else None, sin_s_ref[...] if rope else None, store_side)

    @pl.when(m >= nv_ref[0])
    def _():
        o_ref[...] = jnp.zeros_like(o_ref)

    @pl.when(m < nv_ref[0])
    def _():
        xb = x_ref[...].astype(BF16)
        acc = jnp.dot(xb, wb_refs[0][...], preferred_element_type=F32)
        up = jnp.dot(xb, wb_refs[1][...], preferred_element_type=F32) if n_w == 2 else None
        finish(acc, up, res_ref[...] if has_res else None,
               cos_ref[...] if rope else None, sin_ref[...] if rope else None, store_main)

    if shift:
        for _ in range(SHIFT_PER_STEP):
            shift.advance()
        last_step = jnp.logical_and(n == pl.num_programs(0) - 1, m == pl.num_programs(1) - 1)
        pl.when(last_step)(shift.finish)


def _matmul(x, ws, n_cols, out_dtype, xs, *, tm, tn, col_off=0, te=None, nvalid=None, xs_per_expert=False,
            col_scale=None, res=None, res_s=None, rope=None, cos=None, sin=None, cos_s=None, sin_s=None,
            dil=1, natural_tail=0, shift_cache=None, name="matmul"):
    m, k = x.shape
    s = xs.shape[1]
    n_m = m // tm
    n_n = n_cols // tn
    n_e = ws[0].shape[0]
    if te is None:
        te = jnp.zeros((n_m,), I32)
        nvalid = jnp.full((1,), n_m, I32)

    def row(mi, nv):
        return jnp.minimum(mi, nv[0] - 1)

    def exp(mi, te, nv):
        return te[row(mi, nv)]

    in_specs = [pl.BlockSpec((tm, k), lambda n, mi, te, nv: (row(mi, nv), 0))]
    args = [x]
    for w in ws:
        in_specs.append(pl.BlockSpec((1, k, tn), lambda n, mi, te, nv: (exp(mi, te, nv), 0, n + col_off)))
        args.append(w)
    if col_scale is not None:
        in_specs.append(pl.BlockSpec((1, tn), lambda n, mi, te, nv: (0, n)))
        args.append(col_scale)
    if res is not None:
        in_specs.append(pl.BlockSpec((tm, tn), lambda n, mi, te, nv: (row(mi, nv), n)))
        args.append(res)
    if rope is not None:
        for t in (cos, sin):
            in_specs.append(pl.BlockSpec((tm, ATT_HD), lambda n, mi, te, nv: (row(mi, nv), 0)))
            args.append(t)
    if xs_per_expert:
        in_specs.append(pl.BlockSpec((1, s, k), lambda n, mi, te, nv: (exp(mi, te, nv), 0, 0)))
    else:
        in_specs.append(pl.BlockSpec((1, s, k), lambda n, mi, te, nv: (0, 0, 0)))
    args.append(xs)
    if res is not None:
        in_specs.append(pl.BlockSpec((s, tn), lambda n, mi, te, nv: (0, n)))
        args.append(res_s)
    if rope is not None:
        for t in (cos_s, sin_s):
            in_specs.append(pl.BlockSpec((s, ATT_HD), lambda n, mi, te, nv: (0, 0)))
            args.append(t)

    out_shape = [jax.ShapeDtypeStruct((dil, m // dil, n_cols), out_dtype)]
    out_specs = [pl.BlockSpec((dil, tm // dil, tn), lambda n, mi, te, nv: (0, mi, n))]
    nat_first = None
    if natural_tail:
        nat_first = n_m - natural_tail // tm
        out_shape.append(jax.ShapeDtypeStruct((natural_tail, n_cols), F32))
        out_specs.append(pl.BlockSpec((tm, tn), lambda n, mi, te, nv: (jnp.maximum(mi - nat_first, 0), n)))
    out_shape.append(jax.ShapeDtypeStruct((n_e, s, n_cols), F32))
    out_specs.append(pl.BlockSpec((1, s, tn), lambda n, mi, te, nv: (exp(mi, te, nv), 0, n)))
    scratch = [pltpu.VMEM((k, tn), BF16) for _ in ws]
    if dil > 1:
        scratch.append(pltpu.VMEM((tn // 128, tm, 128), F32))
    shift_chunks = 0
    if shift_cache is not None:
        ns, lb = shift_cache.shape[0], shift_cache.shape[1]
        flat = shift_cache.reshape(ns, lb * KV_ROW, ATT_HD)
        chunk = _shift_chunk_rows(lb) * KV_ROW
        shift_chunks = n_n * n_m * SHIFT_PER_STEP
        assert RING_SLOTS <= shift_chunks <= (ns // SHIFT_SEQS) * ((lb - 1) * KV_ROW // chunk)
        any_spec = pl.BlockSpec(memory_space=pl.ANY)
        in_specs.append(any_spec)
        args.append(flat)
        out_shape.append(jax.ShapeDtypeStruct(flat.shape, flat.dtype))
        out_specs.append(any_spec)
        scratch += [pltpu.VMEM((RING_SLOTS, SHIFT_SEQS, chunk, ATT_HD), F32),
                    pltpu.SemaphoreType.DMA((RING_SLOTS,)), pltpu.SemaphoreType.DMA((RING_SLOTS,)),
                    pltpu.SMEM((1,), I32)]
    kern = functools.partial(_mm_kernel, n_w=len(ws), has_scale=col_scale is not None, has_res=res is not None,
                             rope=rope, dil=dil, nat_first=nat_first, shift_chunks=shift_chunks)
    return pl.pallas_call(
        kern,
        grid_spec=pltpu.PrefetchScalarGridSpec(
            num_scalar_prefetch=2, grid=(n_n, n_m), in_specs=in_specs, out_specs=out_specs,
            scratch_shapes=scratch),
        out_shape=out_shape,
        compiler_params=_cparams(("arbitrary", "arbitrary")), name=name,
    )(te, nvalid, *args)


KV_ROW = 2 * ATT_H
RING_SLOTS = 3
SHIFT_PER_STEP = 2
SHIFT_SEQS = 2
SHIFT_MAX_ROWS = 128


def _shift_chunk_rows(window):
    moved = window - 1
    return max(d for d in range(1, SHIFT_MAX_ROWS + 1) if moved % d == 0)


class _WindowShift:
    def __init__(self, buf_hbm, new_hbm, out_hbm, stage, sem_in, sem_out, sem_new, next_chunk, first=0, last=None):
        self.buf, self.new, self.out, self.stage = buf_hbm, new_hbm, out_hbm, stage
        self.sem_in, self.sem_out, self.sem_new, self.next_chunk = sem_in, sem_out, sem_new, next_chunk
        ns, self.rows = buf_hbm.shape[0], buf_hbm.shape[1]
        self.chunk = stage.shape[2]
        self.per_group = (self.rows - KV_ROW) // self.chunk
        self.first = first
        self.total = (ns // SHIFT_SEQS) * self.per_group if last is None else last

    def _load(self, k, slot):
        seqs = pl.ds((k // self.per_group) * SHIFT_SEQS, SHIFT_SEQS)
        src = self.buf.at[seqs, pl.ds(KV_ROW + (k % self.per_group) * self.chunk, self.chunk)]
        return pltpu.make_async_copy(src, self.stage.at[slot], self.sem_in.at[slot])

    def _store(self, k, slot):
        seqs = pl.ds((k // self.per_group) * SHIFT_SEQS, SHIFT_SEQS)
        dst = self.out.at[seqs, pl.ds((k % self.per_group) * self.chunk, self.chunk)]
        return pltpu.make_async_copy(self.stage.at[slot], dst, self.sem_out.at[slot])

    def _append(self):
        return pltpu.make_async_copy(self.new, self.out.at[:, pl.ds(self.rows - KV_ROW, KV_ROW)], self.sem_new.at[0])

    def begin(self):
        self.next_chunk[0] = self.first
        for k in range(self.first, self.first + RING_SLOTS - 1):
            self._load(k, k % RING_SLOTS).start(priority=1)
        if self.new is not None:
            self._append().start()

    def advance(self):
        k = self.next_chunk[0]

        @pl.when(k < self.total)
        def _():
            self._load(k, k % RING_SLOTS).wait()

            @pl.when(k > self.first)
            def _():
                self._store(k - 1, (k - 1) % RING_SLOTS).wait()

            self._store(k, k % RING_SLOTS).start()
            ahead = k + RING_SLOTS - 1

            @pl.when(ahead < self.total)
            def _():
                self._load(ahead, ahead % RING_SLOTS).start(priority=1)

            self.next_chunk[0] = k + 1

    def finish(self):
        def rest(i, carry):
            self.advance()
            return carry

        lax.fori_loop(self.next_chunk[0], self.total, rest, 0)
        self._store(self.total - 1, (self.total - 1) % RING_SLOTS).wait()
        if self.new is not None:
            self._append().wait()


def _moe_mm_kernel(t0_ref, cnt_ref, nv_ref, x_hbm, *refs, n_w, tm, n_tiles, n_bg, shift_first, shift_started):
    n = pl.program_id(0)
    e = pl.program_id(1)
    n_e = pl.num_programs(1)
    it = iter(refs)
    w_refs = [next(it) for _ in range(n_w)]
    xs_ref = next(it)
    bg_in = [next(it) for _ in range(2 * n_bg)]
    if shift_started:
        next(it)
    o_hbm, os_ref = next(it), next(it)
    bg_out = [next(it) for _ in range(n_bg)]
    wb_refs = [next(it) for _ in range(n_w)]
    xbuf, obuf, sem_in, sem_out = next(it), next(it), next(it), next(it)
    tn = os_ref.shape[2]
    t0 = t0_ref[e]
    cnt = cnt_ref[e]
    first_step = jnp.logical_and(n == 0, e == 0)
    last_step = jnp.logical_and(n == pl.num_programs(0) - 1, e == n_e - 1)

    shift = None
    if n_bg:
        shift = _WindowShift(bg_in[0], bg_in[1], bg_out[0], *[next(it) for _ in range(5)], first=shift_first)
    if shift:
        pl.when(first_step)(shift.begin)

    def x_copy(tile, slot):
        return pltpu.make_async_copy(x_hbm.at[pl.ds(tile * tm, tm)], xbuf.at[slot], sem_in.at[slot])

    def o_copy(tile, slot):
        return pltpu.make_async_copy(obuf.at[slot], o_hbm.at[pl.ds(tile * tm, tm), pl.ds(n * tn, tn)],
                                     sem_out.at[slot])

    def product(xb):
        acc = jnp.dot(xb, wb_refs[0][...], preferred_element_type=F32)
        if n_w == 2:
            acc = (acc * jax.nn.sigmoid(acc)) * jnp.dot(xb, wb_refs[1][...], preferred_element_type=F32)
        return acc

    def start_first_tiles(first_tile, n_tiles_here):
        for j in range(RING_SLOTS - 1):
            @pl.when(j < n_tiles_here)
            def _():
                x_copy(first_tile + j, j).start(priority=1)

    @pl.when(first_step)
    def _():
        start_first_tiles(t0, cnt)

    for w_ref, wb_ref in zip(w_refs, wb_refs):
        wb_ref[...] = w_ref[0].astype(BF16)
    os_ref[0] = product(xs_ref[0].astype(BF16))

    def body(t, carry):
        slot = t % 2
        ahead = t + RING_SLOTS - 1

        @pl.when(ahead < cnt)
        def _():
            x_copy(t0 + ahead, ahead % RING_SLOTS).start(priority=1)

        x_copy(t0 + t, t % RING_SLOTS).wait()

        @pl.when(t >= 2)
        def _():
            o_copy(t0 + t - 2, slot).wait()

        obuf[slot] = product(xbuf[t % RING_SLOTS]).astype(obuf.dtype)
        o_copy(t0 + t, slot).start()
        if shift:
            shift.advance()
        return carry

    lax.fori_loop(0, cnt, body, 0)

    @pl.when(jnp.logical_not(last_step))
    def _():
        e_next = jnp.where(e == n_e - 1, 0, e + 1)
        start_first_tiles(t0_ref[e_next], cnt_ref[e_next])

    @pl.when(cnt >= 2)
    def _():
        o_copy(t0 + cnt - 2, cnt % 2).wait()

    o_copy(t0 + cnt - 1, (cnt - 1) % 2).wait()

    @pl.when(e == n_e - 1)
    def _():
        obuf[0] = jnp.zeros(obuf.shape[1:], obuf.dtype)

        def fill(tile, carry):
            o_copy(tile, 0).start()
            o_copy(tile, 0).wait()
            return carry

        lax.fori_loop(nv_ref[0], n_tiles, fill, 0)

    if shift:
        pl.when(last_step)(shift.finish)


def _moe_matmul(x, ws, out_dtype, xs, t0, cnt, nvalid, *, tm, tn, xs_per_expert, name, window=None,
                started=None):
    r, k = x.shape
    n_e, _, n_cols = ws[0].shape
    s = xs.shape[1]
    n_w = len(ws)
    windows = []
    if window is not None:
        cache, new = window
        ns, lb = cache.shape[0], cache.shape[1]
        assert ns % SHIFT_SEQS == 0
        windows = [(cache.reshape(ns, lb * KV_ROW, ATT_HD), new.reshape(ns, KV_ROW, ATT_HD))]
        chunk = _shift_chunk_rows(lb) * KV_ROW
    n_bg = len(windows)
    extra, aliases, shift_first = [], {}, 0
    if started is not None:
        extra, shift_first = [started[0]], started[1]
        aliases = {3 + 1 + n_w + 1 + 2: 2}
    any_spec = pl.BlockSpec(memory_space=pl.ANY)
    in_specs = [any_spec]
    in_specs += [pl.BlockSpec((1, k, tn), lambda n, e, *_: (e, 0, n)) for _ in ws]
    if xs_per_expert:
        in_specs.append(pl.BlockSpec((1, s, k), lambda n, e, *_: (e, 0, 0)))
    else:
        in_specs.append(pl.BlockSpec((1, s, k), lambda n, e, *_: (0, 0, 0)))
    in_specs += [any_spec] * (2 * n_bg + len(extra))
    scratch = [pltpu.VMEM((k, tn), BF16) for _ in ws]
    scratch += [pltpu.VMEM((RING_SLOTS, tm, k), x.dtype), pltpu.VMEM((2, tm, tn), out_dtype),
                pltpu.SemaphoreType.DMA((RING_SLOTS,)), pltpu.SemaphoreType.DMA((2,))]
    if n_bg:
        scratch += [pltpu.VMEM((RING_SLOTS, SHIFT_SEQS, chunk, ATT_HD), F32),
                    pltpu.SemaphoreType.DMA((RING_SLOTS,)), pltpu.SemaphoreType.DMA((RING_SLOTS,)),
                    pltpu.SemaphoreType.DMA((1,)), pltpu.SMEM((1,), I32)]
    out_shape = [jax.ShapeDtypeStruct((r, n_cols), out_dtype), jax.ShapeDtypeStruct((n_e, s, n_cols), F32)]
    out_shape += [jax.ShapeDtypeStruct(buf.shape, buf.dtype) for buf, _ in windows]
    outs = pl.pallas_call(
        functools.partial(_moe_mm_kernel, n_w=n_w, tm=tm, n_tiles=r // tm, n_bg=n_bg, shift_first=shift_first,
                          shift_started=started is not None),
        grid_spec=pltpu.PrefetchScalarGridSpec(
            num_scalar_prefetch=3, grid=(n_cols // tn, n_e), in_specs=in_specs,
            out_specs=[any_spec, pl.BlockSpec((1, s, tn), lambda n, e, *_: (e, 0, n))] + [any_spec] * n_bg,
            scratch_shapes=scratch),
        out_shape=out_shape, input_output_aliases=aliases,
        compiler_params=_cparams(("arbitrary", "arbitrary")), name=name,
    )(t0, cnt, nvalid, x, *ws, xs, *[a for pair in windows for a in pair], *extra)
    if window is not None:
        return outs[0], outs[1], outs[2].reshape(window[0].shape)
    return outs


def _log_sigmoid(x):
    return jnp.minimum(x, 0.0) - jnp.log1p(jnp.exp(-jnp.abs(x)))


def _mlstm_prompt_kernel(q_ref, k_ref, v_ref, o_ref, gi_ref, gf_ref, bi_ref, bf_ref, bo_ref, ghn_ref,
                         h_ref, c_out_ref, n_out_ref, m_out_ref, ct_s, n_s, m_s):
    c = pl.program_id(0)
    L = ML_CHUNK

    @pl.when(c == 0)
    def _():
        ct_s[...] = jnp.zeros_like(ct_s)
        n_s[...] = jnp.zeros_like(n_s)
        m_s[...] = jnp.zeros_like(m_s)

    ig = gi_ref[...] + bi_ref[...]
    lf = _log_sigmoid(gf_ref[...] + bf_ref[...])
    r = lax.broadcasted_iota(I32, (L, L), 0)
    s = lax.broadcasted_iota(I32, (L, L), 1)
    causal = r >= s
    tril = causal.astype(F32)
    b = jnp.dot(tril, lf, precision=HIGHEST, preferred_element_type=F32)
    b_t = b.T
    ig_t = ig.T
    m_all = m_s[...]
    m_new_all = m_all
    lane = lax.broadcasted_iota(I32, (1, 128), 1)

    H = range(ML_HEADS)
    q = [q_ref[:, h * ML_DQK:(h + 1) * ML_DQK] for h in H]
    k = [k_ref[:, h * ML_DQK:(h + 1) * ML_DQK] for h in H]
    v = [v_ref[:, h * ML_DV:(h + 1) * ML_DV] for h in H]
    ct = [ct_s[h] for h in H]
    n_old = [n_s[h:h + 1, :] for h in H]
    bc = [b[:, h:h + 1] for h in H]
    m_old = [m_all[:, h:h + 1] for h in H]

    sc = [lax.dot_general(q[h], k[h], (((1,), (1,)), ((), ())), preferred_element_type=F32) for h in H]
    qc = [jnp.dot(q[h], ct[h].astype(BF16), preferred_element_type=F32) for h in H]

    logd = [jnp.where(causal, bc[h] - b_t[h:h + 1, :] + ig_t[h:h + 1, :], -jnp.inf) for h in H]
    inter = [bc[h] + m_old[h] for h in H]
    mt = [jnp.maximum(inter[h], jnp.max(logd[h], axis=1, keepdims=True)) for h in H]
    w_inter = [jnp.exp(inter[h] - mt[h]) for h in H]
    m_new = [mt[h][L - 1:L, :] for h in H]
    b_last = [bc[h][L - 1:L, :] for h in H]
    decay = [jnp.exp(b_last[h] + m_old[h] - m_new[h]) for h in H]
    wj = [jnp.exp(b_last[h] - bc[h] + ig[:, h:h + 1] - m_new[h]) for h in H]

    a = [sc[h] * jnp.exp(logd[h] - mt[h]) for h in H]
    qn = [jnp.sum(q[h].astype(F32) * n_old[h], axis=1, keepdims=True) for h in H]
    den = [jnp.sum(a[h], axis=1, keepdims=True) + w_inter[h] * qn[h] for h in H]
    num = [jnp.dot(a[h].astype(BF16), v[h], preferred_element_type=F32) + w_inter[h] * qc[h] for h in H]

    hh = [num[h] / jnp.maximum(jnp.abs(den[h]), jnp.exp(-mt[h])) for h in H]
    for h in H:
        cols = slice(h * ML_DV, (h + 1) * ML_DV)
        og = jax.nn.sigmoid(o_ref[:, cols].astype(F32) + bo_ref[:, cols])
        h_ref[:, cols] = (_rms(hh[h]) * ghn_ref[:, cols] * og).astype(h_ref.dtype)

    kf = [k[h].astype(F32) * wj[h] for h in H]
    for h in H:
        upd = lax.dot_general(kf[h].astype(BF16), v[h], (((0,), (0,)), ((), ())), preferred_element_type=F32)
        ct_s[h] = decay[h] * ct[h] + upd
        n_s[h:h + 1, :] = decay[h] * n_old[h] + jnp.sum(kf[h], axis=0, keepdims=True)
        m_new_all = jnp.where(lane == h, m_new[h], m_new_all)

    m_s[...] = m_new_all

    @pl.when(c == pl.num_programs(0) - 1)
    def _():
        for h in range(ML_HEADS):
            c_out_ref[h] = ct_s[h].T
        n_out_ref[...] = n_s[...]
        m_out_ref[...] = m_s[...]


def _mlstm_prompt(z, gates, bi, bf, bo, ghn):
    seq = z.shape[0]
    nc = seq // ML_CHUNK
    L = ML_CHUNK
    const2 = lambda c: (0, 0)
    in_specs = [
        pl.BlockSpec((L, ML_NQK), lambda c: (c, 0)),
        pl.BlockSpec((L, ML_NQK), lambda c: (c, 1)),
        pl.BlockSpec((L, ML_NV), lambda c: (c, 1)),
        pl.BlockSpec((L, D_MODEL), lambda c: (c, 2)),
        pl.BlockSpec((L, 128), lambda c: (c, 0)),
        pl.BlockSpec((L, 128), lambda c: (c, 1)),
        pl.BlockSpec((1, 128), const2),
        pl.BlockSpec((1, 128), const2),
        pl.BlockSpec((1, D_MODEL), const2),
        pl.BlockSpec((1, ML_NV), const2),
    ]
    out_shape = [
        jax.ShapeDtypeStruct((seq, ML_NV), BF16),
        jax.ShapeDtypeStruct((ML_HEADS, ML_DV, ML_DQK), F32),
        jax.ShapeDtypeStruct((ML_HEADS, ML_DQK), F32),
        jax.ShapeDtypeStruct((1, 128), F32),
    ]
    out_specs = [
        pl.BlockSpec((L, ML_NV), lambda c: (c, 0)),
        pl.BlockSpec((ML_HEADS, ML_DV, ML_DQK), lambda c: (0, 0, 0)),
        pl.BlockSpec((ML_HEADS, ML_DQK), const2),
        pl.BlockSpec((1, 128), const2),
    ]
    return pl.pallas_call(
        _mlstm_prompt_kernel, grid=(nc,), in_specs=in_specs, out_specs=out_specs, out_shape=out_shape,
        scratch_shapes=[pltpu.VMEM((ML_HEADS, ML_DQK, ML_DV), F32), pltpu.VMEM((ML_HEADS, ML_DQK), F32),
                        pltpu.VMEM((1, 128), F32)],
        compiler_params=_cparams(("arbitrary",)), name="mlstm_prompt",
    )(z, z, z, z, gates, gates, bi, bf, bo, ghn)


def _mlstm_sample_kernel(z_ref, g_ref, c_ref, n_ref, m_ref, bi_ref, bf_ref, bo_ref, ghn_ref,
                         h_ref, c_out_ref, n_out_ref, m_out_ref):
    i = pl.program_id(0)
    z = z_ref[0]
    g = g_ref[0]
    ig_all = g[:, 0:128] + bi_ref[...]
    lf_all = _log_sigmoid(g[:, 128:256] + bf_ref[...])
    m_all = m_ref[0]
    mt_all = jnp.maximum(lf_all + m_all, ig_all)
    m_out_ref[0] = mt_all
    outs = []
    for h in range(ML_HEADS):
        q = z[:, h * ML_DQK:(h + 1) * ML_DQK]
        k = z[:, ML_NQK + h * ML_DQK:ML_NQK + (h + 1) * ML_DQK]
        v = z[:, 2 * ML_NQK + h * ML_DV:2 * ML_NQK + (h + 1) * ML_DV]
        ig = ig_all[:, h:h + 1]
        lf = lf_all[:, h:h + 1]
        m0 = m_all[:, h:h + 1]
        mt = mt_all[:, h:h + 1]
        w_inter = jnp.exp(lf + m0 - mt)
        wj = jnp.exp(ig - mt)
        a = jnp.sum(q * k, axis=1, keepdims=True) * wj
        c_h = c_ref[0, h]
        n_h = n_ref[0, h:h + 1, :]
        q8 = jnp.broadcast_to(q, (8, ML_DQK))
        cq = lax.dot_general(q8, c_h, (((1,), (1,)), ((), ())), precision=HIGHEST,
                             preferred_element_type=F32)[0:1, :]
        num = a * v + w_inter * cq
        den = a + w_inter * jnp.sum(n_h * q, axis=1, keepdims=True)
        hh = num / jnp.maximum(jnp.abs(den), jnp.exp(-mt))
        hn = _rms(hh) * ghn_ref[:, h * ML_DV:(h + 1) * ML_DV]
        og = jax.nn.sigmoid(z[:, 2 * ML_NQK + ML_NV + h * ML_DV:2 * ML_NQK + ML_NV + (h + 1) * ML_DV]
                            + bo_ref[:, h * ML_DV:(h + 1) * ML_DV])
        outs.append(hn * og)
        v_col = jnp.broadcast_to(v, (8, ML_DV)).T[:, 0:1]
        c_out_ref[0, h] = w_inter * c_h + wj * (v_col * k)
        n_out_ref[0, h:h + 1, :] = w_inter * n_h + wj * k
    h_ref[pl.ds(i, 1), :] = jnp.concatenate(outs, axis=1)


def _mlstm_sample(z_s, gates_s, c0, n0, m0, bi, bf, bo, ghn):
    ns = z_s.shape[0]
    const2 = lambda i: (0, 0)
    in_specs = [
        pl.BlockSpec((1, 1, ML_MAIN), lambda i: (i, 0, 0)),
        pl.BlockSpec((1, 1, 256), lambda i: (i, 0, 0)),
        pl.BlockSpec((1, ML_HEADS, ML_DV, ML_DQK), lambda i: (i, 0, 0, 0)),
        pl.BlockSpec((1, ML_HEADS, ML_DQK), lambda i: (i, 0, 0)),
        pl.BlockSpec((1, 1, 128), lambda i: (i, 0, 0)),
        pl.BlockSpec((1, 128), const2),
        pl.BlockSpec((1, 128), const2),
        pl.BlockSpec((1, D_MODEL), const2),
        pl.BlockSpec((1, ML_NV), const2),
    ]
    out_shape = [
        jax.ShapeDtypeStruct((ns, ML_NV), F32),
        jax.ShapeDtypeStruct((ns, ML_HEADS, ML_DV, ML_DQK), F32),
        jax.ShapeDtypeStruct((ns, ML_HEADS, ML_DQK), F32),
        jax.ShapeDtypeStruct((ns, 1, 128), F32),
    ]
    out_specs = [
        pl.BlockSpec((ns, ML_NV), const2),
        pl.BlockSpec((1, ML_HEADS, ML_DV, ML_DQK), lambda i: (i, 0, 0, 0)),
        pl.BlockSpec((1, ML_HEADS, ML_DQK), lambda i: (i, 0, 0)),
        pl.BlockSpec((1, 1, 128), lambda i: (i, 0, 0)),
    ]
    return pl.pallas_call(
        _mlstm_sample_kernel, grid=(ns,), in_specs=in_specs, out_specs=out_specs, out_shape=out_shape,
        compiler_params=_cparams(("arbitrary",)), name="mlstm_sample",
    )(z_s, gates_s, c0, n0, m0, bi, bf, bo, ghn)


def _attn_prompt_kernel(q_ref, kp_ref, kc_ref, vp_ref, vc_ref, o_ref, lse_ref):
    blk = pl.program_id(1)
    T = ATT_STEPS
    qi = lax.broadcasted_iota(I32, (T, T), 0)
    kj = lax.broadcasted_iota(I32, (T, T), 1)
    valid_prev = jnp.logical_and(kj >= qi, blk > 0)
    valid_cur = kj <= qi
    scale = ATT_HD ** -0.5
    nt = (((1,), (1,)), ((), ()))
    lane = lax.broadcasted_iota(I32, (T, ATT_HD), 1)
    lse_all = jnp.zeros((T, ATT_HD), F32)
    raw = []
    for h in range(ATT_H):
        sl = slice(h * ATT_HD, (h + 1) * ATT_HD)
        qh = q_ref[:, sl]
        raw.append((lax.dot_general(qh, kp_ref[:, sl], nt, preferred_element_type=F32),
                    lax.dot_general(qh, kc_ref[:, sl], nt, preferred_element_type=F32)))
    for h in range(ATT_H):
        sl = slice(h * ATT_HD, (h + 1) * ATT_HD)
        s1 = jnp.where(valid_prev, raw[h][0] * scale, -jnp.inf)
        s2 = jnp.where(valid_cur, raw[h][1] * scale, -jnp.inf)
        mx = jnp.max(jnp.maximum(s1, s2), axis=1, keepdims=True)
        p1 = jnp.exp(s1 - mx)
        p2 = jnp.exp(s2 - mx)
        den = jnp.sum(p1 + p2, axis=1, keepdims=True)
        acc = jnp.dot(p1.astype(BF16), vp_ref[:, sl], preferred_element_type=F32)
        acc = acc + jnp.dot(p2.astype(BF16), vc_ref[:, sl], preferred_element_type=F32)
        o_ref[:, sl] = acc / den
        lse_all = jnp.where(lane == h, mx + jnp.log(den), lse_all)
    lse_ref[...] = lse_all


def _attn_prompt(q, kv, g):
    dil, L, _ = q.shape
    nb = L // ATT_STEPS
    T = ATT_STEPS
    blk = (None, T, ATT_GW)
    in_specs = [
        pl.BlockSpec(blk, lambda r, b: (r, b, 0)),
        pl.BlockSpec(blk, lambda r, b: (r, jnp.maximum(b - 1, 0), 0)),
        pl.BlockSpec(blk, lambda r, b: (r, b, 0)),
        pl.BlockSpec(blk, lambda r, b: (r, jnp.maximum(b - 1, 0), 1)),
        pl.BlockSpec(blk, lambda r, b: (r, b, 1)),
    ]
    out_specs = [pl.BlockSpec(blk, lambda r, b: (r, b, 0)), pl.BlockSpec((None, T, ATT_HD), lambda r, b: (r, b, 0))]
    return pl.pallas_call(
        _attn_prompt_kernel, grid=(dil, nb), in_specs=in_specs, out_specs=out_specs,
        out_shape=[jax.ShapeDtypeStruct((dil, L, ATT_GW), F32), jax.ShapeDtypeStruct((dil, L, ATT_HD), F32)],
        compiler_params=_cparams(("arbitrary", "arbitrary")), name=f"attn_prompt_g{g}",
    )(q, kv, kv, kv, kv)


def _merge_kernel(*refs):
    in_refs, out_ref, scratch = refs[:2 * ATT_G], refs[2 * ATT_G], refs[2 * ATT_G + 1:]
    tm = out_ref.shape[0]

    def position_order(ref, lanes, buf, dil):
        if dil == 1:
            return ref[0, :, lanes]
        for r in range(dil):
            buf[pl.ds(r, tm // dil, stride=dil), :] = ref[r, :, lanes]
        return buf[...]

    all128 = slice(0, ATT_HD)
    lses = [position_order(in_refs[2 * g + 1], all128, scratch[2 * g + 1], ATT_DILS[g]) for g in range(ATT_G)]
    mx = jnp.maximum(jnp.maximum(lses[0], lses[1]), lses[2])
    es = [jnp.exp(l - mx) for l in lses]
    tot = es[0] + es[1] + es[2]
    wgt = [e / tot for e in es]
    for c in range(ATT_H):
        lanes = slice(c * ATT_HD, (c + 1) * ATT_HD)
        acc = None
        for g in range(ATT_G):
            o = position_order(in_refs[2 * g], lanes, scratch[2 * g], ATT_DILS[g])
            term = wgt[g][:, c:c + 1] * o
            acc = term if acc is None else acc + term
        out_ref[:, lanes] = acc.astype(out_ref.dtype)


def _merge_groups(parts, seq, tm=512):
    in_specs, args, scratch = [], [], []
    for g, pair in enumerate(parts):
        dil = ATT_DILS[g]
        for a in pair:
            in_specs.append(pl.BlockSpec((dil, tm // dil, a.shape[2]), lambda i: (0, i, 0)))
            args.append(a)
            scratch.append(pltpu.VMEM((tm, ATT_HD), F32))
    return pl.pallas_call(
        _merge_kernel, grid=(seq // tm,), in_specs=in_specs,
        out_specs=pl.BlockSpec((tm, ATT_GW), lambda i: (i, 0)),
        out_shape=jax.ShapeDtypeStruct((seq, ATT_GW), BF16), scratch_shapes=scratch,
        compiler_params=_cparams(("arbitrary",)), name="attn_merge",
    )(*args)


def _attn_sample_kernel(q_ref, kvn_ref, b0_ref, b1_ref, b2_ref, out_ref):
    scale = ATT_HD ** -0.5
    outs, lses = [], []
    for g, b_ref in enumerate((b0_ref, b1_ref, b2_ref)):
        qg = q_ref[0, g]
        kn = kvn_ref[0, g, 0]
        vn = kvn_ref[0, g, 1]
        kb = b_ref[:, 0]
        vb = b_ref[:, 1]
        s = jnp.sum(kb * qg[None], axis=2, keepdims=True) * scale
        s_new = jnp.sum(kn * qg, axis=1, keepdims=True) * scale
        mx = jnp.maximum(jnp.max(s, axis=0), s_new)
        p = jnp.exp(s - mx[None])
        p_new = jnp.exp(s_new - mx)
        den = jnp.sum(p, axis=0) + p_new
        o = jnp.sum(p * vb, axis=0) + p_new * vn
        outs.append(o / den)
        lses.append(mx + jnp.log(den))
    mxl = jnp.maximum(jnp.maximum(lses[0], lses[1]), lses[2])
    es = [jnp.exp(l - mxl) for l in lses]
    tot = es[0] + es[1] + es[2]
    out_ref[0] = (es[0] / tot) * outs[0] + (es[1] / tot) * outs[1] + (es[2] / tot) * outs[2]


def _attn_sample(q_s, kv_s, caches):
    ns = q_s.shape[0]
    views, specs = [], []
    for g, cbuf in enumerate(caches):
        lb = cbuf.shape[1]
        dil = ATT_DILS[g]
        views.append(cbuf.reshape(ns, lb // dil, dil, 2, ATT_H, ATT_HD))
        specs.append(pl.BlockSpec((None, ATT_STEPS, None, 2, ATT_H, ATT_HD), lambda i: (i, 0, 0, 0, 0, 0)))
    in_specs = [
        pl.BlockSpec((1, ATT_G, ATT_H, ATT_HD), lambda i: (i, 0, 0, 0)),
        pl.BlockSpec((1, ATT_G, 2, ATT_H, ATT_HD), lambda i: (i, 0, 0, 0, 0)),
    ] + specs
    return pl.pallas_call(
        _attn_sample_kernel, grid=(ns,), in_specs=in_specs,
        out_specs=pl.BlockSpec((1, ATT_H, ATT_HD), lambda i: (i, 0, 0)),
        out_shape=jax.ShapeDtypeStruct((ns, ATT_H, ATT_HD), F32),
        compiler_params=_cparams(("arbitrary",)), name="attn_sample",
    )(q_s, kv_s, *views)


KV_SHIFT_BLOCK = 8192


def _kv_shift_kernel(cur_ref, nxt_ref, new_ref, out_ref):
    blk = out_ref.shape[1]
    out_ref[0, :blk - KV_ROW] = cur_ref[0, KV_ROW:]
    last = pl.program_id(1) == pl.num_programs(1) - 1
    out_ref[0, blk - KV_ROW:] = jnp.where(last, new_ref[0], nxt_ref[0])


def _kv_shift(cache, new):
    ns, lb = cache.shape[0], cache.shape[1]
    rows = lb * KV_ROW
    blk = min(KV_SHIFT_BLOCK, rows)
    nb = rows // blk
    per = blk // KV_ROW
    flat = cache.reshape(ns, rows, ATT_HD)
    out = pl.pallas_call(
        _kv_shift_kernel, grid=(ns, nb),
        in_specs=[pl.BlockSpec((1, blk, ATT_HD), lambda i, j: (i, j, 0)),
                  pl.BlockSpec((1, KV_ROW, ATT_HD), lambda i, j: (i, jnp.minimum((j + 1) * per, lb - 1), 0)),
                  pl.BlockSpec((1, KV_ROW, ATT_HD), lambda i, j: (i, 0, 0))],
        out_specs=pl.BlockSpec((1, blk, ATT_HD), lambda i, j: (i, j, 0)),
        out_shape=jax.ShapeDtypeStruct(flat.shape, flat.dtype),
        compiler_params=_cparams(("arbitrary", "arbitrary")), name="kv_shift",
    )(flat, flat, new.reshape(ns, KV_ROW, ATT_HD))
    return out.reshape(cache.shape)


def _top2(y, wr_ref, br_ref):
    rows = y.shape[0]
    lane = lax.broadcasted_iota(I32, (rows, 128), 1)
    logits = jnp.dot(y, wr_ref[...], precision=HIGHEST, preferred_element_type=F32) + br_ref[...]
    logits = jnp.where(lane < N_EXPERTS, logits, NEG_BIG)
    e = jnp.exp(logits - jnp.max(logits, axis=1, keepdims=True))
    probs = e / jnp.sum(e, axis=1, keepdims=True)
    p1 = jnp.max(probs, axis=1, keepdims=True)
    i1 = jnp.min(jnp.where(probs == p1, lane, 128), axis=1, keepdims=True)
    probs2 = jnp.where(lane == i1, -1.0, probs)
    p2 = jnp.max(probs2, axis=1, keepdims=True)
    i2 = jnp.min(jnp.where(probs2 == p2, lane, 128), axis=1, keepdims=True)
    tot = p1 + p2
    return lane, i1, i2, p1 / tot, p2 / tot


def _router_kernel(x_ref, xs_ref, g_ref, wr_ref, br_ref, xn_ref, eid_ref, gate_ref, rank_ref, cnt_ref,
                   xns_ref, gs_ref, carry):
    i = pl.program_id(0)
    tm = x_ref.shape[0]

    @pl.when(i == 0)
    def _():
        carry[...] = jnp.zeros_like(carry)
        ys = _rms(xs_ref[...]) * g_ref[...]
        xns_ref[...] = ys
        lane, i1, i2, g1, g2 = _top2(ys, wr_ref, br_ref)
        gs_ref[...] = jnp.where(lane == i1, g1, jnp.where(lane == i2, g2, 0.0))

    y = _rms(x_ref[...]) * g_ref[...]
    xn_ref[...] = y
    lane, i1, i2, g1, g2 = _top2(y, wr_ref, br_ref)
    sel1 = lane == i1
    sel2 = lane == i2
    onehot = jnp.where(jnp.logical_or(sel1, sel2), 1.0, 0.0)
    rr = lax.broadcasted_iota(I32, (tm, tm), 0)
    cc = lax.broadcasted_iota(I32, (tm, tm), 1)
    before = (cc < rr).astype(BF16)
    prefix = jnp.dot(before, onehot.astype(BF16), preferred_element_type=F32) + carry[...]
    r1 = jnp.sum(jnp.where(sel1, prefix, 0.0), axis=1, keepdims=True)
    r2 = jnp.sum(jnp.where(sel2, prefix, 0.0), axis=1, keepdims=True)
    carry[...] = carry[...] + jnp.sum(onehot, axis=0, keepdims=True)
    eid_ref[...] = jnp.where(lane == 0, i1, jnp.where(lane == 1, i2, 0))
    gate_ref[...] = jnp.where(lane == 0, g1, jnp.where(lane == 1, g2, 0.0))
    rank_ref[...] = jnp.where(lane == 0, r1, jnp.where(lane == 1, r2, 0.0)).astype(I32)
    cnt_ref[...] = jnp.broadcast_to(carry[...], cnt_ref.shape)


def _router(h, hs, gain, w_router_pad, b_router_pad, tm=ROW_TILE):
    m, d = h.shape
    s = hs.shape[0]
    const = lambda i: (0, 0)
    row_spec = pl.BlockSpec((tm, 128), lambda i: (i, 0))
    return pl.pallas_call(
        _router_kernel, grid=(m // tm,),
        in_specs=[pl.BlockSpec((tm, d), lambda i: (i, 0)), pl.BlockSpec((s, d), const), pl.BlockSpec((1, d), const),
                  pl.BlockSpec((d, 128), const), pl.BlockSpec((1, 128), const)],
        out_specs=[pl.BlockSpec((tm, d), lambda i: (i, 0)), row_spec, row_spec, row_spec,
                   pl.BlockSpec((8, 128), const), pl.BlockSpec((s, d), const), pl.BlockSpec((s, 128), const)],
        out_shape=[jax.ShapeDtypeStruct((m, d), F32), jax.ShapeDtypeStruct((m, 128), I32),
                   jax.ShapeDtypeStruct((m, 128), F32), jax.ShapeDtypeStruct((m, 128), I32),
                   jax.ShapeDtypeStruct((8, 128), F32), jax.ShapeDtypeStruct((s, d), F32),
                   jax.ShapeDtypeStruct((s, 128), F32)],
        scratch_shapes=[pltpu.VMEM((1, 128), F32)],
        compiler_params=_cparams(("arbitrary",)), name="router",
    )(h, hs, gain, w_router_pad, b_router_pad)


def _dispatch_kernel(pos_ref, nv_ref, x_hbm, out_ref, inv, buf, sem, *, n_tok):
    i = pl.program_id(0)
    tg = out_ref.shape[0]

    @pl.when(i == 0)
    def _():
        def clear(s, c):
            inv[s] = 0
            return c

        lax.fori_loop(0, inv.shape[0], clear, 0, unroll=8)

        def fill(t, c):
            inv[pos_ref[2 * t]] = t
            inv[pos_ref[2 * t + 1]] = t
            return c

        lax.fori_loop(0, n_tok, fill, 0, unroll=8)

    def start_gather(tile):
        slot = tile % 2

        def issue(r2, c):
            for j in range(2):
                r = 2 * r2 + j
                pltpu.make_async_copy(x_hbm.at[pl.ds(inv[tile * tg + r], 1)], buf.at[slot, pl.ds(r, 1)],
                                      sem.at[slot]).start(priority=j)
            return c

        lax.fori_loop(0, tg // 2, issue, 0, unroll=4)

    @pl.when(i == 0)
    def _():
        start_gather(i)

    @pl.when(i + 1 < nv_ref[0])
    def _():
        start_gather(i + 1)

    @pl.when(i < nv_ref[0])
    def _():
        slot = i % 2
        pltpu.make_async_copy(x_hbm.at[pl.ds(0, tg)], buf.at[slot], sem.at[slot]).wait()
        out_ref[...] = buf[slot].astype(out_ref.dtype)

    @pl.when(i >= nv_ref[0])
    def _():
        out_ref[...] = jnp.zeros_like(out_ref)


def _dispatch(pos_flat, nvalid, xn, n_tiles, tg):
    n_tok, d = xn.shape
    return pl.pallas_call(
        functools.partial(_dispatch_kernel, n_tok=n_tok),
        grid_spec=pltpu.PrefetchScalarGridSpec(
            num_scalar_prefetch=2, grid=(n_tiles,),
            in_specs=[pl.BlockSpec(memory_space=pl.ANY)],
            out_specs=pl.BlockSpec((tg, d), lambda i, pos, nv: (i, 0)),
            scratch_shapes=[pltpu.SMEM((n_tiles * tg,), I32), pltpu.VMEM((2, tg, d), xn.dtype),
                            pltpu.SemaphoreType.DMA((2,))]),
        out_shape=jax.ShapeDtypeStruct((n_tiles * tg, d), BF16),
        compiler_params=_cparams(("arbitrary",)), name="moe_dispatch",
    )(pos_flat, nvalid, xn)


def _combine_kernel(pos_ref, h_ref, gate_ref, g_ref, hs_ref, gs_ref, ys_s_ref, ys_hbm, out_ref, outs_ref, ybuf, sem):
    i = pl.program_id(0)
    tm = h_ref.shape[0]

    def start_gather(tile):
        slot = tile % 2

        def issue(r, carry):
            for j in range(2):
                pltpu.make_async_copy(ys_hbm.at[pl.ds(pos_ref[2 * (tile * tm + r) + j], 1)],
                                      ybuf.at[slot, j, pl.ds(r, 1)], sem.at[slot]).start(priority=j)
            return carry

        lax.fori_loop(0, tm, issue, 0, unroll=4)

    @pl.when(i == 0)
    def _():
        start_gather(i)

    @pl.when(i + 1 < pl.num_programs(0))
    def _():
        start_gather(i + 1)

    @pl.when(i == 0)
    def _():
        gs = gs_ref[...]
        y = jnp.zeros(hs_ref.shape, F32)
        for e in range(N_EXPERTS):
            y = y + gs[:, e:e + 1] * ys_s_ref[e]
        outs_ref[...] = _rms(hs_ref[...] + y) * g_ref[...]

    slot = i % 2
    for j in range(2):
        pltpu.make_async_copy(ys_hbm.at[pl.ds(0, tm)], ybuf.at[slot, j], sem.at[slot]).wait()
    gate = gate_ref[...]
    y = h_ref[...] + (gate[:, 0:1] * ybuf[slot, 0] + gate[:, 1:2] * ybuf[slot, 1])
    out_ref[...] = _rms(y) * g_ref[...]


def _combine(pos_flat, h, gate, g_final, hs, gs, ys_s, ys, tm=ROW_TILE):
    m, d = h.shape
    s = hs.shape[0]
    c2 = lambda i, pos: (0, 0)
    return pl.pallas_call(
        _combine_kernel,
        grid_spec=pltpu.PrefetchScalarGridSpec(
            num_scalar_prefetch=1, grid=(m // tm,),
            in_specs=[pl.BlockSpec((tm, d), lambda i, pos: (i, 0)),
                      pl.BlockSpec((tm, 128), lambda i, pos: (i, 0)),
                      pl.BlockSpec((1, d), c2),
                      pl.BlockSpec((s, d), c2),
                      pl.BlockSpec((s, 128), c2),
                      pl.BlockSpec((N_EXPERTS, s, d), lambda i, pos: (0, 0, 0)),
                      pl.BlockSpec(memory_space=pl.ANY)],
            out_specs=[pl.BlockSpec((tm, d), lambda i, pos: (i, 0)), pl.BlockSpec((s, d), c2)],
            scratch_shapes=[pltpu.VMEM((2, 2, tm, d), F32), pltpu.SemaphoreType.DMA((2,))]),
        out_shape=[jax.ShapeDtypeStruct((m, d), F32), jax.ShapeDtypeStruct((s, d), F32)],
        compiler_params=_cparams(("arbitrary",)), name="moe_combine",
    )(pos_flat, h, gate, g_final, hs, gs, ys_s, ys)


def _rope_tables(pos):
    half = ATT_HD // 2
    inv = ROPE_THETA ** (-jnp.arange(half, dtype=F32) / half)
    ang = pos.astype(F32)[:, None] * inv[None, :]
    cos, sin = jnp.cos(ang), jnp.sin(ang)
    return jnp.concatenate([cos, cos], axis=1), jnp.concatenate([-sin, sin], axis=1)


def kernel(x_prompt, x_sample, state_mlstm_C, state_mlstm_n, state_mlstm_m, cache_kv_w128, cache_kv_w512, cache_kv_w2048, g_mix, g_ffn, w_ml_in, b_ml_gates, b_ml_o, g_ml_hnorm, w_ml_out, g_kv, w_kv, w_q, w_o, w_ffn_gate, w_ffn_up, w_ffn_down, w_router, b_router, w_exp_gate, w_exp_up, w_exp_down, g_final):
    bp, seq, d = x_prompt.shape
    ns = x_sample.shape[0]
    caches = (cache_kv_w128, cache_kv_w512, cache_kv_w2048)
    assert bp == 1 and x_sample.shape[1] == 1 and d == D_MODEL and ns % 8 == 0
    assert seq % (ATT_STEPS * max(ATT_DILS)) == 0 and seq % 1024 == 0
    assert all(c.shape[1] == w for c, w in zip(caches, ATT_WINDOWS))
    tm = 512

    h0 = x_prompt.reshape(seq, d)
    h0_s = x_sample.reshape(ns, d)

    w_gates = lax.slice_in_dim(w_ml_in, ML_MAIN, ML_MAIN + 2 * ML_HEADS, axis=2)[0]
    w_gates_pad = jnp.zeros((d, 256), F32).at[:, 0:ML_HEADS].set(w_gates[:, :ML_HEADS])
    w_gates_pad = w_gates_pad.at[:, 128:128 + ML_HEADS].set(w_gates[:, ML_HEADS:])
    bi = jnp.zeros((1, 128), F32).at[0, :ML_HEADS].set(b_ml_gates[0, :ML_HEADS])
    bf = jnp.zeros((1, 128), F32).at[0, :ML_HEADS].set(b_ml_gates[0, ML_HEADS:])
    (xn0, gates), (xn0_s, gates_s) = _rmsnorm(h0, h0_s, g_mix[0:1], proj=w_gates_pad)
    k_scale = jnp.concatenate([jnp.ones((1, ML_NQK), F32), jnp.full((1, ML_NQK), ML_DQK ** -0.5, F32),
                               jnp.ones((1, ML_NV + D_MODEL), F32)], axis=1)
    z, z_s = _matmul(xn0, [w_ml_in], ML_MAIN, BF16, xn0_s[None], tm=1024, tn=1024, col_scale=k_scale, name="ml_in")
    bo = b_ml_o[0:1]
    ghn = g_ml_hnorm[0:1]
    hg, p_c, p_n, p_m = _mlstm_prompt(z[0], gates, bi, bf, bo, ghn)
    m0 = jnp.zeros((ns, 1, 128), F32).at[:, 0, :ML_HEADS].set(state_mlstm_m[0])
    hg_s, s_c, s_n, s_m = _mlstm_sample(z_s[0].reshape(ns, 1, ML_MAIN), gates_s.reshape(ns, 1, 256),
                                        state_mlstm_C[0], state_mlstm_n[0], m0, bi, bf, bo, ghn)
    h1, h1_s = _matmul(hg, [w_ml_out], d, F32, hg_s[None], tm=tm, tn=1024, res=h0, res_s=h0_s, name="ml_out")
    h1, h1_s = h1[0], h1_s[0]

    (xf0,), (xf0_s,) = _rmsnorm(h1, h1_s, g_ffn[0:1])
    ffn_dense = w_ffn_gate.shape[2]
    ffn_tm, ffn_tn = 1024, 512
    hid, hid_s, sbuf2_part = _matmul(xf0, [w_ffn_gate, w_ffn_up], ffn_dense, BF16, xf0_s[None], tm=ffn_tm, tn=ffn_tn,
                                     shift_cache=caches[2], name="ffn_up")
    sbuf2_done = (ffn_dense // ffn_tn) * (seq // ffn_tm) * SHIFT_PER_STEP
    h2, h2_s = _matmul(hid[0], [w_ffn_down], d, F32, hid_s, tm=tm, tn=512, res=h1, res_s=h1_s, name="ffn_down")
    h2, h2_s = h2[0], h2_s[0]

    (xq, xkv), (xq_s, xkv_s) = _rmsnorm(h2, h2_s, jnp.stack([g_mix[1], g_kv]))
    cos, sin = _rope_tables(jnp.arange(seq))
    cos_s, sin_s = _rope_tables(jnp.full((ns,), PAST_LEN, I32))
    rope_args = dict(cos=cos, sin=sin, cos_s=cos_s, sin_s=sin_s)
    parts, kv_nat, kv_new, q_new = [], [], [], []
    tail = min(max(ATT_WINDOWS), seq)
    for g in range(ATT_G):
        dil = ATT_DILS[g]
        kvd, kvn, kv_s = _matmul(xkv, [w_kv[None]], 2 * ATT_GW, BF16, xkv_s[None], tm=tm, tn=ATT_GW,
                                 col_off=2 * g, rope="even", dil=dil, natural_tail=tail, name=f"kv_proj_g{g}",
                                 **rope_args)
        qd, q_s = _matmul(xq, [w_q], ATT_GW, BF16, xq_s[None], tm=tm, tn=ATT_GW, col_off=g, rope="all",
                          dil=dil, name=f"q_proj_g{g}", **rope_args)
        parts.append(_attn_prompt(qd, kvd, g))
        kv_nat.append(kvn)
        kv_new.append(kv_s[0])
        q_new.append(q_s[0])
    att = _merge_groups(parts, seq)
    att_s = _attn_sample(jnp.stack(q_new, axis=1).reshape(ns, ATT_G, ATT_H, ATT_HD),
                         jnp.stack(kv_new, axis=1).reshape(ns, ATT_G, 2, ATT_H, ATT_HD), caches)
    h3, h3_s = _matmul(att, [w_o], d, F32, att_s.reshape(1, ns, ATT_GW), tm=tm, tn=1024, res=h2, res_s=h2_s,
                       name="attn_out")
    h3, h3_s = h3[0], h3_s[0]

    wr_pad = jnp.zeros((d, 128), F32).at[:, :N_EXPERTS].set(w_router[0])
    br_pad = jnp.zeros((1, 128), F32).at[0, :N_EXPERTS].set(b_router[0])
    xn2, eid, gate, rank, cnt, xn2_s, gates_moe_s = _router(h3, h3_s, g_ffn[1:2], wr_pad, br_pad)
    tg = MOE_TILE
    n_tiles = -(-(2 * seq + N_EXPERTS * (tg - 1)) // tg)
    counts = cnt[0, :N_EXPERTS].astype(I32)
    padded = jnp.maximum((counts + tg - 1) // tg, 1) * tg
    gend = jnp.cumsum(padded)
    gstart = gend - padded
    pos = (gstart[eid[:, :2]] + rank[:, :2]).astype(I32).reshape(-1)
    nvalid = (gend[-1] // tg).astype(I32).reshape(1)
    xs = _dispatch(pos, nvalid, xn2, n_tiles, tg)
    t0 = (gstart // tg).astype(I32)
    tcnt = (padded // tg).astype(I32)
    hs, hs_s, sbuf2 = _moe_matmul(xs, [w_exp_gate[0], w_exp_up[0]], BF16, xn2_s[None], t0, tcnt, nvalid, tm=tg,
                                  tn=1024, xs_per_expert=False, name="moe_up", window=(caches[2], kv_new[2]),
                                  started=(sbuf2_part, sbuf2_done))
    ys, ys_s, sbuf1 = _moe_matmul(hs, [w_exp_down[0]], F32, hs_s, t0, tcnt, nvalid, tm=tg, tn=512,
                                  xs_per_expert=True, name="moe_down", window=(caches[1], kv_new[1]))
    s_bufs = [_kv_shift(caches[0], kv_new[0]), sbuf1, sbuf2]
    y_p, y_s = _combine(pos, h3, gate, g_final.reshape(1, d), h3_s, gates_moe_s, ys_s, ys)

    p_bufs = []
    for g in range(ATT_G):
        keep = min(ATT_WINDOWS[g], seq)
        p_bufs.append(kv_nat[g][tail - keep:].reshape(1, keep, 2, ATT_H, ATT_HD))
    return (y_p.reshape(1, seq, d), y_s.reshape(ns, 1, d),
            p_c[None, None], p_n[None, None], p_m[:, :ML_HEADS][None],
            s_c[None], s_n[None], s_m[:, 0, :ML_HEADS][None],
            p_bufs[0], p_bufs[1], p_bufs[2], s_bufs[0], s_bufs[1], s_bufs[2])
```

```python
import functools

import jax
import jax.numpy as jnp
from jax import lax
from jax.experimental import pallas as pl
from jax.experimental.pallas import tpu as pltpu

F32 = jnp.float32
BF16 = jnp.bfloat16
I32 = jnp.int32
HIGHEST = lax.Precision.HIGHEST

D_MODEL = 2048
ML_HEADS = 8
ML_DQK = 128
ML_DV = 256
ML_NQK = ML_HEADS * ML_DQK
ML_NV = ML_HEADS * ML_DV
ML_MAIN = 2 * ML_NQK + ML_NV + D_MODEL
ML_CHUNK = 128
ATT_HD = 128
ATT_H = 8
ATT_G = 3
ATT_WINDOWS = (128, 512, 2048)
ATT_DILS = (1, 4, 16)
ATT_STEPS = 128
ATT_GW = ATT_H * ATT_HD
ROPE_THETA = 10000.0
PAST_LEN = 8192
N_EXPERTS = 8
RMS_EPS = 1e-6
NEG_BIG = -1e30

VMEM_LIMIT_BYTES = 58 * 1024 * 1024
ROW_TILE = 512
MOE_TILE = 256


def _cparams(sem):
    return pltpu.CompilerParams(dimension_semantics=sem, vmem_limit_bytes=VMEM_LIMIT_BYTES)


def _rms(x):
    return x * lax.rsqrt(jnp.mean(x * x, axis=-1, keepdims=True) + RMS_EPS)


def _norm_kernel(x_ref, xs_ref, g_ref, *refs, n_out, has_proj):
    ins = 1 if has_proj else 0
    n_each = n_out + ins
    main = refs[ins:ins + n_each]
    side = refs[ins + n_each:ins + 2 * n_each]

    def emit(x, outs):
        y = _rms(x)
        for i in range(n_out):
            outs[i][...] = (y * g_ref[i:i + 1, :]).astype(outs[i].dtype)
        if has_proj:
            outs[n_out][...] = jnp.dot(y * g_ref[0:1, :], refs[0][...], precision=HIGHEST,
                                       preferred_element_type=F32)

    emit(x_ref[...], main)

    @pl.when(pl.program_id(0) == 0)
    def _():
        emit(xs_ref[...], side)


def _rmsnorm(x, xs, gains, proj=None, tm=ROW_TILE):
    m, d = x.shape
    s = xs.shape[0]
    n_out = gains.shape[0]
    const = lambda i: (0, 0)
    in_specs = [pl.BlockSpec((tm, d), lambda i: (i, 0)), pl.BlockSpec((s, d), const),
                pl.BlockSpec((n_out, d), const)]
    args = [x, xs, gains]
    main_shape = [jax.ShapeDtypeStruct((m, d), BF16)] * n_out
    main_specs = [pl.BlockSpec((tm, d), lambda i: (i, 0))] * n_out
    side_shape = [jax.ShapeDtypeStruct((s, d), F32)] * n_out
    side_specs = [pl.BlockSpec((s, d), const)] * n_out
    if proj is not None:
        p = proj.shape[1]
        in_specs.append(pl.BlockSpec((d, p), const))
        args.append(proj)
        main_shape.append(jax.ShapeDtypeStruct((m, p), F32))
        main_specs.append(pl.BlockSpec((tm, p), lambda i: (i, 0)))
        side_shape.append(jax.ShapeDtypeStruct((s, p), F32))
        side_specs.append(pl.BlockSpec((s, p), const))
    outs = pl.pallas_call(
        functools.partial(_norm_kernel, n_out=n_out, has_proj=proj is not None),
        grid=(m // tm,), in_specs=in_specs, out_specs=main_specs + side_specs,
        out_shape=main_shape + side_shape,
        compiler_params=_cparams(("arbitrary",)), name="rmsnorm",
    )(*args)
    k = len(main_shape)
    return outs[:k], outs[k:]


def _rope_heads(acc, cos, sin):
    outs = []
    for h in range(acc.shape[1] // ATT_HD):
        a = acc[:, h * ATT_HD:(h + 1) * ATT_HD]
        outs.append(a * cos + pltpu.roll(a, ATT_HD // 2, 1) * sin)
    return jnp.concatenate(outs, axis=1)


def _mm_kernel(te_ref, nv_ref, x_ref, *refs, n_w, has_scale, has_res, rope, dil, nat_first):
    n = pl.program_id(0)
    m = pl.program_id(1)
    it = iter(refs)
    w_refs = [next(it) for _ in range(n_w)]
    scale_ref = next(it) if has_scale else None
    res_ref = next(it) if has_res else None
    cos_ref, sin_ref = (next(it), next(it)) if rope else (None, None)
    xs_ref = next(it)
    res_s_ref = next(it) if has_res else None
    cos_s_ref, sin_s_ref = (next(it), next(it)) if rope else (None, None)
    o_ref = next(it)
    nat_ref = next(it) if nat_first is not None else None
    os_ref = next(it)
    wb_refs = [next(it) for _ in range(n_w)]
    deint = next(it) if dil > 1 else None

    def finish(acc, up, res, cos, sin, store):
        if n_w == 2:
            acc = (acc * jax.nn.sigmoid(acc)) * up
        if has_scale:
            acc = acc * scale_ref[...]
        if has_res:
            acc = acc + res
        if rope == "all":
            store(_rope_heads(acc, cos, sin))
        elif rope == "even":
            @pl.when(n % 2 == 0)
            def _():
                store(_rope_heads(acc, cos, sin))

            @pl.when(n % 2 == 1)
            def _():
                store(acc)
        else:
            store(acc)

    def store_side(val):
        os_ref[0] = val

    def store_main(val):
        if nat_first is not None:
            @pl.when(m >= nat_first)
            def _():
                nat_ref[...] = val
        if dil == 1:
            o_ref[0] = val.astype(o_ref.dtype)
        else:
            rows = deint.shape[1] // dil
            for c in range(deint.shape[0]):
                lanes = slice(c * 128, (c + 1) * 128)
                deint[c] = val[:, lanes]
                for r in range(dil):
                    o_ref[r, :, lanes] = deint[c, pl.ds(r, rows, stride=dil), :].astype(o_ref.dtype)

    prev = jnp.maximum(m - 1, 0)
    new_weights = jnp.logical_or(m == 0, te_ref[m] != te_ref[prev])

    @pl.when(new_weights)
    def _():
        for w_ref, wb_ref in zip(w_refs, wb_refs):
            wb_ref[...] = w_ref[0].astype(BF16)
        xs = xs_ref[0]
        s_rows = xs.shape[0]
        xh = xs.astype(BF16)
        xl = (xs - xh.astype(F32)).astype(BF16)
        x_hl = jnp.concatenate([xh, xl], axis=0)

        def side_product(i):
            wh = wb_refs[i][...]
            wl = (w_refs[i][0] - wh.astype(F32)).astype(BF16)
            both = jnp.dot(x_hl, wh, preferred_element_type=F32)
            return both[:s_rows] + both[s_rows:] + jnp.dot(xh, wl, preferred_element_type=F32)

        acc = side_product(0)
        up = side_product(1) if n_w == 2 else None
        finish(acc, up, res_s_ref[...] if has_res else None,
               cos_s_ref[...] if rope else None, sin_s_ref[...] if rope else None, store_side)

    @pl.when(m >= nv_ref[0])
    def _():
        o_ref[...] = jnp.zeros_like(o_ref)

    @pl.when(m < nv_ref[0])
    def _():
        xb = x_ref[...].astype(BF16)
        acc = jnp.dot(xb, wb_refs[0][...], preferred_element_type=F32)
        up = jnp.dot(xb, wb_refs[1][...], preferred_element_type=F32) if n_w == 2 else None
        finish(acc, up, res_ref[...] if has_res else None,
               cos_ref[...] if rope else None, sin_ref[...] if rope else None, store_main)


def _matmul(x, ws, n_cols, out_dtype, xs, *, tm, tn, col_off=0, te=None, nvalid=None, xs_per_expert=False,
            col_scale=None, res=None, res_s=None, rope=None, cos=None, sin=None, cos_s=None, sin_s=None,
            dil=1, natural_tail=0, name="matmul"):
    m, k = x.shape
    s = xs.shape[1]
    n_m = m // tm
    n_n = n_cols // tn
    n_e = ws[0].shape[0]
    if te is None:
        te = jnp.zeros((n_m,), I32)
        nvalid = jnp.full((1,), n_m, I32)

    def row(mi, nv):
        return jnp.minimum(mi, nv[0] - 1)

    def exp(mi, te, nv):
        return te[row(mi, nv)]

    in_specs = [pl.BlockSpec((tm, k), lambda n, mi, te, nv: (row(mi, nv), 0))]
    args = [x]
    for w in ws:
        in_specs.append(pl.BlockSpec((1, k, tn), lambda n, mi, te, nv: (exp(mi, te, nv), 0, n + col_off)))
        args.append(w)
    if col_scale is not None:
        in_specs.append(pl.BlockSpec((1, tn), lambda n, mi, te, nv: (0, n)))
        args.append(col_scale)
    if res is not None:
        in_specs.append(pl.BlockSpec((tm, tn), lambda n, mi, te, nv: (row(mi, nv), n)))
        args.append(res)
    if rope is not None:
        for t in (cos, sin):
            in_specs.append(pl.BlockSpec((tm, ATT_HD), lambda n, mi, te, nv: (row(mi, nv), 0)))
            args.append(t)
    if xs_per_expert:
        in_specs.append(pl.BlockSpec((1, s, k), lambda n, mi, te, nv: (exp(mi, te, nv), 0, 0)))
    else:
        in_specs.append(pl.BlockSpec((1, s, k), lambda n, mi, te, nv: (0, 0, 0)))
    args.append(xs)
    if res is not None:
        in_specs.append(pl.BlockSpec((s, tn), lambda n, mi, te, nv: (0, n)))
        args.append(res_s)
    if rope is not None:
        for t in (cos_s, sin_s):
            in_specs.append(pl.BlockSpec((s, ATT_HD), lambda n, mi, te, nv: (0, 0)))
            args.append(t)

    out_shape = [jax.ShapeDtypeStruct((dil, m // dil, n_cols), out_dtype)]
    out_specs = [pl.BlockSpec((dil, tm // dil, tn), lambda n, mi, te, nv: (0, mi, n))]
    nat_first = None
    if natural_tail:
        nat_first = n_m - natural_tail // tm
        out_shape.append(jax.ShapeDtypeStruct((natural_tail, n_cols), F32))
        out_specs.append(pl.BlockSpec((tm, tn), lambda n, mi, te, nv: (jnp.maximum(mi - nat_first, 0), n)))
    out_shape.append(jax.ShapeDtypeStruct((n_e, s, n_cols), F32))
    out_specs.append(pl.BlockSpec((1, s, tn), lambda n, mi, te, nv: (exp(mi, te, nv), 0, n)))
    scratch = [pltpu.VMEM((k, tn), BF16) for _ in ws]
    if dil > 1:
        scratch.append(pltpu.VMEM((tn // 128, tm, 128), F32))
    kern = functools.partial(_mm_kernel, n_w=len(ws), has_scale=col_scale is not None, has_res=res is not None,
                             rope=rope, dil=dil, nat_first=nat_first)
    return pl.pallas_call(
        kern,
        grid_spec=pltpu.PrefetchScalarGridSpec(
            num_scalar_prefetch=2, grid=(n_n, n_m), in_specs=in_specs, out_specs=out_specs,
            scratch_shapes=scratch),
        out_shape=out_shape,
        compiler_params=_cparams(("arbitrary", "arbitrary")), name=name,
    )(te, nvalid, *args)


KV_ROW = 2 * ATT_H
RING_SLOTS = 3
SHIFT_SEQS = 2
SHIFT_MAX_ROWS = 128


def _shift_chunk_rows(window):
    moved = window - 1
    return max(d for d in range(1, SHIFT_MAX_ROWS + 1) if moved % d == 0)


class _WindowShift:
    def __init__(self, buf_hbm, new_hbm, out_hbm, stage, sem_in, sem_out, sem_new, next_chunk):
        self.buf, self.new, self.out, self.stage = buf_hbm, new_hbm, out_hbm, stage
        self.sem_in, self.sem_out, self.sem_new, self.next_chunk = sem_in, sem_out, sem_new, next_chunk
        ns, self.rows = buf_hbm.shape[0], buf_hbm.shape[1]
        self.chunk = stage.shape[2]
        self.per_group = (self.rows - KV_ROW) // self.chunk
        self.total = (ns // SHIFT_SEQS) * self.per_group

    def _load(self, k, slot):
        seqs = pl.ds((k // self.per_group) * SHIFT_SEQS, SHIFT_SEQS)
        src = self.buf.at[seqs, pl.ds(KV_ROW + (k % self.per_group) * self.chunk, self.chunk)]
        return pltpu.make_async_copy(src, self.stage.at[slot], self.sem_in.at[slot])

    def _store(self, k, slot):
        seqs = pl.ds((k // self.per_group) * SHIFT_SEQS, SHIFT_SEQS)
        dst = self.out.at[seqs, pl.ds((k % self.per_group) * self.chunk, self.chunk)]
        return pltpu.make_async_copy(self.stage.at[slot], dst, self.sem_out.at[slot])

    def _append(self):
        return pltpu.make_async_copy(self.new, self.out.at[:, pl.ds(self.rows - KV_ROW, KV_ROW)], self.sem_new.at[0])

    def begin(self):
        self.next_chunk[0] = 0
        for k in range(RING_SLOTS - 1):
            self._load(k, k).start(priority=1)
        self._append().start()

    def advance(self):
        k = self.next_chunk[0]

        @pl.when(k < self.total)
        def _():
            self._load(k, k % RING_SLOTS).wait()

            @pl.when(k >= 1)
            def _():
                self._store(k - 1, (k - 1) % RING_SLOTS).wait()

            self._store(k, k % RING_SLOTS).start()
            ahead = k + RING_SLOTS - 1

            @pl.when(ahead < self.total)
            def _():
                self._load(ahead, ahead % RING_SLOTS).start(priority=1)

            self.next_chunk[0] = k + 1

    def finish(self):
        def rest(i, carry):
            self.advance()
            return carry

        lax.fori_loop(self.next_chunk[0], self.total, rest, 0)
        self._store(self.total - 1, (self.total - 1) % RING_SLOTS).wait()
        self._append().wait()


def _moe_mm_kernel(t0_ref, cnt_ref, nv_ref, x_hbm, *refs, n_w, tm, n_tiles, n_bg):
    n = pl.program_id(0)
    e = pl.program_id(1)
    n_e = pl.num_programs(1)
    it = iter(refs)
    w_refs = [next(it) for _ in range(n_w)]
    xs_ref = next(it)
    bg_in = [next(it) for _ in range(2 * n_bg)]
    o_hbm, os_ref = next(it), next(it)
    bg_out = [next(it) for _ in range(n_bg)]
    wb_refs = [next(it) for _ in range(n_w)]
    xbuf, obuf, sem_in, sem_out = next(it), next(it), next(it), next(it)
    tn = os_ref.shape[2]
    t0 = t0_ref[e]
    cnt = cnt_ref[e]
    first_step = jnp.logical_and(n == 0, e == 0)
    last_step = jnp.logical_and(n == pl.num_programs(0) - 1, e == n_e - 1)

    shift = _WindowShift(bg_in[0], bg_in[1], bg_out[0], *[next(it) for _ in range(5)]) if n_bg else None
    if shift:
        pl.when(first_step)(shift.begin)

    def x_copy(tile, slot):
        return pltpu.make_async_copy(x_hbm.at[pl.ds(tile * tm, tm)], xbuf.at[slot], sem_in.at[slot])

    def o_copy(tile, slot):
        return pltpu.make_async_copy(obuf.at[slot], o_hbm.at[pl.ds(tile * tm, tm), pl.ds(n * tn, tn)],
                                     sem_out.at[slot])

    def product(xb):
        acc = jnp.dot(xb, wb_refs[0][...], preferred_element_type=F32)
        if n_w == 2:
            acc = (acc * jax.nn.sigmoid(acc)) * jnp.dot(xb, wb_refs[1][...], preferred_element_type=F32)
        return acc

    def start_first_tiles(first_tile, n_tiles_here):
        for j in range(RING_SLOTS - 1):
            @pl.when(j < n_tiles_here)
            def _():
                x_copy(first_tile + j, j).start(priority=1)

    @pl.when(first_step)
    def _():
        start_first_tiles(t0, cnt)

    for w_ref, wb_ref in zip(w_refs, wb_refs):
        wb_ref[...] = w_ref[0].astype(BF16)
    os_ref[0] = product(xs_ref[0].astype(BF16))

    def body(t, carry):
        slot = t % 2
        ahead = t + RING_SLOTS - 1

        @pl.when(ahead < cnt)
        def _():
            x_copy(t0 + ahead, ahead % RING_SLOTS).start(priority=1)

        x_copy(t0 + t, t % RING_SLOTS).wait()

        @pl.when(t >= 2)
        def _():
            o_copy(t0 + t - 2, slot).wait()

        obuf[slot] = product(xbuf[t % RING_SLOTS]).astype(obuf.dtype)
        o_copy(t0 + t, slot).start()
        if shift:
            shift.advance()
        return carry

    lax.fori_loop(0, cnt, body, 0)

    @pl.when(jnp.logical_not(last_step))
    def _():
        e_next = jnp.where(e == n_e - 1, 0, e + 1)
        start_first_tiles(t0_ref[e_next], cnt_ref[e_next])

    @pl.when(cnt >= 2)
    def _():
        o_copy(t0 + cnt - 2, cnt % 2).wait()

    o_copy(t0 + cnt - 1, (cnt - 1) % 2).wait()

    @pl.when(e == n_e - 1)
    def _():
        obuf[0] = jnp.zeros(obuf.shape[1:], obuf.dtype)

        def fill(tile, carry):
            o_copy(tile, 0).start()
            o_copy(tile, 0).wait()
            return carry

        lax.fori_loop(nv_ref[0], n_tiles, fill, 0)

    if shift:
        pl.when(last_step)(shift.finish)


def _moe_matmul(x, ws, out_dtype, xs, t0, cnt, nvalid, *, tm, tn, xs_per_expert, name, window=None):
    r, k = x.shape
    n_e, _, n_cols = ws[0].shape
    s = xs.shape[1]
    n_w = len(ws)
    windows = []
    if window is not None:
        cache, new = window
        ns, lb = cache.shape[0], cache.shape[1]
        assert ns % SHIFT_SEQS == 0
        windows = [(cache.reshape(ns, lb * KV_ROW, ATT_HD), new.reshape(ns, KV_ROW, ATT_HD))]
        chunk = _shift_chunk_rows(lb) * KV_ROW
    n_bg = len(windows)
    any_spec = pl.BlockSpec(memory_space=pl.ANY)
    in_specs = [any_spec]
    in_specs += [pl.BlockSpec((1, k, tn), lambda n, e, *_: (e, 0, n)) for _ in ws]
    if xs_per_expert:
        in_specs.append(pl.BlockSpec((1, s, k), lambda n, e, *_: (e, 0, 0)))
    else:
        in_specs.append(pl.BlockSpec((1, s, k), lambda n, e, *_: (0, 0, 0)))
    in_specs += [any_spec] * (2 * n_bg)
    scratch = [pltpu.VMEM((k, tn), BF16) for _ in ws]
    scratch += [pltpu.VMEM((RING_SLOTS, tm, k), x.dtype), pltpu.VMEM((2, tm, tn), out_dtype),
                pltpu.SemaphoreType.DMA((RING_SLOTS,)), pltpu.SemaphoreType.DMA((2,))]
    if n_bg:
        scratch += [pltpu.VMEM((RING_SLOTS, SHIFT_SEQS, chunk, ATT_HD), F32),
                    pltpu.SemaphoreType.DMA((RING_SLOTS,)), pltpu.SemaphoreType.DMA((RING_SLOTS,)),
                    pltpu.SemaphoreType.DMA((1,)), pltpu.SMEM((1,), I32)]
    out_shape = [jax.ShapeDtypeStruct((r, n_cols), out_dtype), jax.ShapeDtypeStruct((n_e, s, n_cols), F32)]
    out_shape += [jax.ShapeDtypeStruct(buf.shape, buf.dtype) for buf, _ in windows]
    outs = pl.pallas_call(
        functools.partial(_moe_mm_kernel, n_w=n_w, tm=tm, n_tiles=r // tm, n_bg=n_bg),
        grid_spec=pltpu.PrefetchScalarGridSpec(
            num_scalar_prefetch=3, grid=(n_cols // tn, n_e), in_specs=in_specs,
            out_specs=[any_spec, pl.BlockSpec((1, s, tn), lambda n, e, *_: (e, 0, n))] + [any_spec] * n_bg,
            scratch_shapes=scratch),
        out_shape=out_shape,
        compiler_params=_cparams(("arbitrary", "arbitrary")), name=name,
    )(t0, cnt, nvalid, x, *ws, xs, *[a for pair in windows for a in pair])
    if window is not None:
        return outs[0], outs[1], outs[2].reshape(window[0].shape)
    return outs


def _log_sigmoid(x):
    return jnp.minimum(x, 0.0) - jnp.log1p(jnp.exp(-jnp.abs(x)))


def _mlstm_prompt_kernel(q_ref, k_ref, v_ref, o_ref, gi_ref, gf_ref, bi_ref, bf_ref, bo_ref, ghn_ref,
                         h_ref, c_out_ref, n_out_ref, m_out_ref, ct_s, n_s, m_s):
    c = pl.program_id(0)
    L = ML_CHUNK

    @pl.when(c == 0)
    def _():
        ct_s[...] = jnp.zeros_like(ct_s)
        n_s[...] = jnp.zeros_like(n_s)
        m_s[...] = jnp.zeros_like(m_s)

    ig = gi_ref[...] + bi_ref[...]
    lf = _log_sigmoid(gf_ref[...] + bf_ref[...])
    r = lax.broadcasted_iota(I32, (L, L), 0)
    s = lax.broadcasted_iota(I32, (L, L), 1)
    causal = r >= s
    tril = causal.astype(F32)
    b = jnp.dot(tril, lf, precision=HIGHEST, preferred_element_type=F32)
    b_t = b.T
    ig_t = ig.T
    m_all = m_s[...]
    m_new_all = m_all
    lane = lax.broadcasted_iota(I32, (1, 128), 1)

    H = range(ML_HEADS)
    q = [q_ref[:, h * ML_DQK:(h + 1) * ML_DQK] for h in H]
    k = [k_ref[:, h * ML_DQK:(h + 1) * ML_DQK] for h in H]
    v = [v_ref[:, h * ML_DV:(h + 1) * ML_DV] for h in H]
    ct = [ct_s[h] for h in H]
    n_old = [n_s[h:h + 1, :] for h in H]
    bc = [b[:, h:h + 1] for h in H]
    m_old = [m_all[:, h:h + 1] for h in H]

    sc = [lax.dot_general(q[h], k[h], (((1,), (1,)), ((), ())), preferred_element_type=F32) for h in H]
    qc = [jnp.dot(q[h], ct[h].astype(BF16), preferred_element_type=F32) for h in H]

    logd = [jnp.where(causal, bc[h] - b_t[h:h + 1, :] + ig_t[h:h + 1, :], -jnp.inf) for h in H]
    inter = [bc[h] + m_old[h] for h in H]
    mt = [jnp.maximum(inter[h], jnp.max(logd[h], axis=1, keepdims=True)) for h in H]
    w_inter = [jnp.exp(inter[h] - mt[h]) for h in H]
    m_new = [mt[h][L - 1:L, :] for h in H]
    b_last = [bc[h][L - 1:L, :] for h in H]
    decay = [jnp.exp(b_last[h] + m_old[h] - m_new[h]) for h in H]
    wj = [jnp.exp(b_last[h] - bc[h] + ig[:, h:h + 1] - m_new[h]) for h in H]

    a = [sc[h] * jnp.exp(logd[h] - mt[h]) for h in H]
    qn = [jnp.sum(q[h].astype(F32) * n_old[h], axis=1, keepdims=True) for h in H]
    den = [jnp.sum(a[h], axis=1, keepdims=True) + w_inter[h] * qn[h] for h in H]
    num = [jnp.dot(a[h].astype(BF16), v[h], preferred_element_type=F32) + w_inter[h] * qc[h] for h in H]

    hh = [num[h] / jnp.maximum(jnp.abs(den[h]), jnp.exp(-mt[h])) for h in H]
    for h in H:
        cols = slice(h * ML_DV, (h + 1) * ML_DV)
        og = jax.nn.sigmoid(o_ref[:, cols].astype(F32) + bo_ref[:, cols])
        h_ref[:, cols] = (_rms(hh[h]) * ghn_ref[:, cols] * og).astype(h_ref.dtype)

    kf = [k[h].astype(F32) * wj[h] for h in H]
    for h in H:
        upd = lax.dot_general(kf[h].astype(BF16), v[h], (((0,), (0,)), ((), ())), preferred_element_type=F32)
        ct_s[h] = decay[h] * ct[h] + upd
        n_s[h:h + 1, :] = decay[h] * n_old[h] + jnp.sum(kf[h], axis=0, keepdims=True)
        m_new_all = jnp.where(lane == h, m_new[h], m_new_all)

    m_s[...] = m_new_all

    @pl.when(c == pl.num_programs(0) - 1)
    def _():
        for h in range(ML_HEADS):
            c_out_ref[h] = ct_s[h].T
        n_out_ref[...] = n_s[...]
        m_out_ref[...] = m_s[...]


def _mlstm_prompt(z, gates, bi, bf, bo, ghn):
    seq = z.shape[0]
    nc = seq // ML_CHUNK
    L = ML_CHUNK
    const2 = lambda c: (0, 0)
    in_specs = [
        pl.BlockSpec((L, ML_NQK), lambda c: (c, 0)),
        pl.BlockSpec((L, ML_NQK), lambda c: (c, 1)),
        pl.BlockSpec((L, ML_NV), lambda c: (c, 1)),
        pl.BlockSpec((L, D_MODEL), lambda c: (c, 2)),
        pl.BlockSpec((L, 128), lambda c: (c, 0)),
        pl.BlockSpec((L, 128), lambda c: (c, 1)),
        pl.BlockSpec((1, 128), const2),
        pl.BlockSpec((1, 128), const2),
        pl.BlockSpec((1, D_MODEL), const2),
        pl.BlockSpec((1, ML_NV), const2),
    ]
    out_shape = [
        jax.ShapeDtypeStruct((seq, ML_NV), BF16),
        jax.ShapeDtypeStruct((ML_HEADS, ML_DV, ML_DQK), F32),
        jax.ShapeDtypeStruct((ML_HEADS, ML_DQK), F32),
        jax.ShapeDtypeStruct((1, 128), F32),
    ]
    out_specs = [
        pl.BlockSpec((L, ML_NV), lambda c: (c, 0)),
        pl.BlockSpec((ML_HEADS, ML_DV, ML_DQK), lambda c: (0, 0, 0)),
        pl.BlockSpec((ML_HEADS, ML_DQK), const2),
        pl.BlockSpec((1, 128), const2),
    ]
    return pl.pallas_call(
        _mlstm_prompt_kernel, grid=(nc,), in_specs=in_specs, out_specs=out_specs, out_shape=out_shape,
        scratch_shapes=[pltpu.VMEM((ML_HEADS, ML_DQK, ML_DV), F32), pltpu.VMEM((ML_HEADS, ML_DQK), F32),
                        pltpu.VMEM((1, 128), F32)],
        compiler_params=_cparams(("arbitrary",)), name="mlstm_prompt",
    )(z, z, z, z, gates, gates, bi, bf, bo, ghn)


def _mlstm_sample_kernel(z_ref, g_ref, c_ref, n_ref, m_ref, bi_ref, bf_ref, bo_ref, ghn_ref,
                         h_ref, c_out_ref, n_out_ref, m_out_ref):
    i = pl.program_id(0)
    z = z_ref[0]
    g = g_ref[0]
    ig_all = g[:, 0:128] + bi_ref[...]
    lf_all = _log_sigmoid(g[:, 128:256] + bf_ref[...])
    m_all = m_ref[0]
    mt_all = jnp.maximum(lf_all + m_all, ig_all)
    m_out_ref[0] = mt_all
    outs = []
    for h in range(ML_HEADS):
        q = z[:, h * ML_DQK:(h + 1) * ML_DQK]
        k = z[:, ML_NQK + h * ML_DQK:ML_NQK + (h + 1) * ML_DQK]
        v = z[:, 2 * ML_NQK + h * ML_DV:2 * ML_NQK + (h + 1) * ML_DV]
        ig = ig_all[:, h:h + 1]
        lf = lf_all[:, h:h + 1]
        m0 = m_all[:, h:h + 1]
        mt = mt_all[:, h:h + 1]
        w_inter = jnp.exp(lf + m0 - mt)
        wj = jnp.exp(ig - mt)
        a = jnp.sum(q * k, axis=1, keepdims=True) * wj
        c_h = c_ref[0, h]
        n_h = n_ref[0, h:h + 1, :]
        q8 = jnp.broadcast_to(q, (8, ML_DQK))
        cq = lax.dot_general(q8, c_h, (((1,), (1,)), ((), ())), precision=HIGHEST,
                             preferred_element_type=F32)[0:1, :]
        num = a * v + w_inter * cq
        den = a + w_inter * jnp.sum(n_h * q, axis=1, keepdims=True)
        hh = num / jnp.maximum(jnp.abs(den), jnp.exp(-mt))
        hn = _rms(hh) * ghn_ref[:, h * ML_DV:(h + 1) * ML_DV]
        og = jax.nn.sigmoid(z[:, 2 * ML_NQK + ML_NV + h * ML_DV:2 * ML_NQK + ML_NV + (h + 1) * ML_DV]
                            + bo_ref[:, h * ML_DV:(h + 1) * ML_DV])
        outs.append(hn * og)
        v_col = jnp.broadcast_to(v, (8, ML_DV)).T[:, 0:1]
        c_out_ref[0, h] = w_inter * c_h + wj * (v_col * k)
        n_out_ref[0, h:h + 1, :] = w_inter * n_h + wj * k
    h_ref[pl.ds(i, 1), :] = jnp.concatenate(outs, axis=1)


def _mlstm_sample(z_s, gates_s, c0, n0, m0, bi, bf, bo, ghn):
    ns = z_s.shape[0]
    const2 = lambda i: (0, 0)
    in_specs = [
        pl.BlockSpec((1, 1, ML_MAIN), lambda i: (i, 0, 0)),
        pl.BlockSpec((1, 1, 256), lambda i: (i, 0, 0)),
        pl.BlockSpec((1, ML_HEADS, ML_DV, ML_DQK), lambda i: (i, 0, 0, 0)),
        pl.BlockSpec((1, ML_HEADS, ML_DQK), lambda i: (i, 0, 0)),
        pl.BlockSpec((1, 1, 128), lambda i: (i, 0, 0)),
        pl.BlockSpec((1, 128), const2),
        pl.BlockSpec((1, 128), const2),
        pl.BlockSpec((1, D_MODEL), const2),
        pl.BlockSpec((1, ML_NV), const2),
    ]
    out_shape = [
        jax.ShapeDtypeStruct((ns, ML_NV), F32),
        jax.ShapeDtypeStruct((ns, ML_HEADS, ML_DV, ML_DQK), F32),
        jax.ShapeDtypeStruct((ns, ML_HEADS, ML_DQK), F32),
        jax.ShapeDtypeStruct((ns, 1, 128), F32),
    ]
    out_specs = [
        pl.BlockSpec((ns, ML_NV), const2),
        pl.BlockSpec((1, ML_HEADS, ML_DV, ML_DQK), lambda i: (i, 0, 0, 0)),
        pl.BlockSpec((1, ML_HEADS, ML_DQK), lambda i: (i, 0, 0)),
        pl.BlockSpec((1, 1, 128), lambda i: (i, 0, 0)),
    ]
    return pl.pallas_call(
        _mlstm_sample_kernel, grid=(ns,), in_specs=in_specs, out_specs=out_specs, out_shape=out_shape,
        compiler_params=_cparams(("arbitrary",)), name="mlstm_sample",
    )(z_s, gates_s, c0, n0, m0, bi, bf, bo, ghn)


def _attn_prompt_kernel(q_ref, kp_ref, kc_ref, vp_ref, vc_ref, o_ref, lse_ref):
    blk = pl.program_id(1)
    T = ATT_STEPS
    qi = lax.broadcasted_iota(I32, (T, T), 0)
    kj = lax.broadcasted_iota(I32, (T, T), 1)
    valid_prev = jnp.logical_and(kj >= qi, blk > 0)
    valid_cur = kj <= qi
    scale = ATT_HD ** -0.5
    nt = (((1,), (1,)), ((), ()))
    lane = lax.broadcasted_iota(I32, (T, ATT_HD), 1)
    lse_all = jnp.zeros((T, ATT_HD), F32)
    raw = []
    for h in range(ATT_H):
        sl = slice(h * ATT_HD, (h + 1) * ATT_HD)
        qh = q_ref[:, sl]
        raw.append((lax.dot_general(qh, kp_ref[:, sl], nt, preferred_element_type=F32),
                    lax.dot_general(qh, kc_ref[:, sl], nt, preferred_element_type=F32)))
    for h in range(ATT_H):
        sl = slice(h * ATT_HD, (h + 1) * ATT_HD)
        s1 = jnp.where(valid_prev, raw[h][0] * scale, -jnp.inf)
        s2 = jnp.where(valid_cur, raw[h][1] * scale, -jnp.inf)
        mx = jnp.max(jnp.maximum(s1, s2), axis=1, keepdims=True)
        p1 = jnp.exp(s1 - mx)
        p2 = jnp.exp(s2 - mx)
        den = jnp.sum(p1 + p2, axis=1, keepdims=True)
        acc = jnp.dot(p1.astype(BF16), vp_ref[:, sl], preferred_element_type=F32)
        acc = acc + jnp.dot(p2.astype(BF16), vc_ref[:, sl], preferred_element_type=F32)
        o_ref[:, sl] = acc / den
        lse_all = jnp.where(lane == h, mx + jnp.log(den), lse_all)
    lse_ref[...] = lse_all


def _attn_prompt(q, kv, g):
    dil, L, _ = q.shape
    nb = L // ATT_STEPS
    T = ATT_STEPS
    blk = (None, T, ATT_GW)
    in_specs = [
        pl.BlockSpec(blk, lambda r, b: (r, b, 0)),
        pl.BlockSpec(blk, lambda r, b: (r, jnp.maximum(b - 1, 0), 0)),
        pl.BlockSpec(blk, lambda r, b: (r, b, 0)),
        pl.BlockSpec(blk, lambda r, b: (r, jnp.maximum(b - 1, 0), 1)),
        pl.BlockSpec(blk, lambda r, b: (r, b, 1)),
    ]
    out_specs = [pl.BlockSpec(blk, lambda r, b: (r, b, 0)), pl.BlockSpec((None, T, ATT_HD), lambda r, b: (r, b, 0))]
    return pl.pallas_call(
        _attn_prompt_kernel, grid=(dil, nb), in_specs=in_specs, out_specs=out_specs,
        out_shape=[jax.ShapeDtypeStruct((dil, L, ATT_GW), F32), jax.ShapeDtypeStruct((dil, L, ATT_HD), F32)],
        compiler_params=_cparams(("arbitrary", "arbitrary")), name=f"attn_prompt_g{g}",
    )(q, kv, kv, kv, kv)


def _merge_kernel(*refs):
    in_refs, out_ref, scratch = refs[:2 * ATT_G], refs[2 * ATT_G], refs[2 * ATT_G + 1:]
    tm = out_ref.shape[0]

    def position_order(ref, lanes, buf, dil):
        if dil == 1:
            return ref[0, :, lanes]
        for r in range(dil):
            buf[pl.ds(r, tm // dil, stride=dil), :] = ref[r, :, lanes]
        return buf[...]

    all128 = slice(0, ATT_HD)
    lses = [position_order(in_refs[2 * g + 1], all128, scratch[2 * g + 1], ATT_DILS[g]) for g in range(ATT_G)]
    mx = jnp.maximum(jnp.maximum(lses[0], lses[1]), lses[2])
    es = [jnp.exp(l - mx) for l in lses]
    tot = es[0] + es[1] + es[2]
    wgt = [e / tot for e in es]
    for c in range(ATT_H):
        lanes = slice(c * ATT_HD, (c + 1) * ATT_HD)
        acc = None
        for g in range(ATT_G):
            o = position_order(in_refs[2 * g], lanes, scratch[2 * g], ATT_DILS[g])
            term = wgt[g][:, c:c + 1] * o
            acc = term if acc is None else acc + term
        out_ref[:, lanes] = acc.astype(out_ref.dtype)


def _merge_groups(parts, seq, tm=512):
    in_specs, args, scratch = [], [], []
    for g, pair in enumerate(parts):
        dil = ATT_DILS[g]
        for a in pair:
            in_specs.append(pl.BlockSpec((dil, tm // dil, a.shape[2]), lambda i: (0, i, 0)))
            args.append(a)
            scratch.append(pltpu.VMEM((tm, ATT_HD), F32))
    return pl.pallas_call(
        _merge_kernel, grid=(seq // tm,), in_specs=in_specs,
        out_specs=pl.BlockSpec((tm, ATT_GW), lambda i: (i, 0)),
        out_shape=jax.ShapeDtypeStruct((seq, ATT_GW), BF16), scratch_shapes=scratch,
        compiler_params=_cparams(("arbitrary",)), name="attn_merge",
    )(*args)


def _attn_sample_kernel(q_ref, kvn_ref, b0_ref, b1_ref, b2_ref, out_ref):
    scale = ATT_HD ** -0.5
    outs, lses = [], []
    for g, b_ref in enumerate((b0_ref, b1_ref, b2_ref)):
        qg = q_ref[0, g]
        kn = kvn_ref[0, g, 0]
        vn = kvn_ref[0, g, 1]
        kb = b_ref[:, 0]
        vb = b_ref[:, 1]
        s = jnp.sum(kb * qg[None], axis=2, keepdims=True) * scale
        s_new = jnp.sum(kn * qg, axis=1, keepdims=True) * scale
        mx = jnp.maximum(jnp.max(s, axis=0), s_new)
        p = jnp.exp(s - mx[None])
        p_new = jnp.exp(s_new - mx)
        den = jnp.sum(p, axis=0) + p_new
        o = jnp.sum(p * vb, axis=0) + p_new * vn
        outs.append(o / den)
        lses.append(mx + jnp.log(den))
    mxl = jnp.maximum(jnp.maximum(lses[0], lses[1]), lses[2])
    es = [jnp.exp(l - mxl) for l in lses]
    tot = es[0] + es[1] + es[2]
    out_ref[0] = (es[0] / tot) * outs[0] + (es[1] / tot) * outs[1] + (es[2] / tot) * outs[2]


def _attn_sample(q_s, kv_s, caches):
    ns = q_s.shape[0]
    views, specs = [], []
    for g, cbuf in enumerate(caches):
        lb = cbuf.shape[1]
        dil = ATT_DILS[g]
        views.append(cbuf.reshape(ns, lb // dil, dil, 2, ATT_H, ATT_HD))
        specs.append(pl.BlockSpec((None, ATT_STEPS, None, 2, ATT_H, ATT_HD), lambda i: (i, 0, 0, 0, 0, 0)))
    in_specs = [
        pl.BlockSpec((1, ATT_G, ATT_H, ATT_HD), lambda i: (i, 0, 0, 0)),
        pl.BlockSpec((1, ATT_G, 2, ATT_H, ATT_HD), lambda i: (i, 0, 0, 0, 0)),
    ] + specs
    return pl.pallas_call(
        _attn_sample_kernel, grid=(ns,), in_specs=in_specs,
        out_specs=pl.BlockSpec((1, ATT_H, ATT_HD), lambda i: (i, 0, 0)),
        out_shape=jax.ShapeDtypeStruct((ns, ATT_H, ATT_HD), F32),
        compiler_params=_cparams(("arbitrary",)), name="attn_sample",
    )(q_s, kv_s, *views)


KV_SHIFT_BLOCK = 8192


def _kv_shift_kernel(cur_ref, nxt_ref, new_ref, out_ref):
    blk = out_ref.shape[1]
    out_ref[0, :blk - KV_ROW] = cur_ref[0, KV_ROW:]
    last = pl.program_id(1) == pl.num_programs(1) - 1
    out_ref[0, blk - KV_ROW:] = jnp.where(last, new_ref[0], nxt_ref[0])


def _kv_shift(cache, new):
    ns, lb = cache.shape[0], cache.shape[1]
    rows = lb * KV_ROW
    blk = min(KV_SHIFT_BLOCK, rows)
    nb = rows // blk
    per = blk // KV_ROW
    flat = cache.reshape(ns, rows, ATT_HD)
    out = pl.pallas_call(
        _kv_shift_kernel, grid=(ns, nb),
        in_specs=[pl.BlockSpec((1, blk, ATT_HD), lambda i, j: (i, j, 0)),
                  pl.BlockSpec((1, KV_ROW, ATT_HD), lambda i, j: (i, jnp.minimum((j + 1) * per, lb - 1), 0)),
                  pl.BlockSpec((1, KV_ROW, ATT_HD), lambda i, j: (i, 0, 0))],
        out_specs=pl.BlockSpec((1, blk, ATT_HD), lambda i, j: (i, j, 0)),
        out_shape=jax.ShapeDtypeStruct(flat.shape, flat.dtype),
        compiler_params=_cparams(("arbitrary", "arbitrary")), name="kv_shift",
    )(flat, flat, new.reshape(ns, KV_ROW, ATT_HD))
    return out.reshape(cache.shape)


def _top2(y, wr_ref, br_ref):
    rows = y.shape[0]
    lane = lax.broadcasted_iota(I32, (rows, 128), 1)
    logits = jnp.dot(y, wr_ref[...], precision=HIGHEST, preferred_element_type=F32) + br_ref[...]
    logits = jnp.where(lane < N_EXPERTS, logits, NEG_BIG)
    e = jnp.exp(logits - jnp.max(logits, axis=1, keepdims=True))
    probs = e / jnp.sum(e, axis=1, keepdims=True)
    p1 = jnp.max(probs, axis=1, keepdims=True)
    i1 = jnp.min(jnp.where(probs == p1, lane, 128), axis=1, keepdims=True)
    probs2 = jnp.where(lane == i1, -1.0, probs)
    p2 = jnp.max(probs2, axis=1, keepdims=True)
    i2 = jnp.min(jnp.where(probs2 == p2, lane, 128), axis=1, keepdims=True)
    tot = p1 + p2
    return lane, i1, i2, p1 / tot, p2 / tot


def _router_kernel(x_ref, xs_ref, g_ref, wr_ref, br_ref, xn_ref, eid_ref, gate_ref, rank_ref, cnt_ref,
                   xns_ref, gs_ref, carry):
    i = pl.program_id(0)
    tm = x_ref.shape[0]

    @pl.when(i == 0)
    def _():
        carry[...] = jnp.zeros_like(carry)
        ys = _rms(xs_ref[...]) * g_ref[...]
        xns_ref[...] = ys
        lane, i1, i2, g1, g2 = _top2(ys, wr_ref, br_ref)
        gs_ref[...] = jnp.where(lane == i1, g1, jnp.where(lane == i2, g2, 0.0))

    y = _rms(x_ref[...]) * g_ref[...]
    xn_ref[...] = y
    lane, i1, i2, g1, g2 = _top2(y, wr_ref, br_ref)
    sel1 = lane == i1
    sel2 = lane == i2
    onehot = jnp.where(jnp.logical_or(sel1, sel2), 1.0, 0.0)
    rr = lax.broadcasted_iota(I32, (tm, tm), 0)
    cc = lax.broadcasted_iota(I32, (tm, tm), 1)
    before = (cc < rr).astype(BF16)
    prefix = jnp.dot(before, onehot.astype(BF16), preferred_element_type=F32) + carry[...]
    r1 = jnp.sum(jnp.where(sel1, prefix, 0.0), axis=1, keepdims=True)
    r2 = jnp.sum(jnp.where(sel2, prefix, 0.0), axis=1, keepdims=True)
    carry[...] = carry[...] + jnp.sum(onehot, axis=0, keepdims=True)
    eid_ref[...] = jnp.where(lane == 0, i1, jnp.where(lane == 1, i2, 0))
    gate_ref[...] = jnp.where(lane == 0, g1, jnp.where(lane == 1, g2, 0.0))
    rank_ref[...] = jnp.where(lane == 0, r1, jnp.where(lane == 1, r2, 0.0)).astype(I32)
    cnt_ref[...] = jnp.broadcast_to(carry[...], cnt_ref.shape)


def _router(h, hs, gain, w_router_pad, b_router_pad, tm=ROW_TILE):
    m, d = h.shape
    s = hs.shape[0]
    const = lambda i: (0, 0)
    row_spec = pl.BlockSpec((tm, 128), lambda i: (i, 0))
    return pl.pallas_call(
        _router_kernel, grid=(m // tm,),
        in_specs=[pl.BlockSpec((tm, d), lambda i: (i, 0)), pl.BlockSpec((s, d), const), pl.BlockSpec((1, d), const),
                  pl.BlockSpec((d, 128), const), pl.BlockSpec((1, 128), const)],
        out_specs=[pl.BlockSpec((tm, d), lambda i: (i, 0)), row_spec, row_spec, row_spec,
                   pl.BlockSpec((8, 128), const), pl.BlockSpec((s, d), const), pl.BlockSpec((s, 128), const)],
        out_shape=[jax.ShapeDtypeStruct((m, d), F32), jax.ShapeDtypeStruct((m, 128), I32),
                   jax.ShapeDtypeStruct((m, 128), F32), jax.ShapeDtypeStruct((m, 128), I32),
                   jax.ShapeDtypeStruct((8, 128), F32), jax.ShapeDtypeStruct((s, d), F32),
                   jax.ShapeDtypeStruct((s, 128), F32)],
        scratch_shapes=[pltpu.VMEM((1, 128), F32)],
        compiler_params=_cparams(("arbitrary",)), name="router",
    )(h, hs, gain, w_router_pad, b_router_pad)


def _dispatch_kernel(pos_ref, nv_ref, x_hbm, out_ref, inv, buf, sem, *, n_tok):
    i = pl.program_id(0)
    tg = out_ref.shape[0]

    @pl.when(i == 0)
    def _():
        def clear(s, c):
            inv[s] = 0
            return c

        lax.fori_loop(0, inv.shape[0], clear, 0, unroll=8)

        def fill(t, c):
            inv[pos_ref[2 * t]] = t
            inv[pos_ref[2 * t + 1]] = t
            return c

        lax.fori_loop(0, n_tok, fill, 0, unroll=8)

    def start_gather(tile):
        slot = tile % 2

        def issue(r2, c):
            for j in range(2):
                r = 2 * r2 + j
                pltpu.make_async_copy(x_hbm.at[pl.ds(inv[tile * tg + r], 1)], buf.at[slot, pl.ds(r, 1)],
                                      sem.at[slot]).start(priority=j)
            return c

        lax.fori_loop(0, tg // 2, issue, 0, unroll=4)

    @pl.when(i == 0)
    def _():
        start_gather(i)

    @pl.when(i + 1 < nv_ref[0])
    def _():
        start_gather(i + 1)

    @pl.when(i < nv_ref[0])
    def _():
        slot = i % 2
        pltpu.make_async_copy(x_hbm.at[pl.ds(0, tg)], buf.at[slot], sem.at[slot]).wait()
        out_ref[...] = buf[slot].astype(out_ref.dtype)

    @pl.when(i >= nv_ref[0])
    def _():
        out_ref[...] = jnp.zeros_like(out_ref)


def _dispatch(pos_flat, nvalid, xn, n_tiles, tg):
    n_tok, d = xn.shape
    return pl.pallas_call(
        functools.partial(_dispatch_kernel, n_tok=n_tok),
        grid_spec=pltpu.PrefetchScalarGridSpec(
            num_scalar_prefetch=2, grid=(n_tiles,),
            in_specs=[pl.BlockSpec(memory_space=pl.ANY)],
            out_specs=pl.BlockSpec((tg, d), lambda i, pos, nv: (i, 0)),
            scratch_shapes=[pltpu.SMEM((n_tiles * tg,), I32), pltpu.VMEM((2, tg, d), xn.dtype),
                            pltpu.SemaphoreType.DMA((2,))]),
        out_shape=jax.ShapeDtypeStruct((n_tiles * tg, d), BF16),
        compiler_params=_cparams(("arbitrary",)), name="moe_dispatch",
    )(pos_flat, nvalid, xn)


def _combine_kernel(pos_ref, h_ref, gate_ref, g_ref, hs_ref, gs_ref, ys_s_ref, ys_hbm, out_ref, outs_ref, ybuf, sem):
    i = pl.program_id(0)
    tm = h_ref.shape[0]

    def start_gather(tile):
        slot = tile % 2

        def issue(r, carry):
            for j in range(2):
                pltpu.make_async_copy(ys_hbm.at[pl.ds(pos_ref[2 * (tile * tm + r) + j], 1)],
                                      ybuf.at[slot, j, pl.ds(r, 1)], sem.at[slot]).start(priority=j)
            return carry

        lax.fori_loop(0, tm, issue, 0, unroll=4)

    @pl.when(i == 0)
    def _():
        start_gather(i)

    @pl.when(i + 1 < pl.num_programs(0))
    def _():
        start_gather(i + 1)

    @pl.when(i == 0)
    def _():
        gs = gs_ref[...]
        y = jnp.zeros(hs_ref.shape, F32)
        for e in range(N_EXPERTS):
            y = y + gs[:, e:e + 1] * ys_s_ref[e]
        outs_ref[...] = _rms(hs_ref[...] + y) * g_ref[...]

    slot = i % 2
    for j in range(2):
        pltpu.make_async_copy(ys_hbm.at[pl.ds(0, tm)], ybuf.at[slot, j], sem.at[slot]).wait()
    gate = gate_ref[...]
    y = h_ref[...] + (gate[:, 0:1] * ybuf[slot, 0] + gate[:, 1:2] * ybuf[slot, 1])
    out_ref[...] = _rms(y) * g_ref[...]


def _combine(pos_flat, h, gate, g_final, hs, gs, ys_s, ys, tm=ROW_TILE):
    m, d = h.shape
    s = hs.shape[0]
    c2 = lambda i, pos: (0, 0)
    return pl.pallas_call(
        _combine_kernel,
        grid_spec=pltpu.PrefetchScalarGridSpec(
            num_scalar_prefetch=1, grid=(m // tm,),
            in_specs=[pl.BlockSpec((tm, d), lambda i, pos: (i, 0)),
                      pl.BlockSpec((tm, 128), lambda i, pos: (i, 0)),
                      pl.BlockSpec((1, d), c2),
                      pl.BlockSpec((s, d), c2),
                      pl.BlockSpec((s, 128), c2),
                      pl.BlockSpec((N_EXPERTS, s, d), lambda i, pos: (0, 0, 0)),
                      pl.BlockSpec(memory_space=pl.ANY)],
            out_specs=[pl.BlockSpec((tm, d), lambda i, pos: (i, 0)), pl.BlockSpec((s, d), c2)],
            scratch_shapes=[pltpu.VMEM((2, 2, tm, d), F32), pltpu.SemaphoreType.DMA((2,))]),
        out_shape=[jax.ShapeDtypeStruct((m, d), F32), jax.ShapeDtypeStruct((s, d), F32)],
        compiler_params=_cparams(("arbitrary",)), name="moe_combine",
    )(pos_flat, h, gate, g_final, hs, gs, ys_s, ys)


def _rope_tables(pos):
    half = ATT_HD // 2
    inv = ROPE_THETA ** (-jnp.arange(half, dtype=F32) / half)
    ang = pos.astype(F32)[:, None] * inv[None, :]
    cos, sin = jnp.cos(ang), jnp.sin(ang)
    return jnp.concatenate([cos, cos], axis=1), jnp.concatenate([-sin, sin], axis=1)


def kernel(x_prompt, x_sample, state_mlstm_C, state_mlstm_n, state_mlstm_m, cache_kv_w128, cache_kv_w512, cache_kv_w2048, g_mix, g_ffn, w_ml_in, b_ml_gates, b_ml_o, g_ml_hnorm, w_ml_out, g_kv, w_kv, w_q, w_o, w_ffn_gate, w_ffn_up, w_ffn_down, w_router, b_router, w_exp_gate, w_exp_up, w_exp_down, g_final):
    bp, seq, d = x_prompt.shape
    ns = x_sample.shape[0]
    caches = (cache_kv_w128, cache_kv_w512, cache_kv_w2048)
    assert bp == 1 and x_sample.shape[1] == 1 and d == D_MODEL and ns % 8 == 0
    assert seq % (ATT_STEPS * max(ATT_DILS)) == 0 and seq % 1024 == 0
    assert all(c.shape[1] == w for c, w in zip(caches, ATT_WINDOWS))
    tm = 512

    h0 = x_prompt.reshape(seq, d)
    h0_s = x_sample.reshape(ns, d)

    w_gates = lax.slice_in_dim(w_ml_in, ML_MAIN, ML_MAIN + 2 * ML_HEADS, axis=2)[0]
    w_gates_pad = jnp.zeros((d, 256), F32).at[:, 0:ML_HEADS].set(w_gates[:, :ML_HEADS])
    w_gates_pad = w_gates_pad.at[:, 128:128 + ML_HEADS].set(w_gates[:, ML_HEADS:])
    bi = jnp.zeros((1, 128), F32).at[0, :ML_HEADS].set(b_ml_gates[0, :ML_HEADS])
    bf = jnp.zeros((1, 128), F32).at[0, :ML_HEADS].set(b_ml_gates[0, ML_HEADS:])
    (xn0, gates), (xn0_s, gates_s) = _rmsnorm(h0, h0_s, g_mix[0:1], proj=w_gates_pad)
    k_scale = jnp.concatenate([jnp.ones((1, ML_NQK), F32), jnp.full((1, ML_NQK), ML_DQK ** -0.5, F32),
                               jnp.ones((1, ML_NV + D_MODEL), F32)], axis=1)
    z, z_s = _matmul(xn0, [w_ml_in], ML_MAIN, BF16, xn0_s[None], tm=1024, tn=1024, col_scale=k_scale, name="ml_in")
    bo = b_ml_o[0:1]
    ghn = g_ml_hnorm[0:1]
    hg, p_c, p_n, p_m = _mlstm_prompt(z[0], gates, bi, bf, bo, ghn)
    m0 = jnp.zeros((ns, 1, 128), F32).at[:, 0, :ML_HEADS].set(state_mlstm_m[0])
    hg_s, s_c, s_n, s_m = _mlstm_sample(z_s[0].reshape(ns, 1, ML_MAIN), gates_s.reshape(ns, 1, 256),
                                        state_mlstm_C[0], state_mlstm_n[0], m0, bi, bf, bo, ghn)
    h1, h1_s = _matmul(hg, [w_ml_out], d, F32, hg_s[None], tm=tm, tn=1024, res=h0, res_s=h0_s, name="ml_out")
    h1, h1_s = h1[0], h1_s[0]

    (xf0,), (xf0_s,) = _rmsnorm(h1, h1_s, g_ffn[0:1])
    ffn_dense = w_ffn_gate.shape[2]
    hid, hid_s = _matmul(xf0, [w_ffn_gate, w_ffn_up], ffn_dense, BF16, xf0_s[None], tm=1024, tn=512, name="ffn_up")
    h2, h2_s = _matmul(hid[0], [w_ffn_down], d, F32, hid_s, tm=tm, tn=512, res=h1, res_s=h1_s, name="ffn_down")
    h2, h2_s = h2[0], h2_s[0]

    (xq, xkv), (xq_s, xkv_s) = _rmsnorm(h2, h2_s, jnp.stack([g_mix[1], g_kv]))
    cos, sin = _rope_tables(jnp.arange(seq))
    cos_s, sin_s = _rope_tables(jnp.full((ns,), PAST_LEN, I32))
    rope_args = dict(cos=cos, sin=sin, cos_s=cos_s, sin_s=sin_s)
    parts, kv_nat, kv_new, q_new = [], [], [], []
    tail = min(max(ATT_WINDOWS), seq)
    for g in range(ATT_G):
        dil = ATT_DILS[g]
        kvd, kvn, kv_s = _matmul(xkv, [w_kv[None]], 2 * ATT_GW, BF16, xkv_s[None], tm=1024, tn=ATT_GW,
                                 col_off=2 * g, rope="even", dil=dil, natural_tail=tail, name=f"kv_proj_g{g}",
                                 **rope_args)
        qd, q_s = _matmul(xq, [w_q], ATT_GW, BF16, xq_s[None], tm=1024, tn=ATT_GW, col_off=g, rope="all",
                          dil=dil, name=f"q_proj_g{g}", **rope_args)
        parts.append(_attn_prompt(qd, kvd, g))
        kv_nat.append(kvn)
        kv_new.append(kv_s[0])
        q_new.append(q_s[0])
    att = _merge_groups(parts, seq)
    att_s = _attn_sample(jnp.stack(q_new, axis=1).reshape(ns, ATT_G, ATT_H, ATT_HD),
                         jnp.stack(kv_new, axis=1).reshape(ns, ATT_G, 2, ATT_H, ATT_HD), caches)
    h3, h3_s = _matmul(att, [w_o], d, F32, att_s.reshape(1, ns, ATT_GW), tm=tm, tn=2048, res=h2, res_s=h2_s,
                       name="attn_out")
    h3, h3_s = h3[0], h3_s[0]

    wr_pad = jnp.zeros((d, 128), F32).at[:, :N_EXPERTS].set(w_router[0])
    br_pad = jnp.zeros((1, 128), F32).at[0, :N_EXPERTS].set(b_router[0])
    xn2, eid, gate, rank, cnt, xn2_s, gates_moe_s = _router(h3, h3_s, g_ffn[1:2], wr_pad, br_pad)
    tg = MOE_TILE
    n_tiles = -(-(2 * seq + N_EXPERTS * (tg - 1)) // tg)
    counts = cnt[0, :N_EXPERTS].astype(I32)
    padded = jnp.maximum((counts + tg - 1) // tg, 1) * tg
    gend = jnp.cumsum(padded)
    gstart = gend - padded
    pos = (gstart[eid[:, :2]] + rank[:, :2]).astype(I32).reshape(-1)
    nvalid = (gend[-1] // tg).astype(I32).reshape(1)
    xs = _dispatch(pos, nvalid, xn2, n_tiles, tg)
    t0 = (gstart // tg).astype(I32)
    tcnt = (padded // tg).astype(I32)
    hs, hs_s, sbuf2 = _moe_matmul(xs, [w_exp_gate[0], w_exp_up[0]], BF16, xn2_s[None], t0, tcnt, nvalid, tm=tg,
                                  tn=1024, xs_per_expert=False, name="moe_up", window=(caches[2], kv_new[2]))
    ys, ys_s, sbuf1 = _moe_matmul(hs, [w_exp_down[0]], F32, hs_s, t0, tcnt, nvalid, tm=tg, tn=512,
                                  xs_per_expert=True, name="moe_down", window=(caches[1], kv_new[1]))
    s_bufs = [_kv_shift(caches[0], kv_new[0]), sbuf1, sbuf2]
    y_p, y_s = _combine(pos, h3, gate, g_final.reshape(1, d), h3_s, gates_moe_s, ys_s, ys)

    p_bufs = []
    for g in range(ATT_G):
        keep = min(ATT_WINDOWS[g], seq)
        p_bufs.append(kv_nat[g][tail - keep:].reshape(1, keep, 2, ATT_H, ATT_HD))
    return (y_p.reshape(1, seq, d), y_s.reshape(ns, 1, d),
            p_c[None, None], p_n[None, None], p_m[:, :ML_HEADS][None],
            s_c[None], s_n[None], s_m[:, 0, :ML_HEADS][None],
            p_bufs[0], p_bufs[1], p_bufs[2], s_bufs[0], s_bufs[1], s_bufs[2])
```

```python
import functools

import jax
import jax.numpy as jnp
from jax import lax
from jax.experimental import pallas as pl
from jax.experimental.pallas import tpu as pltpu

F32 = jnp.float32
BF16 = jnp.bfloat16
I32 = jnp.int32
HIGHEST = lax.Precision.HIGHEST

D_MODEL = 2048
ML_HEADS = 8
ML_DQK = 128
ML_DV = 256
ML_NQK = ML_HEADS * ML_DQK
ML_NV = ML_HEADS * ML_DV
ML_MAIN = 2 * ML_NQK + ML_NV + D_MODEL
ML_CHUNK = 256
ATT_HD = 128
ATT_H = 8
ATT_G = 3
ATT_WINDOWS = (128, 512, 2048)
ATT_DILS = (1, 4, 16)
ATT_STEPS = 128
ATT_GW = ATT_H * ATT_HD
ROPE_THETA = 10000.0
PAST_LEN = 8192
N_EXPERTS = 8
RMS_EPS = 1e-6
NEG_BIG = -1e30

VMEM_LIMIT_BYTES = 58 * 1024 * 1024
ROW_TILE = 512
MOE_TILE = 256


def _cparams(sem):
    return pltpu.CompilerParams(dimension_semantics=sem, vmem_limit_bytes=VMEM_LIMIT_BYTES)


def _rms(x):
    return x * lax.rsqrt(jnp.mean(x * x, axis=-1, keepdims=True) + RMS_EPS)


def _norm_kernel(x_ref, xs_ref, g_ref, *refs, n_out, has_proj):
    ins = 1 if has_proj else 0
    n_each = n_out + ins
    main = refs[ins:ins + n_each]
    side = refs[ins + n_each:ins + 2 * n_each]

    def emit(x, outs):
        y = _rms(x)
        for i in range(n_out):
            outs[i][...] = (y * g_ref[i:i + 1, :]).astype(outs[i].dtype)
        if has_proj:
            outs[n_out][...] = jnp.dot(y * g_ref[0:1, :], refs[0][...], precision=HIGHEST,
                                       preferred_element_type=F32)

    emit(x_ref[...], main)

    @pl.when(pl.program_id(0) == 0)
    def _():
        emit(xs_ref[...], side)


def _rmsnorm(x, xs, gains, proj=None, tm=ROW_TILE):
    m, d = x.shape
    s = xs.shape[0]
    n_out = gains.shape[0]
    const = lambda i: (0, 0)
    in_specs = [pl.BlockSpec((tm, d), lambda i: (i, 0)), pl.BlockSpec((s, d), const),
                pl.BlockSpec((n_out, d), const)]
    args = [x, xs, gains]
    main_shape = [jax.ShapeDtypeStruct((m, d), BF16)] * n_out
    main_specs = [pl.BlockSpec((tm, d), lambda i: (i, 0))] * n_out
    side_shape = [jax.ShapeDtypeStruct((s, d), F32)] * n_out
    side_specs = [pl.BlockSpec((s, d), const)] * n_out
    if proj is not None:
        p = proj.shape[1]
        in_specs.append(pl.BlockSpec((d, p), const))
        args.append(proj)
        main_shape.append(jax.ShapeDtypeStruct((m, p), F32))
        main_specs.append(pl.BlockSpec((tm, p), lambda i: (i, 0)))
        side_shape.append(jax.ShapeDtypeStruct((s, p), F32))
        side_specs.append(pl.BlockSpec((s, p), const))
    outs = pl.pallas_call(
        functools.partial(_norm_kernel, n_out=n_out, has_proj=proj is not None),
        grid=(m // tm,), in_specs=in_specs, out_specs=main_specs + side_specs,
        out_shape=main_shape + side_shape,
        compiler_params=_cparams(("arbitrary",)), name="rmsnorm",
    )(*args)
    k = len(main_shape)
    return outs[:k], outs[k:]


def _rope_heads(acc, cos, sin):
    outs = []
    for h in range(acc.shape[1] // ATT_HD):
        a = acc[:, h * ATT_HD:(h + 1) * ATT_HD]
        outs.append(a * cos + pltpu.roll(a, ATT_HD // 2, 1) * sin)
    return jnp.concatenate(outs, axis=1)


def _mm_kernel(te_ref, nv_ref, x_ref, *refs, n_w, has_scale, has_res, rope, dil, nat_first):
    n = pl.program_id(0)
    m = pl.program_id(1)
    it = iter(refs)
    w_refs = [next(it) for _ in range(n_w)]
    scale_ref = next(it) if has_scale else None
    res_ref = next(it) if has_res else None
    cos_ref, sin_ref = (next(it), next(it)) if rope else (None, None)
    xs_ref = next(it)
    res_s_ref = next(it) if has_res else None
    cos_s_ref, sin_s_ref = (next(it), next(it)) if rope else (None, None)
    o_ref = next(it)
    nat_ref = next(it) if nat_first is not None else None
    os_ref = next(it)
    wb_refs = [next(it) for _ in range(n_w)]
    deint = next(it) if dil > 1 else None

    def finish(acc, up, res, cos, sin, store):
        if n_w == 2:
            acc = (acc * jax.nn.sigmoid(acc)) * up
        if has_scale:
            acc = acc * scale_ref[...]
        if has_res:
            acc = acc + res
        if rope == "all":
            store(_rope_heads(acc, cos, sin))
        elif rope == "even":
            @pl.when(n % 2 == 0)
            def _():
                store(_rope_heads(acc, cos, sin))

            @pl.when(n % 2 == 1)
            def _():
                store(acc)
        else:
            store(acc)

    def store_side(val):
        os_ref[0] = val

    def store_main(val):
        if nat_first is not None:
            @pl.when(m >= nat_first)
            def _():
                nat_ref[...] = val
        if dil == 1:
            o_ref[0] = val.astype(o_ref.dtype)
        else:
            rows = deint.shape[1] // dil
            for c in range(deint.shape[0]):
                lanes = slice(c * 128, (c + 1) * 128)
                deint[c] = val[:, lanes]
                for r in range(dil):
                    o_ref[r, :, lanes] = deint[c, pl.ds(r, rows, stride=dil), :].astype(o_ref.dtype)

    prev = jnp.maximum(m - 1, 0)
    new_weights = jnp.logical_or(m == 0, te_ref[m] != te_ref[prev])

    @pl.when(new_weights)
    def _():
        for w_ref, wb_ref in zip(w_refs, wb_refs):
            wb_ref[...] = w_ref[0].astype(BF16)
        xs = xs_ref[0]
        s_rows = xs.shape[0]
        xh = xs.astype(BF16)
        xl = (xs - xh.astype(F32)).astype(BF16)
        x_hl = jnp.concatenate([xh, xl], axis=0)

        def side_product(i):
            wh = wb_refs[i][...]
            wl = (w_refs[i][0] - wh.astype(F32)).astype(BF16)
            both = jnp.dot(x_hl, wh, preferred_element_type=F32)
            return both[:s_rows] + both[s_rows:] + jnp.dot(xh, wl, preferred_element_type=F32)

        acc = side_product(0)
        up = side_product(1) if n_w == 2 else None
        finish(acc, up, res_s_ref[...] if has_res else None,
               cos_s_ref[...] if rope else None, sin_s_ref[...] if rope else None, store_side)

    @pl.when(m >= nv_ref[0])
    def _():
        o_ref[...] = jnp.zeros_like(o_ref)

    @pl.when(m < nv_ref[0])
    def _():
        xb = x_ref[...].astype(BF16)
        acc = jnp.dot(xb, wb_refs[0][...], preferred_element_type=F32)
        up = jnp.dot(xb, wb_refs[1][...], preferred_element_type=F32) if n_w == 2 else None
        finish(acc, up, res_ref[...] if has_res else None,
               cos_ref[...] if rope else None, sin_ref[...] if rope else None, store_main)


def _matmul(x, ws, n_cols, out_dtype, xs, *, tm, tn, col_off=0, te=None, nvalid=None, xs_per_expert=False,
            col_scale=None, res=None, res_s=None, rope=None, cos=None, sin=None, cos_s=None, sin_s=None,
            dil=1, natural_tail=0, name="matmul"):
    m, k = x.shape
    s = xs.shape[1]
    n_m = m // tm
    n_n = n_cols // tn
    n_e = ws[0].shape[0]
    if te is None:
        te = jnp.zeros((n_m,), I32)
        nvalid = jnp.full((1,), n_m, I32)

    def row(mi, nv):
        return jnp.minimum(mi, nv[0] - 1)

    def exp(mi, te, nv):
        return te[row(mi, nv)]

    in_specs = [pl.BlockSpec((tm, k), lambda n, mi, te, nv: (row(mi, nv), 0))]
    args = [x]
    for w in ws:
        in_specs.append(pl.BlockSpec((1, k, tn), lambda n, mi, te, nv: (exp(mi, te, nv), 0, n + col_off)))
        args.append(w)
    if col_scale is not None:
        in_specs.append(pl.BlockSpec((1, tn), lambda n, mi, te, nv: (0, n)))
        args.append(col_scale)
    if res is not None:
        in_specs.append(pl.BlockSpec((tm, tn), lambda n, mi, te, nv: (row(mi, nv), n)))
        args.append(res)
    if rope is not None:
        for t in (cos, sin):
            in_specs.append(pl.BlockSpec((tm, ATT_HD), lambda n, mi, te, nv: (row(mi, nv), 0)))
            args.append(t)
    if xs_per_expert:
        in_specs.append(pl.BlockSpec((1, s, k), lambda n, mi, te, nv: (exp(mi, te, nv), 0, 0)))
    else:
        in_specs.append(pl.BlockSpec((1, s, k), lambda n, mi, te, nv: (0, 0, 0)))
    args.append(xs)
    if res is not None:
        in_specs.append(pl.BlockSpec((s, tn), lambda n, mi, te, nv: (0, n)))
        args.append(res_s)
    if rope is not None:
        for t in (cos_s, sin_s):
            in_specs.append(pl.BlockSpec((s, ATT_HD), lambda n, mi, te, nv: (0, 0)))
            args.append(t)

    out_shape = [jax.ShapeDtypeStruct((dil, m // dil, n_cols), out_dtype)]
    out_specs = [pl.BlockSpec((dil, tm // dil, tn), lambda n, mi, te, nv: (0, mi, n))]
    nat_first = None
    if natural_tail:
        nat_first = n_m - natural_tail // tm
        out_shape.append(jax.ShapeDtypeStruct((natural_tail, n_cols), F32))
        out_specs.append(pl.BlockSpec((tm, tn), lambda n, mi, te, nv: (jnp.maximum(mi - nat_first, 0), n)))
    out_shape.append(jax.ShapeDtypeStruct((n_e, s, n_cols), F32))
    out_specs.append(pl.BlockSpec((1, s, tn), lambda n, mi, te, nv: (exp(mi, te, nv), 0, n)))
    scratch = [pltpu.VMEM((k, tn), BF16) for _ in ws]
    if dil > 1:
        scratch.append(pltpu.VMEM((tn // 128, tm, 128), F32))
    kern = functools.partial(_mm_kernel, n_w=len(ws), has_scale=col_scale is not None, has_res=res is not None,
                             rope=rope, dil=dil, nat_first=nat_first)
    return pl.pallas_call(
        kern,
        grid_spec=pltpu.PrefetchScalarGridSpec(
            num_scalar_prefetch=2, grid=(n_n, n_m), in_specs=in_specs, out_specs=out_specs,
            scratch_shapes=scratch),
        out_shape=out_shape,
        compiler_params=_cparams(("arbitrary", "arbitrary")), name=name,
    )(te, nvalid, *args)


KV_ROW = 2 * ATT_H
RING_SLOTS = 3
SHIFT_SEQS = 2
SHIFT_MAX_ROWS = 128


def _shift_chunk_rows(window):
    moved = window - 1
    return max(d for d in range(1, SHIFT_MAX_ROWS + 1) if moved % d == 0)


class _WindowShift:
    def __init__(self, buf_hbm, new_hbm, out_hbm, stage, sem_in, sem_out, sem_new, next_chunk):
        self.buf, self.new, self.out, self.stage = buf_hbm, new_hbm, out_hbm, stage
        self.sem_in, self.sem_out, self.sem_new, self.next_chunk = sem_in, sem_out, sem_new, next_chunk
        ns, self.rows = buf_hbm.shape[0], buf_hbm.shape[1]
        self.chunk = stage.shape[2]
        self.per_group = (self.rows - KV_ROW) // self.chunk
        self.total = (ns // SHIFT_SEQS) * self.per_group

    def _load(self, k, slot):
        seqs = pl.ds((k // self.per_group) * SHIFT_SEQS, SHIFT_SEQS)
        src = self.buf.at[seqs, pl.ds(KV_ROW + (k % self.per_group) * self.chunk, self.chunk)]
        return pltpu.make_async_copy(src, self.stage.at[slot], self.sem_in.at[slot])

    def _store(self, k, slot):
        seqs = pl.ds((k // self.per_group) * SHIFT_SEQS, SHIFT_SEQS)
        dst = self.out.at[seqs, pl.ds((k % self.per_group) * self.chunk, self.chunk)]
        return pltpu.make_async_copy(self.stage.at[slot], dst, self.sem_out.at[slot])

    def _append(self):
        return pltpu.make_async_copy(self.new, self.out.at[:, pl.ds(self.rows - KV_ROW, KV_ROW)], self.sem_new.at[0])

    def begin(self):
        self.next_chunk[0] = 0
        for k in range(RING_SLOTS - 1):
            self._load(k, k).start(priority=1)
        self._append().start()

    def advance(self):
        k = self.next_chunk[0]

        @pl.when(k < self.total)
        def _():
            self._load(k, k % RING_SLOTS).wait()

            @pl.when(k >= 1)
            def _():
                self._store(k - 1, (k - 1) % RING_SLOTS).wait()

            self._store(k, k % RING_SLOTS).start()
            ahead = k + RING_SLOTS - 1

            @pl.when(ahead < self.total)
            def _():
                self._load(ahead, ahead % RING_SLOTS).start(priority=1)

            self.next_chunk[0] = k + 1

    def finish(self):
        def rest(i, carry):
            self.advance()
            return carry

        lax.fori_loop(self.next_chunk[0], self.total, rest, 0)
        self._store(self.total - 1, (self.total - 1) % RING_SLOTS).wait()
        self._append().wait()


def _moe_mm_kernel(t0_ref, cnt_ref, nv_ref, x_hbm, *refs, n_w, tm, n_tiles, n_bg):
    n = pl.program_id(0)
    e = pl.program_id(1)
    n_e = pl.num_programs(1)
    it = iter(refs)
    w_refs = [next(it) for _ in range(n_w)]
    xs_ref = next(it)
    bg_in = [next(it) for _ in range(2 * n_bg)]
    o_hbm, os_ref = next(it), next(it)
    bg_out = [next(it) for _ in range(n_bg)]
    wb_refs = [next(it) for _ in range(n_w)]
    xbuf, obuf, sem_in, sem_out = next(it), next(it), next(it), next(it)
    tn = os_ref.shape[2]
    t0 = t0_ref[e]
    cnt = cnt_ref[e]
    first_step = jnp.logical_and(n == 0, e == 0)
    last_step = jnp.logical_and(n == pl.num_programs(0) - 1, e == n_e - 1)

    shift = _WindowShift(bg_in[0], bg_in[1], bg_out[0], *[next(it) for _ in range(5)]) if n_bg else None
    if shift:
        pl.when(first_step)(shift.begin)

    def x_copy(tile, slot):
        return pltpu.make_async_copy(x_hbm.at[pl.ds(tile * tm, tm)], xbuf.at[slot], sem_in.at[slot])

    def o_copy(tile, slot):
        return pltpu.make_async_copy(obuf.at[slot], o_hbm.at[pl.ds(tile * tm, tm), pl.ds(n * tn, tn)],
                                     sem_out.at[slot])

    def product(xb):
        acc = jnp.dot(xb, wb_refs[0][...], preferred_element_type=F32)
        if n_w == 2:
            acc = (acc * jax.nn.sigmoid(acc)) * jnp.dot(xb, wb_refs[1][...], preferred_element_type=F32)
        return acc

    def start_first_tiles(first_tile, n_tiles_here):
        for j in range(RING_SLOTS - 1):
            @pl.when(j < n_tiles_here)
            def _():
                x_copy(first_tile + j, j).start(priority=1)

    @pl.when(first_step)
    def _():
        start_first_tiles(t0, cnt)

    for w_ref, wb_ref in zip(w_refs, wb_refs):
        wb_ref[...] = w_ref[0].astype(BF16)
    os_ref[0] = product(xs_ref[0].astype(BF16))

    def body(t, carry):
        slot = t % 2
        ahead = t + RING_SLOTS - 1

        @pl.when(ahead < cnt)
        def _():
            x_copy(t0 + ahead, ahead % RING_SLOTS).start(priority=1)

        x_copy(t0 + t, t % RING_SLOTS).wait()

        @pl.when(t >= 2)
        def _():
            o_copy(t0 + t - 2, slot).wait()

        obuf[slot] = product(xbuf[t % RING_SLOTS]).astype(obuf.dtype)
        o_copy(t0 + t, slot).start()
        if shift:
            shift.advance()
        return carry

    lax.fori_loop(0, cnt, body, 0)

    @pl.when(jnp.logical_not(last_step))
    def _():
        e_next = jnp.where(e == n_e - 1, 0, e + 1)
        start_first_tiles(t0_ref[e_next], cnt_ref[e_next])

    @pl.when(cnt >= 2)
    def _():
        o_copy(t0 + cnt - 2, cnt % 2).wait()

    o_copy(t0 + cnt - 1, (cnt - 1) % 2).wait()

    @pl.when(e == n_e - 1)
    def _():
        obuf[0] = jnp.zeros(obuf.shape[1:], obuf.dtype)

        def fill(tile, carry):
            o_copy(tile, 0).start()
            o_copy(tile, 0).wait()
            return carry

        lax.fori_loop(nv_ref[0], n_tiles, fill, 0)

    if shift:
        pl.when(last_step)(shift.finish)


def _moe_matmul(x, ws, out_dtype, xs, t0, cnt, nvalid, *, tm, tn, xs_per_expert, name, window=None):
    r, k = x.shape
    n_e, _, n_cols = ws[0].shape
    s = xs.shape[1]
    n_w = len(ws)
    windows = []
    if window is not None:
        cache, new = window
        ns, lb = cache.shape[0], cache.shape[1]
        assert ns % SHIFT_SEQS == 0
        windows = [(cache.reshape(ns, lb * KV_ROW, ATT_HD), new.reshape(ns, KV_ROW, ATT_HD))]
        chunk = _shift_chunk_rows(lb) * KV_ROW
    n_bg = len(windows)
    any_spec = pl.BlockSpec(memory_space=pl.ANY)
    in_specs = [any_spec]
    in_specs += [pl.BlockSpec((1, k, tn), lambda n, e, *_: (e, 0, n)) for _ in ws]
    if xs_per_expert:
        in_specs.append(pl.BlockSpec((1, s, k), lambda n, e, *_: (e, 0, 0)))
    else:
        in_specs.append(pl.BlockSpec((1, s, k), lambda n, e, *_: (0, 0, 0)))
    in_specs += [any_spec] * (2 * n_bg)
    scratch = [pltpu.VMEM((k, tn), BF16) for _ in ws]
    scratch += [pltpu.VMEM((RING_SLOTS, tm, k), x.dtype), pltpu.VMEM((2, tm, tn), out_dtype),
                pltpu.SemaphoreType.DMA((RING_SLOTS,)), pltpu.SemaphoreType.DMA((2,))]
    if n_bg:
        scratch += [pltpu.VMEM((RING_SLOTS, SHIFT_SEQS, chunk, ATT_HD), F32),
                    pltpu.SemaphoreType.DMA((RING_SLOTS,)), pltpu.SemaphoreType.DMA((RING_SLOTS,)),
                    pltpu.SemaphoreType.DMA((1,)), pltpu.SMEM((1,), I32)]
    out_shape = [jax.ShapeDtypeStruct((r, n_cols), out_dtype), jax.ShapeDtypeStruct((n_e, s, n_cols), F32)]
    out_shape += [jax.ShapeDtypeStruct(buf.shape, buf.dtype) for buf, _ in windows]
    outs = pl.pallas_call(
        functools.partial(_moe_mm_kernel, n_w=n_w, tm=tm, n_tiles=r // tm, n_bg=n_bg),
        grid_spec=pltpu.PrefetchScalarGridSpec(
            num_scalar_prefetch=3, grid=(n_cols // tn, n_e), in_specs=in_specs,
            out_specs=[any_spec, pl.BlockSpec((1, s, tn), lambda n, e, *_: (e, 0, n))] + [any_spec] * n_bg,
            scratch_shapes=scratch),
        out_shape=out_shape,
        compiler_params=_cparams(("arbitrary", "arbitrary")), name=name,
    )(t0, cnt, nvalid, x, *ws, xs, *[a for pair in windows for a in pair])
    if window is not None:
        return outs[0], outs[1], outs[2].reshape(window[0].shape)
    return outs


def _log_sigmoid(x):
    return jnp.minimum(x, 0.0) - jnp.log1p(jnp.exp(-jnp.abs(x)))


def _mlstm_prompt_kernel(q_ref, k_ref, v_ref, o_ref, gi_ref, gf_ref, bi_ref, bf_ref, bo_ref, ghn_ref,
                         h_ref, c_out_ref, n_out_ref, m_out_ref, ct_s, n_s, m_s):
    c = pl.program_id(0)
    L = ML_CHUNK

    @pl.when(c == 0)
    def _():
        ct_s[...] = jnp.zeros_like(ct_s)
        n_s[...] = jnp.zeros_like(n_s)
        m_s[...] = jnp.zeros_like(m_s)

    ig = gi_ref[...] + bi_ref[...]
    lf = _log_sigmoid(gf_ref[...] + bf_ref[...])
    r = lax.broadcasted_iota(I32, (L, L), 0)
    s = lax.broadcasted_iota(I32, (L, L), 1)
    causal = r >= s
    tril = causal.astype(F32)
    b = jnp.dot(tril, lf, precision=HIGHEST, preferred_element_type=F32)
    b_t = b.T
    ig_t = ig.T
    m_all = m_s[...]
    m_new_all = m_all
    lane = lax.broadcasted_iota(I32, (1, 128), 1)

    H = range(ML_HEADS)
    q = [q_ref[:, h * ML_DQK:(h + 1) * ML_DQK] for h in H]
    k = [k_ref[:, h * ML_DQK:(h + 1) * ML_DQK] for h in H]
    v = [v_ref[:, h * ML_DV:(h + 1) * ML_DV] for h in H]
    ct = [ct_s[h] for h in H]
    n_old = [n_s[h:h + 1, :] for h in H]
    bc = [b[:, h:h + 1] for h in H]
    m_old = [m_all[:, h:h + 1] for h in H]

    sc = [lax.dot_general(q[h], k[h], (((1,), (1,)), ((), ())), preferred_element_type=F32) for h in H]
    qc = [jnp.dot(q[h], ct[h].astype(BF16), preferred_element_type=F32) for h in H]

    logd = [jnp.where(causal, bc[h] - b_t[h:h + 1, :] + ig_t[h:h + 1, :], -jnp.inf) for h in H]
    inter = [bc[h] + m_old[h] for h in H]
    mt = [jnp.maximum(inter[h], jnp.max(logd[h], axis=1, keepdims=True)) for h in H]
    w_inter = [jnp.exp(inter[h] - mt[h]) for h in H]
    m_new = [mt[h][L - 1:L, :] for h in H]
    b_last = [bc[h][L - 1:L, :] for h in H]
    decay = [jnp.exp(b_last[h] + m_old[h] - m_new[h]) for h in H]
    wj = [jnp.exp(b_last[h] - bc[h] + ig[:, h:h + 1] - m_new[h]) for h in H]

    a = [sc[h] * jnp.exp(logd[h] - mt[h]) for h in H]
    qn = [jnp.sum(q[h].astype(F32) * n_old[h], axis=1, keepdims=True) for h in H]
    den = [jnp.sum(a[h], axis=1, keepdims=True) + w_inter[h] * qn[h] for h in H]
    num = [jnp.dot(a[h].astype(BF16), v[h], preferred_element_type=F32) + w_inter[h] * qc[h] for h in H]

    hh = [num[h] / jnp.maximum(jnp.abs(den[h]), jnp.exp(-mt[h])) for h in H]
    for h in H:
        cols = slice(h * ML_DV, (h + 1) * ML_DV)
        og = jax.nn.sigmoid(o_ref[:, cols].astype(F32) + bo_ref[:, cols])
        h_ref[:, cols] = (_rms(hh[h]) * ghn_ref[:, cols] * og).astype(h_ref.dtype)

    kf = [k[h].astype(F32) * wj[h] for h in H]
    for h in H:
        upd = lax.dot_general(kf[h].astype(BF16), v[h], (((0,), (0,)), ((), ())), preferred_element_type=F32)
        ct_s[h] = decay[h] * ct[h] + upd
        n_s[h:h + 1, :] = decay[h] * n_old[h] + jnp.sum(kf[h], axis=0, keepdims=True)
        m_new_all = jnp.where(lane == h, m_new[h], m_new_all)

    m_s[...] = m_new_all

    @pl.when(c == pl.num_programs(0) - 1)
    def _():
        for h in range(ML_HEADS):
            c_out_ref[h] = ct_s[h].T
        n_out_ref[...] = n_s[...]
        m_out_ref[...] = m_s[...]


def _mlstm_prompt(z, gates, bi, bf, bo, ghn):
    seq = z.shape[0]
    nc = seq // ML_CHUNK
    L = ML_CHUNK
    const2 = lambda c: (0, 0)
    in_specs = [
        pl.BlockSpec((L, ML_NQK), lambda c: (c, 0)),
        pl.BlockSpec((L, ML_NQK), lambda c: (c, 1)),
        pl.BlockSpec((L, ML_NV), lambda c: (c, 1)),
        pl.BlockSpec((L, D_MODEL), lambda c: (c, 2)),
        pl.BlockSpec((L, 128), lambda c: (c, 0)),
        pl.BlockSpec((L, 128), lambda c: (c, 1)),
        pl.BlockSpec((1, 128), const2),
        pl.BlockSpec((1, 128), const2),
        pl.BlockSpec((1, D_MODEL), const2),
        pl.BlockSpec((1, ML_NV), const2),
    ]
    out_shape = [
        jax.ShapeDtypeStruct((seq, ML_NV), BF16),
        jax.ShapeDtypeStruct((ML_HEADS, ML_DV, ML_DQK), F32),
        jax.ShapeDtypeStruct((ML_HEADS, ML_DQK), F32),
        jax.ShapeDtypeStruct((1, 128), F32),
    ]
    out_specs = [
        pl.BlockSpec((L, ML_NV), lambda c: (c, 0)),
        pl.BlockSpec((ML_HEADS, ML_DV, ML_DQK), lambda c: (0, 0, 0)),
        pl.BlockSpec((ML_HEADS, ML_DQK), const2),
        pl.BlockSpec((1, 128), const2),
    ]
    return pl.pallas_call(
        _mlstm_prompt_kernel, grid=(nc,), in_specs=in_specs, out_specs=out_specs, out_shape=out_shape,
        scratch_shapes=[pltpu.VMEM((ML_HEADS, ML_DQK, ML_DV), F32), pltpu.VMEM((ML_HEADS, ML_DQK), F32),
                        pltpu.VMEM((1, 128), F32)],
        compiler_params=_cparams(("arbitrary",)), name="mlstm_prompt",
    )(z, z, z, z, gates, gates, bi, bf, bo, ghn)


def _mlstm_sample_kernel(z_ref, g_ref, c_ref, n_ref, m_ref, bi_ref, bf_ref, bo_ref, ghn_ref,
                         h_ref, c_out_ref, n_out_ref, m_out_ref):
    i = pl.program_id(0)
    z = z_ref[0]
    g = g_ref[0]
    ig_all = g[:, 0:128] + bi_ref[...]
    lf_all = _log_sigmoid(g[:, 128:256] + bf_ref[...])
    m_all = m_ref[0]
    mt_all = jnp.maximum(lf_all + m_all, ig_all)
    m_out_ref[0] = mt_all
    outs = []
    for h in range(ML_HEADS):
        q = z[:, h * ML_DQK:(h + 1) * ML_DQK]
        k = z[:, ML_NQK + h * ML_DQK:ML_NQK + (h + 1) * ML_DQK]
        v = z[:, 2 * ML_NQK + h * ML_DV:2 * ML_NQK + (h + 1) * ML_DV]
        ig = ig_all[:, h:h + 1]
        lf = lf_all[:, h:h + 1]
        m0 = m_all[:, h:h + 1]
        mt = mt_all[:, h:h + 1]
        w_inter = jnp.exp(lf + m0 - mt)
        wj = jnp.exp(ig - mt)
        a = jnp.sum(q * k, axis=1, keepdims=True) * wj
        c_h = c_ref[0, h]
        n_h = n_ref[0, h:h + 1, :]
        q8 = jnp.broadcast_to(q, (8, ML_DQK))
        cq = lax.dot_general(q8, c_h, (((1,), (1,)), ((), ())), precision=HIGHEST,
                             preferred_element_type=F32)[0:1, :]
        num = a * v + w_inter * cq
        den = a + w_inter * jnp.sum(n_h * q, axis=1, keepdims=True)
        hh = num / jnp.maximum(jnp.abs(den), jnp.exp(-mt))
        hn = _rms(hh) * ghn_ref[:, h * ML_DV:(h + 1) * ML_DV]
        og = jax.nn.sigmoid(z[:, 2 * ML_NQK + ML_NV + h * ML_DV:2 * ML_NQK + ML_NV + (h + 1) * ML_DV]
                            + bo_ref[:, h * ML_DV:(h + 1) * ML_DV])
        outs.append(hn * og)
        v_col = jnp.broadcast_to(v, (8, ML_DV)).T[:, 0:1]
        c_out_ref[0, h] = w_inter * c_h + wj * (v_col * k)
        n_out_ref[0, h:h + 1, :] = w_inter * n_h + wj * k
    h_ref[pl.ds(i, 1), :] = jnp.concatenate(outs, axis=1)


def _mlstm_sample(z_s, gates_s, c0, n0, m0, bi, bf, bo, ghn):
    ns = z_s.shape[0]
    const2 = lambda i: (0, 0)
    in_specs = [
        pl.BlockSpec((1, 1, ML_MAIN), lambda i: (i, 0, 0)),
        pl.BlockSpec((1, 1, 256), lambda i: (i, 0, 0)),
        pl.BlockSpec((1, ML_HEADS, ML_DV, ML_DQK), lambda i: (i, 0, 0, 0)),
        pl.BlockSpec((1, ML_HEADS, ML_DQK), lambda i: (i, 0, 0)),
        pl.BlockSpec((1, 1, 128), lambda i: (i, 0, 0)),
        pl.BlockSpec((1, 128), const2),
        pl.BlockSpec((1, 128), const2),
        pl.BlockSpec((1, D_MODEL), const2),
        pl.BlockSpec((1, ML_NV), const2),
    ]
    out_shape = [
        jax.ShapeDtypeStruct((ns, ML_NV), F32),
        jax.ShapeDtypeStruct((ns, ML_HEADS, ML_DV, ML_DQK), F32),
        jax.ShapeDtypeStruct((ns, ML_HEADS, ML_DQK), F32),
        jax.ShapeDtypeStruct((ns, 1, 128), F32),
    ]
    out_specs = [
        pl.BlockSpec((ns, ML_NV), const2),
        pl.BlockSpec((1, ML_HEADS, ML_DV, ML_DQK), lambda i: (i, 0, 0, 0)),
        pl.BlockSpec((1, ML_HEADS, ML_DQK), lambda i: (i, 0, 0)),
        pl.BlockSpec((1, 1, 128), lambda i: (i, 0, 0)),
    ]
    return pl.pallas_call(
        _mlstm_sample_kernel, grid=(ns,), in_specs=in_specs, out_specs=out_specs, out_shape=out_shape,
        compiler_params=_cparams(("arbitrary",)), name="mlstm_sample",
    )(z_s, gates_s, c0, n0, m0, bi, bf, bo, ghn)


def _attn_prompt_kernel(q_ref, kp_ref, kc_ref, vp_ref, vc_ref, o_ref, lse_ref):
    blk = pl.program_id(1)
    T = ATT_STEPS
    qi = lax.broadcasted_iota(I32, (T, T), 0)
    kj = lax.broadcasted_iota(I32, (T, T), 1)
    valid_prev = jnp.logical_and(kj >= qi, blk > 0)
    valid_cur = kj <= qi
    scale = ATT_HD ** -0.5
    nt = (((1,), (1,)), ((), ()))
    lane = lax.broadcasted_iota(I32, (T, ATT_HD), 1)
    lse_all = jnp.zeros((T, ATT_HD), F32)
    raw = []
    for h in range(ATT_H):
        sl = slice(h * ATT_HD, (h + 1) * ATT_HD)
        qh = q_ref[:, sl]
        raw.append((lax.dot_general(qh, kp_ref[:, sl], nt, preferred_element_type=F32),
                    lax.dot_general(qh, kc_ref[:, sl], nt, preferred_element_type=F32)))
    for h in range(ATT_H):
        sl = slice(h * ATT_HD, (h + 1) * ATT_HD)
        s1 = jnp.where(valid_prev, raw[h][0] * scale, -jnp.inf)
        s2 = jnp.where(valid_cur, raw[h][1] * scale, -jnp.inf)
        mx = jnp.max(jnp.maximum(s1, s2), axis=1, keepdims=True)
        p1 = jnp.exp(s1 - mx)
        p2 = jnp.exp(s2 - mx)
        den = jnp.sum(p1 + p2, axis=1, keepdims=True)
        acc = jnp.dot(p1.astype(BF16), vp_ref[:, sl], preferred_element_type=F32)
        acc = acc + jnp.dot(p2.astype(BF16), vc_ref[:, sl], preferred_element_type=F32)
        o_ref[:, sl] = acc / den
        lse_all = jnp.where(lane == h, mx + jnp.log(den), lse_all)
    lse_ref[...] = lse_all


def _attn_prompt(q, kv, g):
    dil, L, _ = q.shape
    nb = L // ATT_STEPS
    T = ATT_STEPS
    blk = (None, T, ATT_GW)
    in_specs = [
        pl.BlockSpec(blk, lambda r, b: (r, b, 0)),
        pl.BlockSpec(blk, lambda r, b: (r, jnp.maximum(b - 1, 0), 0)),
        pl.BlockSpec(blk, lambda r, b: (r, b, 0)),
        pl.BlockSpec(blk, lambda r, b: (r, jnp.maximum(b - 1, 0), 1)),
        pl.BlockSpec(blk, lambda r, b: (r, b, 1)),
    ]
    out_specs = [pl.BlockSpec(blk, lambda r, b: (r, b, 0)), pl.BlockSpec((None, T, ATT_HD), lambda r, b: (r, b, 0))]
    return pl.pallas_call(
        _attn_prompt_kernel, grid=(dil, nb), in_specs=in_specs, out_specs=out_specs,
        out_shape=[jax.ShapeDtypeStruct((dil, L, ATT_GW), F32), jax.ShapeDtypeStruct((dil, L, ATT_HD), F32)],
        compiler_params=_cparams(("arbitrary", "arbitrary")), name=f"attn_prompt_g{g}",
    )(q, kv, kv, kv, kv)


def _merge_kernel(*refs):
    in_refs, out_ref, scratch = refs[:2 * ATT_G], refs[2 * ATT_G], refs[2 * ATT_G + 1:]
    tm = out_ref.shape[0]

    def position_order(ref, lanes, buf, dil):
        if dil == 1:
            return ref[0, :, lanes]
        for r in range(dil):
            buf[pl.ds(r, tm // dil, stride=dil), :] = ref[r, :, lanes]
        return buf[...]

    all128 = slice(0, ATT_HD)
    lses = [position_order(in_refs[2 * g + 1], all128, scratch[2 * g + 1], ATT_DILS[g]) for g in range(ATT_G)]
    mx = jnp.maximum(jnp.maximum(lses[0], lses[1]), lses[2])
    es = [jnp.exp(l - mx) for l in lses]
    tot = es[0] + es[1] + es[2]
    wgt = [e / tot for e in es]
    for c in range(ATT_H):
        lanes = slice(c * ATT_HD, (c + 1) * ATT_HD)
        acc = None
        for g in range(ATT_G):
            o = position_order(in_refs[2 * g], lanes, scratch[2 * g], ATT_DILS[g])
            term = wgt[g][:, c:c + 1] * o
            acc = term if acc is None else acc + term
        out_ref[:, lanes] = acc.astype(out_ref.dtype)


def _merge_groups(parts, seq, tm=512):
    in_specs, args, scratch = [], [], []
    for g, pair in enumerate(parts):
        dil = ATT_DILS[g]
        for a in pair:
            in_specs.append(pl.BlockSpec((dil, tm // dil, a.shape[2]), lambda i: (0, i, 0)))
            args.append(a)
            scratch.append(pltpu.VMEM((tm, ATT_HD), F32))
    return pl.pallas_call(
        _merge_kernel, grid=(seq // tm,), in_specs=in_specs,
        out_specs=pl.BlockSpec((tm, ATT_GW), lambda i: (i, 0)),
        out_shape=jax.ShapeDtypeStruct((seq, ATT_GW), BF16), scratch_shapes=scratch,
        compiler_params=_cparams(("arbitrary",)), name="attn_merge",
    )(*args)


def _attn_sample_kernel(q_ref, kvn_ref, b0_ref, b1_ref, b2_ref, out_ref):
    scale = ATT_HD ** -0.5
    outs, lses = [], []
    for g, b_ref in enumerate((b0_ref, b1_ref, b2_ref)):
        qg = q_ref[0, g]
        kn = kvn_ref[0, g, 0]
        vn = kvn_ref[0, g, 1]
        kb = b_ref[:, 0]
        vb = b_ref[:, 1]
        s = jnp.sum(kb * qg[None], axis=2, keepdims=True) * scale
        s_new = jnp.sum(kn * qg, axis=1, keepdims=True) * scale
        mx = jnp.maximum(jnp.max(s, axis=0), s_new)
        p = jnp.exp(s - mx[None])
        p_new = jnp.exp(s_new - mx)
        den = jnp.sum(p, axis=0) + p_new
        o = jnp.sum(p * vb, axis=0) + p_new * vn
        outs.append(o / den)
        lses.append(mx + jnp.log(den))
    mxl = jnp.maximum(jnp.maximum(lses[0], lses[1]), lses[2])
    es = [jnp.exp(l - mxl) for l in lses]
    tot = es[0] + es[1] + es[2]
    out_ref[0] = (es[0] / tot) * outs[0] + (es[1] / tot) * outs[1] + (es[2] / tot) * outs[2]


def _attn_sample(q_s, kv_s, caches):
    ns = q_s.shape[0]
    views, specs = [], []
    for g, cbuf in enumerate(caches):
        lb = cbuf.shape[1]
        dil = ATT_DILS[g]
        views.append(cbuf.reshape(ns, lb // dil, dil, 2, ATT_H, ATT_HD))
        specs.append(pl.BlockSpec((None, ATT_STEPS, None, 2, ATT_H, ATT_HD), lambda i: (i, 0, 0, 0, 0, 0)))
    in_specs = [
        pl.BlockSpec((1, ATT_G, ATT_H, ATT_HD), lambda i: (i, 0, 0, 0)),
        pl.BlockSpec((1, ATT_G, 2, ATT_H, ATT_HD), lambda i: (i, 0, 0, 0, 0)),
    ] + specs
    return pl.pallas_call(
        _attn_sample_kernel, grid=(ns,), in_specs=in_specs,
        out_specs=pl.BlockSpec((1, ATT_H, ATT_HD), lambda i: (i, 0, 0)),
        out_shape=jax.ShapeDtypeStruct((ns, ATT_H, ATT_HD), F32),
        compiler_params=_cparams(("arbitrary",)), name="attn_sample",
    )(q_s, kv_s, *views)


KV_SHIFT_BLOCK = 8192


def _kv_shift_kernel(cur_ref, nxt_ref, new_ref, out_ref):
    blk = out_ref.shape[1]
    out_ref[0, :blk - KV_ROW] = cur_ref[0, KV_ROW:]
    last = pl.program_id(1) == pl.num_programs(1) - 1
    out_ref[0, blk - KV_ROW:] = jnp.where(last, new_ref[0], nxt_ref[0])


def _kv_shift(cache, new):
    ns, lb = cache.shape[0], cache.shape[1]
    rows = lb * KV_ROW
    blk = min(KV_SHIFT_BLOCK, rows)
    nb = rows // blk
    per = blk // KV_ROW
    flat = cache.reshape(ns, rows, ATT_HD)
    out = pl.pallas_call(
        _kv_shift_kernel, grid=(ns, nb),
        in_specs=[pl.BlockSpec((1, blk, ATT_HD), lambda i, j: (i, j, 0)),
                  pl.BlockSpec((1, KV_ROW, ATT_HD), lambda i, j: (i, jnp.minimum((j + 1) * per, lb - 1), 0)),
                  pl.BlockSpec((1, KV_ROW, ATT_HD), lambda i, j: (i, 0, 0))],
        out_specs=pl.BlockSpec((1, blk, ATT_HD), lambda i, j: (i, j, 0)),
        out_shape=jax.ShapeDtypeStruct(flat.shape, flat.dtype),
        compiler_params=_cparams(("arbitrary", "arbitrary")), name="kv_shift",
    )(flat, flat, new.reshape(ns, KV_ROW, ATT_HD))
    return out.reshape(cache.shape)


def _top2(y, wr_ref, br_ref):
    rows = y.shape[0]
    lane = lax.broadcasted_iota(I32, (rows, 128), 1)
    logits = jnp.dot(y, wr_ref[...], precision=HIGHEST, preferred_element_type=F32) + br_ref[...]
    logits = jnp.where(lane < N_EXPERTS, logits, NEG_BIG)
    e = jnp.exp(logits - jnp.max(logits, axis=1, keepdims=True))
    probs = e / jnp.sum(e, axis=1, keepdims=True)
    p1 = jnp.max(probs, axis=1, keepdims=True)
    i1 = jnp.min(jnp.where(probs == p1, lane, 128), axis=1, keepdims=True)
    probs2 = jnp.where(lane == i1, -1.0, probs)
    p2 = jnp.max(probs2, axis=1, keepdims=True)
    i2 = jnp.min(jnp.where(probs2 == p2, lane, 128), axis=1, keepdims=True)
    tot = p1 + p2
    return lane, i1, i2, p1 / tot, p2 / tot


def _router_kernel(x_ref, xs_ref, g_ref, wr_ref, br_ref, xn_ref, eid_ref, gate_ref, rank_ref, cnt_ref,
                   xns_ref, gs_ref, carry):
    i = pl.program_id(0)
    tm = x_ref.shape[0]

    @pl.when(i == 0)
    def _():
        carry[...] = jnp.zeros_like(carry)
        ys = _rms(xs_ref[...]) * g_ref[...]
        xns_ref[...] = ys
        lane, i1, i2, g1, g2 = _top2(ys, wr_ref, br_ref)
        gs_ref[...] = jnp.where(lane == i1, g1, jnp.where(lane == i2, g2, 0.0))

    y = _rms(x_ref[...]) * g_ref[...]
    xn_ref[...] = y
    lane, i1, i2, g1, g2 = _top2(y, wr_ref, br_ref)
    sel1 = lane == i1
    sel2 = lane == i2
    onehot = jnp.where(jnp.logical_or(sel1, sel2), 1.0, 0.0)
    rr = lax.broadcasted_iota(I32, (tm, tm), 0)
    cc = lax.broadcasted_iota(I32, (tm, tm), 1)
    before = (cc < rr).astype(BF16)
    prefix = jnp.dot(before, onehot.astype(BF16), preferred_element_type=F32) + carry[...]
    r1 = jnp.sum(jnp.where(sel1, prefix, 0.0), axis=1, keepdims=True)
    r2 = jnp.sum(jnp.where(sel2, prefix, 0.0), axis=1, keepdims=True)
    carry[...] = carry[...] + jnp.sum(onehot, axis=0, keepdims=True)
    eid_ref[...] = jnp.where(lane == 0, i1, jnp.where(lane == 1, i2, 0))
    gate_ref[...] = jnp.where(lane == 0, g1, jnp.where(lane == 1, g2, 0.0))
    rank_ref[...] = jnp.where(lane == 0, r1, jnp.where(lane == 1, r2, 0.0)).astype(I32)
    cnt_ref[...] = jnp.broadcast_to(carry[...], cnt_ref.shape)


def _router(h, hs, gain, w_router_pad, b_router_pad, tm=ROW_TILE):
    m, d = h.shape
    s = hs.shape[0]
    const = lambda i: (0, 0)
    row_spec = pl.BlockSpec((tm, 128), lambda i: (i, 0))
    return pl.pallas_call(
        _router_kernel, grid=(m // tm,),
        in_specs=[pl.BlockSpec((tm, d), lambda i: (i, 0)), pl.BlockSpec((s, d), const), pl.BlockSpec((1, d), const),
                  pl.BlockSpec((d, 128), const), pl.BlockSpec((1, 128), const)],
        out_specs=[pl.BlockSpec((tm, d), lambda i: (i, 0)), row_spec, row_spec, row_spec,
                   pl.BlockSpec((8, 128), const), pl.BlockSpec((s, d), const), pl.BlockSpec((s, 128), const)],
        out_shape=[jax.ShapeDtypeStruct((m, d), F32), jax.ShapeDtypeStruct((m, 128), I32),
                   jax.ShapeDtypeStruct((m, 128), F32), jax.ShapeDtypeStruct((m, 128), I32),
                   jax.ShapeDtypeStruct((8, 128), F32), jax.ShapeDtypeStruct((s, d), F32),
                   jax.ShapeDtypeStruct((s, 128), F32)],
        scratch_shapes=[pltpu.VMEM((1, 128), F32)],
        compiler_params=_cparams(("arbitrary",)), name="router",
    )(h, hs, gain, w_router_pad, b_router_pad)


def _dispatch_kernel(pos_ref, nv_ref, x_hbm, out_ref, inv, buf, sem, *, n_tok):
    i = pl.program_id(0)
    tg = out_ref.shape[0]

    @pl.when(i == 0)
    def _():
        def clear(s, c):
            inv[s] = 0
            return c

        lax.fori_loop(0, inv.shape[0], clear, 0, unroll=8)

        def fill(t, c):
            inv[pos_ref[2 * t]] = t
            inv[pos_ref[2 * t + 1]] = t
            return c

        lax.fori_loop(0, n_tok, fill, 0, unroll=8)

    def start_gather(tile):
        slot = tile % 2

        def issue(r2, c):
            for j in range(2):
                r = 2 * r2 + j
                pltpu.make_async_copy(x_hbm.at[pl.ds(inv[tile * tg + r], 1)], buf.at[slot, pl.ds(r, 1)],
                                      sem.at[slot]).start(priority=j)
            return c

        lax.fori_loop(0, tg // 2, issue, 0, unroll=4)

    @pl.when(i == 0)
    def _():
        start_gather(i)

    @pl.when(i + 1 < nv_ref[0])
    def _():
        start_gather(i + 1)

    @pl.when(i < nv_ref[0])
    def _():
        slot = i % 2
        pltpu.make_async_copy(x_hbm.at[pl.ds(0, tg)], buf.at[slot], sem.at[slot]).wait()
        out_ref[...] = buf[slot].astype(out_ref.dtype)

    @pl.when(i >= nv_ref[0])
    def _():
        out_ref[...] = jnp.zeros_like(out_ref)


def _dispatch(pos_flat, nvalid, xn, n_tiles, tg):
    n_tok, d = xn.shape
    return pl.pallas_call(
        functools.partial(_dispatch_kernel, n_tok=n_tok),
        grid_spec=pltpu.PrefetchScalarGridSpec(
            num_scalar_prefetch=2, grid=(n_tiles,),
            in_specs=[pl.BlockSpec(memory_space=pl.ANY)],
            out_specs=pl.BlockSpec((tg, d), lambda i, pos, nv: (i, 0)),
            scratch_shapes=[pltpu.SMEM((n_tiles * tg,), I32), pltpu.VMEM((2, tg, d), xn.dtype),
                            pltpu.SemaphoreType.DMA((2,))]),
        out_shape=jax.ShapeDtypeStruct((n_tiles * tg, d), BF16),
        compiler_params=_cparams(("arbitrary",)), name="moe_dispatch",
    )(pos_flat, nvalid, xn)


def _combine_kernel(pos_ref, h_ref, gate_ref, g_ref, hs_ref, gs_ref, ys_s_ref, ys_hbm, out_ref, outs_ref, ybuf, sem):
    i = pl.program_id(0)
    tm = h_ref.shape[0]

    def start_gather(tile):
        slot = tile % 2

        def issue(r, carry):
            for j in range(2):
                pltpu.make_async_copy(ys_hbm.at[pl.ds(pos_ref[2 * (tile * tm + r) + j], 1)],
                                      ybuf.at[slot, j, pl.ds(r, 1)], sem.at[slot]).start(priority=j)
            return carry

        lax.fori_loop(0, tm, issue, 0, unroll=4)

    @pl.when(i == 0)
    def _():
        start_gather(i)

    @pl.when(i + 1 < pl.num_programs(0))
    def _():
        start_gather(i + 1)

    @pl.when(i == 0)
    def _():
        gs = gs_ref[...]
        y = jnp.zeros(hs_ref.shape, F32)
        for e in range(N_EXPERTS):
            y = y + gs[:, e:e + 1] * ys_s_ref[e]
        outs_ref[...] = _rms(hs_ref[...] + y) * g_ref[...]

    slot = i % 2
    for j in range(2):
        pltpu.make_async_copy(ys_hbm.at[pl.ds(0, tm)], ybuf.at[slot, j], sem.at[slot]).wait()
    gate = gate_ref[...]
    y = h_ref[...] + (gate[:, 0:1] * ybuf[slot, 0] + gate[:, 1:2] * ybuf[slot, 1])
    out_ref[...] = _rms(y) * g_ref[...]


def _combine(pos_flat, h, gate, g_final, hs, gs, ys_s, ys, tm=ROW_TILE):
    m, d = h.shape
    s = hs.shape[0]
    c2 = lambda i, pos: (0, 0)
    return pl.pallas_call(
        _combine_kernel,
        grid_spec=pltpu.PrefetchScalarGridSpec(
            num_scalar_prefetch=1, grid=(m // tm,),
            in_specs=[pl.BlockSpec((tm, d), lambda i, pos: (i, 0)),
                      pl.BlockSpec((tm, 128), lambda i, pos: (i, 0)),
                      pl.BlockSpec((1, d), c2),
                      pl.BlockSpec((s, d), c2),
                      pl.BlockSpec((s, 128), c2),
                      pl.BlockSpec((N_EXPERTS, s, d), lambda i, pos: (0, 0, 0)),
                      pl.BlockSpec(memory_space=pl.ANY)],
            out_specs=[pl.BlockSpec((tm, d), lambda i, pos: (i, 0)), pl.BlockSpec((s, d), c2)],
            scratch_shapes=[pltpu.VMEM((2, 2, tm, d), F32), pltpu.SemaphoreType.DMA((2,))]),
        out_shape=[jax.ShapeDtypeStruct((m, d), F32), jax.ShapeDtypeStruct((s, d), F32)],
        compiler_params=_cparams(("arbitrary",)), name="moe_combine",
    )(pos_flat, h, gate, g_final, hs, gs, ys_s, ys)


def _rope_tables(pos):
    half = ATT_HD // 2
    inv = ROPE_THETA ** (-jnp.arange(half, dtype=F32) / half)
    ang = pos.astype(F32)[:, None] * inv[None, :]
    cos, sin = jnp.cos(ang), jnp.sin(ang)
    return jnp.concatenate([cos, cos], axis=1), jnp.concatenate([-sin, sin], axis=1)


def kernel(x_prompt, x_sample, state_mlstm_C, state_mlstm_n, state_mlstm_m, cache_kv_w128, cache_kv_w512, cache_kv_w2048, g_mix, g_ffn, w_ml_in, b_ml_gates, b_ml_o, g_ml_hnorm, w_ml_out, g_kv, w_kv, w_q, w_o, w_ffn_gate, w_ffn_up, w_ffn_down, w_router, b_router, w_exp_gate, w_exp_up, w_exp_down, g_final):
    bp, seq, d = x_prompt.shape
    ns = x_sample.shape[0]
    caches = (cache_kv_w128, cache_kv_w512, cache_kv_w2048)
    assert bp == 1 and x_sample.shape[1] == 1 and d == D_MODEL and ns % 8 == 0
    assert seq % (ATT_STEPS * max(ATT_DILS)) == 0 and seq % 1024 == 0
    assert all(c.shape[1] == w for c, w in zip(caches, ATT_WINDOWS))
    tm = 512

    h0 = x_prompt.reshape(seq, d)
    h0_s = x_sample.reshape(ns, d)

    w_gates = lax.slice_in_dim(w_ml_in, ML_MAIN, ML_MAIN + 2 * ML_HEADS, axis=2)[0]
    w_gates_pad = jnp.zeros((d, 256), F32).at[:, 0:ML_HEADS].set(w_gates[:, :ML_HEADS])
    w_gates_pad = w_gates_pad.at[:, 128:128 + ML_HEADS].set(w_gates[:, ML_HEADS:])
    bi = jnp.zeros((1, 128), F32).at[0, :ML_HEADS].set(b_ml_gates[0, :ML_HEADS])
    bf = jnp.zeros((1, 128), F32).at[0, :ML_HEADS].set(b_ml_gates[0, ML_HEADS:])
    (xn0, gates), (xn0_s, gates_s) = _rmsnorm(h0, h0_s, g_mix[0:1], proj=w_gates_pad)
    k_scale = jnp.concatenate([jnp.ones((1, ML_NQK), F32), jnp.full((1, ML_NQK), ML_DQK ** -0.5, F32),
                               jnp.ones((1, ML_NV + D_MODEL), F32)], axis=1)
    z, z_s = _matmul(xn0, [w_ml_in], ML_MAIN, BF16, xn0_s[None], tm=1024, tn=1024, col_scale=k_scale, name="ml_in")
    bo = b_ml_o[0:1]
    ghn = g_ml_hnorm[0:1]
    hg, p_c, p_n, p_m = _mlstm_prompt(z[0], gates, bi, bf, bo, ghn)
    m0 = jnp.zeros((ns, 1, 128), F32).at[:, 0, :ML_HEADS].set(state_mlstm_m[0])
    hg_s, s_c, s_n, s_m = _mlstm_sample(z_s[0].reshape(ns, 1, ML_MAIN), gates_s.reshape(ns, 1, 256),
                                        state_mlstm_C[0], state_mlstm_n[0], m0, bi, bf, bo, ghn)
    h1, h1_s = _matmul(hg, [w_ml_out], d, F32, hg_s[None], tm=tm, tn=1024, res=h0, res_s=h0_s, name="ml_out")
    h1, h1_s = h1[0], h1_s[0]

    (xf0,), (xf0_s,) = _rmsnorm(h1, h1_s, g_ffn[0:1])
    ffn_dense = w_ffn_gate.shape[2]
    hid, hid_s = _matmul(xf0, [w_ffn_gate, w_ffn_up], ffn_dense, BF16, xf0_s[None], tm=1024, tn=512, name="ffn_up")
    h2, h2_s = _matmul(hid[0], [w_ffn_down], d, F32, hid_s, tm=tm, tn=512, res=h1, res_s=h1_s, name="ffn_down")
    h2, h2_s = h2[0], h2_s[0]

    (xq, xkv), (xq_s, xkv_s) = _rmsnorm(h2, h2_s, jnp.stack([g_mix[1], g_kv]))
    cos, sin = _rope_tables(jnp.arange(seq))
    cos_s, sin_s = _rope_tables(jnp.full((ns,), PAST_LEN, I32))
    rope_args = dict(cos=cos, sin=sin, cos_s=cos_s, sin_s=sin_s)
    parts, kv_nat, kv_new, q_new = [], [], [], []
    tail = min(max(ATT_WINDOWS), seq)
    for g in range(ATT_G):
        dil = ATT_DILS[g]
        kvd, kvn, kv_s = _matmul(xkv, [w_kv[None]], 2 * ATT_GW, BF16, xkv_s[None], tm=1024, tn=ATT_GW,
                                 col_off=2 * g, rope="even", dil=dil, natural_tail=tail, name=f"kv_proj_g{g}",
                                 **rope_args)
        qd, q_s = _matmul(xq, [w_q], ATT_GW, BF16, xq_s[None], tm=1024, tn=ATT_GW, col_off=g, rope="all",
                          dil=dil, name=f"q_proj_g{g}", **rope_args)
        parts.append(_attn_prompt(qd, kvd, g))
        kv_nat.append(kvn)
        kv_new.append(kv_s[0])
        q_new.append(q_s[0])
    att = _merge_groups(parts, seq)
    att_s = _attn_sample(jnp.stack(q_new, axis=1).reshape(ns, ATT_G, ATT_H, ATT_HD),
                         jnp.stack(kv_new, axis=1).reshape(ns, ATT_G, 2, ATT_H, ATT_HD), caches)
    h3, h3_s = _matmul(att, [w_o], d, F32, att_s.reshape(1, ns, ATT_GW), tm=tm, tn=2048, res=h2, res_s=h2_s,
                       name="attn_out")
    h3, h3_s = h3[0], h3_s[0]

    wr_pad = jnp.zeros((d, 128), F32).at[:, :N_EXPERTS].set(w_router[0])
    br_pad = jnp.zeros((1, 128), F32).at[0, :N_EXPERTS].set(b_router[0])
    xn2, eid, gate, rank, cnt, xn2_s, gates_moe_s = _router(h3, h3_s, g_ffn[1:2], wr_pad, br_pad)
    tg = MOE_TILE
    n_tiles = -(-(2 * seq + N_EXPERTS * (tg - 1)) // tg)
    counts = cnt[0, :N_EXPERTS].astype(I32)
    padded = jnp.maximum((counts + tg - 1) // tg, 1) * tg
    gend = jnp.cumsum(padded)
    gstart = gend - padded
    pos = (gstart[eid[:, :2]] + rank[:, :2]).astype(I32).reshape(-1)
    nvalid = (gend[-1] // tg).astype(I32).reshape(1)
    xs = _dispatch(pos, nvalid, xn2, n_tiles, tg)
    t0 = (gstart // tg).astype(I32)
    tcnt = (padded // tg).astype(I32)
    hs, hs_s, sbuf2 = _moe_matmul(xs, [w_exp_gate[0], w_exp_up[0]], BF16, xn2_s[None], t0, tcnt, nvalid, tm=tg,
                                  tn=1024, xs_per_expert=False, name="moe_up", window=(caches[2], kv_new[2]))
    ys, ys_s, sbuf1 = _moe_matmul(hs, [w_exp_down[0]], F32, hs_s, t0, tcnt, nvalid, tm=tg, tn=512,
                                  xs_per_expert=True, name="moe_down", window=(caches[1], kv_new[1]))
    s_bufs = [_kv_shift(caches[0], kv_new[0]), sbuf1, sbuf2]
    y_p, y_s = _combine(pos, h3, gate, g_final.reshape(1, d), h3_s, gates_moe_s, ys_s, ys)

    p_bufs = []
    for g in range(ATT_G):
        keep = min(ATT_WINDOWS[g], seq)
        p_bufs.append(kv_nat[g][tail - keep:].reshape(1, keep, 2, ATT_H, ATT_HD))
    return (y_p.reshape(1, seq, d), y_s.reshape(ns, 1, d),
            p_c[None, None], p_n[None, None], p_m[:, :ML_HEADS][None],
            s_c[None], s_n[None], s_m[:, 0, :ML_HEADS][None],
            p_bufs[0], p_bufs[1], p_bufs[2], s_bufs[0], s_bufs[1], s_bufs[2])
```
